```python
import jax, jax.numpy as jnp
from jax import lax
import numpy as np

D_MODEL = 2048
BATCH = 2
SEQ = 16384
DEPTH = 2

N_META = 16
HEAD_DIM = 128
N_Q_HEADS = 16
N_KV_HEADS = 4
Q_PER_KV = N_Q_HEADS // N_KV_HEADS
WINDOW = 128
BLOCK = 128
ROPE_THETA = 500000.0
ROT_DIM = HEAD_DIM // 4
N_FOURIER_GROUPS = 8
FOURIER_GROUP_DIM = 128
FOURIER_DIM = N_FOURIER_GROUPS * FOURIER_GROUP_DIM
Q_DIM = N_Q_HEADS * HEAD_DIM
KV_DIM = N_KV_HEADS * HEAD_DIM
IN_DIM = Q_DIM + 2 * KV_DIM + FOURIER_DIM
N_BRANCHES = 2
D_FF = 5632
RMS_EPS = 1e-6
NEG_INF = -1e30

kernel_name = "macaron_gated_swa_fnet_hybrid"


def rms_norm(x, g):
    xf = x.astype(jnp.float32)
    y = xf * lax.rsqrt(jnp.mean(xf * xf, axis=-1, keepdims=True) + RMS_EPS)
    return (y * g.astype(jnp.float32)).astype(x.dtype)


def swiglu(x, w_gate, w_up, w_down):
    return (jax.nn.silu(x @ w_gate) * (x @ w_up)) @ w_down


def rope_tables(n_pos):
    inv = 1.0 / (ROPE_THETA ** (jnp.arange(0, ROT_DIM, 2, dtype=jnp.float32) / ROT_DIM))
    ang = jnp.arange(n_pos, dtype=jnp.float32)[:, None] * inv[None, :]
    return jnp.cos(ang), jnp.sin(ang)


def partial_rope(x, cos, sin):
    half = ROT_DIM // 2
    x1 = x[..., :half].astype(jnp.float32)
    x2 = x[..., half:ROT_DIM].astype(jnp.float32)
    c = cos[None, :, None, :]
    s = sin[None, :, None, :]
    rot = jnp.concatenate([x1 * c - x2 * s, x2 * c + x1 * s], axis=-1).astype(x.dtype)
    return jnp.concatenate([rot, x[..., ROT_DIM:]], axis=-1)


def windowed_gqa(q, k, v, sink):
    B, T = q.shape[0], q.shape[1]
    front = BLOCK - N_META
    L = T + front
    nb = L // BLOCK
    pad = lambda a, lo, hi: jnp.pad(a, ((0, 0), (lo, hi), (0, 0), (0, 0)))
    qb = pad(q, front, 0).reshape(B, nb, BLOCK, N_KV_HEADS, Q_PER_KV, HEAD_DIM)
    kp = pad(k, front + BLOCK, BLOCK).reshape(B, nb + 2, BLOCK, N_KV_HEADS, HEAD_DIM)
    vp = pad(v, front + BLOCK, BLOCK).reshape(B, nb + 2, BLOCK, N_KV_HEADS, HEAD_DIM)
    kband = jnp.concatenate([kp[:, :-2], kp[:, 1:-1], kp[:, 2:]], axis=2)
    vband = jnp.concatenate([vp[:, :-2], vp[:, 1:-1], vp[:, 2:]], axis=2)
    k_meta = k[:, :N_META]
    v_meta = v[:, :N_META]
    tq = jnp.arange(nb * BLOCK).reshape(nb, BLOCK) - front
    tk = jnp.arange(nb)[:, None] * BLOCK + jnp.arange(3 * BLOCK)[None, :] - BLOCK - front
    band_ok = ((tk[:, None, :] >= N_META) & (tk[:, None, :] < T)
               & (jnp.abs(tq[:, :, None] - tk[:, None, :]) <= WINDOW))
    scale = HEAD_DIM ** -0.5
    s_band = jnp.einsum('bnqkgd,bnjkd->bnkgqj', qb, kband).astype(jnp.float32) * scale
    s_band = jnp.where(band_ok[None, :, None, None], s_band, NEG_INF)
    s_meta = jnp.einsum('bnqkgd,bmkd->bnkgqm', qb, k_meta).astype(jnp.float32) * scale
    sink_col = jnp.broadcast_to(sink.astype(jnp.float32).reshape(1, 1, N_KV_HEADS, Q_PER_KV, 1, 1),
                                s_band.shape[:-1] + (1,))
    p = jax.nn.softmax(jnp.concatenate([s_meta, s_band, sink_col], axis=-1), axis=-1)
    p_meta = p[..., :N_META].astype(v.dtype)
    p_band = p[..., N_META:N_META + 3 * BLOCK].astype(v.dtype)
    out = (jnp.einsum('bnkgqm,bmkd->bnqkgd', p_meta, v_meta)
           + jnp.einsum('bnkgqj,bnjkd->bnqkgd', p_band, vband))
    return out.reshape(B, L, Q_DIM)[:, front:]


def fourier_mix(u):
    B, T = u.shape[0], u.shape[1]
    ug = u.astype(jnp.float32).reshape(B, T, N_FOURIER_GROUPS, FOURIER_GROUP_DIM)
    f = jnp.real(jnp.fft.fftn(ug, axes=(1, 3), norm="ortho"))
    return f.reshape(B, T, FOURIER_DIM).astype(u.dtype)


def hybrid_mixer(xn, w_in, w_gate, sink, w_attn_branch, w_fourier_branch, w_out, cos, sin):
    B, T = xn.shape[0], xn.shape[1]
    proj = xn @ w_in
    q, k, v, u = jnp.split(proj, [Q_DIM, Q_DIM + KV_DIM, Q_DIM + 2 * KV_DIM], axis=-1)
    q = partial_rope(q.reshape(B, T, N_Q_HEADS, HEAD_DIM), cos, sin)
    k = partial_rope(k.reshape(B, T, N_KV_HEADS, HEAD_DIM), cos, sin)
    v = v.reshape(B, T, N_KV_HEADS, HEAD_DIM)
    a = windowed_gqa(q, k, v, sink) @ w_attn_branch
    f = fourier_mix(u) @ w_fourier_branch
    g = jax.nn.sigmoid(xn @ w_gate)
    g_a, g_f = jnp.split(g, N_BRANCHES, axis=-1)
    return (g_a * a + g_f * f) @ w_out


def setup_inputs(seed: int = 0) -> dict:
    key = jax.random.key(seed)
    ks = jax.random.split(key, 20)
    D = D_MODEL

    def dense(k, shape, fan_in):
        return jax.random.normal(k, shape, jnp.float32) * (fan_in ** -0.5)

    def gain(k, shape):
        return 1.0 + 0.02 * jax.random.normal(k, shape, jnp.float32)

    return {
        "x": jax.random.normal(ks[0], (BATCH, SEQ, D), jnp.float32),
        "meta_tokens": jax.random.normal(ks[1], (N_META, D), jnp.float32),
        "ffn1_norm": gain(ks[2], (DEPTH, D)),
        "ffn1_w_gate": dense(ks[3], (DEPTH, D, D_FF), D),
        "ffn1_w_up": dense(ks[4], (DEPTH, D, D_FF), D),
        "ffn1_w_down": dense(ks[5], (DEPTH, D_FF, D), D_FF),
        "mix_norm": gain(ks[6], (DEPTH, D)),
        "w_in": dense(ks[7], (DEPTH, D, IN_DIM), D),
        "w_gate": dense(ks[8], (DEPTH, D, N_BRANCHES * D), D),
        "sink": 0.5 * jax.random.normal(ks[9], (DEPTH, N_Q_HEADS), jnp.float32),
        "w_attn_branch": dense(ks[10], (DEPTH, Q_DIM, D), Q_DIM),
        "w_fourier_branch": dense(ks[11], (DEPTH, FOURIER_DIM, D), FOURIER_DIM),
        "w_out": dense(ks[12], (DEPTH, D, D), D),
        "ffn2_norm": gain(ks[13], (DEPTH, D)),
        "ffn2_w_gate": dense(ks[14], (DEPTH, D, D_FF), D),
        "ffn2_w_up": dense(ks[15], (DEPTH, D, D_FF), D),
        "ffn2_w_down": dense(ks[16], (DEPTH, D_FF, D), D_FF),
        "final_norm": gain(ks[17], (D,)),
    }


def reference(x, meta_tokens, ffn1_norm, ffn1_w_gate, ffn1_w_up, ffn1_w_down, mix_norm, w_in,
              w_gate, sink, w_attn_branch, w_fourier_branch, w_out, ffn2_norm, ffn2_w_gate,
              ffn2_w_up, ffn2_w_down, final_norm):
    B = x.shape[0]
    meta = jnp.broadcast_to(meta_tokens[None].astype(x.dtype), (B, N_META, D_MODEL))
    h = jnp.concatenate([meta, x], axis=1)
    cos, sin = rope_tables(h.shape[1])
    for l in range(DEPTH):
        h = h + 0.5 * swiglu(rms_norm(h, ffn1_norm[l]), ffn1_w_gate[l], ffn1_w_up[l], ffn1_w_down[l])
        h = h + hybrid_mixer(rms_norm(h, mix_norm[l]), w_in[l], w_gate[l], sink[l],
                             w_attn_branch[l], w_fourier_branch[l], w_out[l], cos, sin)
        h = h + 0.5 * swiglu(rms_norm(h, ffn2_norm[l]), ffn2_w_gate[l], ffn2_w_up[l], ffn2_w_down[l])
    return rms_norm(h, final_norm)[:, N_META:]
```

```python
import functools

import numpy as np
import jax
import jax.numpy as jnp
from jax import lax
from jax.experimental import pallas as pl
from jax.experimental.pallas import tpu as pltpu

_F32 = jnp.float32
_BF16 = jnp.bfloat16

HEAD_DIM = 128
N_Q_HEADS = 16
N_KV_HEADS = 4
Q_PER_KV = N_Q_HEADS // N_KV_HEADS
WINDOW = 128
BLOCK = 128
ROPE_THETA = 500000.0
ROT_DIM = HEAD_DIM // 4
N_FOURIER_GROUPS = 8
FOURIER_GROUP_DIM = 128
FOURIER_DIM = N_FOURIER_GROUPS * FOURIER_GROUP_DIM
Q_DIM = N_Q_HEADS * HEAD_DIM
KV_DIM = N_KV_HEADS * HEAD_DIM
IN_DIM = Q_DIM + 2 * KV_DIM + FOURIER_DIM
RMS_EPS = 1e-6
NEG_INF = -1e30

LANES = 128
PROJ_CHUNK = Q_PER_KV * HEAD_DIM
V7X_VMEM_BYTES = 64 * 1024 * 1024


def _round_up(a, m):
    return (a + m - 1) // m * m


def _cparams(sem, vmem_bytes):
    return pltpu.CompilerParams(dimension_semantics=sem,
                                vmem_limit_bytes=int(min(vmem_bytes, V7X_VMEM_BYTES - (4 << 20))))


def _rms(x, g):
    return x * lax.rsqrt(jnp.mean(x * x, axis=-1, keepdims=True) + RMS_EPS) * g


def _ffn_body(*refs, tf, nj, final):
    if final:
        h_ref, gain_ref, wgu_ref, wd_ref, fgain_ref, o_ref, xn_ref = refs
    else:
        h_ref, gain_ref, wgu_ref, wd_ref, o_ref, xn_ref = refs
    j = pl.program_id(2)

    @pl.when(j == 0)
    def _():
        x = h_ref[0]
        xn_ref[...] = _rms(x, gain_ref[...]).astype(_BF16)
        o_ref[0] = x

    gu = jnp.dot(xn_ref[...], wgu_ref[0], preferred_element_type=_F32)
    g = gu[:, :tf]
    u = gu[:, tf:]
    act = ((g * jax.nn.sigmoid(g)) * u * 0.5).astype(_BF16)
    o_ref[0] += jnp.dot(act, wd_ref[...], preferred_element_type=_F32)

    if final:
        @pl.when(j == nj - 1)
        def _():
            o_ref[0] = _rms(o_ref[0], fgain_ref[...])


def _ffn(h, gain, wgu, wd, *, tm, tf, final_gain=None, out_rows=None):
    B, T, D = h.shape
    nj = wgu.shape[0]
    final = final_gain is not None
    rows = T if out_rows is None else out_rows
    ni = pl.cdiv(rows, tm)
    in_specs = [
        pl.BlockSpec((1, tm, D), lambda b, i, j: (b, i, 0)),
        pl.BlockSpec((1, D), lambda b, i, j: (0, 0)),
        pl.BlockSpec((1, D, 2 * tf), lambda b, i, j: (j, 0, 0)),
        pl.BlockSpec((tf, D), lambda b, i, j: (j, 0)),
    ]
    args = [h, gain, wgu, wd]
    if final:
        in_specs.append(pl.BlockSpec((1, D), lambda b, i, j: (0, 0)))
        args.append(final_gain)
    vmem = (4 * tm * D * 4 + tm * D * 2 + 2 * (D * 2 * tf + tf * D) * 2
            + tm * 2 * tf * 4 * 2 + tm * tf * 2 + (6 << 20))
    return pl.pallas_call(
        functools.partial(_ffn_body, tf=tf, nj=nj, final=final),
        grid=(B, ni, nj),
        in_specs=in_specs,
        out_specs=pl.BlockSpec((1, tm, D), lambda b, i, j: (b, i, 0)),
        out_shape=jax.ShapeDtypeStruct((B, rows, D), _F32),
        scratch_shapes=[pltpu.VMEM((tm, D), _BF16)],
        compiler_params=_cparams(("parallel", "parallel", "arbitrary"), vmem),
        name="ffn_final" if final else "ffn",
    )(*args)


def _proj_body(h_ref, gain_ref, w_ref, c_ref, s1_ref, s2_ref, dft_ref, qkv_ref, z_ref, xn_ref,
               *, n_rope, n_qkv):
    j = pl.program_id(2)

    @pl.when(j == 0)
    def _():
        xn_ref[...] = _rms(h_ref[0], gain_ref[...]).astype(_BF16)

    y = jnp.dot(xn_ref[...], w_ref[0], preferred_element_type=_F32)

    @pl.when(j < n_rope)
    def _():
        c = c_ref[...]
        s1 = s1_ref[...]
        s2 = s2_ref[...]
        half = ROT_DIM // 2
        for hh in range(PROJ_CHUNK // HEAD_DIM):
            xh = y[:, hh * HEAD_DIM:(hh + 1) * HEAD_DIM]
            up = pltpu.roll(xh, HEAD_DIM - half, 1)
            dn = pltpu.roll(xh, half, 1)
            r = xh * c + up * s1 + dn * s2
            qkv_ref[0, :, hh * HEAD_DIM:(hh + 1) * HEAD_DIM] = r.astype(_BF16)

    @pl.when(jnp.logical_and(j >= n_rope, j < n_qkv))
    def _():
        qkv_ref[0] = y.astype(_BF16)

    @pl.when(j >= n_qkv)
    def _():
        ub = y.astype(_BF16)
        for gg in range(PROJ_CHUNK // FOURIER_GROUP_DIM):
            sl = slice(gg * FOURIER_GROUP_DIM, (gg + 1) * FOURIER_GROUP_DIM)
            zz = jnp.dot(ub[:, sl], dft_ref[...], preferred_element_type=_F32)
            z_ref[0, 0, :, sl] = zz[:, :FOURIER_GROUP_DIM].astype(_BF16)
            z_ref[0, 1, :, sl] = zz[:, FOURIER_GROUP_DIM:].astype(_BF16)


def _proj(h, gain, w_in_c, rope_c, rope_s1, rope_s2, dft_c, *, tm):
    B, T, D = h.shape
    nj = w_in_c.shape[0]
    n_rope = (Q_DIM + KV_DIM) // PROJ_CHUNK
    n_qkv = (Q_DIM + 2 * KV_DIM) // PROJ_CHUNK
    ni = pl.cdiv(T, tm)
    vmem = (2 * tm * D * 4 + tm * D * 2 + 2 * D * PROJ_CHUNK * 2 + 6 * tm * LANES * 4
            + 2 * tm * PROJ_CHUNK * 2 + 4 * tm * PROJ_CHUNK * 2 + 6 * tm * PROJ_CHUNK * 4 + (6 << 20))
    return pl.pallas_call(
        functools.partial(_proj_body, n_rope=n_rope, n_qkv=n_qkv),
        grid=(B, ni, nj),
        in_specs=[
            pl.BlockSpec((1, tm, D), lambda b, i, j: (b, i, 0)),
            pl.BlockSpec((1, D), lambda b, i, j: (0, 0)),
            pl.BlockSpec((1, D, PROJ_CHUNK), lambda b, i, j: (j, 0, 0)),
            pl.BlockSpec((tm, LANES), lambda b, i, j: (i, 0)),
            pl.BlockSpec((tm, LANES), lambda b, i, j: (i, 0)),
            pl.BlockSpec((tm, LANES), lambda b, i, j: (i, 0)),
            pl.BlockSpec((FOURIER_GROUP_DIM, 2 * FOURIER_GROUP_DIM), lambda b, i, j: (0, 0)),
        ],
        out_specs=[
            pl.BlockSpec((1, tm, PROJ_CHUNK), lambda b, i, j: (b, i, jnp.minimum(j, n_qkv - 1))),
            pl.BlockSpec((1, 2, tm, PROJ_CHUNK), lambda b, i, j: (b, 0, i, jnp.maximum(j - n_qkv, 0))),
        ],
        out_shape=[
            jax.ShapeDtypeStruct((B, T, Q_DIM + 2 * KV_DIM), _BF16),
            jax.ShapeDtypeStruct((B, 2, T, FOURIER_DIM), _BF16),
        ],
        scratch_shapes=[pltpu.VMEM((tm, D), _BF16)],
        compiler_params=_cparams(("parallel", "parallel", "arbitrary"), vmem),
        name="mixer_proj",
    )(h, gain, w_in_c, rope_c, rope_s1, rope_s2, dft_c)


def _attn_body(sink_ref, q_ref, kp_ref, kc_ref, kn_ref, km_ref, vp_ref, vc_ref, vn_ref, vm_ref, o_ref,
               *, nblk, n_meta):
    g = pl.program_id(1)
    j = pl.program_id(2)
    scale = HEAD_DIM ** -0.5
    q = q_ref[0]
    q4 = jnp.concatenate([q[:, h * HEAD_DIM:(h + 1) * HEAD_DIM] for h in range(Q_PER_KV)], axis=0)

    def scores(k):
        return lax.dot_general(q4, k, (((1,), (1,)), ((), ())), preferred_element_type=_F32) * scale

    s_m = scores(km_ref[0])
    s_p = scores(kp_ref[0])
    s_c = scores(kc_ref[0])
    s_n = scores(kn_ref[0])

    rows = Q_PER_KV * BLOCK
    row = lax.broadcasted_iota(jnp.int32, (rows, BLOCK), 0) % BLOCK
    col = lax.broadcasted_iota(jnp.int32, (rows, BLOCK), 1)
    is_meta_q = j == nblk
    never = 2 * BLOCK
    off_p = jnp.where(jnp.logical_and(j >= 1, jnp.logical_not(is_meta_q)), 0, never)
    off_c = jnp.where(is_meta_q, never, 0)
    off_n = jnp.where(is_meta_q, WINDOW - n_meta, jnp.where(j <= nblk - 2, 0, -never))
    ok_p = col >= row + off_p
    ok_c = col >= off_c
    ok_n = col <= row + off_n
    s_p = jnp.where(ok_p, s_p, NEG_INF)
    s_c = jnp.where(ok_c, s_c, NEG_INF)
    s_n = jnp.where(ok_n, s_n, NEG_INF)

    sink = jnp.concatenate(
        [jnp.full((BLOCK, 1), sink_ref[g * Q_PER_KV + h], _F32) for h in range(Q_PER_KV)], axis=0)
    mx = jnp.maximum(jnp.maximum(jnp.max(s_m, axis=-1, keepdims=True), jnp.max(s_p, axis=-1, keepdims=True)),
                     jnp.maximum(jnp.max(s_c, axis=-1, keepdims=True), jnp.max(s_n, axis=-1, keepdims=True)))
    mx = jnp.maximum(mx, sink)
    e_m = jnp.exp(s_m - mx)
    e_p = jnp.exp(s_p - mx)
    e_c = jnp.exp(s_c - mx)
    e_n = jnp.exp(s_n - mx)
    den = (jnp.sum(e_m, axis=-1, keepdims=True) + jnp.sum(e_p, axis=-1, keepdims=True)
           + jnp.sum(e_c, axis=-1, keepdims=True) + jnp.sum(e_n, axis=-1, keepdims=True)
           + jnp.exp(sink - mx))
    inv = 1.0 / den

    def pv(e, v):
        return jnp.dot((e * inv).astype(_BF16), v, preferred_element_type=_F32)

    out = pv(e_m, vm_ref[0]) + pv(e_p, vp_ref[0]) + pv(e_c, vc_ref[0]) + pv(e_n, vn_ref[0])
    for h in range(Q_PER_KV):
        o_ref[0, :, h * HEAD_DIM:(h + 1) * HEAD_DIM] = out[h * BLOCK:(h + 1) * BLOCK].astype(_BF16)


def _attention(qkv, sink, *, n_real, n_meta):
    B, T, _ = qkv.shape
    nblk = n_real // BLOCK
    kcol = Q_DIM // HEAD_DIM
    vcol = (Q_DIM + KV_DIM) // HEAD_DIM
    mrow = n_real // n_meta

    def prev_i(j):
        return jnp.clip(j - 1, 0, nblk - 1)

    def cur_i(j):
        return jnp.minimum(j, nblk - 1)

    def next_i(j):
        return jnp.where(j == nblk, 0, jnp.minimum(j + 1, nblk - 1))

    def kv_spec(rowf, col0):
        return pl.BlockSpec((1, BLOCK, HEAD_DIM), lambda b, g, j, s: (b, rowf(j), col0 + g))

    def meta_spec(col0):
        return pl.BlockSpec((1, n_meta, HEAD_DIM), lambda b, g, j, s: (b, mrow, col0 + g))

    grid_spec = pltpu.PrefetchScalarGridSpec(
        num_scalar_prefetch=1,
        grid=(B, N_KV_HEADS, nblk + 1),
        in_specs=[
            pl.BlockSpec((1, BLOCK, PROJ_CHUNK), lambda b, g, j, s: (b, j, g)),
            kv_spec(prev_i, kcol), kv_spec(cur_i, kcol), kv_spec(next_i, kcol), meta_spec(kcol),
            kv_spec(prev_i, vcol), kv_spec(cur_i, vcol), kv_spec(next_i, vcol), meta_spec(vcol),
        ],
        out_specs=pl.BlockSpec((1, BLOCK, PROJ_CHUNK), lambda b, g, j, s: (b, j, g)),
    )
    return pl.pallas_call(
        functools.partial(_attn_body, nblk=nblk, n_meta=n_meta),
        grid_spec=grid_spec,
        out_shape=jax.ShapeDtypeStruct((B, T, Q_DIM), _BF16),
        compiler_params=_cparams(("parallel", "parallel", "arbitrary"), 32 << 20),
        name="windowed_gqa",
    )(sink, qkv, qkv, qkv, qkv, qkv, qkv, qkv, qkv, qkv)


def _fft_factors(T):
    best = None
    for n2 in range(16, T + 1, 16):
        if T % n2:
            continue
        n1 = T // n2
        cost = 4 * n1 + 2 * n2
        if best is None or cost < best[0]:
            best = (cost, n1, n2)
    assert best is not None, "sequence length must be a multiple of 16"
    return best[1], best[2]


def _fft1_body(z_ref, m_ref, a_ref, rhs_ref, *, n1, n1h, kh, nb, C):
    rhs_ref[...] = jnp.zeros_like(rhs_ref)
    for t in range(nb):
        cs = slice(t * C, (t + 1) * C)
        rhs_ref[0:n1, :] = z_ref[0, 0, :, cs].astype(_F32)
        rhs_ref[kh:kh + n1, :] = z_ref[0, 1, :, cs].astype(_F32)
        res = jnp.dot(m_ref[t], rhs_ref[...].astype(_BF16), preferred_element_type=_F32)
        a_ref[0, 0, :, cs] = res[:n1h].astype(_BF16)
        a_ref[0, 1, :, cs] = res[n1h:].astype(_BF16)


def _fft2_body(ar_ref, ai_ref, mc_ref, ms_ref, f_ref):
    res = (jnp.dot(mc_ref[0], ar_ref[0, 0, 0], preferred_element_type=_F32)
           + jnp.dot(ms_ref[0], ai_ref[0, 0, 0], preferred_element_type=_F32))
    f_ref[0] = res.astype(_BF16)


def _fft_tables(T, n1, n2, n_meta, n1h, kh):
    two_pi = 2.0 * np.pi
    k1 = jnp.arange(n1, dtype=jnp.int32)[None, :, None]
    nn1 = jnp.arange(n1, dtype=jnp.int32)[None, None, :]
    nn2 = jnp.arange(n2, dtype=jnp.int32)[:, None, None]
    ph = (k1 * (nn2 + n_meta + n2 * nn1)) % T
    ang = ph.astype(_F32) * (two_pi / T)
    c = jnp.cos(ang) * (n1 ** -0.5)
    s = jnp.sin(ang) * (n1 ** -0.5)
    m1 = jnp.zeros((n2, 2 * n1h, 2 * kh), _F32)
    m1 = m1.at[:, 0:n1, 0:n1].set(c).at[:, 0:n1, kh:kh + n1].set(s)
    m1 = m1.at[:, n1h:n1h + n1, 0:n1].set(-s).at[:, n1h:n1h + n1, kh:kh + n1].set(c)
    k2 = jnp.arange(n2, dtype=jnp.int32)[:, None]
    mm2 = jnp.arange(n2, dtype=jnp.int32)[None, :]
    outs_c, outs_s = [], []
    for rot in (n_meta // n1, n_meta // n1 + 1):
        ph2 = (((k2 + rot) % n2) * (mm2 + n_meta)) % n2
        ang2 = ph2.astype(_F32) * (two_pi / n2)
        outs_c.append(jnp.cos(ang2) * (n2 ** -0.5))
        outs_s.append(jnp.sin(ang2) * (n2 ** -0.5))
    return m1.astype(_BF16), jnp.stack(outs_c).astype(_BF16), jnp.stack(outs_s).astype(_BF16)


def _position_dft(z, tables, *, n1, n2, n1h, kh, n_meta):
    B, _, T, C = z.shape
    m1, m2c, m2s = tables
    nb = 8 if n2 % 8 == 0 else 1
    zv = z.reshape(B, 2, n1, n2 * C)
    a = pl.pallas_call(
        functools.partial(_fft1_body, n1=n1, n1h=n1h, kh=kh, nb=nb, C=C),
        grid=(B, n2 // nb),
        in_specs=[
            pl.BlockSpec((1, 2, n1, nb * C), lambda b, t: (b, 0, 0, t)),
            pl.BlockSpec((nb, 2 * n1h, 2 * kh), lambda b, t: (t, 0, 0)),
        ],
        out_specs=pl.BlockSpec((1, 2, n1h, nb * C), lambda b, t: (b, 0, 0, t)),
        out_shape=jax.ShapeDtypeStruct((B, 2, n1h, n2 * C), _BF16),
        scratch_shapes=[pltpu.VMEM((2 * kh, C), _F32)],
        compiler_params=_cparams(("parallel", "arbitrary"), 32 << 20),
        name="dft_stage1",
    )(zv, m1)
    av = a.reshape(B, 2, n1h, n2, C)
    shift = n1 - (n_meta % n1)

    f = pl.pallas_call(
        _fft2_body,
        grid=(B, n1),
        in_specs=[
            pl.BlockSpec((1, 1, 1, n2, C), lambda b, k: (b, 0, k, 0, 0)),
            pl.BlockSpec((1, 1, 1, n2, C), lambda b, k: (b, 1, k, 0, 0)),
            pl.BlockSpec((1, n2, n2), lambda b, k: (jnp.where(k < n_meta % n1, 1, 0), 0, 0)),
            pl.BlockSpec((1, n2, n2), lambda b, k: (jnp.where(k < n_meta % n1, 1, 0), 0, 0)),
        ],
        out_specs=pl.BlockSpec((1, n2, C), lambda b, k: (b, 0, (k + shift) % n1)),
        out_shape=jax.ShapeDtypeStruct((B, n2, n1 * C), _BF16),
        compiler_params=_cparams(("parallel", "arbitrary"), 32 << 20),
        name="dft_stage2",
    )(av, av, m2c, m2s)
    return f.reshape(B, T, C)


def _mix_body(h_ref, gain_ref, a_ref, f_ref, wpa_ref, wpf_ref, wg_ref, wo_ref, o_ref, xn_ref, *, tn):
    j = pl.program_id(2)

    @pl.when(j == 0)
    def _():
        x = h_ref[0]
        xn_ref[...] = _rms(x, gain_ref[...]).astype(_BF16)
        o_ref[0] = x

    av = jnp.dot(a_ref[0], wpa_ref[0], preferred_element_type=_F32)
    fv = jnp.dot(f_ref[0], wpf_ref[0], preferred_element_type=_F32)
    gv = jax.nn.sigmoid(jnp.dot(xn_ref[...], wg_ref[0], preferred_element_type=_F32))
    m = (gv[:, :tn] * av + gv[:, tn:] * fv).astype(_BF16)
    o_ref[0] += jnp.dot(m, wo_ref[...], preferred_element_type=_F32)


def _mix_out(h, gain, attn, four, wpa_c, wpf_c, wg_c, wo, *, tm, tn):
    B, T, D = h.shape
    nj = wpa_c.shape[0]
    ni = pl.cdiv(T, tm)
    vmem = (4 * tm * D * 4 + tm * D * 2 + 2 * tm * (Q_DIM + FOURIER_DIM) * 2
            + 2 * (Q_DIM * tn + FOURIER_DIM * tn + D * 2 * tn + tn * D) * 2 + 8 * tm * tn * 4 + (6 << 20))
    return pl.pallas_call(
        functools.partial(_mix_body, tn=tn),
        grid=(B, ni, nj),
        in_specs=[
            pl.BlockSpec((1, tm, D), lambda b, i, j: (b, i, 0)),
            pl.BlockSpec((1, D), lambda b, i, j: (0, 0)),
            pl.BlockSpec((1, tm, Q_DIM), lambda b, i, j: (b, i, 0)),
            pl.BlockSpec((1, tm, FOURIER_DIM), lambda b, i, j: (b, i, 0)),
            pl.BlockSpec((1, Q_DIM, tn), lambda b, i, j: (j, 0, 0)),
            pl.BlockSpec((1, FOURIER_DIM, tn), lambda b, i, j: (j, 0, 0)),
            pl.BlockSpec((1, D, 2 * tn), lambda b, i, j: (j, 0, 0)),
            pl.BlockSpec((tn, D), lambda b, i, j: (j, 0)),
        ],
        out_specs=pl.BlockSpec((1, tm, D), lambda b, i, j: (b, i, 0)),
        out_shape=jax.ShapeDtypeStruct((B, T, D), _F32),
        scratch_shapes=[pltpu.VMEM((tm, D), _BF16)],
        compiler_params=_cparams(("parallel", "parallel", "arbitrary"), vmem),
        name="mixer_out",
    )(h, gain, attn, four, wpa_c, wpf_c, wg_c, wo)


def _col_chunks(w, tn):
    K, N = w.shape
    return w.reshape(K, N // tn, tn).transpose(1, 0, 2).astype(_BF16)


def _paired_col_chunks(wa, wb, tn):
    K, N = wa.shape
    a = wa.reshape(K, N // tn, tn)
    b = wb.reshape(K, N // tn, tn)
    return jnp.concatenate([a, b], axis=-1).transpose(1, 0, 2).astype(_BF16)


def _rope_tables(n_real, n_meta):
    T = n_real + n_meta
    half = ROT_DIM // 2
    inv = 1.0 / (ROPE_THETA ** (jnp.arange(0, ROT_DIM, 2, dtype=_F32) / ROT_DIM))
    ang = jnp.arange(T, dtype=_F32)[:, None] * inv[None, :]
    cos, sin = jnp.cos(ang), jnp.sin(ang)
    cos = jnp.concatenate([cos[n_meta:], cos[:n_meta]], axis=0)
    sin = jnp.concatenate([sin[n_meta:], sin[:n_meta]], axis=0)
    ones = jnp.ones((T, HEAD_DIM - ROT_DIM), _F32)
    zeros_h = jnp.zeros((T, half), _F32)
    zeros_r = jnp.zeros((T, HEAD_DIM - ROT_DIM), _F32)
    c = jnp.concatenate([cos, cos, ones], axis=1)
    s1 = jnp.concatenate([-sin, zeros_h, zeros_r], axis=1)
    s2 = jnp.concatenate([zeros_h, sin, zeros_r], axis=1)
    return c, s1, s2


def _channel_dft_table():
    n = FOURIER_GROUP_DIM
    c = jnp.arange(n, dtype=jnp.int32)[:, None]
    k = jnp.arange(n, dtype=jnp.int32)[None, :]
    ang = ((c * k) % n).astype(_F32) * (2.0 * np.pi / n)
    return (jnp.concatenate([jnp.cos(ang), -jnp.sin(ang)], axis=1) * (n ** -0.5)).astype(_BF16)


def _pick_tile(n, pref):
    t = pref
    while n % t:
        t //= 2
    return t


def kernel(x, meta_tokens, ffn1_norm, ffn1_w_gate, ffn1_w_up, ffn1_w_down, mix_norm, w_in, w_gate, sink,
           w_attn_branch, w_fourier_branch, w_out, ffn2_norm, ffn2_w_gate, ffn2_w_up, ffn2_w_down, final_norm):
    B, S, D = x.shape
    n_meta = meta_tokens.shape[0]
    depth = ffn1_norm.shape[0]
    d_ff = ffn1_w_gate.shape[-1]
    T = S + n_meta
    assert S % BLOCK == 0 and S % n_meta == 0 and BLOCK % n_meta == 0
    assert w_in.shape[-1] == IN_DIM and KV_DIM == PROJ_CHUNK

    tm = _pick_tile(S, 512)
    tf = _pick_tile(d_ff, 512)
    tn = _pick_tile(D, 512)

    n1, n2 = _fft_factors(T)
    n1h = _round_up(n1, 16)
    kh = _round_up(n1h, LANES // 2)
    fft_tables = _fft_tables(T, n1, n2, n_meta, n1h, kh)
    rope_c, rope_s1, rope_s2 = _rope_tables(S, n_meta)
    dft_c = _channel_dft_table()

    meta = jnp.broadcast_to(meta_tokens[None].astype(x.dtype), (B, n_meta, D))
    h = jnp.concatenate([x, meta], axis=1)

    for l in range(depth):
        h = _ffn(h, ffn1_norm[l][None], _paired_col_chunks(ffn1_w_gate[l], ffn1_w_up[l], tf),
                 ffn1_w_down[l].astype(_BF16), tm=tm, tf=tf)

        gain = mix_norm[l][None]
        qkv, z = _proj(h, gain, _col_chunks(w_in[l], PROJ_CHUNK), rope_c, rope_s1, rope_s2, dft_c, tm=tm)
        attn = _attention(qkv, sink[l], n_real=S, n_meta=n_meta)
        four = _position_dft(z, fft_tables, n1=n1, n2=n2, n1h=n1h, kh=kh, n_meta=n_meta)
        wg = w_gate[l]
        h = _mix_out(h, gain, attn, four, _col_chunks(w_attn_branch[l], tn), _col_chunks(w_fourier_branch[l], tn),
                     _paired_col_chunks(wg[:, :D], wg[:, D:], tn), w_out[l].astype(_BF16), tm=tm, tn=tn)

        last = l == depth - 1
        h = _ffn(h, ffn2_norm[l][None], _paired_col_chunks(ffn2_w_gate[l], ffn2_w_up[l], tf),
                 ffn2_w_down[l].astype(_BF16), tm=tm, tf=tf,
                 final_gain=final_norm[None] if last else None, out_rows=S if last else None)
    return h
```

```python
import functools

import numpy as np
import jax
import jax.numpy as jnp
from jax import lax
from jax.experimental import pallas as pl
from jax.experimental.pallas import tpu as pltpu

_F32 = jnp.float32
_BF16 = jnp.bfloat16

HEAD_DIM = 128
N_Q_HEADS = 16
N_KV_HEADS = 4
Q_PER_KV = N_Q_HEADS // N_KV_HEADS
WINDOW = 128
BLOCK = 128
ROPE_THETA = 500000.0
ROT_DIM = HEAD_DIM // 4
N_FOURIER_GROUPS = 8
FOURIER_GROUP_DIM = 128
FOURIER_DIM = N_FOURIER_GROUPS * FOURIER_GROUP_DIM
Q_DIM = N_Q_HEADS * HEAD_DIM
KV_DIM = N_KV_HEADS * HEAD_DIM
IN_DIM = Q_DIM + 2 * KV_DIM + FOURIER_DIM
RMS_EPS = 1e-6
NEG_INF = -1e30

LANES = 128
PROJ_CHUNK = Q_PER_KV * HEAD_DIM
V7X_VMEM_BYTES = 64 * 1024 * 1024


def _round_up(a, m):
    return (a + m - 1) // m * m


def _cparams(sem, vmem_bytes):
    return pltpu.CompilerParams(dimension_semantics=sem,
                                vmem_limit_bytes=int(min(vmem_bytes, V7X_VMEM_BYTES - (4 << 20))))


def _rms(x, g):
    return x * lax.rsqrt(jnp.mean(x * x, axis=-1, keepdims=True) + RMS_EPS) * g


def _ffn_body(*refs, tf, nj, final):
    if final:
        h_ref, gain_ref, wgu_ref, wd_ref, fgain_ref, o_ref, xn_ref = refs
    else:
        h_ref, gain_ref, wgu_ref, wd_ref, o_ref, xn_ref = refs
    j = pl.program_id(2)

    @pl.when(j == 0)
    def _():
        x = h_ref[0]
        xn_ref[...] = _rms(x, gain_ref[...]).astype(_BF16)
        o_ref[0] = x

    gu = jnp.dot(xn_ref[...], wgu_ref[0], preferred_element_type=_F32)
    g = gu[:, :tf]
    u = gu[:, tf:]
    act = ((g * jax.nn.sigmoid(g)) * u * 0.5).astype(_BF16)
    o_ref[0] += jnp.dot(act, wd_ref[...], preferred_element_type=_F32)

    if final:
        @pl.when(j == nj - 1)
        def _():
            o_ref[0] = _rms(o_ref[0], fgain_ref[...])


def _ffn(h, gain, wgu, wd, *, tm, tf, final_gain=None, out_rows=None):
    B, T, D = h.shape
    nj = wgu.shape[0]
    final = final_gain is not None
    rows = T if out_rows is None else out_rows
    ni = pl.cdiv(rows, tm)
    in_specs = [
        pl.BlockSpec((1, tm, D), lambda b, i, j: (b, i, 0)),
        pl.BlockSpec((1, D), lambda b, i, j: (0, 0)),
        pl.BlockSpec((1, D, 2 * tf), lambda b, i, j: (j, 0, 0)),
        pl.BlockSpec((tf, D), lambda b, i, j: (j, 0)),
    ]
    args = [h, gain, wgu, wd]
    if final:
        in_specs.append(pl.BlockSpec((1, D), lambda b, i, j: (0, 0)))
        args.append(final_gain)
    vmem = (4 * tm * D * 4 + tm * D * 2 + 2 * (D * 2 * tf + tf * D) * 2
            + tm * 2 * tf * 4 * 2 + tm * tf * 2 + (6 << 20))
    return pl.pallas_call(
        functools.partial(_ffn_body, tf=tf, nj=nj, final=final),
        grid=(B, ni, nj),
        in_specs=in_specs,
        out_specs=pl.BlockSpec((1, tm, D), lambda b, i, j: (b, i, 0)),
        out_shape=jax.ShapeDtypeStruct((B, rows, D), _F32),
        scratch_shapes=[pltpu.VMEM((tm, D), _BF16)],
        compiler_params=_cparams(("parallel", "parallel", "arbitrary"), vmem),
        name="ffn_final" if final else "ffn",
    )(*args)


def _proj_body(h_ref, gain_ref, w_ref, c_ref, s1_ref, s2_ref, dft_ref, qkv_ref, z_ref, xn_ref,
               *, n_rope, n_qkv):
    j = pl.program_id(2)

    @pl.when(j == 0)
    def _():
        xn_ref[...] = _rms(h_ref[0], gain_ref[...]).astype(_BF16)

    y = jnp.dot(xn_ref[...], w_ref[0], preferred_element_type=_F32)

    @pl.when(j < n_rope)
    def _():
        c = c_ref[...]
        s1 = s1_ref[...]
        s2 = s2_ref[...]
        half = ROT_DIM // 2
        for hh in range(PROJ_CHUNK // HEAD_DIM):
            xh = y[:, hh * HEAD_DIM:(hh + 1) * HEAD_DIM]
            up = pltpu.roll(xh, HEAD_DIM - half, 1)
            dn = pltpu.roll(xh, half, 1)
            r = xh * c + up * s1 + dn * s2
            qkv_ref[0, :, hh * HEAD_DIM:(hh + 1) * HEAD_DIM] = r.astype(_BF16)

    @pl.when(jnp.logical_and(j >= n_rope, j < n_qkv))
    def _():
        qkv_ref[0] = y.astype(_BF16)

    @pl.when(j >= n_qkv)
    def _():
        ub = y.astype(_BF16)
        for gg in range(PROJ_CHUNK // FOURIER_GROUP_DIM):
            sl = slice(gg * FOURIER_GROUP_DIM, (gg + 1) * FOURIER_GROUP_DIM)
            zz = jnp.dot(ub[:, sl], dft_ref[...], preferred_element_type=_F32)
            z_ref[0, 0, :, sl] = zz[:, :FOURIER_GROUP_DIM].astype(_BF16)
            z_ref[0, 1, :, sl] = zz[:, FOURIER_GROUP_DIM:].astype(_BF16)


def _proj(h, gain, w_in_c, rope_c, rope_s1, rope_s2, dft_c, *, tm):
    B, T, D = h.shape
    nj = w_in_c.shape[0]
    n_rope = (Q_DIM + KV_DIM) // PROJ_CHUNK
    n_qkv = (Q_DIM + 2 * KV_DIM) // PROJ_CHUNK
    ni = pl.cdiv(T, tm)
    vmem = (2 * tm * D * 4 + tm * D * 2 + 2 * D * PROJ_CHUNK * 2 + 6 * tm * LANES * 4
            + 2 * tm * PROJ_CHUNK * 2 + 4 * tm * PROJ_CHUNK * 2 + 6 * tm * PROJ_CHUNK * 4 + (6 << 20))
    return pl.pallas_call(
        functools.partial(_proj_body, n_rope=n_rope, n_qkv=n_qkv),
        grid=(B, ni, nj),
        in_specs=[
            pl.BlockSpec((1, tm, D), lambda b, i, j: (b, i, 0)),
            pl.BlockSpec((1, D), lambda b, i, j: (0, 0)),
            pl.BlockSpec((1, D, PROJ_CHUNK), lambda b, i, j: (j, 0, 0)),
            pl.BlockSpec((tm, LANES), lambda b, i, j: (i, 0)),
            pl.BlockSpec((tm, LANES), lambda b, i, j: (i, 0)),
            pl.BlockSpec((tm, LANES), lambda b, i, j: (i, 0)),
            pl.BlockSpec((FOURIER_GROUP_DIM, 2 * FOURIER_GROUP_DIM), lambda b, i, j: (0, 0)),
        ],
        out_specs=[
            pl.BlockSpec((1, tm, PROJ_CHUNK), lambda b, i, j: (b, i, jnp.minimum(j, n_qkv - 1))),
            pl.BlockSpec((1, 2, tm, PROJ_CHUNK), lambda b, i, j: (b, 0, i, jnp.maximum(j - n_qkv, 0))),
        ],
        out_shape=[
            jax.ShapeDtypeStruct((B, T, Q_DIM + 2 * KV_DIM), _BF16),
            jax.ShapeDtypeStruct((B, 2, T, FOURIER_DIM), _BF16),
        ],
        scratch_shapes=[pltpu.VMEM((tm, D), _BF16)],
        compiler_params=_cparams(("parallel", "parallel", "arbitrary"), vmem),
        name="mixer_proj",
    )(h, gain, w_in_c, rope_c, rope_s1, rope_s2, dft_c)


def _stack_heads(q):
    return jnp.concatenate([q[:, h * HEAD_DIM:(h + 1) * HEAD_DIM] for h in range(Q_PER_KV)], axis=0)


def _sink_column(sink_ref, g, rows_per_head):
    return jnp.concatenate(
        [jnp.full((rows_per_head, 1), sink_ref[g * Q_PER_KV + h], _F32) for h in range(Q_PER_KV)], axis=0)


def _qk(q4, k):
    return lax.dot_general(q4, k, (((1,), (1,)), ((), ())), preferred_element_type=_F32) * (HEAD_DIM ** -0.5)


def _softmax_pv(s_m, s_band, sink, v_band, v_meta):
    band_max = functools.reduce(jnp.maximum, s_band)
    mx = jnp.maximum(jnp.maximum(jnp.max(s_m, axis=-1, keepdims=True), jnp.max(band_max, axis=-1, keepdims=True)),
                     sink)
    e_m = jnp.exp(s_m - mx)
    e_band = [jnp.exp(sb - mx) for sb in s_band]
    den = (jnp.sum(e_m, axis=-1, keepdims=True) + jnp.sum(functools.reduce(jnp.add, e_band), axis=-1, keepdims=True)
           + jnp.exp(sink - mx))
    inv = 1.0 / den
    p_band = jnp.concatenate([(e * inv).astype(_BF16) for e in e_band], axis=1)
    return (jnp.dot(p_band, v_band, preferred_element_type=_F32)
            + jnp.dot((e_m * inv).astype(_BF16), v_meta, preferred_element_type=_F32))


def _attn_body(sink_ref, q_ref, kp_ref, kc_ref, kn_ref, km_ref, vp_ref, vc_ref, vn_ref, vm_ref, o_ref,
               *, qb, ntile):
    g = pl.program_id(1)
    jt = pl.program_id(2)
    rows = Q_PER_KV * BLOCK
    row = lax.broadcasted_iota(jnp.int32, (rows, BLOCK), 0) % BLOCK
    col = lax.broadcasted_iota(jnp.int32, (rows, BLOCK), 1)
    never = 2 * BLOCK
    sink = _sink_column(sink_ref, g, BLOCK)
    k_meta = km_ref[0]
    v_meta = vm_ref[0]

    def band(prev_ref, cur_ref, next_ref, qi):
        lo, hi = (qi - 1) * BLOCK, (qi + 2) * BLOCK
        if 0 <= lo and hi <= qb * BLOCK:
            return cur_ref[0, lo:hi]
        parts = [prev_ref[0] if qi == 0 else cur_ref[0, lo:lo + BLOCK],
                 cur_ref[0, qi * BLOCK:(qi + 1) * BLOCK],
                 next_ref[0] if qi == qb - 1 else cur_ref[0, hi - BLOCK:hi]]
        return jnp.concatenate(parts, axis=0)

    for qi in range(qb):
        q4 = _stack_heads(q_ref[0, qi * BLOCK:(qi + 1) * BLOCK, :])
        s = _qk(q4, band(kp_ref, kc_ref, kn_ref, qi))
        off_p = jnp.where(jt >= 1, 0, never) if qi == 0 else 0
        off_n = jnp.where(jt <= ntile - 2, 0, -never) if qi == qb - 1 else 0
        s_p = jnp.where(col >= row + off_p, s[:, :BLOCK], NEG_INF)
        s_c = s[:, BLOCK:2 * BLOCK]
        s_n = jnp.where(col <= row + off_n, s[:, 2 * BLOCK:], NEG_INF)
        out = _softmax_pv(_qk(q4, k_meta), [s_p, s_c, s_n], sink, band(vp_ref, vc_ref, vn_ref, qi), v_meta)
        for h in range(Q_PER_KV):
            o_ref[0, qi * BLOCK:(qi + 1) * BLOCK, h * HEAD_DIM:(h + 1) * HEAD_DIM] = (
                out[h * BLOCK:(h + 1) * BLOCK].astype(_BF16))


def _attn_meta_body(sink_ref, q_ref, k0_ref, km_ref, v0_ref, vm_ref, prev_out_ref, o_ref, *, n_meta):
    del prev_out_ref
    g = pl.program_id(1)
    rows = Q_PER_KV * n_meta
    row = lax.broadcasted_iota(jnp.int32, (rows, BLOCK), 0) % n_meta
    col = lax.broadcasted_iota(jnp.int32, (rows, BLOCK), 1)
    q4 = _stack_heads(q_ref[0])
    s_n = jnp.where(col <= row + (WINDOW - n_meta), _qk(q4, k0_ref[0]), NEG_INF)
    out = _softmax_pv(_qk(q4, km_ref[0]), [s_n], _sink_column(sink_ref, g, n_meta), v0_ref[0], vm_ref[0])
    for h in range(Q_PER_KV):
        o_ref[0, :, h * HEAD_DIM:(h + 1) * HEAD_DIM] = out[h * n_meta:(h + 1) * n_meta].astype(_BF16)


def _attention(qkv, sink, *, n_real, n_meta, qb):
    B, T, _ = qkv.shape
    nblk = n_real // BLOCK
    ntile = nblk // qb
    kcol = Q_DIM // HEAD_DIM
    vcol = (Q_DIM + KV_DIM) // HEAD_DIM
    mrow = n_real // n_meta

    def edge_spec(rowf, col0):
        return pl.BlockSpec((1, BLOCK, HEAD_DIM), lambda b, g, j, s: (b, rowf(j), col0 + g))

    def main_spec(col0):
        return pl.BlockSpec((1, qb * BLOCK, HEAD_DIM), lambda b, g, j, s: (b, j, col0 + g))

    def meta_spec(col0):
        return pl.BlockSpec((1, n_meta, HEAD_DIM), lambda b, g, j, s: (b, mrow, col0 + g))

    def prev_i(j):
        return jnp.maximum(qb * j - 1, 0)

    def next_i(j):
        return jnp.minimum(qb * j + qb, nblk - 1)

    attn = pl.pallas_call(
        functools.partial(_attn_body, qb=qb, ntile=ntile),
        grid_spec=pltpu.PrefetchScalarGridSpec(
            num_scalar_prefetch=1,
            grid=(B, N_KV_HEADS, ntile),
            in_specs=[
                pl.BlockSpec((1, qb * BLOCK, PROJ_CHUNK), lambda b, g, j, s: (b, j, g)),
                edge_spec(prev_i, kcol), main_spec(kcol), edge_spec(next_i, kcol), meta_spec(kcol),
                edge_spec(prev_i, vcol), main_spec(vcol), edge_spec(next_i, vcol), meta_spec(vcol),
            ],
            out_specs=pl.BlockSpec((1, qb * BLOCK, PROJ_CHUNK), lambda b, g, j, s: (b, j, g)),
        ),
        out_shape=jax.ShapeDtypeStruct((B, T, Q_DIM), _BF16),
        compiler_params=_cparams(("parallel", "parallel", "arbitrary"), 32 << 20),
        name="windowed_gqa",
    )(sink, qkv, qkv, qkv, qkv, qkv, qkv, qkv, qkv, qkv)

    def blk0_spec(col0):
        return pl.BlockSpec((1, BLOCK, HEAD_DIM), lambda b, g, s: (b, 0, col0 + g))

    def meta2_spec(col0):
        return pl.BlockSpec((1, n_meta, HEAD_DIM), lambda b, g, s: (b, mrow, col0 + g))

    return pl.pallas_call(
        functools.partial(_attn_meta_body, n_meta=n_meta),
        grid_spec=pltpu.PrefetchScalarGridSpec(
            num_scalar_prefetch=1,
            grid=(B, N_KV_HEADS),
            in_specs=[
                pl.BlockSpec((1, n_meta, PROJ_CHUNK), lambda b, g, s: (b, mrow, g)),
                blk0_spec(kcol), meta2_spec(kcol), blk0_spec(vcol), meta2_spec(vcol),
                pl.BlockSpec(memory_space=pl.ANY),
            ],
            out_specs=pl.BlockSpec((1, n_meta, PROJ_CHUNK), lambda b, g, s: (b, mrow, g)),
        ),
        out_shape=jax.ShapeDtypeStruct((B, T, Q_DIM), _BF16),
        input_output_aliases={6: 0},
        compiler_params=_cparams(("parallel", "arbitrary"), 32 << 20),
        name="windowed_gqa_meta",
    )(sink, qkv, qkv, qkv, qkv, qkv, attn)


def _fft_factors(T):
    best = None
    for n2 in range(16, T + 1, 16):
        if T % n2:
            continue
        n1 = T // n2
        cost = 4 * n1 + 2 * n2
        if best is None or cost < best[0]:
            best = (cost, n1, n2)
    assert best is not None, "sequence length must be a multiple of 16"
    return best[1], best[2]


def _fft1_body(z_ref, m_ref, a_ref, rhs_ref, *, n1, n1h, kh, nb, C):
    rhs_ref[...] = jnp.zeros_like(rhs_ref)
    for t in range(nb):
        cs = slice(t * C, (t + 1) * C)
        rhs_ref[0:n1, :] = z_ref[0, 0, :, cs].astype(_F32)
        rhs_ref[kh:kh + n1, :] = z_ref[0, 1, :, cs].astype(_F32)
        res = jnp.dot(m_ref[t], rhs_ref[...].astype(_BF16), preferred_element_type=_F32)
        a_ref[0, 0, :, cs] = res[:n1h].astype(_BF16)
        a_ref[0, 1, :, cs] = res[n1h:].astype(_BF16)


def _fft2_body(ar_ref, ai_ref, mc_ref, ms_ref, f_ref):
    res = (jnp.dot(mc_ref[0], ar_ref[0, 0, 0], preferred_element_type=_F32)
           + jnp.dot(ms_ref[0], ai_ref[0, 0, 0], preferred_element_type=_F32))
    f_ref[0] = res.astype(_BF16)


def _fft_tables(T, n1, n2, n_meta, n1h, kh):
    two_pi = 2.0 * np.pi
    k1 = jnp.arange(n1, dtype=jnp.int32)[None, :, None]
    nn1 = jnp.arange(n1, dtype=jnp.int32)[None, None, :]
    nn2 = jnp.arange(n2, dtype=jnp.int32)[:, None, None]
    ph = (k1 * (nn2 + n_meta + n2 * nn1)) % T
    ang = ph.astype(_F32) * (two_pi / T)
    c = jnp.cos(ang) * (n1 ** -0.5)
    s = jnp.sin(ang) * (n1 ** -0.5)

    def padded(a):
        return jnp.pad(a, ((0, 0), (0, n1h - n1), (0, kh - n1)))

    m1 = jnp.concatenate([jnp.concatenate([padded(c), padded(s)], axis=2),
                          jnp.concatenate([padded(-s), padded(c)], axis=2)], axis=1)
    k2 = jnp.arange(n2, dtype=jnp.int32)[:, None]
    mm2 = jnp.arange(n2, dtype=jnp.int32)[None, :]
    outs_c, outs_s = [], []
    for rot in (n_meta // n1, n_meta // n1 + 1):
        ph2 = (((k2 + rot) % n2) * (mm2 + n_meta)) % n2
        ang2 = ph2.astype(_F32) * (two_pi / n2)
        outs_c.append(jnp.cos(ang2) * (n2 ** -0.5))
        outs_s.append(jnp.sin(ang2) * (n2 ** -0.5))
    return m1.astype(_BF16), jnp.stack(outs_c).astype(_BF16), jnp.stack(outs_s).astype(_BF16)


def _position_dft(z, tables, *, n1, n2, n1h, kh, n_meta):
    B, _, T, C = z.shape
    m1, m2c, m2s = tables
    nb = 8 if n2 % 8 == 0 else 1
    zv = z.reshape(B, 2, n1, n2 * C)
    a = pl.pallas_call(
        functools.partial(_fft1_body, n1=n1, n1h=n1h, kh=kh, nb=nb, C=C),
        grid=(B, n2 // nb),
        in_specs=[
            pl.BlockSpec((1, 2, n1, nb * C), lambda b, t: (b, 0, 0, t)),
            pl.BlockSpec((nb, 2 * n1h, 2 * kh), lambda b, t: (t, 0, 0)),
        ],
        out_specs=pl.BlockSpec((1, 2, n1h, nb * C), lambda b, t: (b, 0, 0, t)),
        out_shape=jax.ShapeDtypeStruct((B, 2, n1h, n2 * C), _BF16),
        scratch_shapes=[pltpu.VMEM((2 * kh, C), _F32)],
        compiler_params=_cparams(("parallel", "arbitrary"), 32 << 20),
        name="dft_stage1",
    )(zv, m1)
    av = a.reshape(B, 2, n1h, n2, C)
    shift = n1 - (n_meta % n1)

    f = pl.pallas_call(
        _fft2_body,
        grid=(B, n1),
        in_specs=[
            pl.BlockSpec((1, 1, 1, n2, C), lambda b, k: (b, 0, k, 0, 0)),
            pl.BlockSpec((1, 1, 1, n2, C), lambda b, k: (b, 1, k, 0, 0)),
            pl.BlockSpec((1, n2, n2), lambda b, k: (jnp.where(k < n_meta % n1, 1, 0), 0, 0)),
            pl.BlockSpec((1, n2, n2), lambda b, k: (jnp.where(k < n_meta % n1, 1, 0), 0, 0)),
        ],
        out_specs=pl.BlockSpec((1, n2, C), lambda b, k: (b, 0, (k + shift) % n1)),
        out_shape=jax.ShapeDtypeStruct((B, n2, n1 * C), _BF16),
        compiler_params=_cparams(("parallel", "arbitrary"), 32 << 20),
        name="dft_stage2",
    )(av, av, m2c, m2s)
    return f.reshape(B, T, C)


def _mix_body(h_ref, gain_ref, a_ref, f_ref, wpa_ref, wpf_ref, wg_ref, wo_ref, o_ref, xn_ref, *, tn):
    j = pl.program_id(2)

    @pl.when(j == 0)
    def _():
        x = h_ref[0]
        xn_ref[...] = _rms(x, gain_ref[...]).astype(_BF16)
        o_ref[0] = x

    av = jnp.dot(a_ref[0], wpa_ref[0], preferred_element_type=_F32)
    fv = jnp.dot(f_ref[0], wpf_ref[0], preferred_element_type=_F32)
    gv = jax.nn.sigmoid(jnp.dot(xn_ref[...], wg_ref[0], preferred_element_type=_F32))
    m = (gv[:, :tn] * av + gv[:, tn:] * fv).astype(_BF16)
    o_ref[0] += jnp.dot(m, wo_ref[...], preferred_element_type=_F32)


def _mix_out(h, gain, attn, four, wpa_c, wpf_c, wg_c, wo, *, tm, tn):
    B, T, D = h.shape
    nj = wpa_c.shape[0]
    ni = pl.cdiv(T, tm)
    vmem = (4 * tm * D * 4 + tm * D * 2 + 2 * tm * (Q_DIM + FOURIER_DIM) * 2
            + 2 * (Q_DIM * tn + FOURIER_DIM * tn + D * 2 * tn + tn * D) * 2 + 8 * tm * tn * 4 + (6 << 20))
    return pl.pallas_call(
        functools.partial(_mix_body, tn=tn),
        grid=(B, ni, nj),
        in_specs=[
            pl.BlockSpec((1, tm, D), lambda b, i, j: (b, i, 0)),
            pl.BlockSpec((1, D), lambda b, i, j: (0, 0)),
            pl.BlockSpec((1, tm, Q_DIM), lambda b, i, j: (b, i, 0)),
            pl.BlockSpec((1, tm, FOURIER_DIM), lambda b, i, j: (b, i, 0)),
            pl.BlockSpec((1, Q_DIM, tn), lambda b, i, j: (j, 0, 0)),
            pl.BlockSpec((1, FOURIER_DIM, tn), lambda b, i, j: (j, 0, 0)),
            pl.BlockSpec((1, D, 2 * tn), lambda b, i, j: (j, 0, 0)),
            pl.BlockSpec((tn, D), lambda b, i, j: (j, 0)),
        ],
        out_specs=pl.BlockSpec((1, tm, D), lambda b, i, j: (b, i, 0)),
        out_shape=jax.ShapeDtypeStruct((B, T, D), _F32),
        scratch_shapes=[pltpu.VMEM((tm, D), _BF16)],
        compiler_params=_cparams(("parallel", "parallel", "arbitrary"), vmem),
        name="mixer_out",
    )(h, gain, attn, four, wpa_c, wpf_c, wg_c, wo)


def _col_chunks(w, tn):
    K, N = w.shape
    return w.reshape(K, N // tn, tn).transpose(1, 0, 2).astype(_BF16)


def _paired_col_chunks(wa, wb, tn):
    K, N = wa.shape
    a = wa.reshape(K, N // tn, tn)
    b = wb.reshape(K, N // tn, tn)
    return jnp.concatenate([a, b], axis=-1).transpose(1, 0, 2).astype(_BF16)


def _rope_tables(n_real, n_meta):
    T = n_real + n_meta
    half = ROT_DIM // 2
    inv = 1.0 / (ROPE_THETA ** (jnp.arange(0, ROT_DIM, 2, dtype=_F32) / ROT_DIM))
    ang = jnp.arange(T, dtype=_F32)[:, None] * inv[None, :]
    cos, sin = jnp.cos(ang), jnp.sin(ang)
    cos = jnp.concatenate([cos[n_meta:], cos[:n_meta]], axis=0)
    sin = jnp.concatenate([sin[n_meta:], sin[:n_meta]], axis=0)
    ones = jnp.ones((T, HEAD_DIM - ROT_DIM), _F32)
    zeros_h = jnp.zeros((T, half), _F32)
    zeros_r = jnp.zeros((T, HEAD_DIM - ROT_DIM), _F32)
    c = jnp.concatenate([cos, cos, ones], axis=1)
    s1 = jnp.concatenate([-sin, zeros_h, zeros_r], axis=1)
    s2 = jnp.concatenate([zeros_h, sin, zeros_r], axis=1)
    return c, s1, s2


def _channel_dft_table():
    n = FOURIER_GROUP_DIM
    c = jnp.arange(n, dtype=jnp.int32)[:, None]
    k = jnp.arange(n, dtype=jnp.int32)[None, :]
    ang = ((c * k) % n).astype(_F32) * (2.0 * np.pi / n)
    return (jnp.concatenate([jnp.cos(ang), -jnp.sin(ang)], axis=1) * (n ** -0.5)).astype(_BF16)


def _pick_tile(n, pref):
    t = pref
    while n % t:
        t //= 2
    return t


def kernel(x, meta_tokens, ffn1_norm, ffn1_w_gate, ffn1_w_up, ffn1_w_down, mix_norm, w_in, w_gate, sink,
           w_attn_branch, w_fourier_branch, w_out, ffn2_norm, ffn2_w_gate, ffn2_w_up, ffn2_w_down, final_norm):
    B, S, D = x.shape
    n_meta = meta_tokens.shape[0]
    depth = ffn1_norm.shape[0]
    d_ff = ffn1_w_gate.shape[-1]
    T = S + n_meta
    assert S % BLOCK == 0 and S % n_meta == 0 and BLOCK % n_meta == 0
    assert w_in.shape[-1] == IN_DIM and KV_DIM == PROJ_CHUNK

    tm = _pick_tile(S, 512)
    tf = _pick_tile(d_ff, 512)
    tn = _pick_tile(D, 512)
    qb = _pick_tile(S // BLOCK, 4)

    n1, n2 = _fft_factors(T)
    n1h = _round_up(n1, 16)
    kh = _round_up(n1h, LANES // 2)
    fft_tables = _fft_tables(T, n1, n2, n_meta, n1h, kh)
    rope_c, rope_s1, rope_s2 = _rope_tables(S, n_meta)
    dft_c = _channel_dft_table()

    meta = jnp.broadcast_to(meta_tokens[None].astype(x.dtype), (B, n_meta, D))
    h = jnp.concatenate([x, meta], axis=1)

    for l in range(depth):
        h = _ffn(h, ffn1_norm[l][None], _paired_col_chunks(ffn1_w_gate[l], ffn1_w_up[l], tf),
                 ffn1_w_down[l].astype(_BF16), tm=tm, tf=tf)

        gain = mix_norm[l][None]
        qkv, z = _proj(h, gain, _col_chunks(w_in[l], PROJ_CHUNK), rope_c, rope_s1, rope_s2, dft_c, tm=tm)
        attn = _attention(qkv, sink[l], n_real=S, n_meta=n_meta, qb=qb)
        four = _position_dft(z, fft_tables, n1=n1, n2=n2, n1h=n1h, kh=kh, n_meta=n_meta)
        wg = w_gate[l]
        h = _mix_out(h, gain, attn, four, _col_chunks(w_attn_branch[l], tn), _col_chunks(w_fourier_branch[l], tn),
                     _paired_col_chunks(wg[:, :D], wg[:, D:], tn), w_out[l].astype(_BF16), tm=tm, tn=tn)

        last = l == depth - 1
        h = _ffn(h, ffn2_norm[l][None], _paired_col_chunks(ffn2_w_gate[l], ffn2_w_up[l], tf),
                 ffn2_w_down[l].astype(_BF16), tm=tm, tf=tf,
                 final_gain=final_norm[None] if last else None, out_rows=S if last else None)
    return h
```

```python
import functools

import numpy as np
import jax
import jax.numpy as jnp
from jax import lax
from jax.experimental import pallas as pl
from jax.experimental.pallas import tpu as pltpu

_F32 = jnp.float32
_BF16 = jnp.bfloat16

HEAD_DIM = 128
N_Q_HEADS = 16
N_KV_HEADS = 4
Q_PER_KV = N_Q_HEADS // N_KV_HEADS
WINDOW = 128
BLOCK = 128
ROPE_THETA = 500000.0
ROT_DIM = HEAD_DIM // 4
N_FOURIER_GROUPS = 8
FOURIER_GROUP_DIM = 128
FOURIER_DIM = N_FOURIER_GROUPS * FOURIER_GROUP_DIM
Q_DIM = N_Q_HEADS * HEAD_DIM
KV_DIM = N_KV_HEADS * HEAD_DIM
IN_DIM = Q_DIM + 2 * KV_DIM + FOURIER_DIM
RMS_EPS = 1e-6
NEG_INF = -1e30

LANES = 128
BF16_SUBLANES = 16
PROJ_CHUNK = Q_PER_KV * HEAD_DIM
V7X_VMEM_BYTES = 64 * 1024 * 1024


def _round_up(a, m):
    return (a + m - 1) // m * m


def _cparams(sem, vmem_bytes):
    return pltpu.CompilerParams(dimension_semantics=sem,
                                vmem_limit_bytes=int(min(vmem_bytes, V7X_VMEM_BYTES - (4 << 20))))


def _rms(x, g):
    return x * lax.rsqrt(jnp.mean(x * x, axis=-1, keepdims=True) + RMS_EPS) * g


def _ffn_body(*refs, tf, nj, final):
    if final:
        h_ref, gain_ref, wgu_ref, wd_ref, fgain_ref, o_ref, xn_ref = refs
    else:
        h_ref, gain_ref, wgu_ref, wd_ref, o_ref, xn_ref = refs
    j = pl.program_id(2)

    @pl.when(j == 0)
    def _():
        x = h_ref[0]
        xn_ref[...] = _rms(x, gain_ref[...]).astype(_BF16)
        o_ref[0] = x

    gu = jnp.dot(xn_ref[...], wgu_ref[0], preferred_element_type=_F32)
    g = gu[:, :tf]
    u = gu[:, tf:]
    act = ((g * jax.nn.sigmoid(g)) * u * 0.5).astype(_BF16)
    o_ref[0] += jnp.dot(act, wd_ref[...], preferred_element_type=_F32)

    if final:
        @pl.when(j == nj - 1)
        def _():
            o_ref[0] = _rms(o_ref[0], fgain_ref[...])


def _ffn(h, gain, wgu, wd, *, tm, tf, final_gain=None, out_rows=None):
    B, T, D = h.shape
    nj = wgu.shape[0]
    final = final_gain is not None
    rows = T if out_rows is None else out_rows
    ni = pl.cdiv(rows, tm)
    in_specs = [
        pl.BlockSpec((1, tm, D), lambda b, i, j: (b, i, 0)),
        pl.BlockSpec((1, D), lambda b, i, j: (0, 0)),
        pl.BlockSpec((1, D, 2 * tf), lambda b, i, j: (j, 0, 0)),
        pl.BlockSpec((tf, D), lambda b, i, j: (j, 0)),
    ]
    args = [h, gain, wgu, wd]
    if final:
        in_specs.append(pl.BlockSpec((1, D), lambda b, i, j: (0, 0)))
        args.append(final_gain)
    vmem = (4 * tm * D * 4 + tm * D * 2 + 2 * (D * 2 * tf + tf * D) * 2
            + tm * 2 * tf * 4 * 2 + tm * tf * 2 + (6 << 20))
    return pl.pallas_call(
        functools.partial(_ffn_body, tf=tf, nj=nj, final=final),
        grid=(B, ni, nj),
        in_specs=in_specs,
        out_specs=pl.BlockSpec((1, tm, D), lambda b, i, j: (b, i, 0)),
        out_shape=jax.ShapeDtypeStruct((B, rows, D), _F32),
        scratch_shapes=[pltpu.VMEM((tm, D), _BF16)],
        compiler_params=_cparams(("parallel", "parallel", "arbitrary"), vmem),
        name="ffn_final" if final else "ffn",
    )(*args)


def _proj_body(h_ref, gain_ref, w_ref, c_ref, s1_ref, s2_ref, dft_ref, qkv_ref, z_ref, xn_ref,
               *, n_rope, n_qkv):
    j = pl.program_id(2)

    @pl.when(j == 0)
    def _():
        xn_ref[...] = _rms(h_ref[0], gain_ref[...]).astype(_BF16)

    y = jnp.dot(xn_ref[...], w_ref[0], preferred_element_type=_F32)

    @pl.when(j < n_rope)
    def _():
        c = c_ref[...]
        s1 = s1_ref[...]
        s2 = s2_ref[...]
        half = ROT_DIM // 2
        for hh in range(PROJ_CHUNK // HEAD_DIM):
            xh = y[:, hh * HEAD_DIM:(hh + 1) * HEAD_DIM]
            up = pltpu.roll(xh, HEAD_DIM - half, 1)
            dn = pltpu.roll(xh, half, 1)
            r = xh * c + up * s1 + dn * s2
            qkv_ref[0, :, hh * HEAD_DIM:(hh + 1) * HEAD_DIM] = r.astype(_BF16)

    @pl.when(jnp.logical_and(j >= n_rope, j < n_qkv))
    def _():
        qkv_ref[0] = y.astype(_BF16)

    @pl.when(j >= n_qkv)
    def _():
        ub = y.astype(_BF16)
        for gg in range(PROJ_CHUNK // FOURIER_GROUP_DIM):
            sl = slice(gg * FOURIER_GROUP_DIM, (gg + 1) * FOURIER_GROUP_DIM)
            zz = jnp.dot(ub[:, sl], dft_ref[...], preferred_element_type=_F32)
            z_ref[0, 0, :, sl] = zz[:, :FOURIER_GROUP_DIM].astype(_BF16)
            z_ref[0, 1, :, sl] = zz[:, FOURIER_GROUP_DIM:].astype(_BF16)


def _proj(h, gain, w_in_c, rope_c, rope_s1, rope_s2, dft_c, *, tm):
    B, T, D = h.shape
    nj = w_in_c.shape[0]
    n_rope = (Q_DIM + KV_DIM) // PROJ_CHUNK
    n_qkv = (Q_DIM + 2 * KV_DIM) // PROJ_CHUNK
    ni = pl.cdiv(T, tm)
    vmem = (2 * tm * D * 4 + tm * D * 2 + 2 * D * PROJ_CHUNK * 2 + 6 * tm * LANES * 4
            + 2 * tm * PROJ_CHUNK * 2 + 4 * tm * PROJ_CHUNK * 2 + 6 * tm * PROJ_CHUNK * 4 + (6 << 20))
    return pl.pallas_call(
        functools.partial(_proj_body, n_rope=n_rope, n_qkv=n_qkv),
        grid=(B, ni, nj),
        in_specs=[
            pl.BlockSpec((1, tm, D), lambda b, i, j: (b, i, 0)),
            pl.BlockSpec((1, D), lambda b, i, j: (0, 0)),
            pl.BlockSpec((1, D, PROJ_CHUNK), lambda b, i, j: (j, 0, 0)),
            pl.BlockSpec((tm, LANES), lambda b, i, j: (i, 0)),
            pl.BlockSpec((tm, LANES), lambda b, i, j: (i, 0)),
            pl.BlockSpec((tm, LANES), lambda b, i, j: (i, 0)),
            pl.BlockSpec((FOURIER_GROUP_DIM, 2 * FOURIER_GROUP_DIM), lambda b, i, j: (0, 0)),
        ],
        out_specs=[
            pl.BlockSpec((1, tm, PROJ_CHUNK), lambda b, i, j: (b, i, jnp.minimum(j, n_qkv - 1))),
            pl.BlockSpec((1, 2, tm, PROJ_CHUNK), lambda b, i, j: (b, 0, i, jnp.maximum(j - n_qkv, 0))),
        ],
        out_shape=[
            jax.ShapeDtypeStruct((B, T, Q_DIM + 2 * KV_DIM), _BF16),
            jax.ShapeDtypeStruct((B, 2, T, FOURIER_DIM), _BF16),
        ],
        scratch_shapes=[pltpu.VMEM((tm, D), _BF16)],
        compiler_params=_cparams(("parallel", "parallel", "arbitrary"), vmem),
        name="mixer_proj",
    )(h, gain, w_in_c, rope_c, rope_s1, rope_s2, dft_c)


def _stack_heads(q):
    return jnp.concatenate([q[:, h * HEAD_DIM:(h + 1) * HEAD_DIM] for h in range(Q_PER_KV)], axis=0)


def _sink_column(sink_ref, g, rows_per_head):
    return jnp.concatenate(
        [jnp.full((rows_per_head, 1), sink_ref[g * Q_PER_KV + h], _F32) for h in range(Q_PER_KV)], axis=0)


def _qk(q4, k):
    return lax.dot_general(q4, k, (((1,), (1,)), ((), ())), preferred_element_type=_F32) * (HEAD_DIM ** -0.5)


def _softmax_pv(s_m, s_band, sink, v_band, v_meta):
    band_max = functools.reduce(jnp.maximum, s_band)
    mx = jnp.maximum(jnp.maximum(jnp.max(s_m, axis=-1, keepdims=True), jnp.max(band_max, axis=-1, keepdims=True)),
                     sink)
    e_m = jnp.exp(s_m - mx)
    e_band = [jnp.exp(sb - mx) for sb in s_band]
    den = (jnp.sum(e_m, axis=-1, keepdims=True) + jnp.sum(functools.reduce(jnp.add, e_band), axis=-1, keepdims=True)
           + jnp.exp(sink - mx))
    inv = 1.0 / den
    p_band = jnp.concatenate([(e * inv).astype(_BF16) for e in e_band], axis=1)
    return (jnp.dot(p_band, v_band, preferred_element_type=_F32)
            + jnp.dot((e_m * inv).astype(_BF16), v_meta, preferred_element_type=_F32))


def _attn_body(sink_ref, q_ref, kp_ref, kc_ref, kn_ref, km_ref, vp_ref, vc_ref, vn_ref, vm_ref, o_ref,
               *, qb, ntile):
    g = pl.program_id(1)
    jt = pl.program_id(2)
    rows = Q_PER_KV * BLOCK
    row = lax.broadcasted_iota(jnp.int32, (rows, BLOCK), 0) % BLOCK
    col = lax.broadcasted_iota(jnp.int32, (rows, BLOCK), 1)
    never = 2 * BLOCK
    sink = _sink_column(sink_ref, g, BLOCK)
    k_meta = km_ref[0]
    v_meta = vm_ref[0]

    def band(prev_ref, cur_ref, next_ref, qi):
        lo, hi = (qi - 1) * BLOCK, (qi + 2) * BLOCK
        if 0 <= lo and hi <= qb * BLOCK:
            return cur_ref[0, lo:hi]
        parts = [prev_ref[0] if qi == 0 else cur_ref[0, lo:lo + BLOCK],
                 cur_ref[0, qi * BLOCK:(qi + 1) * BLOCK],
                 next_ref[0] if qi == qb - 1 else cur_ref[0, hi - BLOCK:hi]]
        return jnp.concatenate(parts, axis=0)

    for qi in range(qb):
        q4 = _stack_heads(q_ref[0, qi * BLOCK:(qi + 1) * BLOCK, :])
        s = _qk(q4, band(kp_ref, kc_ref, kn_ref, qi))
        off_p = jnp.where(jt >= 1, 0, never) if qi == 0 else 0
        off_n = jnp.where(jt <= ntile - 2, 0, -never) if qi == qb - 1 else 0
        s_p = jnp.where(col >= row + off_p, s[:, :BLOCK], NEG_INF)
        s_c = s[:, BLOCK:2 * BLOCK]
        s_n = jnp.where(col <= row + off_n, s[:, 2 * BLOCK:], NEG_INF)
        out = _softmax_pv(_qk(q4, k_meta), [s_p, s_c, s_n], sink, band(vp_ref, vc_ref, vn_ref, qi), v_meta)
        for h in range(Q_PER_KV):
            o_ref[0, qi * BLOCK:(qi + 1) * BLOCK, h * HEAD_DIM:(h + 1) * HEAD_DIM] = (
                out[h * BLOCK:(h + 1) * BLOCK].astype(_BF16))


def _attn_meta_body(sink_ref, q_ref, k0_ref, km_ref, v0_ref, vm_ref, prev_out_ref, o_ref, *, n_meta):
    del prev_out_ref
    g = pl.program_id(1)
    rows = Q_PER_KV * n_meta
    row = lax.broadcasted_iota(jnp.int32, (rows, BLOCK), 0) % n_meta
    col = lax.broadcasted_iota(jnp.int32, (rows, BLOCK), 1)
    q4 = _stack_heads(q_ref[0])
    s_n = jnp.where(col <= row + (WINDOW - n_meta), _qk(q4, k0_ref[0]), NEG_INF)
    out = _softmax_pv(_qk(q4, km_ref[0]), [s_n], _sink_column(sink_ref, g, n_meta), v0_ref[0], vm_ref[0])
    for h in range(Q_PER_KV):
        o_ref[0, :, h * HEAD_DIM:(h + 1) * HEAD_DIM] = out[h * n_meta:(h + 1) * n_meta].astype(_BF16)


def _attention(qkv, sink, *, n_real, n_meta, qb):
    B, T, _ = qkv.shape
    nblk = n_real // BLOCK
    ntile = nblk // qb
    kcol = Q_DIM // HEAD_DIM
    vcol = (Q_DIM + KV_DIM) // HEAD_DIM
    mrow = n_real // n_meta

    def edge_spec(rowf, col0):
        return pl.BlockSpec((1, BLOCK, HEAD_DIM), lambda b, g, j, s: (b, rowf(j), col0 + g))

    def main_spec(col0):
        return pl.BlockSpec((1, qb * BLOCK, HEAD_DIM), lambda b, g, j, s: (b, j, col0 + g))

    def meta_spec(col0):
        return pl.BlockSpec((1, n_meta, HEAD_DIM), lambda b, g, j, s: (b, mrow, col0 + g))

    def prev_i(j):
        return jnp.maximum(qb * j - 1, 0)

    def next_i(j):
        return jnp.minimum(qb * j + qb, nblk - 1)

    attn = pl.pallas_call(
        functools.partial(_attn_body, qb=qb, ntile=ntile),
        grid_spec=pltpu.PrefetchScalarGridSpec(
            num_scalar_prefetch=1,
            grid=(B, N_KV_HEADS, ntile),
            in_specs=[
                pl.BlockSpec((1, qb * BLOCK, PROJ_CHUNK), lambda b, g, j, s: (b, j, g)),
                edge_spec(prev_i, kcol), main_spec(kcol), edge_spec(next_i, kcol), meta_spec(kcol),
                edge_spec(prev_i, vcol), main_spec(vcol), edge_spec(next_i, vcol), meta_spec(vcol),
            ],
            out_specs=pl.BlockSpec((1, qb * BLOCK, PROJ_CHUNK), lambda b, g, j, s: (b, j, g)),
        ),
        out_shape=jax.ShapeDtypeStruct((B, T, Q_DIM), _BF16),
        compiler_params=_cparams(("parallel", "parallel", "arbitrary"), 32 << 20),
        name="windowed_gqa",
    )(sink, qkv, qkv, qkv, qkv, qkv, qkv, qkv, qkv, qkv)

    def blk0_spec(col0):
        return pl.BlockSpec((1, BLOCK, HEAD_DIM), lambda b, g, s: (b, 0, col0 + g))

    def meta2_spec(col0):
        return pl.BlockSpec((1, n_meta, HEAD_DIM), lambda b, g, s: (b, mrow, col0 + g))

    return pl.pallas_call(
        functools.partial(_attn_meta_body, n_meta=n_meta),
        grid_spec=pltpu.PrefetchScalarGridSpec(
            num_scalar_prefetch=1,
            grid=(B, N_KV_HEADS),
            in_specs=[
                pl.BlockSpec((1, n_meta, PROJ_CHUNK), lambda b, g, s: (b, mrow, g)),
                blk0_spec(kcol), meta2_spec(kcol), blk0_spec(vcol), meta2_spec(vcol),
                pl.BlockSpec(memory_space=pl.ANY),
            ],
            out_specs=pl.BlockSpec((1, n_meta, PROJ_CHUNK), lambda b, g, s: (b, mrow, g)),
        ),
        out_shape=jax.ShapeDtypeStruct((B, T, Q_DIM), _BF16),
        input_output_aliases={6: 0},
        compiler_params=_cparams(("parallel", "arbitrary"), 32 << 20),
        name="windowed_gqa_meta",
    )(sink, qkv, qkv, qkv, qkv, qkv, attn)


def _fft_factors(T):
    best = None
    for n2 in range(16, T + 1, 16):
        if T % n2:
            continue
        n1 = T // n2
        cost = 4 * n1 + 2 * n2
        if best is None or cost < best[0]:
            best = (cost, n1, n2)
    assert best is not None, "sequence length must be a multiple of 16"
    return best[1], best[2]


def _fft1_body(z_ref, m_ref, a_ref, *, n1, R, C):
    rows = n1 * R
    rhs = jnp.concatenate([z_ref[0, 0].reshape(rows, C), z_ref[0, 1].reshape(rows, C)], axis=0)
    res = jnp.dot(m_ref[0], rhs, preferred_element_type=_F32)
    a_ref[0, 0] = res[:rows].reshape(n1, R, C).astype(_BF16)
    a_ref[0, 1] = res[rows:].reshape(n1, R, C).astype(_BF16)


def _fft2_body(ar_ref, ai_ref, mc_ref, ms_ref, f_ref):
    res = (jnp.dot(mc_ref[0], ar_ref[0, 0, 0], preferred_element_type=_F32)
           + jnp.dot(ms_ref[0], ai_ref[0, 0, 0], preferred_element_type=_F32))
    f_ref[0] = res.astype(_BF16)


def _fft_tables(T, n1, n2, n_meta, R):
    two_pi = 2.0 * np.pi
    k1 = jnp.arange(n1, dtype=jnp.int32)[None, :, None]
    nn1 = jnp.arange(n1, dtype=jnp.int32)[None, None, :]
    nn2 = jnp.arange(n2, dtype=jnp.int32)[:, None, None]
    ph = (k1 * (nn2 + n_meta + n2 * nn1)) % T
    ang = ph.astype(_F32) * (two_pi / T)
    c = jnp.cos(ang) * (n1 ** -0.5)
    s = jnp.sin(ang) * (n1 ** -0.5)
    eye = jnp.eye(R, dtype=_F32)

    def blockdiag(a):
        a4 = a.reshape(n2 // R, R, n1, n1)
        return jnp.einsum("jrkn,rs->jkrns", a4, eye).reshape(n2 // R, n1 * R, n1 * R).astype(_BF16)

    bc, bs = blockdiag(c), blockdiag(s)
    m1 = jnp.concatenate([jnp.concatenate([bc, bs], axis=2), jnp.concatenate([-bs, bc], axis=2)], axis=1)
    k2 = jnp.arange(n2, dtype=jnp.int32)[:, None]
    mm2 = jnp.arange(n2, dtype=jnp.int32)[None, :]
    outs_c, outs_s = [], []
    for rot in (n_meta // n1, n_meta // n1 + 1):
        ph2 = (((k2 + rot) % n2) * (mm2 + n_meta)) % n2
        ang2 = ph2.astype(_F32) * (two_pi / n2)
        outs_c.append(jnp.cos(ang2) * (n2 ** -0.5))
        outs_s.append(jnp.sin(ang2) * (n2 ** -0.5))
    return m1, jnp.stack(outs_c).astype(_BF16), jnp.stack(outs_s).astype(_BF16)


def _position_dft(z, tables, *, n1, n2, R, n_meta):
    B, _, T, C = z.shape
    m1, m2c, m2s = tables
    rows = n1 * R
    zv = z.reshape(B, 2, n1, n2, C)
    av = pl.pallas_call(
        functools.partial(_fft1_body, n1=n1, R=R, C=C),
        grid=(B, n2 // R),
        in_specs=[
            pl.BlockSpec((1, 2, n1, R, C), lambda b, t: (b, 0, 0, t, 0)),
            pl.BlockSpec((1, 2 * rows, 2 * rows), lambda b, t: (t, 0, 0)),
        ],
        out_specs=pl.BlockSpec((1, 2, n1, R, C), lambda b, t: (b, 0, 0, t, 0)),
        out_shape=jax.ShapeDtypeStruct((B, 2, n1, n2, C), _BF16),
        compiler_params=_cparams(("parallel", "arbitrary"), 48 << 20),
        name="dft_stage1",
    )(zv, m1)
    shift = n1 - (n_meta % n1)

    f = pl.pallas_call(
        _fft2_body,
        grid=(B, n1),
        in_specs=[
            pl.BlockSpec((1, 1, 1, n2, C), lambda b, k: (b, 0, k, 0, 0)),
            pl.BlockSpec((1, 1, 1, n2, C), lambda b, k: (b, 1, k, 0, 0)),
            pl.BlockSpec((1, n2, n2), lambda b, k: (jnp.where(k < n_meta % n1, 1, 0), 0, 0)),
            pl.BlockSpec((1, n2, n2), lambda b, k: (jnp.where(k < n_meta % n1, 1, 0), 0, 0)),
        ],
        out_specs=pl.BlockSpec((1, n2, C), lambda b, k: (b, 0, (k + shift) % n1)),
        out_shape=jax.ShapeDtypeStruct((B, n2, n1 * C), _BF16),
        compiler_params=_cparams(("parallel", "arbitrary"), 32 << 20),
        name="dft_stage2",
    )(av, av, m2c, m2s)
    return f.reshape(B, T, C)


def _mix_body(h_ref, gain_ref, a_ref, f_ref, wpa_ref, wpf_ref, wg_ref, wo_ref, o_ref, xn_ref, *, tn):
    j = pl.program_id(2)

    @pl.when(j == 0)
    def _():
        x = h_ref[0]
        xn_ref[...] = _rms(x, gain_ref[...]).astype(_BF16)
        o_ref[0] = x

    av = jnp.dot(a_ref[0], wpa_ref[0], preferred_element_type=_F32)
    fv = jnp.dot(f_ref[0], wpf_ref[0], preferred_element_type=_F32)
    gv = jax.nn.sigmoid(jnp.dot(xn_ref[...], wg_ref[0], preferred_element_type=_F32))
    m = (gv[:, :tn] * av + gv[:, tn:] * fv).astype(_BF16)
    o_ref[0] += jnp.dot(m, wo_ref[...], preferred_element_type=_F32)


def _mix_out(h, gain, attn, four, wpa_c, wpf_c, wg_c, wo, *, tm, tn):
    B, T, D = h.shape
    nj = wpa_c.shape[0]
    ni = pl.cdiv(T, tm)
    vmem = (4 * tm * D * 4 + tm * D * 2 + 2 * tm * (Q_DIM + FOURIER_DIM) * 2
            + 2 * (Q_DIM * tn + FOURIER_DIM * tn + D * 2 * tn + tn * D) * 2 + 8 * tm * tn * 4 + (6 << 20))
    return pl.pallas_call(
        functools.partial(_mix_body, tn=tn),
        grid=(B, ni, nj),
        in_specs=[
            pl.BlockSpec((1, tm, D), lambda b, i, j: (b, i, 0)),
            pl.BlockSpec((1, D), lambda b, i, j: (0, 0)),
            pl.BlockSpec((1, tm, Q_DIM), lambda b, i, j: (b, i, 0)),
            pl.BlockSpec((1, tm, FOURIER_DIM), lambda b, i, j: (b, i, 0)),
            pl.BlockSpec((1, Q_DIM, tn), lambda b, i, j: (j, 0, 0)),
            pl.BlockSpec((1, FOURIER_DIM, tn), lambda b, i, j: (j, 0, 0)),
            pl.BlockSpec((1, D, 2 * tn), lambda b, i, j: (j, 0, 0)),
            pl.BlockSpec((tn, D), lambda b, i, j: (j, 0)),
        ],
        out_specs=pl.BlockSpec((1, tm, D), lambda b, i, j: (b, i, 0)),
        out_shape=jax.ShapeDtypeStruct((B, T, D), _F32),
        scratch_shapes=[pltpu.VMEM((tm, D), _BF16)],
        compiler_params=_cparams(("parallel", "parallel", "arbitrary"), vmem),
        name="mixer_out",
    )(h, gain, attn, four, wpa_c, wpf_c, wg_c, wo)


def _col_chunks(w, tn):
    K, N = w.shape
    return w.reshape(K, N // tn, tn).transpose(1, 0, 2).astype(_BF16)


def _paired_col_chunks(wa, wb, tn):
    K, N = wa.shape
    a = wa.reshape(K, N // tn, tn)
    b = wb.reshape(K, N // tn, tn)
    return jnp.concatenate([a, b], axis=-1).transpose(1, 0, 2).astype(_BF16)


def _rope_tables(n_real, n_meta):
    T = n_real + n_meta
    half = ROT_DIM // 2
    inv = 1.0 / (ROPE_THETA ** (jnp.arange(0, ROT_DIM, 2, dtype=_F32) / ROT_DIM))
    ang = jnp.arange(T, dtype=_F32)[:, None] * inv[None, :]
    cos, sin = jnp.cos(ang), jnp.sin(ang)
    cos = jnp.concatenate([cos[n_meta:], cos[:n_meta]], axis=0)
    sin = jnp.concatenate([sin[n_meta:], sin[:n_meta]], axis=0)
    ones = jnp.ones((T, HEAD_DIM - ROT_DIM), _F32)
    zeros_h = jnp.zeros((T, half), _F32)
    zeros_r = jnp.zeros((T, HEAD_DIM - ROT_DIM), _F32)
    c = jnp.concatenate([cos, cos, ones], axis=1)
    s1 = jnp.concatenate([-sin, zeros_h, zeros_r], axis=1)
    s2 = jnp.concatenate([zeros_h, sin, zeros_r], axis=1)
    return c, s1, s2


def _channel_dft_table():
    n = FOURIER_GROUP_DIM
    c = jnp.arange(n, dtype=jnp.int32)[:, None]
    k = jnp.arange(n, dtype=jnp.int32)[None, :]
    ang = ((c * k) % n).astype(_F32) * (2.0 * np.pi / n)
    return (jnp.concatenate([jnp.cos(ang), -jnp.sin(ang)], axis=1) * (n ** -0.5)).astype(_BF16)


def _pick_tile(n, pref):
    t = pref
    while n % t:
        t //= 2
    return t


def kernel(x, meta_tokens, ffn1_norm, ffn1_w_gate, ffn1_w_up, ffn1_w_down, mix_norm, w_in, w_gate, sink,
           w_attn_branch, w_fourier_branch, w_out, ffn2_norm, ffn2_w_gate, ffn2_w_up, ffn2_w_down, final_norm):
    B, S, D = x.shape
    n_meta = meta_tokens.shape[0]
    depth = ffn1_norm.shape[0]
    d_ff = ffn1_w_gate.shape[-1]
    T = S + n_meta
    assert S % BLOCK == 0 and S % n_meta == 0 and BLOCK % n_meta == 0
    assert w_in.shape[-1] == IN_DIM and KV_DIM == PROJ_CHUNK

    tm = _pick_tile(S, 512)
    tf = _pick_tile(d_ff, 512)
    tn = _pick_tile(D, 512)
    qb = _pick_tile(S // BLOCK, 4)

    n1, n2 = _fft_factors(T)
    fft_tables = _fft_tables(T, n1, n2, n_meta, BF16_SUBLANES)
    rope_c, rope_s1, rope_s2 = _rope_tables(S, n_meta)
    dft_c = _channel_dft_table()

    meta = jnp.broadcast_to(meta_tokens[None].astype(x.dtype), (B, n_meta, D))
    h = jnp.concatenate([x, meta], axis=1)

    for l in range(depth):
        h = _ffn(h, ffn1_norm[l][None], _paired_col_chunks(ffn1_w_gate[l], ffn1_w_up[l], tf),
                 ffn1_w_down[l].astype(_BF16), tm=tm, tf=tf)

        gain = mix_norm[l][None]
        qkv, z = _proj(h, gain, _col_chunks(w_in[l], PROJ_CHUNK), rope_c, rope_s1, rope_s2, dft_c, tm=tm)
        attn = _attention(qkv, sink[l], n_real=S, n_meta=n_meta, qb=qb)
        four = _position_dft(z, fft_tables, n1=n1, n2=n2, R=BF16_SUBLANES, n_meta=n_meta)
        wg = w_gate[l]
        h = _mix_out(h, gain, attn, four, _col_chunks(w_attn_branch[l], tn), _col_chunks(w_fourier_branch[l], tn),
                     _paired_col_chunks(wg[:, :D], wg[:, D:], tn), w_out[l].astype(_BF16), tm=tm, tn=tn)

        last = l == depth - 1
        h = _ffn(h, ffn2_norm[l][None], _paired_col_chunks(ffn2_w_gate[l], ffn2_w_up[l], tf),
                 ffn2_w_down[l].astype(_BF16), tm=tm, tf=tf,
                 final_gain=final_norm[None] if last else None, out_rows=S if last else None)
    return h
```

```python
import functools

import numpy as np
import jax
import jax.numpy as jnp
from jax import lax
from jax.experimental import pallas as pl
from jax.experimental.pallas import tpu as pltpu

_F32 = jnp.float32
_BF16 = jnp.bfloat16

HEAD_DIM = 128
N_Q_HEADS = 16
N_KV_HEADS = 4
Q_PER_KV = N_Q_HEADS // N_KV_HEADS
WINDOW = 128
BLOCK = 128
ROPE_THETA = 500000.0
ROT_DIM = HEAD_DIM // 4
N_FOURIER_GROUPS = 8
FOURIER_GROUP_DIM = 128
FOURIER_DIM = N_FOURIER_GROUPS * FOURIER_GROUP_DIM
Q_DIM = N_Q_HEADS * HEAD_DIM
KV_DIM = N_KV_HEADS * HEAD_DIM
IN_DIM = Q_DIM + 2 * KV_DIM + FOURIER_DIM
RMS_EPS = 1e-6
NEG_INF = -1e30

LANES = 128
BF16_SUBLANES = 16
PROJ_CHUNK = Q_PER_KV * HEAD_DIM
V7X_VMEM_BYTES = 64 * 1024 * 1024


def _round_up(a, m):
    return (a + m - 1) // m * m


def _cparams(sem, vmem_bytes):
    return pltpu.CompilerParams(dimension_semantics=sem,
                                vmem_limit_bytes=int(min(vmem_bytes, V7X_VMEM_BYTES - (4 << 20))))


def _rms(x, g):
    return x * lax.rsqrt(jnp.mean(x * x, axis=-1, keepdims=True) + RMS_EPS) * g


class _Rows:
    def __init__(self, B, n_real, n_meta, tm, meta):
        self.B, self.n_real, self.n_meta, self.meta = B, n_real, n_meta, meta
        self.nb = B if meta else 1
        self.rows = n_meta if meta else tm
        self.grid = (1, 1) if meta else (B, n_real // tm)
        self.m = self.nb * self.rows

    def spec(self, width, lead=(), row_block0=None):
        nl = len(lead)
        if self.meta:
            r0 = self.n_real // self.n_meta if row_block0 is None else row_block0
            return pl.BlockSpec((self.nb,) + tuple(lead) + (self.rows, width),
                                lambda b, i, *_: (0,) + (0,) * nl + (r0, 0))
        return pl.BlockSpec((1,) + tuple(lead) + (self.rows, width),
                            lambda b, i, *_: (b,) + (0,) * nl + (i, 0))

    def table_spec(self, width):
        if self.meta:
            r0 = self.n_real // self.n_meta
            return pl.BlockSpec((self.rows, width), lambda b, i, *_: (r0, 0))
        return pl.BlockSpec((self.rows, width), lambda b, i, *_: (i, 0))


def _const_spec(shape, single_buffer=False):
    nd = len(shape)
    if single_buffer:
        return pl.BlockSpec(shape, lambda *_: (0,) * nd, pipeline_mode=pl.Buffered(1))
    return pl.BlockSpec(shape, lambda *_: (0,) * nd)


def _alias_tail(n_in, outs):
    specs = [pl.BlockSpec(memory_space=pl.ANY) for _ in outs]
    return specs, {n_in + k: k for k in range(len(outs))}


def _ffn_body(*refs, tf, nj, final, n_alias):
    h_ref, gain_ref, wgu_ref, wd_ref = refs[:4]
    rest = refs[4:]
    fgain_ref = None
    if final:
        fgain_ref, rest = rest[0], rest[1:]
    o_ref, xn_ref = rest[n_alias:]
    j = pl.program_id(2)
    D = h_ref.shape[-1]

    @pl.when(j == 0)
    def _():
        x = h_ref[...].reshape(-1, D)
        xn_ref[...] = _rms(x, gain_ref[...]).astype(_BF16)
        o_ref[...] = x.reshape(o_ref.shape)

    gu = jnp.dot(xn_ref[...], wgu_ref[0], preferred_element_type=_F32)
    g = gu[:, :tf]
    u = gu[:, tf:]
    act = ((g * jax.nn.sigmoid(g)) * u * 0.5).astype(_BF16)
    o_ref[...] += jnp.dot(act, wd_ref[...], preferred_element_type=_F32).reshape(o_ref.shape)

    if final:
        @pl.when(j == nj - 1)
        def _():
            o_ref[...] = _rms(o_ref[...].reshape(-1, D), fgain_ref[...]).reshape(o_ref.shape)


def _ffn(rows, h, gain, wgu, wd, *, tf, T, final_gain=None, h_row_block0=None, prev_out=None):
    D = h.shape[-1]
    nj = wgu.shape[0]
    final = final_gain is not None
    in_specs = [rows.spec(D, row_block0=h_row_block0), _const_spec((1, D)),
                pl.BlockSpec((1, D, 2 * tf), lambda b, i, j: (j, 0, 0)),
                pl.BlockSpec((tf, D), lambda b, i, j: (j, 0))]
    args = [h, gain, wgu, wd]
    if final:
        in_specs.append(_const_spec((1, D)))
        args.append(final_gain)
    aliases = {}
    if prev_out is not None:
        extra, aliases = _alias_tail(len(args), [prev_out])
        in_specs += extra
        args.append(prev_out)
    m = rows.m
    vmem = (4 * m * D * 4 + m * D * 2 + 2 * (D * 2 * tf + tf * D) * 2 + m * 2 * tf * 4 + m * tf * 2 + (4 << 20))
    out_rows = rows.n_real if final else T
    return pl.pallas_call(
        functools.partial(_ffn_body, tf=tf, nj=nj, final=final, n_alias=len(aliases)),
        grid=rows.grid + (nj,),
        in_specs=in_specs,
        out_specs=rows.spec(D),
        out_shape=jax.ShapeDtypeStruct((rows.B, out_rows, D), _F32),
        scratch_shapes=[pltpu.VMEM((m, D), _BF16)],
        input_output_aliases=aliases,
        compiler_params=_cparams(("parallel", "parallel", "arbitrary"), vmem),
        name=("ffn_final" if final else "ffn") + ("_meta" if rows.meta else ""),
    )(*args)


def _proj_body(*refs, n_rope, n_qkv, n_chunks, nb, n_alias):
    h_ref, gain_ref, w_ref, c_ref, s1_ref, s2_ref, dft_ref = refs[:7]
    qkv_ref, z_ref, xn_ref = refs[7 + n_alias:]
    D = h_ref.shape[-1]
    rows = h_ref.shape[-2]
    xn_ref[...] = _rms(h_ref[...].reshape(-1, D), gain_ref[...]).astype(_BF16)
    c = jnp.concatenate([c_ref[...]] * nb, axis=0)
    s1 = jnp.concatenate([s1_ref[...]] * nb, axis=0)
    s2 = jnp.concatenate([s2_ref[...]] * nb, axis=0)
    half = ROT_DIM // 2
    for ch in range(n_chunks):
        cols = slice(ch * PROJ_CHUNK, (ch + 1) * PROJ_CHUNK)
        y = jnp.dot(xn_ref[...], w_ref[:, cols], preferred_element_type=_F32)
        if ch < n_rope:
            for hh in range(PROJ_CHUNK // HEAD_DIM):
                xh = y[:, hh * HEAD_DIM:(hh + 1) * HEAD_DIM]
                up = pltpu.roll(xh, HEAD_DIM - half, 1)
                dn = pltpu.roll(xh, half, 1)
                r = (xh * c + up * s1 + dn * s2).astype(_BF16)
                lo = ch * PROJ_CHUNK + hh * HEAD_DIM
                qkv_ref[:, :, lo:lo + HEAD_DIM] = r.reshape(nb, rows, HEAD_DIM)
        elif ch < n_qkv:
            qkv_ref[:, :, cols] = y.astype(_BF16).reshape(nb, rows, PROJ_CHUNK)
        else:
            ub = y.astype(_BF16)
            for gg in range(PROJ_CHUNK // FOURIER_GROUP_DIM):
                zz = jnp.dot(ub[:, gg * FOURIER_GROUP_DIM:(gg + 1) * FOURIER_GROUP_DIM], dft_ref[...],
                             preferred_element_type=_F32)
                lo = (ch - n_qkv) * PROJ_CHUNK + gg * FOURIER_GROUP_DIM
                z_ref[:, 0, :, lo:lo + FOURIER_GROUP_DIM] = (
                    zz[:, :FOURIER_GROUP_DIM].astype(_BF16).reshape(nb, rows, FOURIER_GROUP_DIM))
                z_ref[:, 1, :, lo:lo + FOURIER_GROUP_DIM] = (
                    zz[:, FOURIER_GROUP_DIM:].astype(_BF16).reshape(nb, rows, FOURIER_GROUP_DIM))


def _proj(rows, h, gain, w_in, rope_c, rope_s1, rope_s2, dft_c, *, prev_out=None):
    B, T, D = h.shape
    n_rope = (Q_DIM + KV_DIM) // PROJ_CHUNK
    n_qkv = (Q_DIM + 2 * KV_DIM) // PROJ_CHUNK
    n_chunks = IN_DIM // PROJ_CHUNK
    in_specs = [rows.spec(D), _const_spec((1, D)), _const_spec((D, IN_DIM), single_buffer=True),
                rows.table_spec(LANES), rows.table_spec(LANES), rows.table_spec(LANES),
                _const_spec((FOURIER_GROUP_DIM, 2 * FOURIER_GROUP_DIM))]
    args = [h, gain, w_in, rope_c, rope_s1, rope_s2, dft_c]
    aliases = {}
    if prev_out is not None:
        extra, aliases = _alias_tail(len(args), prev_out)
        in_specs += extra
        args += list(prev_out)
    m = rows.m
    vmem = (2 * m * D * 4 + m * D * 2 + D * IN_DIM * 2 + 6 * m * LANES * 4
            + 2 * m * (Q_DIM + 2 * KV_DIM) * 2 + 4 * m * FOURIER_DIM * 2 + 8 * m * PROJ_CHUNK * 4 + (6 << 20))
    return pl.pallas_call(
        functools.partial(_proj_body, n_rope=n_rope, n_qkv=n_qkv, n_chunks=n_chunks, nb=rows.nb,
                          n_alias=len(aliases)),
        grid=rows.grid,
        in_specs=in_specs,
        out_specs=[rows.spec(Q_DIM + 2 * KV_DIM), rows.spec(FOURIER_DIM, lead=(2,))],
        out_shape=[jax.ShapeDtypeStruct((B, T, Q_DIM + 2 * KV_DIM), _BF16),
                   jax.ShapeDtypeStruct((B, 2, T, FOURIER_DIM), _BF16)],
        scratch_shapes=[pltpu.VMEM((m, D), _BF16)],
        input_output_aliases=aliases,
        compiler_params=_cparams(("parallel", "arbitrary"), vmem),
        name="mixer_proj" + ("_meta" if rows.meta else ""),
    )(*args)


def _stack_heads(q):
    return jnp.concatenate([q[:, h * HEAD_DIM:(h + 1) * HEAD_DIM] for h in range(Q_PER_KV)], axis=0)


def _sink_column(sink_ref, g, rows_per_head):
    return jnp.concatenate(
        [jnp.full((rows_per_head, 1), sink_ref[g * Q_PER_KV + h], _F32) for h in range(Q_PER_KV)], axis=0)


def _qk(q4, k):
    return lax.dot_general(q4, k, (((1,), (1,)), ((), ())), preferred_element_type=_F32) * (HEAD_DIM ** -0.5)


def _softmax_pv(s_m, s_band, sink, v_band, v_meta):
    band_max = functools.reduce(jnp.maximum, s_band)
    mx = jnp.maximum(jnp.maximum(jnp.max(s_m, axis=-1, keepdims=True), jnp.max(band_max, axis=-1, keepdims=True)),
                     sink)
    e_m = jnp.exp(s_m - mx)
    e_band = [jnp.exp(sb - mx) for sb in s_band]
    den = (jnp.sum(e_m, axis=-1, keepdims=True) + jnp.sum(functools.reduce(jnp.add, e_band), axis=-1, keepdims=True)
           + jnp.exp(sink - mx))
    inv = 1.0 / den
    p_band = jnp.concatenate([(e * inv).astype(_BF16) for e in e_band], axis=1)
    return (jnp.dot(p_band, v_band, preferred_element_type=_F32)
            + jnp.dot((e_m * inv).astype(_BF16), v_meta, preferred_element_type=_F32))


def _attn_body(sink_ref, q_ref, kp_ref, kc_ref, kn_ref, km_ref, vp_ref, vc_ref, vn_ref, vm_ref, o_ref,
               *, qb, ntile):
    g = pl.program_id(1)
    jt = pl.program_id(2)
    rows = Q_PER_KV * BLOCK
    row = lax.broadcasted_iota(jnp.int32, (rows, BLOCK), 0) % BLOCK
    col = lax.broadcasted_iota(jnp.int32, (rows, BLOCK), 1)
    never = 2 * BLOCK
    sink = _sink_column(sink_ref, g, BLOCK)
    k_meta = km_ref[0]
    v_meta = vm_ref[0]

    def band(prev_ref, cur_ref, next_ref, qi):
        lo, hi = (qi - 1) * BLOCK, (qi + 2) * BLOCK
        if 0 <= lo and hi <= qb * BLOCK:
            return cur_ref[0, lo:hi]
        parts = [prev_ref[0] if qi == 0 else cur_ref[0, lo:lo + BLOCK],
                 cur_ref[0, qi * BLOCK:(qi + 1) * BLOCK],
                 next_ref[0] if qi == qb - 1 else cur_ref[0, hi - BLOCK:hi]]
        return jnp.concatenate(parts, axis=0)

    for qi in range(qb):
        q4 = _stack_heads(q_ref[0, qi * BLOCK:(qi + 1) * BLOCK, :])
        s = _qk(q4, band(kp_ref, kc_ref, kn_ref, qi))
        off_p = jnp.where(jt >= 1, 0, never) if qi == 0 else 0
        off_n = jnp.where(jt <= ntile - 2, 0, -never) if qi == qb - 1 else 0
        s_p = jnp.where(col >= row + off_p, s[:, :BLOCK], NEG_INF)
        s_c = s[:, BLOCK:2 * BLOCK]
        s_n = jnp.where(col <= row + off_n, s[:, 2 * BLOCK:], NEG_INF)
        out = _softmax_pv(_qk(q4, k_meta), [s_p, s_c, s_n], sink, band(vp_ref, vc_ref, vn_ref, qi), v_meta)
        for h in range(Q_PER_KV):
            o_ref[0, qi * BLOCK:(qi + 1) * BLOCK, h * HEAD_DIM:(h + 1) * HEAD_DIM] = (
                out[h * BLOCK:(h + 1) * BLOCK].astype(_BF16))


def _attn_meta_body(sink_ref, q_ref, k0_ref, km_ref, v0_ref, vm_ref, prev_out_ref, o_ref, *, n_meta):
    del prev_out_ref
    g = pl.program_id(1)
    rows = Q_PER_KV * n_meta
    row = lax.broadcasted_iota(jnp.int32, (rows, BLOCK), 0) % n_meta
    col = lax.broadcasted_iota(jnp.int32, (rows, BLOCK), 1)
    q4 = _stack_heads(q_ref[0])
    s_n = jnp.where(col <= row + (WINDOW - n_meta), _qk(q4, k0_ref[0]), NEG_INF)
    out = _softmax_pv(_qk(q4, km_ref[0]), [s_n], _sink_column(sink_ref, g, n_meta), v0_ref[0], vm_ref[0])
    for h in range(Q_PER_KV):
        o_ref[0, :, h * HEAD_DIM:(h + 1) * HEAD_DIM] = out[h * n_meta:(h + 1) * n_meta].astype(_BF16)


def _attention(qkv, sink, *, n_real, n_meta, qb):
    B, T, _ = qkv.shape
    nblk = n_real // BLOCK
    ntile = nblk // qb
    kcol = Q_DIM // HEAD_DIM
    vcol = (Q_DIM + KV_DIM) // HEAD_DIM
    mrow = n_real // n_meta

    def edge_spec(rowf, col0):
        return pl.BlockSpec((1, BLOCK, HEAD_DIM), lambda b, g, j, s: (b, rowf(j), col0 + g))

    def main_spec(col0):
        return pl.BlockSpec((1, qb * BLOCK, HEAD_DIM), lambda b, g, j, s: (b, j, col0 + g))

    def meta_spec(col0):
        return pl.BlockSpec((1, n_meta, HEAD_DIM), lambda b, g, j, s: (b, mrow, col0 + g))

    def prev_i(j):
        return jnp.maximum(qb * j - 1, 0)

    def next_i(j):
        return jnp.minimum(qb * j + qb, nblk - 1)

    attn = pl.pallas_call(
        functools.partial(_attn_body, qb=qb, ntile=ntile),
        grid_spec=pltpu.PrefetchScalarGridSpec(
            num_scalar_prefetch=1,
            grid=(B, N_KV_HEADS, ntile),
            in_specs=[
                pl.BlockSpec((1, qb * BLOCK, PROJ_CHUNK), lambda b, g, j, s: (b, j, g)),
                edge_spec(prev_i, kcol), main_spec(kcol), edge_spec(next_i, kcol), meta_spec(kcol),
                edge_spec(prev_i, vcol), main_spec(vcol), edge_spec(next_i, vcol), meta_spec(vcol),
            ],
            out_specs=pl.BlockSpec((1, qb * BLOCK, PROJ_CHUNK), lambda b, g, j, s: (b, j, g)),
        ),
        out_shape=jax.ShapeDtypeStruct((B, T, Q_DIM), _BF16),
        compiler_params=_cparams(("parallel", "parallel", "arbitrary"), 32 << 20),
        name="windowed_gqa",
    )(sink, qkv, qkv, qkv, qkv, qkv, qkv, qkv, qkv, qkv)

    def blk0_spec(col0):
        return pl.BlockSpec((1, BLOCK, HEAD_DIM), lambda b, g, s: (b, 0, col0 + g))

    def meta2_spec(col0):
        return pl.BlockSpec((1, n_meta, HEAD_DIM), lambda b, g, s: (b, mrow, col0 + g))

    return pl.pallas_call(
        functools.partial(_attn_meta_body, n_meta=n_meta),
        grid_spec=pltpu.PrefetchScalarGridSpec(
            num_scalar_prefetch=1,
            grid=(B, N_KV_HEADS),
            in_specs=[
                pl.BlockSpec((1, n_meta, PROJ_CHUNK), lambda b, g, s: (b, mrow, g)),
                blk0_spec(kcol), meta2_spec(kcol), blk0_spec(vcol), meta2_spec(vcol),
                pl.BlockSpec(memory_space=pl.ANY),
            ],
            out_specs=pl.BlockSpec((1, n_meta, PROJ_CHUNK), lambda b, g, s: (b, mrow, g)),
        ),
        out_shape=jax.ShapeDtypeStruct((B, T, Q_DIM), _BF16),
        input_output_aliases={6: 0},
        compiler_params=_cparams(("parallel", "arbitrary"), 32 << 20),
        name="windowed_gqa_meta",
    )(sink, qkv, qkv, qkv, qkv, qkv, attn)


def _fft_factors(T):
    best = None
    for n2 in range(16, T + 1, 16):
        if T % n2:
            continue
        n1 = T // n2
        cost = 4 * n1 + 2 * n2
        if best is None or cost < best[0]:
            best = (cost, n1, n2)
    assert best is not None, "sequence length must be a multiple of 16"
    return best[1], best[2]


def _fft1_body(z_ref, m_ref, a_ref, *, n1, R, C):
    rows = n1 * R
    rhs = jnp.concatenate([z_ref[0, 0].reshape(rows, C), z_ref[0, 1].reshape(rows, C)], axis=0)
    res = jnp.dot(m_ref[0], rhs, preferred_element_type=_F32)
    a_ref[0, 0] = res[:rows].reshape(n1, R, C).astype(_BF16)
    a_ref[0, 1] = res[rows:].reshape(n1, R, C).astype(_BF16)


def _fft2_body(ar_ref, ai_ref, mc_ref, ms_ref, f_ref):
    res = (jnp.dot(mc_ref[0], ar_ref[0, 0, 0], preferred_element_type=_F32)
           + jnp.dot(ms_ref[0], ai_ref[0, 0, 0], preferred_element_type=_F32))
    f_ref[0] = res.astype(_BF16)


def _fft_tables(T, n1, n2, n_meta, R):
    two_pi = 2.0 * np.pi
    k1 = jnp.arange(n1, dtype=jnp.int32)[None, :, None]
    nn1 = jnp.arange(n1, dtype=jnp.int32)[None, None, :]
    nn2 = jnp.arange(n2, dtype=jnp.int32)[:, None, None]
    ph = (k1 * (nn2 + n_meta + n2 * nn1)) % T
    ang = ph.astype(_F32) * (two_pi / T)
    c = jnp.cos(ang) * (n1 ** -0.5)
    s = jnp.sin(ang) * (n1 ** -0.5)
    eye = jnp.eye(R, dtype=_F32)

    def blockdiag(a):
        a4 = a.reshape(n2 // R, R, n1, n1)
        return jnp.einsum("jrkn,rs->jkrns", a4, eye).reshape(n2 // R, n1 * R, n1 * R).astype(_BF16)

    bc, bs = blockdiag(c), blockdiag(s)
    m1 = jnp.concatenate([jnp.concatenate([bc, bs], axis=2), jnp.concatenate([-bs, bc], axis=2)], axis=1)
    k2 = jnp.arange(n2, dtype=jnp.int32)[:, None]
    mm2 = jnp.arange(n2, dtype=jnp.int32)[None, :]
    outs_c, outs_s = [], []
    for rot in (n_meta // n1, n_meta // n1 + 1):
        ph2 = (((k2 + rot) % n2) * (mm2 + n_meta)) % n2
        ang2 = ph2.astype(_F32) * (two_pi / n2)
        outs_c.append(jnp.cos(ang2) * (n2 ** -0.5))
        outs_s.append(jnp.sin(ang2) * (n2 ** -0.5))
    return m1, jnp.stack(outs_c).astype(_BF16), jnp.stack(outs_s).astype(_BF16)


def _position_dft(z, tables, *, n1, n2, R, n_meta):
    B, _, T, C = z.shape
    m1, m2c, m2s = tables
    rows = n1 * R
    zv = z.reshape(B, 2, n1, n2, C)
    av = pl.pallas_call(
        functools.partial(_fft1_body, n1=n1, R=R, C=C),
        grid=(B, n2 // R),
        in_specs=[
            pl.BlockSpec((1, 2, n1, R, C), lambda b, t: (b, 0, 0, t, 0)),
            pl.BlockSpec((1, 2 * rows, 2 * rows), lambda b, t: (t, 0, 0)),
        ],
        out_specs=pl.BlockSpec((1, 2, n1, R, C), lambda b, t: (b, 0, 0, t, 0)),
        out_shape=jax.ShapeDtypeStruct((B, 2, n1, n2, C), _BF16),
        compiler_params=_cparams(("parallel", "arbitrary"), 48 << 20),
        name="dft_stage1",
    )(zv, m1)
    shift = n1 - (n_meta % n1)

    f = pl.pallas_call(
        _fft2_body,
        grid=(B, n1),
        in_specs=[
            pl.BlockSpec((1, 1, 1, n2, C), lambda b, k: (b, 0, k, 0, 0)),
            pl.BlockSpec((1, 1, 1, n2, C), lambda b, k: (b, 1, k, 0, 0)),
            pl.BlockSpec((1, n2, n2), lambda b, k: (jnp.where(k < n_meta % n1, 1, 0), 0, 0)),
            pl.BlockSpec((1, n2, n2), lambda b, k: (jnp.where(k < n_meta % n1, 1, 0), 0, 0)),
        ],
        out_specs=pl.BlockSpec((1, n2, C), lambda b, k: (b, 0, (k + shift) % n1)),
        out_shape=jax.ShapeDtypeStruct((B, n2, n1 * C), _BF16),
        compiler_params=_cparams(("parallel", "arbitrary"), 32 << 20),
        name="dft_stage2",
    )(av, av, m2c, m2s)
    return f.reshape(B, T, C)


def _mix_body(*refs, tn, n_alias):
    h_ref, gain_ref, a_ref, f_ref, wpa_ref, wpf_ref, wg_ref, wo_ref = refs[:8]
    o_ref, xn_ref = refs[8 + n_alias:]
    j = pl.program_id(2)
    D = h_ref.shape[-1]

    @pl.when(j == 0)
    def _():
        x = h_ref[...].reshape(-1, D)
        xn_ref[...] = _rms(x, gain_ref[...]).astype(_BF16)
        o_ref[...] = x.reshape(o_ref.shape)

    av = jnp.dot(a_ref[...].reshape(-1, Q_DIM), wpa_ref[0], preferred_element_type=_F32)
    fv = jnp.dot(f_ref[...].reshape(-1, FOURIER_DIM), wpf_ref[0], preferred_element_type=_F32)
    gv = jax.nn.sigmoid(jnp.dot(xn_ref[...], wg_ref[0], preferred_element_type=_F32))
    mixed = (gv[:, :tn] * av + gv[:, tn:] * fv).astype(_BF16)
    o_ref[...] += jnp.dot(mixed, wo_ref[...], preferred_element_type=_F32).reshape(o_ref.shape)


def _mix_out(rows, h, gain, attn, four, wpa_c, wpf_c, wg_c, wo, *, tn, prev_out=None):
    B, T, D = h.shape
    nj = wpa_c.shape[0]
    in_specs = [rows.spec(D), _const_spec((1, D)), rows.spec(Q_DIM), rows.spec(FOURIER_DIM),
                pl.BlockSpec((1, Q_DIM, tn), lambda b, i, j: (j, 0, 0)),
                pl.BlockSpec((1, FOURIER_DIM, tn), lambda b, i, j: (j, 0, 0)),
                pl.BlockSpec((1, D, 2 * tn), lambda b, i, j: (j, 0, 0)),
                pl.BlockSpec((tn, D), lambda b, i, j: (j, 0))]
    args = [h, gain, attn, four, wpa_c, wpf_c, wg_c, wo]
    aliases = {}
    if prev_out is not None:
        extra, aliases = _alias_tail(len(args), [prev_out])
        in_specs += extra
        args.append(prev_out)
    m = rows.m
    vmem = (4 * m * D * 4 + m * D * 2 + 2 * m * (Q_DIM + FOURIER_DIM) * 2
            + 2 * (Q_DIM * tn + FOURIER_DIM * tn + D * 2 * tn + tn * D) * 2 + 8 * m * tn * 4 + (4 << 20))
    return pl.pallas_call(
        functools.partial(_mix_body, tn=tn, n_alias=len(aliases)),
        grid=rows.grid + (nj,),
        in_specs=in_specs,
        out_specs=rows.spec(D),
        out_shape=jax.ShapeDtypeStruct((B, T, D), _F32),
        scratch_shapes=[pltpu.VMEM((m, D), _BF16)],
        input_output_aliases=aliases,
        compiler_params=_cparams(("parallel", "parallel", "arbitrary"), vmem),
        name="mixer_out" + ("_meta" if rows.meta else ""),
    )(*args)


def _col_chunks(w, tn):
    K, N = w.shape
    return w.reshape(K, N // tn, tn).transpose(1, 0, 2).astype(_BF16)


def _paired_col_chunks(wa, wb, tn):
    K, N = wa.shape
    a = wa.reshape(K, N // tn, tn)
    b = wb.reshape(K, N // tn, tn)
    return jnp.concatenate([a, b], axis=-1).transpose(1, 0, 2).astype(_BF16)


def _rope_tables(n_real, n_meta):
    T = n_real + n_meta
    half = ROT_DIM // 2
    inv = 1.0 / (ROPE_THETA ** (jnp.arange(0, ROT_DIM, 2, dtype=_F32) / ROT_DIM))
    ang = jnp.arange(T, dtype=_F32)[:, None] * inv[None, :]
    cos, sin = jnp.cos(ang), jnp.sin(ang)
    cos = jnp.concatenate([cos[n_meta:], cos[:n_meta]], axis=0)
    sin = jnp.concatenate([sin[n_meta:], sin[:n_meta]], axis=0)
    ones = jnp.ones((T, HEAD_DIM - ROT_DIM), _F32)
    zeros_h = jnp.zeros((T, half), _F32)
    zeros_r = jnp.zeros((T, HEAD_DIM - ROT_DIM), _F32)
    c = jnp.concatenate([cos, cos, ones], axis=1)
    s1 = jnp.concatenate([-sin, zeros_h, zeros_r], axis=1)
    s2 = jnp.concatenate([zeros_h, sin, zeros_r], axis=1)
    return c, s1, s2


def _channel_dft_table():
    n = FOURIER_GROUP_DIM
    c = jnp.arange(n, dtype=jnp.int32)[:, None]
    k = jnp.arange(n, dtype=jnp.int32)[None, :]
    ang = ((c * k) % n).astype(_F32) * (2.0 * np.pi / n)
    return (jnp.concatenate([jnp.cos(ang), -jnp.sin(ang)], axis=1) * (n ** -0.5)).astype(_BF16)


def _pick_tile(n, pref):
    t = pref
    while n % t:
        t //= 2
    return t


def kernel(x, meta_tokens, ffn1_norm, ffn1_w_gate, ffn1_w_up, ffn1_w_down, mix_norm, w_in, w_gate, sink,
           w_attn_branch, w_fourier_branch, w_out, ffn2_norm, ffn2_w_gate, ffn2_w_up, ffn2_w_down, final_norm):
    B, S, D = x.shape
    n_meta = meta_tokens.shape[0]
    depth = ffn1_norm.shape[0]
    d_ff = ffn1_w_gate.shape[-1]
    T = S + n_meta
    assert S % BLOCK == 0 and S % n_meta == 0 and BLOCK % n_meta == 0
    assert w_in.shape[-1] == IN_DIM and KV_DIM == PROJ_CHUNK

    tm_ffn = _pick_tile(S, 1024)
    tm = _pick_tile(S, 512)
    tf = _pick_tile(d_ff, 512)
    tn = _pick_tile(D, 512)
    qb = _pick_tile(S // BLOCK, 4)
    rows_ffn = _Rows(B, S, n_meta, tm_ffn, meta=False)
    rows_main = _Rows(B, S, n_meta, tm, meta=False)
    rows_meta = _Rows(B, S, n_meta, tm, meta=True)

    n1, n2 = _fft_factors(T)
    fft_tables = _fft_tables(T, n1, n2, n_meta, BF16_SUBLANES)
    rope_c, rope_s1, rope_s2 = _rope_tables(S, n_meta)
    dft_c = _channel_dft_table()

    def ffn(h, h_meta, gain, w_gate_l, w_up_l, w_down_l, final_gain=None):
        wgu = _paired_col_chunks(w_gate_l, w_up_l, tf)
        wd = w_down_l.astype(_BF16)
        out = _ffn(rows_ffn, h, gain, wgu, wd, tf=tf, T=T, final_gain=final_gain)
        if final_gain is None:
            src, blk0 = h_meta
            out = _ffn(rows_meta, src, gain, wgu, wd, tf=tf, T=T, h_row_block0=blk0, prev_out=out)
        return out

    meta = jnp.broadcast_to(meta_tokens[None].astype(x.dtype), (B, n_meta, D))
    h = None
    for l in range(depth):
        if l == 0:
            h = ffn(x, (meta, 0), ffn1_norm[l][None], ffn1_w_gate[l], ffn1_w_up[l], ffn1_w_down[l])
        else:
            h = ffn(h, (h, None), ffn1_norm[l][None], ffn1_w_gate[l], ffn1_w_up[l], ffn1_w_down[l])

        gain = mix_norm[l][None]
        w_in_l = w_in[l].astype(_BF16)
        proj_args = (gain, w_in_l, rope_c, rope_s1, rope_s2, dft_c)
        qkv, z = _proj(rows_main, h, *proj_args)
        qkv, z = _proj(rows_meta, h, *proj_args, prev_out=(qkv, z))
        attn = _attention(qkv, sink[l], n_real=S, n_meta=n_meta, qb=qb)
        four = _position_dft(z, fft_tables, n1=n1, n2=n2, R=BF16_SUBLANES, n_meta=n_meta)
        wg = w_gate[l]
        mix_args = (gain, attn, four, _col_chunks(w_attn_branch[l], tn), _col_chunks(w_fourier_branch[l], tn),
                    _paired_col_chunks(wg[:, :D], wg[:, D:], tn), w_out[l].astype(_BF16))
        h2 = _mix_out(rows_main, h, *mix_args, tn=tn)
        h = _mix_out(rows_meta, h, *mix_args, tn=tn, prev_out=h2)

        last = l == depth - 1
        h = ffn(h, (h, None), ffn2_norm[l][None], ffn2_w_gate[l], ffn2_w_up[l], ffn2_w_down[l],
                final_gain=final_norm[None] if last else None)
    return h
```

```python
import functools

import numpy as np
import jax
import jax.numpy as jnp
from jax import lax
from jax.experimental import pallas as pl
from jax.experimental.pallas import tpu as pltpu

_F32 = jnp.float32
_BF16 = jnp.bfloat16

HEAD_DIM = 128
N_Q_HEADS = 16
N_KV_HEADS = 4
Q_PER_KV = N_Q_HEADS // N_KV_HEADS
WINDOW = 128
BLOCK = 128
ROPE_THETA = 500000.0
ROT_DIM = HEAD_DIM // 4
N_FOURIER_GROUPS = 8
FOURIER_GROUP_DIM = 128
FOURIER_DIM = N_FOURIER_GROUPS * FOURIER_GROUP_DIM
Q_DIM = N_Q_HEADS * HEAD_DIM
KV_DIM = N_KV_HEADS * HEAD_DIM
IN_DIM = Q_DIM + 2 * KV_DIM + FOURIER_DIM
RMS_EPS = 1e-6
NEG_INF = -1e30

LANES = 128
BF16_SUBLANES = 16
PROJ_CHUNK = Q_PER_KV * HEAD_DIM
V7X_VMEM_BYTES = 64 * 1024 * 1024


def _round_up(a, m):
    return (a + m - 1) // m * m


def _cparams(sem, vmem_bytes):
    return pltpu.CompilerParams(dimension_semantics=sem,
                                vmem_limit_bytes=int(min(vmem_bytes, V7X_VMEM_BYTES - (4 << 20))))


def _rms(x, g):
    return x * lax.rsqrt(jnp.mean(x * x, axis=-1, keepdims=True) + RMS_EPS) * g


class _Rows:
    def __init__(self, B, n_real, n_meta, tm, meta):
        self.B, self.n_real, self.n_meta, self.meta = B, n_real, n_meta, meta
        self.nb = B if meta else 1
        self.rows = n_meta if meta else tm
        self.grid = (1, 1) if meta else (B, n_real // tm)
        self.m = self.nb * self.rows

    def spec(self, width, lead=(), row_block0=None):
        nl = len(lead)
        if self.meta:
            r0 = self.n_real // self.n_meta if row_block0 is None else row_block0
            return pl.BlockSpec((self.nb,) + tuple(lead) + (self.rows, width),
                                lambda b, i, *_: (0,) + (0,) * nl + (r0, 0))
        return pl.BlockSpec((1,) + tuple(lead) + (self.rows, width),
                            lambda b, i, *_: (b,) + (0,) * nl + (i, 0))

    def table_spec(self, width):
        if self.meta:
            r0 = self.n_real // self.n_meta
            return pl.BlockSpec((self.rows, width), lambda b, i, *_: (r0, 0))
        return pl.BlockSpec((self.rows, width), lambda b, i, *_: (i, 0))


def _const_spec(shape, single_buffer=False):
    nd = len(shape)
    if single_buffer:
        return pl.BlockSpec(shape, lambda *_: (0,) * nd, pipeline_mode=pl.Buffered(1))
    return pl.BlockSpec(shape, lambda *_: (0,) * nd)


def _alias_tail(n_in, outs):
    specs = [pl.BlockSpec(memory_space=pl.ANY) for _ in outs]
    return specs, {n_in + k: k for k in range(len(outs))}


def _ffn_body(*refs, nj, final, n_alias):
    h_ref, gain_ref, wg_ref, wu_ref, wd_ref = refs[:5]
    rest = refs[5:]
    fgain_ref = None
    if final:
        fgain_ref, rest = rest[0], rest[1:]
    o_ref, xn_ref = rest[n_alias:]
    j = pl.program_id(2)
    D = h_ref.shape[-1]

    @pl.when(j == 0)
    def _():
        x = h_ref[...].reshape(-1, D)
        xn_ref[...] = _rms(x, gain_ref[...]).astype(_BF16)
        o_ref[...] = x.reshape(o_ref.shape)

    g = jnp.dot(xn_ref[...], wg_ref[...], preferred_element_type=_F32)
    u = jnp.dot(xn_ref[...], wu_ref[...], preferred_element_type=_F32)
    act = ((g * jax.nn.sigmoid(g)) * u * 0.5).astype(_BF16)
    o_ref[...] += jnp.dot(act, wd_ref[...], preferred_element_type=_F32).reshape(o_ref.shape)

    if final:
        @pl.when(j == nj - 1)
        def _():
            o_ref[...] = _rms(o_ref[...].reshape(-1, D), fgain_ref[...]).reshape(o_ref.shape)


def _ffn(rows, h, gain, wg, wu, wd, *, tf, T, final_gain=None, h_row_block0=None, prev_out=None):
    D = h.shape[-1]
    nj = wg.shape[1] // tf
    final = final_gain is not None
    in_specs = [rows.spec(D, row_block0=h_row_block0), _const_spec((1, D)),
                pl.BlockSpec((D, tf), lambda b, i, j: (0, j)),
                pl.BlockSpec((D, tf), lambda b, i, j: (0, j)),
                pl.BlockSpec((tf, D), lambda b, i, j: (j, 0))]
    args = [h, gain, wg, wu, wd]
    if final:
        in_specs.append(_const_spec((1, D)))
        args.append(final_gain)
    aliases = {}
    if prev_out is not None:
        extra, aliases = _alias_tail(len(args), [prev_out])
        in_specs += extra
        args.append(prev_out)
    m = rows.m
    vmem = (4 * m * D * 4 + m * D * 2 + 2 * (D * 2 * tf + tf * D) * 2 + m * 2 * tf * 4 + m * tf * 2 + (4 << 20))
    out_rows = rows.n_real if final else T
    return pl.pallas_call(
        functools.partial(_ffn_body, nj=nj, final=final, n_alias=len(aliases)),
        grid=rows.grid + (nj,),
        in_specs=in_specs,
        out_specs=rows.spec(D),
        out_shape=jax.ShapeDtypeStruct((rows.B, out_rows, D), _F32),
        scratch_shapes=[pltpu.VMEM((m, D), _BF16)],
        input_output_aliases=aliases,
        compiler_params=_cparams(("parallel", "parallel", "arbitrary"), vmem),
        name=("ffn_final" if final else "ffn") + ("_meta" if rows.meta else ""),
    )(*args)


def _proj_body(*refs, n_rope, n_qkv, n_chunks, nb, n_alias):
    h_ref, gain_ref, w_ref, c_ref, s1_ref, s2_ref, dft_ref = refs[:7]
    qkv_ref, z_ref, xn_ref = refs[7 + n_alias:]
    D = h_ref.shape[-1]
    rows = h_ref.shape[-2]
    xn_ref[...] = _rms(h_ref[...].reshape(-1, D), gain_ref[...]).astype(_BF16)
    c = jnp.concatenate([c_ref[...]] * nb, axis=0)
    s1 = jnp.concatenate([s1_ref[...]] * nb, axis=0)
    s2 = jnp.concatenate([s2_ref[...]] * nb, axis=0)
    half = ROT_DIM // 2
    for ch in range(n_chunks):
        cols = slice(ch * PROJ_CHUNK, (ch + 1) * PROJ_CHUNK)
        y = jnp.dot(xn_ref[...], w_ref[:, cols], preferred_element_type=_F32)
        if ch < n_rope:
            for hh in range(PROJ_CHUNK // HEAD_DIM):
                xh = y[:, hh * HEAD_DIM:(hh + 1) * HEAD_DIM]
                up = pltpu.roll(xh, HEAD_DIM - half, 1)
                dn = pltpu.roll(xh, half, 1)
                r = (xh * c + up * s1 + dn * s2).astype(_BF16)
                lo = ch * PROJ_CHUNK + hh * HEAD_DIM
                qkv_ref[:, :, lo:lo + HEAD_DIM] = r.reshape(nb, rows, HEAD_DIM)
        elif ch < n_qkv:
            qkv_ref[:, :, cols] = y.astype(_BF16).reshape(nb, rows, PROJ_CHUNK)
        else:
            ub = y.astype(_BF16)
            for gg in range(PROJ_CHUNK // FOURIER_GROUP_DIM):
                zz = jnp.dot(ub[:, gg * FOURIER_GROUP_DIM:(gg + 1) * FOURIER_GROUP_DIM], dft_ref[...],
                             preferred_element_type=_F32)
                lo = (ch - n_qkv) * PROJ_CHUNK + gg * FOURIER_GROUP_DIM
                z_ref[:, 0, :, lo:lo + FOURIER_GROUP_DIM] = (
                    zz[:, :FOURIER_GROUP_DIM].astype(_BF16).reshape(nb, rows, FOURIER_GROUP_DIM))
                z_ref[:, 1, :, lo:lo + FOURIER_GROUP_DIM] = (
                    zz[:, FOURIER_GROUP_DIM:].astype(_BF16).reshape(nb, rows, FOURIER_GROUP_DIM))


def _proj(rows, h, gain, w_in, rope_c, rope_s1, rope_s2, dft_c, *, prev_out=None):
    B, T, D = h.shape
    n_rope = (Q_DIM + KV_DIM) // PROJ_CHUNK
    n_qkv = (Q_DIM + 2 * KV_DIM) // PROJ_CHUNK
    n_chunks = IN_DIM // PROJ_CHUNK
    in_specs = [rows.spec(D), _const_spec((1, D)), _const_spec((D, IN_DIM), single_buffer=True),
                rows.table_spec(LANES), rows.table_spec(LANES), rows.table_spec(LANES),
                _const_spec((FOURIER_GROUP_DIM, 2 * FOURIER_GROUP_DIM))]
    args = [h, gain, w_in, rope_c, rope_s1, rope_s2, dft_c]
    aliases = {}
    if prev_out is not None:
        extra, aliases = _alias_tail(len(args), prev_out)
        in_specs += extra
        args += list(prev_out)
    m = rows.m
    vmem = (2 * m * D * 4 + m * D * 2 + D * IN_DIM * 2 + 6 * m * LANES * 4
            + 2 * m * (Q_DIM + 2 * KV_DIM) * 2 + 4 * m * FOURIER_DIM * 2 + 8 * m * PROJ_CHUNK * 4 + (6 << 20))
    return pl.pallas_call(
        functools.partial(_proj_body, n_rope=n_rope, n_qkv=n_qkv, n_chunks=n_chunks, nb=rows.nb,
                          n_alias=len(aliases)),
        grid=rows.grid,
        in_specs=in_specs,
        out_specs=[rows.spec(Q_DIM + 2 * KV_DIM), rows.spec(FOURIER_DIM, lead=(2,))],
        out_shape=[jax.ShapeDtypeStruct((B, T, Q_DIM + 2 * KV_DIM), _BF16),
                   jax.ShapeDtypeStruct((B, 2, T, FOURIER_DIM), _BF16)],
        scratch_shapes=[pltpu.VMEM((m, D), _BF16)],
        input_output_aliases=aliases,
        compiler_params=_cparams(("parallel", "arbitrary"), vmem),
        name="mixer_proj" + ("_meta" if rows.meta else ""),
    )(*args)


def _stack_heads(q):
    return jnp.concatenate([q[:, h * HEAD_DIM:(h + 1) * HEAD_DIM] for h in range(Q_PER_KV)], axis=0)


def _sink_column(sink_ref, g, rows_per_head):
    return jnp.concatenate(
        [jnp.full((rows_per_head, 1), sink_ref[g * Q_PER_KV + h], _F32) for h in range(Q_PER_KV)], axis=0)


def _pad_rows(a, rows):
    return jnp.concatenate([a, jnp.zeros((rows - a.shape[0], a.shape[1]), a.dtype)], axis=0)


def _softmax_pv(q4, k_all, v_all, masks, sink):
    rows = q4.shape[0]
    s = lax.dot_general(q4, k_all, (((1,), (1,)), ((), ())), preferred_element_type=_F32) * (HEAD_DIM ** -0.5)
    blocks = []
    for i, mk in enumerate(masks):
        sb = s[:, i * BLOCK:(i + 1) * BLOCK]
        blocks.append(sb if mk is None else jnp.where(mk, sb, NEG_INF))
    mx = jnp.maximum(jnp.max(functools.reduce(jnp.maximum, blocks), axis=-1, keepdims=True), sink)
    p = jnp.concatenate([jnp.exp(sb - mx).astype(_BF16) for sb in blocks], axis=1)
    v_ext = jnp.concatenate([v_all, jnp.ones(v_all.shape, _BF16)], axis=1)
    o = jnp.dot(p, v_ext, preferred_element_type=_F32)
    den = o[:, HEAD_DIM:] + jnp.exp(sink - mx)
    return o[:, :HEAD_DIM] / den


def _attn_body(sink_ref, q_ref, kp_ref, kc_ref, kn_ref, km_ref, vp_ref, vc_ref, vn_ref, vm_ref, o_ref,
               *, qb, ntile, n_meta):
    g = pl.program_id(1)
    jt = pl.program_id(2)
    rows = Q_PER_KV * BLOCK
    row = lax.broadcasted_iota(jnp.int32, (rows, BLOCK), 0) % BLOCK
    col = lax.broadcasted_iota(jnp.int32, (rows, BLOCK), 1)
    diff = col - row
    never = 2 * BLOCK
    sink = _sink_column(sink_ref, g, BLOCK)
    k_meta = _pad_rows(km_ref[0], BLOCK)
    v_meta = _pad_rows(vm_ref[0], BLOCK)
    meta_ok = col < n_meta

    def band(prev_ref, cur_ref, next_ref, meta, qi):
        lo, hi = (qi - 1) * BLOCK, (qi + 2) * BLOCK
        if 0 <= lo and hi <= qb * BLOCK:
            return jnp.concatenate([cur_ref[0, lo:hi], meta], axis=0)
        parts = [prev_ref[0] if qi == 0 else cur_ref[0, lo:lo + BLOCK],
                 cur_ref[0, qi * BLOCK:(qi + 1) * BLOCK],
                 next_ref[0] if qi == qb - 1 else cur_ref[0, hi - BLOCK:hi],
                 meta]
        return jnp.concatenate(parts, axis=0)

    for qi in range(qb):
        q4 = _stack_heads(q_ref[0, qi * BLOCK:(qi + 1) * BLOCK, :])
        off_p = jnp.where(jt >= 1, 0, never) if qi == 0 else 0
        off_n = jnp.where(jt <= ntile - 2, 0, -never) if qi == qb - 1 else 0
        masks = [diff >= off_p, None, diff <= off_n, meta_ok]
        out = _softmax_pv(q4, band(kp_ref, kc_ref, kn_ref, k_meta, qi), band(vp_ref, vc_ref, vn_ref, v_meta, qi),
                          masks, sink)
        for h in range(Q_PER_KV):
            o_ref[0, qi * BLOCK:(qi + 1) * BLOCK, h * HEAD_DIM:(h + 1) * HEAD_DIM] = (
                out[h * BLOCK:(h + 1) * BLOCK].astype(_BF16))


def _attn_meta_body(sink_ref, q_ref, k0_ref, km_ref, v0_ref, vm_ref, prev_out_ref, o_ref, *, n_meta):
    del prev_out_ref
    g = pl.program_id(1)
    rows = Q_PER_KV * n_meta
    row = lax.broadcasted_iota(jnp.int32, (rows, BLOCK), 0) % n_meta
    col = lax.broadcasted_iota(jnp.int32, (rows, BLOCK), 1)
    q4 = _stack_heads(q_ref[0])
    masks = [col <= row + (WINDOW - n_meta), col < n_meta]
    k_all = jnp.concatenate([k0_ref[0], _pad_rows(km_ref[0], BLOCK)], axis=0)
    v_all = jnp.concatenate([v0_ref[0], _pad_rows(vm_ref[0], BLOCK)], axis=0)
    out = _softmax_pv(q4, k_all, v_all, masks, _sink_column(sink_ref, g, n_meta))
    for h in range(Q_PER_KV):
        o_ref[0, :, h * HEAD_DIM:(h + 1) * HEAD_DIM] = out[h * n_meta:(h + 1) * n_meta].astype(_BF16)


def _attention(qkv, sink, *, n_real, n_meta, qb):
    B, T, _ = qkv.shape
    nblk = n_real // BLOCK
    ntile = nblk // qb
    kcol = Q_DIM // HEAD_DIM
    vcol = (Q_DIM + KV_DIM) // HEAD_DIM
    mrow = n_real // n_meta

    def edge_spec(rowf, col0):
        return pl.BlockSpec((1, BLOCK, HEAD_DIM), lambda b, g, j, s: (b, rowf(j), col0 + g))

    def main_spec(col0):
        return pl.BlockSpec((1, qb * BLOCK, HEAD_DIM), lambda b, g, j, s: (b, j, col0 + g))

    def meta_spec(col0):
        return pl.BlockSpec((1, n_meta, HEAD_DIM), lambda b, g, j, s: (b, mrow, col0 + g))

    def prev_i(j):
        return jnp.maximum(qb * j - 1, 0)

    def next_i(j):
        return jnp.minimum(qb * j + qb, nblk - 1)

    attn = pl.pallas_call(
        functools.partial(_attn_body, qb=qb, ntile=ntile, n_meta=n_meta),
        grid_spec=pltpu.PrefetchScalarGridSpec(
            num_scalar_prefetch=1,
            grid=(B, N_KV_HEADS, ntile),
            in_specs=[
                pl.BlockSpec((1, qb * BLOCK, PROJ_CHUNK), lambda b, g, j, s: (b, j, g)),
                edge_spec(prev_i, kcol), main_spec(kcol), edge_spec(next_i, kcol), meta_spec(kcol),
                edge_spec(prev_i, vcol), main_spec(vcol), edge_spec(next_i, vcol), meta_spec(vcol),
            ],
            out_specs=pl.BlockSpec((1, qb * BLOCK, PROJ_CHUNK), lambda b, g, j, s: (b, j, g)),
        ),
        out_shape=jax.ShapeDtypeStruct((B, T, Q_DIM), _BF16),
        compiler_params=_cparams(("parallel", "parallel", "arbitrary"), 32 << 20),
        name="windowed_gqa",
    )(sink, qkv, qkv, qkv, qkv, qkv, qkv, qkv, qkv, qkv)

    def blk0_spec(col0):
        return pl.BlockSpec((1, BLOCK, HEAD_DIM), lambda b, g, s: (b, 0, col0 + g))

    def meta2_spec(col0):
        return pl.BlockSpec((1, n_meta, HEAD_DIM), lambda b, g, s: (b, mrow, col0 + g))

    return pl.pallas_call(
        functools.partial(_attn_meta_body, n_meta=n_meta),
        grid_spec=pltpu.PrefetchScalarGridSpec(
            num_scalar_prefetch=1,
            grid=(B, N_KV_HEADS),
            in_specs=[
                pl.BlockSpec((1, n_meta, PROJ_CHUNK), lambda b, g, s: (b, mrow, g)),
                blk0_spec(kcol), meta2_spec(kcol), blk0_spec(vcol), meta2_spec(vcol),
                pl.BlockSpec(memory_space=pl.ANY),
            ],
            out_specs=pl.BlockSpec((1, n_meta, PROJ_CHUNK), lambda b, g, s: (b, mrow, g)),
        ),
        out_shape=jax.ShapeDtypeStruct((B, T, Q_DIM), _BF16),
        input_output_aliases={6: 0},
        compiler_params=_cparams(("parallel", "arbitrary"), 32 << 20),
        name="windowed_gqa_meta",
    )(sink, qkv, qkv, qkv, qkv, qkv, attn)


def _fft_factors(T):
    best = None
    for n2 in range(16, T + 1, 16):
        if T % n2:
            continue
        n1 = T // n2
        cost = 4 * n1 + 2 * n2
        if best is None or cost < best[0]:
            best = (cost, n1, n2)
    assert best is not None, "sequence length must be a multiple of 16"
    return best[1], best[2]


def _fft1_body(z_ref, m_ref, a_ref, *, n1, R, C):
    rows = n1 * R
    rhs = jnp.concatenate([z_ref[0, 0].reshape(rows, C), z_ref[0, 1].reshape(rows, C)], axis=0)
    res = jnp.dot(m_ref[0], rhs, preferred_element_type=_F32)
    a_ref[0, 0] = res[:rows].reshape(n1, R, C).astype(_BF16)
    a_ref[0, 1] = res[rows:].reshape(n1, R, C).astype(_BF16)


def _fft2_body(ar_ref, ai_ref, mc_ref, ms_ref, f_ref):
    res = (jnp.dot(mc_ref[0], ar_ref[0, 0, 0], preferred_element_type=_F32)
           + jnp.dot(ms_ref[0], ai_ref[0, 0, 0], preferred_element_type=_F32))
    f_ref[0] = res.astype(_BF16)


def _fft_tables(T, n1, n2, n_meta, R):
    two_pi = 2.0 * np.pi
    k1 = jnp.arange(n1, dtype=jnp.int32)[None, :, None]
    nn1 = jnp.arange(n1, dtype=jnp.int32)[None, None, :]
    nn2 = jnp.arange(n2, dtype=jnp.int32)[:, None, None]
    ph = (k1 * (nn2 + n_meta + n2 * nn1)) % T
    ang = ph.astype(_F32) * (two_pi / T)
    c = jnp.cos(ang) * (n1 ** -0.5)
    s = jnp.sin(ang) * (n1 ** -0.5)
    eye = jnp.eye(R, dtype=_F32)

    def blockdiag(a):
        a4 = a.reshape(n2 // R, R, n1, n1)
        return jnp.einsum("jrkn,rs->jkrns", a4, eye).reshape(n2 // R, n1 * R, n1 * R).astype(_BF16)

    bc, bs = blockdiag(c), blockdiag(s)
    m1 = jnp.concatenate([jnp.concatenate([bc, bs], axis=2), jnp.concatenate([-bs, bc], axis=2)], axis=1)
    k2 = jnp.arange(n2, dtype=jnp.int32)[:, None]
    mm2 = jnp.arange(n2, dtype=jnp.int32)[None, :]
    outs_c, outs_s = [], []
    for rot in (n_meta // n1, n_meta // n1 + 1):
        ph2 = (((k2 + rot) % n2) * (mm2 + n_meta)) % n2
        ang2 = ph2.astype(_F32) * (two_pi / n2)
        outs_c.append(jnp.cos(ang2) * (n2 ** -0.5))
        outs_s.append(jnp.sin(ang2) * (n2 ** -0.5))
    return m1, jnp.stack(outs_c).astype(_BF16), jnp.stack(outs_s).astype(_BF16)


def _position_dft(z, tables, *, n1, n2, R, n_meta):
    B, _, T, C = z.shape
    m1, m2c, m2s = tables
    rows = n1 * R
    zv = z.reshape(B, 2, n1, n2, C)
    av = pl.pallas_call(
        functools.partial(_fft1_body, n1=n1, R=R, C=C),
        grid=(B, n2 // R),
        in_specs=[
            pl.BlockSpec((1, 2, n1, R, C), lambda b, t: (b, 0, 0, t, 0)),
            pl.BlockSpec((1, 2 * rows, 2 * rows), lambda b, t: (t, 0, 0)),
        ],
        out_specs=pl.BlockSpec((1, 2, n1, R, C), lambda b, t: (b, 0, 0, t, 0)),
        out_shape=jax.ShapeDtypeStruct((B, 2, n1, n2, C), _BF16),
        compiler_params=_cparams(("parallel", "arbitrary"), 48 << 20),
        name="dft_stage1",
    )(zv, m1)
    shift = n1 - (n_meta % n1)

    f = pl.pallas_call(
        _fft2_body,
        grid=(B, n1),
        in_specs=[
            pl.BlockSpec((1, 1, 1, n2, C), lambda b, k: (b, 0, k, 0, 0)),
            pl.BlockSpec((1, 1, 1, n2, C), lambda b, k: (b, 1, k, 0, 0)),
            pl.BlockSpec((1, n2, n2), lambda b, k: (jnp.where(k < n_meta % n1, 1, 0), 0, 0)),
            pl.BlockSpec((1, n2, n2), lambda b, k: (jnp.where(k < n_meta % n1, 1, 0), 0, 0)),
        ],
        out_specs=pl.BlockSpec((1, n2, C), lambda b, k: (b, 0, (k + shift) % n1)),
        out_shape=jax.ShapeDtypeStruct((B, n2, n1 * C), _BF16),
        compiler_params=_cparams(("parallel", "arbitrary"), 32 << 20),
        name="dft_stage2",
    )(av, av, m2c, m2s)
    return f.reshape(B, T, C)


def _mix_body(*refs, n_alias):
    h_ref, gain_ref, a_ref, f_ref, wpa_ref, wpf_ref, wga_ref, wgf_ref, wo_ref = refs[:9]
    o_ref, xn_ref = refs[9 + n_alias:]
    j = pl.program_id(2)
    D = h_ref.shape[-1]

    @pl.when(j == 0)
    def _():
        x = h_ref[...].reshape(-1, D)
        xn_ref[...] = _rms(x, gain_ref[...]).astype(_BF16)
        o_ref[...] = x.reshape(o_ref.shape)

    av = jnp.dot(a_ref[...].reshape(-1, Q_DIM), wpa_ref[...], preferred_element_type=_F32)
    fv = jnp.dot(f_ref[...].reshape(-1, FOURIER_DIM), wpf_ref[...], preferred_element_type=_F32)
    ga = jax.nn.sigmoid(jnp.dot(xn_ref[...], wga_ref[...], preferred_element_type=_F32))
    gf = jax.nn.sigmoid(jnp.dot(xn_ref[...], wgf_ref[...], preferred_element_type=_F32))
    mixed = (ga * av + gf * fv).astype(_BF16)
    o_ref[...] += jnp.dot(mixed, wo_ref[...], preferred_element_type=_F32).reshape(o_ref.shape)


def _mix_out(rows, h, gain, attn, four, wpa, wpf, wg, wo, *, tn, prev_out=None):
    B, T, D = h.shape
    nj = D // tn
    in_specs = [rows.spec(D), _const_spec((1, D)), rows.spec(Q_DIM), rows.spec(FOURIER_DIM),
                pl.BlockSpec((Q_DIM, tn), lambda b, i, j: (0, j)),
                pl.BlockSpec((FOURIER_DIM, tn), lambda b, i, j: (0, j)),
                pl.BlockSpec((D, tn), lambda b, i, j: (0, j)),
                pl.BlockSpec((D, tn), lambda b, i, j: (0, nj + j)),
                pl.BlockSpec((tn, D), lambda b, i, j: (j, 0))]
    args = [h, gain, attn, four, wpa, wpf, wg, wg, wo]
    aliases = {}
    if prev_out is not None:
        extra, aliases = _alias_tail(len(args), [prev_out])
        in_specs += extra
        args.append(prev_out)
    m = rows.m
    vmem = (4 * m * D * 4 + m * D * 2 + 2 * m * (Q_DIM + FOURIER_DIM) * 2
            + 2 * (Q_DIM * tn + FOURIER_DIM * tn + D * 2 * tn + tn * D) * 2 + 8 * m * tn * 4 + (4 << 20))
    return pl.pallas_call(
        functools.partial(_mix_body, n_alias=len(aliases)),
        grid=rows.grid + (nj,),
        in_specs=in_specs,
        out_specs=rows.spec(D),
        out_shape=jax.ShapeDtypeStruct((B, T, D), _F32),
        scratch_shapes=[pltpu.VMEM((m, D), _BF16)],
        input_output_aliases=aliases,
        compiler_params=_cparams(("parallel", "parallel", "arbitrary"), vmem),
        name="mixer_out" + ("_meta" if rows.meta else ""),
    )(*args)


def _rope_tables(n_real, n_meta):
    T = n_real + n_meta
    half = ROT_DIM // 2
    inv = 1.0 / (ROPE_THETA ** (jnp.arange(0, ROT_DIM, 2, dtype=_F32) / ROT_DIM))
    ang = jnp.arange(T, dtype=_F32)[:, None] * inv[None, :]
    cos, sin = jnp.cos(ang), jnp.sin(ang)
    cos = jnp.concatenate([cos[n_meta:], cos[:n_meta]], axis=0)
    sin = jnp.concatenate([sin[n_meta:], sin[:n_meta]], axis=0)
    ones = jnp.ones((T, HEAD_DIM - ROT_DIM), _F32)
    zeros_h = jnp.zeros((T, half), _F32)
    zeros_r = jnp.zeros((T, HEAD_DIM - ROT_DIM), _F32)
    c = jnp.concatenate([cos, cos, ones], axis=1)
    s1 = jnp.concatenate([-sin, zeros_h, zeros_r], axis=1)
    s2 = jnp.concatenate([zeros_h, sin, zeros_r], axis=1)
    return c, s1, s2


def _channel_dft_table():
    n = FOURIER_GROUP_DIM
    c = jnp.arange(n, dtype=jnp.int32)[:, None]
    k = jnp.arange(n, dtype=jnp.int32)[None, :]
    ang = ((c * k) % n).astype(_F32) * (2.0 * np.pi / n)
    return (jnp.concatenate([jnp.cos(ang), -jnp.sin(ang)], axis=1) * (n ** -0.5)).astype(_BF16)


def _pick_tile(n, pref):
    t = pref
    while n % t:
        t //= 2
    return t


def kernel(x, meta_tokens, ffn1_norm, ffn1_w_gate, ffn1_w_up, ffn1_w_down, mix_norm, w_in, w_gate, sink,
           w_attn_branch, w_fourier_branch, w_out, ffn2_norm, ffn2_w_gate, ffn2_w_up, ffn2_w_down, final_norm):
    B, S, D = x.shape
    n_meta = meta_tokens.shape[0]
    depth = ffn1_norm.shape[0]
    d_ff = ffn1_w_gate.shape[-1]
    T = S + n_meta
    assert S % BLOCK == 0 and S % n_meta == 0 and BLOCK % n_meta == 0
    assert w_in.shape[-1] == IN_DIM and KV_DIM == PROJ_CHUNK

    tm_ffn = _pick_tile(S, 1024)
    tm = _pick_tile(S, 512)
    tf = _pick_tile(d_ff, 512)
    tn = _pick_tile(D, 512)
    qb = _pick_tile(S // BLOCK, 4)
    rows_ffn = _Rows(B, S, n_meta, tm_ffn, meta=False)
    rows_main = _Rows(B, S, n_meta, tm, meta=False)
    rows_meta = _Rows(B, S, n_meta, tm, meta=True)

    n1, n2 = _fft_factors(T)
    fft_tables = _fft_tables(T, n1, n2, n_meta, BF16_SUBLANES)
    rope_c, rope_s1, rope_s2 = _rope_tables(S, n_meta)
    dft_c = _channel_dft_table()

    def ffn(h, h_meta, gain, w_gate_l, w_up_l, w_down_l, final_gain=None):
        wts = (w_gate_l.astype(_BF16), w_up_l.astype(_BF16), w_down_l.astype(_BF16))
        out = _ffn(rows_ffn, h, gain, *wts, tf=tf, T=T, final_gain=final_gain)
        if final_gain is None:
            src, blk0 = h_meta
            out = _ffn(rows_meta, src, gain, *wts, tf=tf, T=T, h_row_block0=blk0, prev_out=out)
        return out

    meta = jnp.broadcast_to(meta_tokens[None].astype(x.dtype), (B, n_meta, D))
    h = None
    for l in range(depth):
        if l == 0:
            h = ffn(x, (meta, 0), ffn1_norm[l][None], ffn1_w_gate[l], ffn1_w_up[l], ffn1_w_down[l])
        else:
            h = ffn(h, (h, None), ffn1_norm[l][None], ffn1_w_gate[l], ffn1_w_up[l], ffn1_w_down[l])

        gain = mix_norm[l][None]
        w_in_l = w_in[l].astype(_BF16)
        proj_args = (gain, w_in_l, rope_c, rope_s1, rope_s2, dft_c)
        qkv, z = _proj(rows_main, h, *proj_args)
        qkv, z = _proj(rows_meta, h, *proj_args, prev_out=(qkv, z))
        attn = _attention(qkv, sink[l], n_real=S, n_meta=n_meta, qb=qb)
        four = _position_dft(z, fft_tables, n1=n1, n2=n2, R=BF16_SUBLANES, n_meta=n_meta)
        mix_args = (gain, attn, four, w_attn_branch[l].astype(_BF16), w_fourier_branch[l].astype(_BF16),
                    w_gate[l].astype(_BF16), w_out[l].astype(_BF16))
        h2 = _mix_out(rows_main, h, *mix_args, tn=tn)
        h = _mix_out(rows_meta, h, *mix_args, tn=tn, prev_out=h2)

        last = l == depth - 1
        h = ffn(h, (h, None), ffn2_norm[l][None], ffn2_w_gate[l], ffn2_w_up[l], ffn2_w_down[l],
                final_gain=final_norm[None] if last else None)
    return h
```

```python
import functools

import numpy as np
import jax
import jax.numpy as jnp
from jax import lax
from jax.experimental import pallas as pl
from jax.experimental.pallas import tpu as pltpu

_F32 = jnp.float32
_BF16 = jnp.bfloat16

HEAD_DIM = 128
N_Q_HEADS = 16
N_KV_HEADS = 4
Q_PER_KV = N_Q_HEADS // N_KV_HEADS
WINDOW = 128
BLOCK = 128
ROPE_THETA = 500000.0
ROT_DIM = HEAD_DIM // 4
N_FOURIER_GROUPS = 8
FOURIER_GROUP_DIM = 128
FOURIER_DIM = N_FOURIER_GROUPS * FOURIER_GROUP_DIM
Q_DIM = N_Q_HEADS * HEAD_DIM
KV_DIM = N_KV_HEADS * HEAD_DIM
IN_DIM = Q_DIM + 2 * KV_DIM + FOURIER_DIM
RMS_EPS = 1e-6
NEG_INF = -1e30

LANES = 128
BF16_SUBLANES = 16
PROJ_CHUNK = Q_PER_KV * HEAD_DIM
V7X_VMEM_BYTES = 64 * 1024 * 1024


def _round_up(a, m):
    return (a + m - 1) // m * m


def _cparams(sem, vmem_bytes):
    return pltpu.CompilerParams(dimension_semantics=sem,
                                vmem_limit_bytes=int(min(vmem_bytes, V7X_VMEM_BYTES - (4 << 20))))


def _rms(x, g):
    return x * lax.rsqrt(jnp.mean(x * x, axis=-1, keepdims=True) + RMS_EPS) * g


class _Rows:
    def __init__(self, B, n_real, n_meta, tm, meta):
        self.B, self.n_real, self.n_meta, self.meta = B, n_real, n_meta, meta
        self.nb = B if meta else 1
        self.rows = n_meta if meta else tm
        self.grid = (1, 1) if meta else (B, n_real // tm)
        self.m = self.nb * self.rows

    def spec(self, width, lead=(), row_block0=None):
        nl = len(lead)
        if self.meta:
            r0 = self.n_real // self.n_meta if row_block0 is None else row_block0
            return pl.BlockSpec((self.nb,) + tuple(lead) + (self.rows, width),
                                lambda b, i, *_: (0,) + (0,) * nl + (r0, 0))
        return pl.BlockSpec((1,) + tuple(lead) + (self.rows, width),
                            lambda b, i, *_: (b,) + (0,) * nl + (i, 0))

    def table_spec(self, width):
        if self.meta:
            r0 = self.n_real // self.n_meta
            return pl.BlockSpec((self.rows, width), lambda b, i, *_: (r0, 0))
        return pl.BlockSpec((self.rows, width), lambda b, i, *_: (i, 0))


def _const_spec(shape, single_buffer=False):
    nd = len(shape)
    if single_buffer:
        return pl.BlockSpec(shape, lambda *_: (0,) * nd, pipeline_mode=pl.Buffered(1))
    return pl.BlockSpec(shape, lambda *_: (0,) * nd)


def _alias_tail(n_in, outs):
    specs = [pl.BlockSpec(memory_space=pl.ANY) for _ in outs]
    return specs, {n_in + k: k for k in range(len(outs))}


def _cast_body(w_ref, o_ref):
    o_ref[...] = w_ref[...].astype(_BF16)


def _to_bf16(w):
    L, K, N = w.shape
    tk = _pick_tile(K, 512)
    while tk > 8 and tk * N * 4 > (6 << 20):
        tk //= 2
    return pl.pallas_call(
        _cast_body,
        grid=(L, K // tk),
        in_specs=[pl.BlockSpec((1, tk, N), lambda l, i: (l, i, 0))],
        out_specs=pl.BlockSpec((1, tk, N), lambda l, i: (l, i, 0)),
        out_shape=jax.ShapeDtypeStruct(w.shape, _BF16),
        compiler_params=_cparams(("parallel", "parallel"), 6 * tk * N * 2 + (4 << 20)),
        name="cast_bf16",
    )(w)


def _ffn_body(*refs, nj, final, n_alias):
    h_ref, gain_ref, wg_ref, wu_ref, wd_ref = refs[:5]
    rest = refs[5:]
    fgain_ref = None
    if final:
        fgain_ref, rest = rest[0], rest[1:]
    o_ref, xn_ref = rest[n_alias:]
    j = pl.program_id(2)
    D = h_ref.shape[-1]

    @pl.when(j == 0)
    def _():
        x = h_ref[...].reshape(-1, D)
        xn_ref[...] = _rms(x, gain_ref[...]).astype(_BF16)
        o_ref[...] = x.reshape(o_ref.shape)

    g = jnp.dot(xn_ref[...], wg_ref[...], preferred_element_type=_F32)
    u = jnp.dot(xn_ref[...], wu_ref[...], preferred_element_type=_F32)
    act = ((g * jax.nn.sigmoid(g)) * u * 0.5).astype(_BF16)
    o_ref[...] += jnp.dot(act, wd_ref[...], preferred_element_type=_F32).reshape(o_ref.shape)

    if final:
        @pl.when(j == nj - 1)
        def _():
            o_ref[...] = _rms(o_ref[...].reshape(-1, D), fgain_ref[...]).reshape(o_ref.shape)


def _ffn(rows, h, gain, wg, wu, wd, *, layer, tf, T, final_gain=None, h_row_block0=None, prev_out=None):
    D = h.shape[-1]
    nj = wg.shape[2] // tf
    final = final_gain is not None
    in_specs = [rows.spec(D, row_block0=h_row_block0), _const_spec((1, D)),
                pl.BlockSpec((None, D, tf), lambda b, i, j: (layer, 0, j)),
                pl.BlockSpec((None, D, tf), lambda b, i, j: (layer, 0, j)),
                pl.BlockSpec((None, tf, D), lambda b, i, j: (layer, j, 0))]
    args = [h, gain, wg, wu, wd]
    if final:
        in_specs.append(_const_spec((1, D)))
        args.append(final_gain)
    aliases = {}
    if prev_out is not None:
        extra, aliases = _alias_tail(len(args), [prev_out])
        in_specs += extra
        args.append(prev_out)
    m = rows.m
    vmem = (4 * m * D * 4 + m * D * 2 + 2 * (D * 2 * tf + tf * D) * 2 + m * 2 * tf * 4 + m * tf * 2 + (4 << 20))
    out_rows = rows.n_real if final else T
    return pl.pallas_call(
        functools.partial(_ffn_body, nj=nj, final=final, n_alias=len(aliases)),
        grid=rows.grid + (nj,),
        in_specs=in_specs,
        out_specs=rows.spec(D),
        out_shape=jax.ShapeDtypeStruct((rows.B, out_rows, D), _F32),
        scratch_shapes=[pltpu.VMEM((m, D), _BF16)],
        input_output_aliases=aliases,
        compiler_params=_cparams(("parallel", "parallel", "arbitrary"), vmem),
        name=("ffn_final" if final else "ffn") + ("_meta" if rows.meta else ""),
    )(*args)


def _proj_body(*refs, n_rope, n_qkv, n_chunks, nb, n_alias):
    h_ref, gain_ref, w_ref, c_ref, s1_ref, s2_ref, dft_ref = refs[:7]
    qkv_ref, z_ref, xn_ref = refs[7 + n_alias:]
    D = h_ref.shape[-1]
    rows = h_ref.shape[-2]
    xn_ref[...] = _rms(h_ref[...].reshape(-1, D), gain_ref[...]).astype(_BF16)
    c = jnp.concatenate([c_ref[...]] * nb, axis=0)
    s1 = jnp.concatenate([s1_ref[...]] * nb, axis=0)
    s2 = jnp.concatenate([s2_ref[...]] * nb, axis=0)
    half = ROT_DIM // 2
    for ch in range(n_chunks):
        cols = slice(ch * PROJ_CHUNK, (ch + 1) * PROJ_CHUNK)
        y = jnp.dot(xn_ref[...], w_ref[:, cols], preferred_element_type=_F32)
        if ch < n_rope:
            for hh in range(PROJ_CHUNK // HEAD_DIM):
                xh = y[:, hh * HEAD_DIM:(hh + 1) * HEAD_DIM]
                up = pltpu.roll(xh, HEAD_DIM - half, 1)
                dn = pltpu.roll(xh, half, 1)
                r = (xh * c + up * s1 + dn * s2).astype(_BF16)
                lo = ch * PROJ_CHUNK + hh * HEAD_DIM
                qkv_ref[:, :, lo:lo + HEAD_DIM] = r.reshape(nb, rows, HEAD_DIM)
        elif ch < n_qkv:
            qkv_ref[:, :, cols] = y.astype(_BF16).reshape(nb, rows, PROJ_CHUNK)
        else:
            ub = y.astype(_BF16)
            for gg in range(PROJ_CHUNK // FOURIER_GROUP_DIM):
                zz = jnp.dot(ub[:, gg * FOURIER_GROUP_DIM:(gg + 1) * FOURIER_GROUP_DIM], dft_ref[...],
                             preferred_element_type=_F32)
                lo = (ch - n_qkv) * PROJ_CHUNK + gg * FOURIER_GROUP_DIM
                z_ref[:, 0, :, lo:lo + FOURIER_GROUP_DIM] = (
                    zz[:, :FOURIER_GROUP_DIM].astype(_BF16).reshape(nb, rows, FOURIER_GROUP_DIM))
                z_ref[:, 1, :, lo:lo + FOURIER_GROUP_DIM] = (
                    zz[:, FOURIER_GROUP_DIM:].astype(_BF16).reshape(nb, rows, FOURIER_GROUP_DIM))


def _proj(rows, h, gain, w_in, rope_c, rope_s1, rope_s2, dft_c, *, layer, prev_out=None):
    B, T, D = h.shape
    n_rope = (Q_DIM + KV_DIM) // PROJ_CHUNK
    n_qkv = (Q_DIM + 2 * KV_DIM) // PROJ_CHUNK
    n_chunks = IN_DIM // PROJ_CHUNK
    w_spec = pl.BlockSpec((None, D, IN_DIM), lambda *_: (layer, 0, 0), pipeline_mode=pl.Buffered(1))
    in_specs = [rows.spec(D), _const_spec((1, D)), w_spec,
                rows.table_spec(LANES), rows.table_spec(LANES), rows.table_spec(LANES),
                _const_spec((FOURIER_GROUP_DIM, 2 * FOURIER_GROUP_DIM))]
    args = [h, gain, w_in, rope_c, rope_s1, rope_s2, dft_c]
    aliases = {}
    if prev_out is not None:
        extra, aliases = _alias_tail(len(args), prev_out)
        in_specs += extra
        args += list(prev_out)
    m = rows.m
    vmem = (2 * m * D * 4 + m * D * 2 + D * IN_DIM * 2 + 6 * m * LANES * 4
            + 2 * m * (Q_DIM + 2 * KV_DIM) * 2 + 4 * m * FOURIER_DIM * 2 + 8 * m * PROJ_CHUNK * 4 + (6 << 20))
    return pl.pallas_call(
        functools.partial(_proj_body, n_rope=n_rope, n_qkv=n_qkv, n_chunks=n_chunks, nb=rows.nb,
                          n_alias=len(aliases)),
        grid=rows.grid,
        in_specs=in_specs,
        out_specs=[rows.spec(Q_DIM + 2 * KV_DIM), rows.spec(FOURIER_DIM, lead=(2,))],
        out_shape=[jax.ShapeDtypeStruct((B, T, Q_DIM + 2 * KV_DIM), _BF16),
                   jax.ShapeDtypeStruct((B, 2, T, FOURIER_DIM), _BF16)],
        scratch_shapes=[pltpu.VMEM((m, D), _BF16)],
        input_output_aliases=aliases,
        compiler_params=_cparams(("parallel", "arbitrary"), vmem),
        name="mixer_proj" + ("_meta" if rows.meta else ""),
    )(*args)


def _stack_heads(q):
    return jnp.concatenate([q[:, h * HEAD_DIM:(h + 1) * HEAD_DIM] for h in range(Q_PER_KV)], axis=0)


def _sink_column(sink_ref, g, rows_per_head):
    return jnp.concatenate(
        [jnp.full((rows_per_head, 1), sink_ref[g * Q_PER_KV + h], _F32) for h in range(Q_PER_KV)], axis=0)


def _pad_rows(a, rows):
    return jnp.concatenate([a, jnp.zeros((rows - a.shape[0], a.shape[1]), a.dtype)], axis=0)


def _softmax_pv(q4, k_all, v_all, masks, sink):
    rows = q4.shape[0]
    s = lax.dot_general(q4, k_all, (((1,), (1,)), ((), ())), preferred_element_type=_F32) * (HEAD_DIM ** -0.5)
    blocks = []
    for i, mk in enumerate(masks):
        sb = s[:, i * BLOCK:(i + 1) * BLOCK]
        blocks.append(sb if mk is None else jnp.where(mk, sb, NEG_INF))
    mx = jnp.maximum(jnp.max(functools.reduce(jnp.maximum, blocks), axis=-1, keepdims=True), sink)
    p = jnp.concatenate([jnp.exp(sb - mx).astype(_BF16) for sb in blocks], axis=1)
    v_ext = jnp.concatenate([v_all, jnp.ones(v_all.shape, _BF16)], axis=1)
    o = jnp.dot(p, v_ext, preferred_element_type=_F32)
    den = o[:, HEAD_DIM:] + jnp.exp(sink - mx)
    return o[:, :HEAD_DIM] / den


def _attn_body(sink_ref, q_ref, kp_ref, kc_ref, kn_ref, km_ref, vp_ref, vc_ref, vn_ref, vm_ref, o_ref,
               *, qb, ntile, n_meta):
    g = pl.program_id(1)
    jt = pl.program_id(2)
    rows = Q_PER_KV * BLOCK
    row = lax.broadcasted_iota(jnp.int32, (rows, BLOCK), 0) % BLOCK
    col = lax.broadcasted_iota(jnp.int32, (rows, BLOCK), 1)
    diff = col - row
    never = 2 * BLOCK
    sink = _sink_column(sink_ref, g, BLOCK)
    k_meta = _pad_rows(km_ref[0], BLOCK)
    v_meta = _pad_rows(vm_ref[0], BLOCK)
    meta_ok = col < n_meta

    def band(prev_ref, cur_ref, next_ref, meta, qi):
        lo, hi = (qi - 1) * BLOCK, (qi + 2) * BLOCK
        if 0 <= lo and hi <= qb * BLOCK:
            return jnp.concatenate([cur_ref[0, lo:hi], meta], axis=0)
        parts = [prev_ref[0] if qi == 0 else cur_ref[0, lo:lo + BLOCK],
                 cur_ref[0, qi * BLOCK:(qi + 1) * BLOCK],
                 next_ref[0] if qi == qb - 1 else cur_ref[0, hi - BLOCK:hi],
                 meta]
        return jnp.concatenate(parts, axis=0)

    for qi in range(qb):
        q4 = _stack_heads(q_ref[0, qi * BLOCK:(qi + 1) * BLOCK, :])
        off_p = jnp.where(jt >= 1, 0, never) if qi == 0 else 0
        off_n = jnp.where(jt <= ntile - 2, 0, -never) if qi == qb - 1 else 0
        masks = [diff >= off_p, None, diff <= off_n, meta_ok]
        out = _softmax_pv(q4, band(kp_ref, kc_ref, kn_ref, k_meta, qi), band(vp_ref, vc_ref, vn_ref, v_meta, qi),
                          masks, sink)
        for h in range(Q_PER_KV):
            o_ref[0, qi * BLOCK:(qi + 1) * BLOCK, h * HEAD_DIM:(h + 1) * HEAD_DIM] = (
                out[h * BLOCK:(h + 1) * BLOCK].astype(_BF16))


def _attn_meta_body(sink_ref, q_ref, k0_ref, km_ref, v0_ref, vm_ref, prev_out_ref, o_ref, *, n_meta):
    del prev_out_ref
    g = pl.program_id(1)
    rows = Q_PER_KV * n_meta
    row = lax.broadcasted_iota(jnp.int32, (rows, BLOCK), 0) % n_meta
    col = lax.broadcasted_iota(jnp.int32, (rows, BLOCK), 1)
    q4 = _stack_heads(q_ref[0])
    masks = [col <= row + (WINDOW - n_meta), col < n_meta]
    k_all = jnp.concatenate([k0_ref[0], _pad_rows(km_ref[0], BLOCK)], axis=0)
    v_all = jnp.concatenate([v0_ref[0], _pad_rows(vm_ref[0], BLOCK)], axis=0)
    out = _softmax_pv(q4, k_all, v_all, masks, _sink_column(sink_ref, g, n_meta))
    for h in range(Q_PER_KV):
        o_ref[0, :, h * HEAD_DIM:(h + 1) * HEAD_DIM] = out[h * n_meta:(h + 1) * n_meta].astype(_BF16)


def _attention(qkv, sink, *, n_real, n_meta, qb):
    B, T, _ = qkv.shape
    nblk = n_real // BLOCK
    ntile = nblk // qb
    kcol = Q_DIM // HEAD_DIM
    vcol = (Q_DIM + KV_DIM) // HEAD_DIM
    mrow = n_real // n_meta

    def edge_spec(rowf, col0):
        return pl.BlockSpec((1, BLOCK, HEAD_DIM), lambda b, g, j, s: (b, rowf(j), col0 + g))

    def main_spec(col0):
        return pl.BlockSpec((1, qb * BLOCK, HEAD_DIM), lambda b, g, j, s: (b, j, col0 + g))

    def meta_spec(col0):
        return pl.BlockSpec((1, n_meta, HEAD_DIM), lambda b, g, j, s: (b, mrow, col0 + g))

    def prev_i(j):
        return jnp.maximum(qb * j - 1, 0)

    def next_i(j):
        return jnp.minimum(qb * j + qb, nblk - 1)

    attn = pl.pallas_call(
        functools.partial(_attn_body, qb=qb, ntile=ntile, n_meta=n_meta),
        grid_spec=pltpu.PrefetchScalarGridSpec(
            num_scalar_prefetch=1,
            grid=(B, N_KV_HEADS, ntile),
            in_specs=[
                pl.BlockSpec((1, qb * BLOCK, PROJ_CHUNK), lambda b, g, j, s: (b, j, g)),
                edge_spec(prev_i, kcol), main_spec(kcol), edge_spec(next_i, kcol), meta_spec(kcol),
                edge_spec(prev_i, vcol), main_spec(vcol), edge_spec(next_i, vcol), meta_spec(vcol),
            ],
            out_specs=pl.BlockSpec((1, qb * BLOCK, PROJ_CHUNK), lambda b, g, j, s: (b, j, g)),
        ),
        out_shape=jax.ShapeDtypeStruct((B, T, Q_DIM), _BF16),
        compiler_params=_cparams(("parallel", "parallel", "arbitrary"), 32 << 20),
        name="windowed_gqa",
    )(sink, qkv, qkv, qkv, qkv, qkv, qkv, qkv, qkv, qkv)

    def blk0_spec(col0):
        return pl.BlockSpec((1, BLOCK, HEAD_DIM), lambda b, g, s: (b, 0, col0 + g))

    def meta2_spec(col0):
        return pl.BlockSpec((1, n_meta, HEAD_DIM), lambda b, g, s: (b, mrow, col0 + g))

    return pl.pallas_call(
        functools.partial(_attn_meta_body, n_meta=n_meta),
        grid_spec=pltpu.PrefetchScalarGridSpec(
            num_scalar_prefetch=1,
            grid=(B, N_KV_HEADS),
            in_specs=[
                pl.BlockSpec((1, n_meta, PROJ_CHUNK), lambda b, g, s: (b, mrow, g)),
                blk0_spec(kcol), meta2_spec(kcol), blk0_spec(vcol), meta2_spec(vcol),
                pl.BlockSpec(memory_space=pl.ANY),
            ],
            out_specs=pl.BlockSpec((1, n_meta, PROJ_CHUNK), lambda b, g, s: (b, mrow, g)),
        ),
        out_shape=jax.ShapeDtypeStruct((B, T, Q_DIM), _BF16),
        input_output_aliases={6: 0},
        compiler_params=_cparams(("parallel", "arbitrary"), 32 << 20),
        name="windowed_gqa_meta",
    )(sink, qkv, qkv, qkv, qkv, qkv, attn)


def _fft_factors(T):
    best = None
    for n2 in range(16, T + 1, 16):
        if T % n2:
            continue
        n1 = T // n2
        cost = 4 * n1 + 2 * n2
        if best is None or cost < best[0]:
            best = (cost, n1, n2)
    assert best is not None, "sequence length must be a multiple of 16"
    return best[1], best[2]


def _fft1_body(z_ref, mc_ref, ms_ref, a_ref, *, n1, R, C):
    rows = n1 * R
    rhs = jnp.concatenate([z_ref[0, 0].reshape(rows, C), z_ref[0, 1].reshape(rows, C)], axis=1)
    xc = jnp.dot(mc_ref[0], rhs, preferred_element_type=_F32)
    xs = jnp.dot(ms_ref[0], rhs, preferred_element_type=_F32)
    a_ref[0, 0] = (xc[:, :C] + xs[:, C:]).reshape(n1, R, C).astype(_BF16)
    a_ref[0, 1] = (xc[:, C:] - xs[:, :C]).reshape(n1, R, C).astype(_BF16)


def _fft2_body(ar_ref, ai_ref, mc_ref, ms_ref, f_ref):
    res = (jnp.dot(mc_ref[0], ar_ref[0, 0, 0], preferred_element_type=_F32)
           + jnp.dot(ms_ref[0], ai_ref[0, 0, 0], preferred_element_type=_F32))
    f_ref[0] = res.astype(_BF16)


def _fft_tables(T, n1, n2, n_meta, R):
    two_pi = 2.0 * np.pi
    k1 = jnp.arange(n1, dtype=jnp.int32)[None, :, None]
    nn1 = jnp.arange(n1, dtype=jnp.int32)[None, None, :]
    nn2 = jnp.arange(n2, dtype=jnp.int32)[:, None, None]
    ph = (k1 * (nn2 + n_meta + n2 * nn1)) % T
    ang = ph.astype(_F32) * (two_pi / T)
    c = jnp.cos(ang) * (n1 ** -0.5)
    s = jnp.sin(ang) * (n1 ** -0.5)
    eye = jnp.eye(R, dtype=_F32)

    def blockdiag(a):
        a4 = a.reshape(n2 // R, R, n1, n1)
        return jnp.einsum("jrkn,rs->jkrns", a4, eye).reshape(n2 // R, n1 * R, n1 * R).astype(_BF16)

    m1c, m1s = blockdiag(c), blockdiag(s)
    k2 = jnp.arange(n2, dtype=jnp.int32)[:, None]
    mm2 = jnp.arange(n2, dtype=jnp.int32)[None, :]
    outs_c, outs_s = [], []
    for rot in (n_meta // n1, n_meta // n1 + 1):
        ph2 = (((k2 + rot) % n2) * (mm2 + n_meta)) % n2
        ang2 = ph2.astype(_F32) * (two_pi / n2)
        outs_c.append(jnp.cos(ang2) * (n2 ** -0.5))
        outs_s.append(jnp.sin(ang2) * (n2 ** -0.5))
    return m1c, m1s, jnp.stack(outs_c).astype(_BF16), jnp.stack(outs_s).astype(_BF16)


def _position_dft(z, tables, *, n1, n2, R, n_meta):
    B, _, T, C = z.shape
    m1c, m1s, m2c, m2s = tables
    rows = n1 * R
    zv = z.reshape(B, 2, n1, n2, C)
    av = pl.pallas_call(
        functools.partial(_fft1_body, n1=n1, R=R, C=C),
        grid=(B, n2 // R),
        in_specs=[
            pl.BlockSpec((1, 2, n1, R, C), lambda b, t: (b, 0, 0, t, 0)),
            pl.BlockSpec((1, rows, rows), lambda b, t: (t, 0, 0)),
            pl.BlockSpec((1, rows, rows), lambda b, t: (t, 0, 0)),
        ],
        out_specs=pl.BlockSpec((1, 2, n1, R, C), lambda b, t: (b, 0, 0, t, 0)),
        out_shape=jax.ShapeDtypeStruct((B, 2, n1, n2, C), _BF16),
        compiler_params=_cparams(("parallel", "arbitrary"), 48 << 20),
        name="dft_stage1",
    )(zv, m1c, m1s)
    shift = n1 - (n_meta % n1)

    f = pl.pallas_call(
        _fft2_body,
        grid=(B, n1),
        in_specs=[
            pl.BlockSpec((1, 1, 1, n2, C), lambda b, k: (b, 0, k, 0, 0)),
            pl.BlockSpec((1, 1, 1, n2, C), lambda b, k: (b, 1, k, 0, 0)),
            pl.BlockSpec((1, n2, n2), lambda b, k: (jnp.where(k < n_meta % n1, 1, 0), 0, 0)),
            pl.BlockSpec((1, n2, n2), lambda b, k: (jnp.where(k < n_meta % n1, 1, 0), 0, 0)),
        ],
        out_specs=pl.BlockSpec((1, n2, C), lambda b, k: (b, 0, (k + shift) % n1)),
        out_shape=jax.ShapeDtypeStruct((B, n2, n1 * C), _BF16),
        compiler_params=_cparams(("parallel", "arbitrary"), 32 << 20),
        name="dft_stage2",
    )(av, av, m2c, m2s)
    return f.reshape(B, T, C)


def _mix_body(*refs, n_alias):
    h_ref, gain_ref, a_ref, f_ref, wpa_ref, wpf_ref, wga_ref, wgf_ref, wo_ref = refs[:9]
    o_ref, xn_ref = refs[9 + n_alias:]
    j = pl.program_id(2)
    D = h_ref.shape[-1]

    @pl.when(j == 0)
    def _():
        x = h_ref[...].reshape(-1, D)
        xn_ref[...] = _rms(x, gain_ref[...]).astype(_BF16)
        o_ref[...] = x.reshape(o_ref.shape)

    av = jnp.dot(a_ref[...].reshape(-1, Q_DIM), wpa_ref[...], preferred_element_type=_F32)
    fv = jnp.dot(f_ref[...].reshape(-1, FOURIER_DIM), wpf_ref[...], preferred_element_type=_F32)
    ga = jax.nn.sigmoid(jnp.dot(xn_ref[...], wga_ref[...], preferred_element_type=_F32))
    gf = jax.nn.sigmoid(jnp.dot(xn_ref[...], wgf_ref[...], preferred_element_type=_F32))
    mixed = (ga * av + gf * fv).astype(_BF16)
    o_ref[...] += jnp.dot(mixed, wo_ref[...], preferred_element_type=_F32).reshape(o_ref.shape)


def _mix_out(rows, h, gain, attn, four, wpa, wpf, wg, wo, *, layer, tn, prev_out=None):
    B, T, D = h.shape
    nj = D // tn
    in_specs = [rows.spec(D), _const_spec((1, D)), rows.spec(Q_DIM), rows.spec(FOURIER_DIM),
                pl.BlockSpec((None, Q_DIM, tn), lambda b, i, j: (layer, 0, j)),
                pl.BlockSpec((None, FOURIER_DIM, tn), lambda b, i, j: (layer, 0, j)),
                pl.BlockSpec((None, D, tn), lambda b, i, j: (layer, 0, j)),
                pl.BlockSpec((None, D, tn), lambda b, i, j: (layer, 0, nj + j)),
                pl.BlockSpec((None, tn, D), lambda b, i, j: (layer, j, 0))]
    args = [h, gain, attn, four, wpa, wpf, wg, wg, wo]
    aliases = {}
    if prev_out is not None:
        extra, aliases = _alias_tail(len(args), [prev_out])
        in_specs += extra
        args.append(prev_out)
    m = rows.m
    vmem = (4 * m * D * 4 + m * D * 2 + 2 * m * (Q_DIM + FOURIER_DIM) * 2
            + 2 * (Q_DIM * tn + FOURIER_DIM * tn + D * 2 * tn + tn * D) * 2 + 8 * m * tn * 4 + (4 << 20))
    return pl.pallas_call(
        functools.partial(_mix_body, n_alias=len(aliases)),
        grid=rows.grid + (nj,),
        in_specs=in_specs,
        out_specs=rows.spec(D),
        out_shape=jax.ShapeDtypeStruct((B, T, D), _F32),
        scratch_shapes=[pltpu.VMEM((m, D), _BF16)],
        input_output_aliases=aliases,
        compiler_params=_cparams(("parallel", "parallel", "arbitrary"), vmem),
        name="mixer_out" + ("_meta" if rows.meta else ""),
    )(*args)


def _rope_tables(n_real, n_meta):
    T = n_real + n_meta
    half = ROT_DIM // 2
    inv = 1.0 / (ROPE_THETA ** (jnp.arange(0, ROT_DIM, 2, dtype=_F32) / ROT_DIM))
    ang = jnp.arange(T, dtype=_F32)[:, None] * inv[None, :]
    cos, sin = jnp.cos(ang), jnp.sin(ang)
    cos = jnp.concatenate([cos[n_meta:], cos[:n_meta]], axis=0)
    sin = jnp.concatenate([sin[n_meta:], sin[:n_meta]], axis=0)
    ones = jnp.ones((T, HEAD_DIM - ROT_DIM), _F32)
    zeros_h = jnp.zeros((T, half), _F32)
    zeros_r = jnp.zeros((T, HEAD_DIM - ROT_DIM), _F32)
    c = jnp.concatenate([cos, cos, ones], axis=1)
    s1 = jnp.concatenate([-sin, zeros_h, zeros_r], axis=1)
    s2 = jnp.concatenate([zeros_h, sin, zeros_r], axis=1)
    return c, s1, s2


def _channel_dft_table():
    n = FOURIER_GROUP_DIM
    c = jnp.arange(n, dtype=jnp.int32)[:, None]
    k = jnp.arange(n, dtype=jnp.int32)[None, :]
    ang = ((c * k) % n).astype(_F32) * (2.0 * np.pi / n)
    return (jnp.concatenate([jnp.cos(ang), -jnp.sin(ang)], axis=1) * (n ** -0.5)).astype(_BF16)


def _pick_tile(n, pref):
    t = pref
    while n % t:
        t //= 2
    return t


def kernel(x, meta_tokens, ffn1_norm, ffn1_w_gate, ffn1_w_up, ffn1_w_down, mix_norm, w_in, w_gate, sink,
           w_attn_branch, w_fourier_branch, w_out, ffn2_norm, ffn2_w_gate, ffn2_w_up, ffn2_w_down, final_norm):
    B, S, D = x.shape
    n_meta = meta_tokens.shape[0]
    depth = ffn1_norm.shape[0]
    d_ff = ffn1_w_gate.shape[-1]
    T = S + n_meta
    assert S % BLOCK == 0 and S % n_meta == 0 and BLOCK % n_meta == 0
    assert w_in.shape[-1] == IN_DIM and KV_DIM == PROJ_CHUNK

    tm_ffn = _pick_tile(S, 1024)
    tm = _pick_tile(S, 512)
    tf = _pick_tile(d_ff, 512)
    tn = _pick_tile(D, 512)
    qb = _pick_tile(S // BLOCK, 8)
    rows_ffn = _Rows(B, S, n_meta, tm_ffn, meta=False)
    rows_main = _Rows(B, S, n_meta, tm, meta=False)
    rows_meta = _Rows(B, S, n_meta, tm, meta=True)

    n1, n2 = _fft_factors(T)
    fft_tables = _fft_tables(T, n1, n2, n_meta, BF16_SUBLANES)
    rope_c, rope_s1, rope_s2 = _rope_tables(S, n_meta)
    dft_c = _channel_dft_table()

    ffn1_w = tuple(_to_bf16(w) for w in (ffn1_w_gate, ffn1_w_up, ffn1_w_down))
    ffn2_w = tuple(_to_bf16(w) for w in (ffn2_w_gate, ffn2_w_up, ffn2_w_down))
    w_in_b = _to_bf16(w_in)
    mix_w = tuple(_to_bf16(w) for w in (w_attn_branch, w_fourier_branch, w_gate, w_out))

    def ffn(h, h_meta, gain, wts, layer, final_gain=None):
        out = _ffn(rows_ffn, h, gain, *wts, layer=layer, tf=tf, T=T, final_gain=final_gain)
        if final_gain is None:
            src, blk0 = h_meta
            out = _ffn(rows_meta, src, gain, *wts, layer=layer, tf=tf, T=T, h_row_block0=blk0, prev_out=out)
        return out

    meta = jnp.broadcast_to(meta_tokens[None].astype(x.dtype), (B, n_meta, D))
    h = None
    for l in range(depth):
        if l == 0:
            h = ffn(x, (meta, 0), ffn1_norm[l][None], ffn1_w, l)
        else:
            h = ffn(h, (h, None), ffn1_norm[l][None], ffn1_w, l)

        gain = mix_norm[l][None]
        proj_args = (gain, w_in_b, rope_c, rope_s1, rope_s2, dft_c)
        qkv, z = _proj(rows_main, h, *proj_args, layer=l)
        qkv, z = _proj(rows_meta, h, *proj_args, layer=l, prev_out=(qkv, z))
        attn = _attention(qkv, sink[l], n_real=S, n_meta=n_meta, qb=qb)
        four = _position_dft(z, fft_tables, n1=n1, n2=n2, R=BF16_SUBLANES, n_meta=n_meta)
        mix_args = (gain, attn, four) + mix_w
        h2 = _mix_out(rows_main, h, *mix_args, layer=l, tn=tn)
        h = _mix_out(rows_meta, h, *mix_args, layer=l, tn=tn, prev_out=h2)

        last = l == depth - 1
        h = ffn(h, (h, None), ffn2_norm[l][None], ffn2_w, l, final_gain=final_norm[None] if last else None)
    return h
```

```python
import functools

import numpy as np
import jax
import jax.numpy as jnp
from jax import lax
from jax.experimental import pallas as pl
from jax.experimental.pallas import tpu as pltpu

_F32 = jnp.float32
_BF16 = jnp.bfloat16

HEAD_DIM = 128
N_Q_HEADS = 16
N_KV_HEADS = 4
Q_PER_KV = N_Q_HEADS // N_KV_HEADS
WINDOW = 128
BLOCK = 128
ROPE_THETA = 500000.0
ROT_DIM = HEAD_DIM // 4
N_FOURIER_GROUPS = 8
FOURIER_GROUP_DIM = 128
FOURIER_DIM = N_FOURIER_GROUPS * FOURIER_GROUP_DIM
Q_DIM = N_Q_HEADS * HEAD_DIM
KV_DIM = N_KV_HEADS * HEAD_DIM
IN_DIM = Q_DIM + 2 * KV_DIM + FOURIER_DIM
RMS_EPS = 1e-6
NEG_INF = -1e30

LANES = 128
BF16_SUBLANES = 16
PROJ_CHUNK = Q_PER_KV * HEAD_DIM
V7X_VMEM_BYTES = 64 * 1024 * 1024


def _round_up(a, m):
    return (a + m - 1) // m * m


def _cparams(sem, vmem_bytes):
    return pltpu.CompilerParams(dimension_semantics=sem,
                                vmem_limit_bytes=int(min(vmem_bytes, V7X_VMEM_BYTES - (4 << 20))))


def _rms(x, g):
    return x * lax.rsqrt(jnp.mean(x * x, axis=-1, keepdims=True) + RMS_EPS) * g


class _Rows:
    def __init__(self, B, n_real, n_meta, tm, meta):
        self.B, self.n_real, self.n_meta, self.meta = B, n_real, n_meta, meta
        self.nb = B if meta else 1
        self.rows = n_meta if meta else tm
        self.grid = (1, 1) if meta else (B, n_real // tm)
        self.m = self.nb * self.rows

    def spec(self, width, lead=(), row_block0=None):
        nl = len(lead)
        if self.meta:
            r0 = self.n_real // self.n_meta if row_block0 is None else row_block0
            return pl.BlockSpec((self.nb,) + tuple(lead) + (self.rows, width),
                                lambda b, i, *_: (0,) + (0,) * nl + (r0, 0))
        return pl.BlockSpec((1,) + tuple(lead) + (self.rows, width),
                            lambda b, i, *_: (b,) + (0,) * nl + (i, 0))

    def table_spec(self, width):
        if self.meta:
            r0 = self.n_real // self.n_meta
            return pl.BlockSpec((self.rows, width), lambda b, i, *_: (r0, 0))
        return pl.BlockSpec((self.rows, width), lambda b, i, *_: (i, 0))


def _const_spec(shape, single_buffer=False):
    nd = len(shape)
    if single_buffer:
        return pl.BlockSpec(shape, lambda *_: (0,) * nd, pipeline_mode=pl.Buffered(1))
    return pl.BlockSpec(shape, lambda *_: (0,) * nd)


def _alias_tail(n_in, outs):
    specs = [pl.BlockSpec(memory_space=pl.ANY) for _ in outs]
    return specs, {n_in + k: k for k in range(len(outs))}


def _cast_body(w_ref, o_ref):
    o_ref[...] = w_ref[...].astype(_BF16)


def _to_bf16(w):
    L, K, N = w.shape
    tk = _pick_tile(K, 512)
    while tk > 8 and tk * N * 4 > (6 << 20):
        tk //= 2
    return pl.pallas_call(
        _cast_body,
        grid=(L, K // tk),
        in_specs=[pl.BlockSpec((1, tk, N), lambda l, i: (l, i, 0))],
        out_specs=pl.BlockSpec((1, tk, N), lambda l, i: (l, i, 0)),
        out_shape=jax.ShapeDtypeStruct(w.shape, _BF16),
        compiler_params=_cparams(("parallel", "parallel"), 6 * tk * N * 2 + (4 << 20)),
        name="cast_bf16",
    )(w)


def _ffn_body(*refs, nj, final, n_alias):
    h_ref, gain_ref, wg_ref, wu_ref, wd_ref = refs[:5]
    rest = refs[5:]
    fgain_ref = None
    if final:
        fgain_ref, rest = rest[0], rest[1:]
    o_ref, xn_ref = rest[n_alias:]
    j = pl.program_id(2)
    D = h_ref.shape[-1]

    @pl.when(j == 0)
    def _():
        x = h_ref[...].reshape(-1, D)
        xn_ref[...] = _rms(x, gain_ref[...]).astype(_BF16)
        o_ref[...] = x.reshape(o_ref.shape)

    g = jnp.dot(xn_ref[...], wg_ref[...], preferred_element_type=_F32)
    u = jnp.dot(xn_ref[...], wu_ref[...], preferred_element_type=_F32)
    act = ((g * jax.nn.sigmoid(g)) * u * 0.5).astype(_BF16)
    o_ref[...] += jnp.dot(act, wd_ref[...], preferred_element_type=_F32).reshape(o_ref.shape)

    if final:
        @pl.when(j == nj - 1)
        def _():
            o_ref[...] = _rms(o_ref[...].reshape(-1, D), fgain_ref[...]).reshape(o_ref.shape)


def _ffn(rows, h, gain, wg, wu, wd, *, layer, tf, T, final_gain=None, h_row_block0=None, prev_out=None):
    D = h.shape[-1]
    nj = wg.shape[2] // tf
    final = final_gain is not None
    in_specs = [rows.spec(D, row_block0=h_row_block0), _const_spec((1, D)),
                pl.BlockSpec((None, D, tf), lambda b, i, j: (layer, 0, j)),
                pl.BlockSpec((None, D, tf), lambda b, i, j: (layer, 0, j)),
                pl.BlockSpec((None, tf, D), lambda b, i, j: (layer, j, 0))]
    args = [h, gain, wg, wu, wd]
    if final:
        in_specs.append(_const_spec((1, D)))
        args.append(final_gain)
    aliases = {}
    if prev_out is not None:
        extra, aliases = _alias_tail(len(args), [prev_out])
        in_specs += extra
        args.append(prev_out)
    m = rows.m
    vmem = (4 * m * D * 4 + m * D * 2 + 2 * (D * 2 * tf + tf * D) * 2 + m * 2 * tf * 4 + m * tf * 2 + (4 << 20))
    out_rows = rows.n_real if final else T
    return pl.pallas_call(
        functools.partial(_ffn_body, nj=nj, final=final, n_alias=len(aliases)),
        grid=rows.grid + (nj,),
        in_specs=in_specs,
        out_specs=rows.spec(D),
        out_shape=jax.ShapeDtypeStruct((rows.B, out_rows, D), _F32),
        scratch_shapes=[pltpu.VMEM((m, D), _BF16)],
        input_output_aliases=aliases,
        compiler_params=_cparams(("parallel", "parallel", "arbitrary"), vmem),
        name=("ffn_final" if final else "ffn") + ("_meta" if rows.meta else ""),
    )(*args)


def _proj_body(*refs, n_rope, n_qkv, n_chunks, nb, n_alias):
    h_ref, gain_ref, w_ref, c_ref, s1_ref, s2_ref, dft_ref = refs[:7]
    qkv_ref, z_ref, xn_ref = refs[7 + n_alias:]
    D = h_ref.shape[-1]
    rows = h_ref.shape[-2]
    xn_ref[...] = _rms(h_ref[...].reshape(-1, D), gain_ref[...]).astype(_BF16)
    c = jnp.concatenate([c_ref[...]] * nb, axis=0)
    s1 = jnp.concatenate([s1_ref[...]] * nb, axis=0)
    s2 = jnp.concatenate([s2_ref[...]] * nb, axis=0)
    half = ROT_DIM // 2
    for ch in range(n_chunks):
        cols = slice(ch * PROJ_CHUNK, (ch + 1) * PROJ_CHUNK)
        y = jnp.dot(xn_ref[...], w_ref[:, cols], preferred_element_type=_F32)
        if ch < n_rope:
            for hh in range(PROJ_CHUNK // HEAD_DIM):
                xh = y[:, hh * HEAD_DIM:(hh + 1) * HEAD_DIM]
                up = pltpu.roll(xh, HEAD_DIM - half, 1)
                dn = pltpu.roll(xh, half, 1)
                r = (xh * c + up * s1 + dn * s2).astype(_BF16)
                lo = ch * PROJ_CHUNK + hh * HEAD_DIM
                qkv_ref[:, :, lo:lo + HEAD_DIM] = r.reshape(nb, rows, HEAD_DIM)
        elif ch < n_qkv:
            qkv_ref[:, :, cols] = y.astype(_BF16).reshape(nb, rows, PROJ_CHUNK)
        else:
            ub = y.astype(_BF16)
            for gg in range(PROJ_CHUNK // FOURIER_GROUP_DIM):
                zz = jnp.dot(ub[:, gg * FOURIER_GROUP_DIM:(gg + 1) * FOURIER_GROUP_DIM], dft_ref[...],
                             preferred_element_type=_F32)
                lo = (ch - n_qkv) * PROJ_CHUNK + gg * FOURIER_GROUP_DIM
                z_ref[:, 0, :, lo:lo + FOURIER_GROUP_DIM] = (
                    zz[:, :FOURIER_GROUP_DIM].astype(_BF16).reshape(nb, rows, FOURIER_GROUP_DIM))
                z_ref[:, 1, :, lo:lo + FOURIER_GROUP_DIM] = (
                    zz[:, FOURIER_GROUP_DIM:].astype(_BF16).reshape(nb, rows, FOURIER_GROUP_DIM))


def _proj(rows, h, gain, w_in, rope_c, rope_s1, rope_s2, dft_c, *, layer, prev_out=None):
    B, T, D = h.shape
    n_rope = (Q_DIM + KV_DIM) // PROJ_CHUNK
    n_qkv = (Q_DIM + 2 * KV_DIM) // PROJ_CHUNK
    n_chunks = IN_DIM // PROJ_CHUNK
    w_spec = pl.BlockSpec((None, D, IN_DIM), lambda *_: (layer, 0, 0), pipeline_mode=pl.Buffered(1))
    in_specs = [rows.spec(D), _const_spec((1, D)), w_spec,
                rows.table_spec(LANES), rows.table_spec(LANES), rows.table_spec(LANES),
                _const_spec((FOURIER_GROUP_DIM, 2 * FOURIER_GROUP_DIM))]
    args = [h, gain, w_in, rope_c, rope_s1, rope_s2, dft_c]
    aliases = {}
    if prev_out is not None:
        extra, aliases = _alias_tail(len(args), prev_out)
        in_specs += extra
        args += list(prev_out)
    m = rows.m
    vmem = (2 * m * D * 4 + m * D * 2 + D * IN_DIM * 2 + 6 * m * LANES * 4
            + 2 * m * (Q_DIM + 2 * KV_DIM) * 2 + 4 * m * FOURIER_DIM * 2 + 8 * m * PROJ_CHUNK * 4 + (6 << 20))
    return pl.pallas_call(
        functools.partial(_proj_body, n_rope=n_rope, n_qkv=n_qkv, n_chunks=n_chunks, nb=rows.nb,
                          n_alias=len(aliases)),
        grid=rows.grid,
        in_specs=in_specs,
        out_specs=[rows.spec(Q_DIM + 2 * KV_DIM), rows.spec(FOURIER_DIM, lead=(2,))],
        out_shape=[jax.ShapeDtypeStruct((B, T, Q_DIM + 2 * KV_DIM), _BF16),
                   jax.ShapeDtypeStruct((B, 2, T, FOURIER_DIM), _BF16)],
        scratch_shapes=[pltpu.VMEM((m, D), _BF16)],
        input_output_aliases=aliases,
        compiler_params=_cparams(("parallel", "arbitrary"), vmem),
        name="mixer_proj" + ("_meta" if rows.meta else ""),
    )(*args)


def _stack_heads(q):
    return jnp.concatenate([q[:, h * HEAD_DIM:(h + 1) * HEAD_DIM] for h in range(Q_PER_KV)], axis=0)


def _sink_column(sink_ref, g, rows_per_head):
    return jnp.concatenate(
        [jnp.full((rows_per_head, 1), sink_ref[g * Q_PER_KV + h], _F32) for h in range(Q_PER_KV)], axis=0)


def _pad_rows(a, rows):
    return jnp.concatenate([a, jnp.zeros((rows - a.shape[0], a.shape[1]), a.dtype)], axis=0)


def _softmax_pv(q4, k_all, v_all, masks, sink):
    rows = q4.shape[0]
    s = lax.dot_general(q4, k_all, (((1,), (1,)), ((), ())), preferred_element_type=_F32) * (HEAD_DIM ** -0.5)
    blocks = []
    for i, mk in enumerate(masks):
        sb = s[:, i * BLOCK:(i + 1) * BLOCK]
        blocks.append(sb if mk is None else jnp.where(mk, sb, NEG_INF))
    mx = jnp.maximum(jnp.max(functools.reduce(jnp.maximum, blocks), axis=-1, keepdims=True), sink)
    p = jnp.concatenate([jnp.exp(sb - mx).astype(_BF16) for sb in blocks], axis=1)
    v_ext = jnp.concatenate([v_all, jnp.ones(v_all.shape, _BF16)], axis=1)
    o = jnp.dot(p, v_ext, preferred_element_type=_F32)
    den = o[:, HEAD_DIM:] + jnp.exp(sink - mx)
    return o[:, :HEAD_DIM] / den


def _attn_body(sink_ref, q_ref, kp_ref, kc_ref, kn_ref, km_ref, vp_ref, vc_ref, vn_ref, vm_ref, o_ref,
               *, qb, ntile, n_meta):
    g = pl.program_id(1)
    jt = pl.program_id(2)
    rows = Q_PER_KV * BLOCK
    row = lax.broadcasted_iota(jnp.int32, (rows, BLOCK), 0) % BLOCK
    col = lax.broadcasted_iota(jnp.int32, (rows, BLOCK), 1)
    diff = col - row
    never = 2 * BLOCK
    sink = _sink_column(sink_ref, g, BLOCK)
    k_meta = _pad_rows(km_ref[0], BLOCK)
    v_meta = _pad_rows(vm_ref[0], BLOCK)
    meta_ok = col < n_meta

    def band(prev_ref, cur_ref, next_ref, meta, qi):
        lo, hi = (qi - 1) * BLOCK, (qi + 2) * BLOCK
        if 0 <= lo and hi <= qb * BLOCK:
            return jnp.concatenate([cur_ref[0, lo:hi], meta], axis=0)
        parts = [prev_ref[0] if qi == 0 else cur_ref[0, lo:lo + BLOCK],
                 cur_ref[0, qi * BLOCK:(qi + 1) * BLOCK],
                 next_ref[0] if qi == qb - 1 else cur_ref[0, hi - BLOCK:hi],
                 meta]
        return jnp.concatenate(parts, axis=0)

    for qi in range(qb):
        q4 = _stack_heads(q_ref[0, qi * BLOCK:(qi + 1) * BLOCK, :])
        off_p = jnp.where(jt >= 1, 0, never) if qi == 0 else 0
        off_n = jnp.where(jt <= ntile - 2, 0, -never) if qi == qb - 1 else 0
        masks = [diff >= off_p, None, diff <= off_n, meta_ok]
        out = _softmax_pv(q4, band(kp_ref, kc_ref, kn_ref, k_meta, qi), band(vp_ref, vc_ref, vn_ref, v_meta, qi),
                          masks, sink)
        for h in range(Q_PER_KV):
            o_ref[0, qi * BLOCK:(qi + 1) * BLOCK, h * HEAD_DIM:(h + 1) * HEAD_DIM] = (
                out[h * BLOCK:(h + 1) * BLOCK].astype(_BF16))


def _attn_meta_body(sink_ref, q_ref, k0_ref, km_ref, v0_ref, vm_ref, prev_out_ref, o_ref, *, n_meta):
    del prev_out_ref
    g = pl.program_id(1)
    rows = Q_PER_KV * n_meta
    row = lax.broadcasted_iota(jnp.int32, (rows, BLOCK), 0) % n_meta
    col = lax.broadcasted_iota(jnp.int32, (rows, BLOCK), 1)
    q4 = _stack_heads(q_ref[0])
    masks = [col <= row + (WINDOW - n_meta), col < n_meta]
    k_all = jnp.concatenate([k0_ref[0], _pad_rows(km_ref[0], BLOCK)], axis=0)
    v_all = jnp.concatenate([v0_ref[0], _pad_rows(vm_ref[0], BLOCK)], axis=0)
    out = _softmax_pv(q4, k_all, v_all, masks, _sink_column(sink_ref, g, n_meta))
    for h in range(Q_PER_KV):
        o_ref[0, :, h * HEAD_DIM:(h + 1) * HEAD_DIM] = out[h * n_meta:(h + 1) * n_meta].astype(_BF16)


def _attention(qkv, sink, *, n_real, n_meta, qb):
    B, T, _ = qkv.shape
    nblk = n_real // BLOCK
    ntile = nblk // qb
    kcol = Q_DIM // HEAD_DIM
    vcol = (Q_DIM + KV_DIM) // HEAD_DIM
    mrow = n_real // n_meta

    def edge_spec(rowf, col0):
        return pl.BlockSpec((1, BLOCK, HEAD_DIM), lambda b, g, j, s: (b, rowf(j), col0 + g))

    def main_spec(col0):
        return pl.BlockSpec((1, qb * BLOCK, HEAD_DIM), lambda b, g, j, s: (b, j, col0 + g))

    def meta_spec(col0):
        return pl.BlockSpec((1, n_meta, HEAD_DIM), lambda b, g, j, s: (b, mrow, col0 + g))

    def prev_i(j):
        return jnp.maximum(qb * j - 1, 0)

    def next_i(j):
        return jnp.minimum(qb * j + qb, nblk - 1)

    attn = pl.pallas_call(
        functools.partial(_attn_body, qb=qb, ntile=ntile, n_meta=n_meta),
        grid_spec=pltpu.PrefetchScalarGridSpec(
            num_scalar_prefetch=1,
            grid=(B, N_KV_HEADS, ntile),
            in_specs=[
                pl.BlockSpec((1, qb * BLOCK, PROJ_CHUNK), lambda b, g, j, s: (b, j, g)),
                edge_spec(prev_i, kcol), main_spec(kcol), edge_spec(next_i, kcol), meta_spec(kcol),
                edge_spec(prev_i, vcol), main_spec(vcol), edge_spec(next_i, vcol), meta_spec(vcol),
            ],
            out_specs=pl.BlockSpec((1, qb * BLOCK, PROJ_CHUNK), lambda b, g, j, s: (b, j, g)),
        ),
        out_shape=jax.ShapeDtypeStruct((B, T, Q_DIM), _BF16),
        compiler_params=_cparams(("parallel", "parallel", "arbitrary"), 32 << 20),
        name="windowed_gqa",
    )(sink, qkv, qkv, qkv, qkv, qkv, qkv, qkv, qkv, qkv)

    def blk0_spec(col0):
        return pl.BlockSpec((1, BLOCK, HEAD_DIM), lambda b, g, s: (b, 0, col0 + g))

    def meta2_spec(col0):
        return pl.BlockSpec((1, n_meta, HEAD_DIM), lambda b, g, s: (b, mrow, col0 + g))

    return pl.pallas_call(
        functools.partial(_attn_meta_body, n_meta=n_meta),
        grid_spec=pltpu.PrefetchScalarGridSpec(
            num_scalar_prefetch=1,
            grid=(B, N_KV_HEADS),
            in_specs=[
                pl.BlockSpec((1, n_meta, PROJ_CHUNK), lambda b, g, s: (b, mrow, g)),
                blk0_spec(kcol), meta2_spec(kcol), blk0_spec(vcol), meta2_spec(vcol),
                pl.BlockSpec(memory_space=pl.ANY),
            ],
            out_specs=pl.BlockSpec((1, n_meta, PROJ_CHUNK), lambda b, g, s: (b, mrow, g)),
        ),
        out_shape=jax.ShapeDtypeStruct((B, T, Q_DIM), _BF16),
        input_output_aliases={6: 0},
        compiler_params=_cparams(("parallel", "arbitrary"), 32 << 20),
        name="windowed_gqa_meta",
    )(sink, qkv, qkv, qkv, qkv, qkv, attn)


def _fft_factors(T):
    best = None
    for n2 in range(16, T + 1, 16):
        if T % n2:
            continue
        n1 = T // n2
        cost = 4 * n1 + 2 * n2
        if best is None or cost < best[0]:
            best = (cost, n1, n2)
    assert best is not None, "sequence length must be a multiple of 16"
    return best[1], best[2]


def _fft1_body(z_ref, mc_ref, ms_ref, twc_ref, tws_ref, a_ref, *, n1, R, C):
    rows = n1 * R
    rhs = jnp.concatenate([z_ref[0, 0].reshape(rows, C), z_ref[0, 1].reshape(rows, C)], axis=1)
    xc = jnp.dot(mc_ref[...], rhs, preferred_element_type=_F32)
    xs = jnp.dot(ms_ref[...], rhs, preferred_element_type=_F32)
    ar = xc[:, :C] + xs[:, C:]
    ai = xc[:, C:] - xs[:, :C]
    twc = jnp.concatenate([twc_ref[0]] * (C // LANES), axis=1)
    tws = jnp.concatenate([tws_ref[0]] * (C // LANES), axis=1)
    a_ref[0, 0] = (ar * twc + ai * tws).reshape(n1, R, C).astype(_BF16)
    a_ref[0, 1] = (ai * twc - ar * tws).reshape(n1, R, C).astype(_BF16)


def _fft2_body(ar_ref, ai_ref, mc_ref, ms_ref, f_ref):
    res = (jnp.dot(mc_ref[0], ar_ref[0, 0, 0], preferred_element_type=_F32)
           + jnp.dot(ms_ref[0], ai_ref[0, 0, 0], preferred_element_type=_F32))
    f_ref[0] = res.astype(_BF16)


def _fft_tables(T, n1, n2, n_meta, R):
    two_pi = 2.0 * np.pi
    k1 = jnp.arange(n1, dtype=jnp.int32)
    ang1 = ((k1[:, None] * k1[None, :]) % n1).astype(_F32) * (two_pi / n1)
    eye = jnp.eye(R, dtype=_F32)
    m1c = jnp.kron(jnp.cos(ang1) * (n1 ** -0.5), eye).astype(_BF16)
    m1s = jnp.kron(jnp.sin(ang1) * (n1 ** -0.5), eye).astype(_BF16)
    nn2 = jnp.arange(n2, dtype=jnp.int32).reshape(n2 // R, 1, R)
    angt = ((k1[None, :, None] * (nn2 + n_meta)) % T).astype(_F32) * (two_pi / T)
    angt = jnp.broadcast_to(angt.reshape(n2 // R, n1 * R, 1), (n2 // R, n1 * R, LANES))
    twc, tws = jnp.cos(angt), jnp.sin(angt)
    k2 = jnp.arange(n2, dtype=jnp.int32)[:, None]
    mm2 = jnp.arange(n2, dtype=jnp.int32)[None, :]
    outs_c, outs_s = [], []
    for rot in (n_meta // n1, n_meta // n1 + 1):
        ph2 = (((k2 + rot) % n2) * (mm2 + n_meta)) % n2
        ang2 = ph2.astype(_F32) * (two_pi / n2)
        outs_c.append(jnp.cos(ang2) * (n2 ** -0.5))
        outs_s.append(jnp.sin(ang2) * (n2 ** -0.5))
    return m1c, m1s, twc, tws, jnp.stack(outs_c).astype(_BF16), jnp.stack(outs_s).astype(_BF16)


def _position_dft(z, tables, *, n1, n2, R, n_meta):
    B, _, T, C = z.shape
    m1c, m1s, twc, tws, m2c, m2s = tables
    rows = n1 * R
    zv = z.reshape(B, 2, n1, n2, C)
    av = pl.pallas_call(
        functools.partial(_fft1_body, n1=n1, R=R, C=C),
        grid=(B, n2 // R),
        in_specs=[
            pl.BlockSpec((1, 2, n1, R, C), lambda b, t: (b, 0, 0, t, 0)),
            _const_spec((rows, rows)),
            _const_spec((rows, rows)),
            pl.BlockSpec((1, rows, LANES), lambda b, t: (t, 0, 0)),
            pl.BlockSpec((1, rows, LANES), lambda b, t: (t, 0, 0)),
        ],
        out_specs=pl.BlockSpec((1, 2, n1, R, C), lambda b, t: (b, 0, 0, t, 0)),
        out_shape=jax.ShapeDtypeStruct((B, 2, n1, n2, C), _BF16),
        compiler_params=_cparams(("parallel", "arbitrary"), 48 << 20),
        name="dft_stage1",
    )(zv, m1c, m1s, twc, tws)
    shift = n1 - (n_meta % n1)

    f = pl.pallas_call(
        _fft2_body,
        grid=(B, n1),
        in_specs=[
            pl.BlockSpec((1, 1, 1, n2, C), lambda b, k: (b, 0, k, 0, 0)),
            pl.BlockSpec((1, 1, 1, n2, C), lambda b, k: (b, 1, k, 0, 0)),
            pl.BlockSpec((1, n2, n2), lambda b, k: (jnp.where(k < n_meta % n1, 1, 0), 0, 0)),
            pl.BlockSpec((1, n2, n2), lambda b, k: (jnp.where(k < n_meta % n1, 1, 0), 0, 0)),
        ],
        out_specs=pl.BlockSpec((1, n2, C), lambda b, k: (b, 0, (k + shift) % n1)),
        out_shape=jax.ShapeDtypeStruct((B, n2, n1 * C), _BF16),
        compiler_params=_cparams(("parallel", "arbitrary"), 32 << 20),
        name="dft_stage2",
    )(av, av, m2c, m2s)
    return f.reshape(B, T, C)


def _mix_body(*refs, n_alias):
    h_ref, gain_ref, a_ref, f_ref, wpa_ref, wpf_ref, wga_ref, wgf_ref, wo_ref = refs[:9]
    o_ref, xn_ref = refs[9 + n_alias:]
    j = pl.program_id(2)
    D = h_ref.shape[-1]

    @pl.when(j == 0)
    def _():
        x = h_ref[...].reshape(-1, D)
        xn_ref[...] = _rms(x, gain_ref[...]).astype(_BF16)
        o_ref[...] = x.reshape(o_ref.shape)

    av = jnp.dot(a_ref[...].reshape(-1, Q_DIM), wpa_ref[...], preferred_element_type=_F32)
    fv = jnp.dot(f_ref[...].reshape(-1, FOURIER_DIM), wpf_ref[...], preferred_element_type=_F32)
    ga = jax.nn.sigmoid(jnp.dot(xn_ref[...], wga_ref[...], preferred_element_type=_F32))
    gf = jax.nn.sigmoid(jnp.dot(xn_ref[...], wgf_ref[...], preferred_element_type=_F32))
    mixed = (ga * av + gf * fv).astype(_BF16)
    o_ref[...] += jnp.dot(mixed, wo_ref[...], preferred_element_type=_F32).reshape(o_ref.shape)


def _mix_out(rows, h, gain, attn, four, wpa, wpf, wg, wo, *, layer, tn, prev_out=None):
    B, T, D = h.shape
    nj = D // tn
    in_specs = [rows.spec(D), _const_spec((1, D)), rows.spec(Q_DIM), rows.spec(FOURIER_DIM),
                pl.BlockSpec((None, Q_DIM, tn), lambda b, i, j: (layer, 0, j)),
                pl.BlockSpec((None, FOURIER_DIM, tn), lambda b, i, j: (layer, 0, j)),
                pl.BlockSpec((None, D, tn), lambda b, i, j: (layer, 0, j)),
                pl.BlockSpec((None, D, tn), lambda b, i, j: (layer, 0, nj + j)),
                pl.BlockSpec((None, tn, D), lambda b, i, j: (layer, j, 0))]
    args = [h, gain, attn, four, wpa, wpf, wg, wg, wo]
    aliases = {}
    if prev_out is not None:
        extra, aliases = _alias_tail(len(args), [prev_out])
        in_specs += extra
        args.append(prev_out)
    m = rows.m
    vmem = (4 * m * D * 4 + m * D * 2 + 2 * m * (Q_DIM + FOURIER_DIM) * 2
            + 2 * (Q_DIM * tn + FOURIER_DIM * tn + D * 2 * tn + tn * D) * 2 + 8 * m * tn * 4 + (4 << 20))
    return pl.pallas_call(
        functools.partial(_mix_body, n_alias=len(aliases)),
        grid=rows.grid + (nj,),
        in_specs=in_specs,
        out_specs=rows.spec(D),
        out_shape=jax.ShapeDtypeStruct((B, T, D), _F32),
        scratch_shapes=[pltpu.VMEM((m, D), _BF16)],
        input_output_aliases=aliases,
        compiler_params=_cparams(("parallel", "parallel", "arbitrary"), vmem),
        name="mixer_out" + ("_meta" if rows.meta else ""),
    )(*args)


def _rope_tables(n_real, n_meta):
    T = n_real + n_meta
    half = ROT_DIM // 2
    inv = 1.0 / (ROPE_THETA ** (jnp.arange(0, ROT_DIM, 2, dtype=_F32) / ROT_DIM))
    ang = jnp.arange(T, dtype=_F32)[:, None] * inv[None, :]
    cos, sin = jnp.cos(ang), jnp.sin(ang)
    cos = jnp.concatenate([cos[n_meta:], cos[:n_meta]], axis=0)
    sin = jnp.concatenate([sin[n_meta:], sin[:n_meta]], axis=0)
    ones = jnp.ones((T, HEAD_DIM - ROT_DIM), _F32)
    zeros_h = jnp.zeros((T, half), _F32)
    zeros_r = jnp.zeros((T, HEAD_DIM - ROT_DIM), _F32)
    c = jnp.concatenate([cos, cos, ones], axis=1)
    s1 = jnp.concatenate([-sin, zeros_h, zeros_r], axis=1)
    s2 = jnp.concatenate([zeros_h, sin, zeros_r], axis=1)
    return c, s1, s2


def _channel_dft_table():
    n = FOURIER_GROUP_DIM
    c = jnp.arange(n, dtype=jnp.int32)[:, None]
    k = jnp.arange(n, dtype=jnp.int32)[None, :]
    ang = ((c * k) % n).astype(_F32) * (2.0 * np.pi / n)
    return (jnp.concatenate([jnp.cos(ang), -jnp.sin(ang)], axis=1) * (n ** -0.5)).astype(_BF16)


def _pick_tile(n, pref):
    t = pref
    while n % t:
        t //= 2
    return t


def kernel(x, meta_tokens, ffn1_norm, ffn1_w_gate, ffn1_w_up, ffn1_w_down, mix_norm, w_in, w_gate, sink,
           w_attn_branch, w_fourier_branch, w_out, ffn2_norm, ffn2_w_gate, ffn2_w_up, ffn2_w_down, final_norm):
    B, S, D = x.shape
    n_meta = meta_tokens.shape[0]
    depth = ffn1_norm.shape[0]
    d_ff = ffn1_w_gate.shape[-1]
    T = S + n_meta
    assert S % BLOCK == 0 and S % n_meta == 0 and BLOCK % n_meta == 0
    assert w_in.shape[-1] == IN_DIM and KV_DIM == PROJ_CHUNK

    tm_ffn = _pick_tile(S, 1024)
    tm = _pick_tile(S, 512)
    tf = _pick_tile(d_ff, 512)
    tn = _pick_tile(D, 512)
    qb = _pick_tile(S // BLOCK, 8)
    rows_ffn = _Rows(B, S, n_meta, tm_ffn, meta=False)
    rows_main = _Rows(B, S, n_meta, tm, meta=False)
    rows_meta = _Rows(B, S, n_meta, tm, meta=True)

    n1, n2 = _fft_factors(T)
    fft_tables = _fft_tables(T, n1, n2, n_meta, BF16_SUBLANES)
    rope_c, rope_s1, rope_s2 = _rope_tables(S, n_meta)
    dft_c = _channel_dft_table()

    ffn1_w = tuple(_to_bf16(w) for w in (ffn1_w_gate, ffn1_w_up, ffn1_w_down))
    ffn2_w = tuple(_to_bf16(w) for w in (ffn2_w_gate, ffn2_w_up, ffn2_w_down))
    w_in_b = _to_bf16(w_in)
    mix_w = tuple(_to_bf16(w) for w in (w_attn_branch, w_fourier_branch, w_gate, w_out))

    def ffn(h, h_meta, gain, wts, layer, final_gain=None):
        out = _ffn(rows_ffn, h, gain, *wts, layer=layer, tf=tf, T=T, final_gain=final_gain)
        if final_gain is None:
            src, blk0 = h_meta
            out = _ffn(rows_meta, src, gain, *wts, layer=layer, tf=tf, T=T, h_row_block0=blk0, prev_out=out)
        return out

    meta = jnp.broadcast_to(meta_tokens[None].astype(x.dtype), (B, n_meta, D))
    h = None
    for l in range(depth):
        if l == 0:
            h = ffn(x, (meta, 0), ffn1_norm[l][None], ffn1_w, l)
        else:
            h = ffn(h, (h, None), ffn1_norm[l][None], ffn1_w, l)

        gain = mix_norm[l][None]
        proj_args = (gain, w_in_b, rope_c, rope_s1, rope_s2, dft_c)
        qkv, z = _proj(rows_main, h, *proj_args, layer=l)
        qkv, z = _proj(rows_meta, h, *proj_args, layer=l, prev_out=(qkv, z))
        attn = _attention(qkv, sink[l], n_real=S, n_meta=n_meta, qb=qb)
        four = _position_dft(z, fft_tables, n1=n1, n2=n2, R=BF16_SUBLANES, n_meta=n_meta)
        mix_args = (gain, attn, four) + mix_w
        h2 = _mix_out(rows_main, h, *mix_args, layer=l, tn=tn)
        h = _mix_out(rows_meta, h, *mix_args, layer=l, tn=tn, prev_out=h2)

        last = l == depth - 1
        h = ffn(h, (h, None), ffn2_norm[l][None], ffn2_w, l, final_gain=final_norm[None] if last else None)
    return h
```

```python
import functools

import numpy as np
import jax
import jax.numpy as jnp
from jax import lax
from jax.experimental import pallas as pl
from jax.experimental.pallas import tpu as pltpu

_F32 = jnp.float32
_BF16 = jnp.bfloat16

HEAD_DIM = 128
N_Q_HEADS = 16
N_KV_HEADS = 4
Q_PER_KV = N_Q_HEADS // N_KV_HEADS
WINDOW = 128
BLOCK = 128
ROPE_THETA = 500000.0
ROT_DIM = HEAD_DIM // 4
N_FOURIER_GROUPS = 8
FOURIER_GROUP_DIM = 128
FOURIER_DIM = N_FOURIER_GROUPS * FOURIER_GROUP_DIM
Q_DIM = N_Q_HEADS * HEAD_DIM
KV_DIM = N_KV_HEADS * HEAD_DIM
IN_DIM = Q_DIM + 2 * KV_DIM + FOURIER_DIM
RMS_EPS = 1e-6
NEG_INF = -1e30

LANES = 128
BF16_SUBLANES = 16
PROJ_CHUNK = Q_PER_KV * HEAD_DIM
V7X_VMEM_BYTES = 64 * 1024 * 1024


def _round_up(a, m):
    return (a + m - 1) // m * m


def _cparams(sem, vmem_bytes):
    return pltpu.CompilerParams(dimension_semantics=sem,
                                vmem_limit_bytes=int(min(vmem_bytes, V7X_VMEM_BYTES - (4 << 20))))


def _rms(x, g):
    return x * lax.rsqrt(jnp.mean(x * x, axis=-1, keepdims=True) + RMS_EPS) * g


class _Rows:
    def __init__(self, B, n_real, n_meta, tm, meta):
        self.B, self.n_real, self.n_meta, self.meta = B, n_real, n_meta, meta
        self.nb = B if meta else 1
        self.rows = n_meta if meta else tm
        self.grid = (1, 1) if meta else (B, n_real // tm)
        self.m = self.nb * self.rows

    def spec(self, width, lead=(), row_block0=None):
        nl = len(lead)
        if self.meta:
            r0 = self.n_real // self.n_meta if row_block0 is None else row_block0
            return pl.BlockSpec((self.nb,) + tuple(lead) + (self.rows, width),
                                lambda b, i, *_: (0,) + (0,) * nl + (r0, 0))
        return pl.BlockSpec((1,) + tuple(lead) + (self.rows, width),
                            lambda b, i, *_: (b,) + (0,) * nl + (i, 0))

    def table_spec(self, width):
        if self.meta:
            r0 = self.n_real // self.n_meta
            return pl.BlockSpec((self.rows, width), lambda b, i, *_: (r0, 0))
        return pl.BlockSpec((self.rows, width), lambda b, i, *_: (i, 0))


def _const_spec(shape, single_buffer=False):
    nd = len(shape)
    if single_buffer:
        return pl.BlockSpec(shape, lambda *_: (0,) * nd, pipeline_mode=pl.Buffered(1))
    return pl.BlockSpec(shape, lambda *_: (0,) * nd)


def _alias_tail(n_in, outs):
    specs = [pl.BlockSpec(memory_space=pl.ANY) for _ in outs]
    return specs, {n_in + k: k for k in range(len(outs))}


def _cast_body(w_ref, o_ref):
    o_ref[...] = w_ref[...].astype(_BF16)


def _to_bf16(w):
    L, K, N = w.shape
    tk = _pick_tile(K, 512)
    while tk > 8 and tk * N * 4 > (6 << 20):
        tk //= 2
    return pl.pallas_call(
        _cast_body,
        grid=(L, K // tk),
        in_specs=[pl.BlockSpec((1, tk, N), lambda l, i: (l, i, 0))],
        out_specs=pl.BlockSpec((1, tk, N), lambda l, i: (l, i, 0)),
        out_shape=jax.ShapeDtypeStruct(w.shape, _BF16),
        compiler_params=_cparams(("parallel", "parallel"), 6 * tk * N * 2 + (4 << 20)),
        name="cast_bf16",
    )(w)


def _ffn_body(*refs, nj, final, n_alias):
    h_ref, gain_ref, wg_ref, wu_ref, wd_ref = refs[:5]
    rest = refs[5:]
    fgain_ref = None
    if final:
        fgain_ref, rest = rest[0], rest[1:]
    o_ref, xn_ref = rest[n_alias:]
    j = pl.program_id(2)
    D = h_ref.shape[-1]

    @pl.when(j == 0)
    def _():
        x = h_ref[...].reshape(-1, D)
        xn_ref[...] = _rms(x, gain_ref[...]).astype(_BF16)
        o_ref[...] = x.reshape(o_ref.shape)

    g = jnp.dot(xn_ref[...], wg_ref[...], preferred_element_type=_F32)
    u = jnp.dot(xn_ref[...], wu_ref[...], preferred_element_type=_F32)
    act = ((g * jax.nn.sigmoid(g)) * u * 0.5).astype(_BF16)
    o_ref[...] += jnp.dot(act, wd_ref[...], preferred_element_type=_F32).reshape(o_ref.shape)

    if final:
        @pl.when(j == nj - 1)
        def _():
            o_ref[...] = _rms(o_ref[...].reshape(-1, D), fgain_ref[...]).reshape(o_ref.shape)


def _ffn(rows, h, gain, wg, wu, wd, *, layer, tf, T, final_gain=None, h_row_block0=None, prev_out=None):
    D = h.shape[-1]
    nj = wg.shape[2] // tf
    final = final_gain is not None
    in_specs = [rows.spec(D, row_block0=h_row_block0), _const_spec((1, D)),
                pl.BlockSpec((None, D, tf), lambda b, i, j: (layer, 0, j)),
                pl.BlockSpec((None, D, tf), lambda b, i, j: (layer, 0, j)),
                pl.BlockSpec((None, tf, D), lambda b, i, j: (layer, j, 0))]
    args = [h, gain, wg, wu, wd]
    if final:
        in_specs.append(_const_spec((1, D)))
        args.append(final_gain)
    aliases = {}
    if prev_out is not None:
        extra, aliases = _alias_tail(len(args), [prev_out])
        in_specs += extra
        args.append(prev_out)
    m = rows.m
    vmem = (4 * m * D * 4 + m * D * 2 + 2 * (D * 2 * tf + tf * D) * 2 + m * 2 * tf * 4 + m * tf * 2 + (4 << 20))
    out_rows = rows.n_real if final else T
    return pl.pallas_call(
        functools.partial(_ffn_body, nj=nj, final=final, n_alias=len(aliases)),
        grid=rows.grid + (nj,),
        in_specs=in_specs,
        out_specs=rows.spec(D),
        out_shape=jax.ShapeDtypeStruct((rows.B, out_rows, D), _F32),
        scratch_shapes=[pltpu.VMEM((m, D), _BF16)],
        input_output_aliases=aliases,
        compiler_params=_cparams(("parallel", "parallel", "arbitrary"), vmem),
        name=("ffn_final" if final else "ffn") + ("_meta" if rows.meta else ""),
    )(*args)


def _proj_body(*refs, n_rope, n_qkv, n_chunks, nb, n_alias):
    h_ref, gain_ref, w_ref, c_ref, s1_ref, s2_ref, dft_ref = refs[:7]
    qkv_ref, z_ref, xn_ref = refs[7 + n_alias:]
    D = h_ref.shape[-1]
    rows = h_ref.shape[-2]
    xn_ref[...] = _rms(h_ref[...].reshape(-1, D), gain_ref[...]).astype(_BF16)
    c = jnp.concatenate([c_ref[...]] * nb, axis=0)
    s1 = jnp.concatenate([s1_ref[...]] * nb, axis=0)
    s2 = jnp.concatenate([s2_ref[...]] * nb, axis=0)
    half = ROT_DIM // 2
    for ch in range(n_chunks):
        cols = slice(ch * PROJ_CHUNK, (ch + 1) * PROJ_CHUNK)
        y = jnp.dot(xn_ref[...], w_ref[:, cols], preferred_element_type=_F32)
        if ch < n_rope:
            for hh in range(PROJ_CHUNK // HEAD_DIM):
                xh = y[:, hh * HEAD_DIM:(hh + 1) * HEAD_DIM]
                up = pltpu.roll(xh, HEAD_DIM - half, 1)
                dn = pltpu.roll(xh, half, 1)
                r = (xh * c + up * s1 + dn * s2).astype(_BF16)
                lo = ch * PROJ_CHUNK + hh * HEAD_DIM
                qkv_ref[:, :, lo:lo + HEAD_DIM] = r.reshape(nb, rows, HEAD_DIM)
        elif ch < n_qkv:
            qkv_ref[:, :, cols] = y.astype(_BF16).reshape(nb, rows, PROJ_CHUNK)
        else:
            ub = y.astype(_BF16)
            for gg in range(PROJ_CHUNK // FOURIER_GROUP_DIM):
                zz = jnp.dot(ub[:, gg * FOURIER_GROUP_DIM:(gg + 1) * FOURIER_GROUP_DIM], dft_ref[...],
                             preferred_element_type=_F32)
                lo = (ch - n_qkv) * PROJ_CHUNK + gg * FOURIER_GROUP_DIM
                z_ref[:, 0, :, lo:lo + FOURIER_GROUP_DIM] = (
                    zz[:, :FOURIER_GROUP_DIM].astype(_BF16).reshape(nb, rows, FOURIER_GROUP_DIM))
                z_ref[:, 1, :, lo:lo + FOURIER_GROUP_DIM] = (
                    zz[:, FOURIER_GROUP_DIM:].astype(_BF16).reshape(nb, rows, FOURIER_GROUP_DIM))


def _proj(rows, h, gain, w_in, rope_c, rope_s1, rope_s2, dft_c, *, layer, prev_out=None):
    B, T, D = h.shape
    n_rope = (Q_DIM + KV_DIM) // PROJ_CHUNK
    n_qkv = (Q_DIM + 2 * KV_DIM) // PROJ_CHUNK
    n_chunks = IN_DIM // PROJ_CHUNK
    w_spec = pl.BlockSpec((None, D, IN_DIM), lambda *_: (layer, 0, 0), pipeline_mode=pl.Buffered(1))
    in_specs = [rows.spec(D), _const_spec((1, D)), w_spec,
                rows.table_spec(LANES), rows.table_spec(LANES), rows.table_spec(LANES),
                _const_spec((FOURIER_GROUP_DIM, 2 * FOURIER_GROUP_DIM))]
    args = [h, gain, w_in, rope_c, rope_s1, rope_s2, dft_c]
    aliases = {}
    if prev_out is not None:
        extra, aliases = _alias_tail(len(args), prev_out)
        in_specs += extra
        args += list(prev_out)
    m = rows.m
    vmem = (2 * m * D * 4 + m * D * 2 + D * IN_DIM * 2 + 6 * m * LANES * 4
            + 2 * m * (Q_DIM + 2 * KV_DIM) * 2 + 4 * m * FOURIER_DIM * 2 + 8 * m * PROJ_CHUNK * 4 + (6 << 20))
    return pl.pallas_call(
        functools.partial(_proj_body, n_rope=n_rope, n_qkv=n_qkv, n_chunks=n_chunks, nb=rows.nb,
                          n_alias=len(aliases)),
        grid=rows.grid,
        in_specs=in_specs,
        out_specs=[rows.spec(Q_DIM + 2 * KV_DIM), rows.spec(FOURIER_DIM, lead=(2,))],
        out_shape=[jax.ShapeDtypeStruct((B, T, Q_DIM + 2 * KV_DIM), _BF16),
                   jax.ShapeDtypeStruct((B, 2, T, FOURIER_DIM), _BF16)],
        scratch_shapes=[pltpu.VMEM((m, D), _BF16)],
        input_output_aliases=aliases,
        compiler_params=_cparams(("parallel", "arbitrary"), vmem),
        name="mixer_proj" + ("_meta" if rows.meta else ""),
    )(*args)


def _stack_heads(q):
    return jnp.concatenate([q[:, h * HEAD_DIM:(h + 1) * HEAD_DIM] for h in range(Q_PER_KV)], axis=0)


def _sink_column(sink_ref, g, rows_per_head):
    return jnp.concatenate(
        [jnp.full((rows_per_head, 1), sink_ref[g * Q_PER_KV + h], _F32) for h in range(Q_PER_KV)], axis=0)


def _pad_rows(a, rows):
    return jnp.concatenate([a, jnp.zeros((rows - a.shape[0], a.shape[1]), a.dtype)], axis=0)


def _softmax_pv(q4, k_all, v_all, masks, sink):
    rows = q4.shape[0]
    s = lax.dot_general(q4, k_all, (((1,), (1,)), ((), ())), preferred_element_type=_F32) * (HEAD_DIM ** -0.5)
    blocks = []
    for i, mk in enumerate(masks):
        sb = s[:, i * BLOCK:(i + 1) * BLOCK]
        blocks.append(sb if mk is None else jnp.where(mk, sb, NEG_INF))
    mx = jnp.maximum(jnp.max(functools.reduce(jnp.maximum, blocks), axis=-1, keepdims=True), sink)
    p = jnp.concatenate([jnp.exp(sb - mx).astype(_BF16) for sb in blocks], axis=1)
    v_ext = jnp.concatenate([v_all, jnp.ones(v_all.shape, _BF16)], axis=1)
    o = jnp.dot(p, v_ext, preferred_element_type=_F32)
    den = o[:, HEAD_DIM:] + jnp.exp(sink - mx)
    return o[:, :HEAD_DIM] / den


def _attn_body(sink_ref, q_ref, kp_ref, kc_ref, kn_ref, km_ref, vp_ref, vc_ref, vn_ref, vm_ref, o_ref,
               *, qb, ntile, n_meta):
    g = pl.program_id(1)
    jt = pl.program_id(2)
    rows = Q_PER_KV * BLOCK
    row = lax.broadcasted_iota(jnp.int32, (rows, BLOCK), 0) % BLOCK
    col = lax.broadcasted_iota(jnp.int32, (rows, BLOCK), 1)
    diff = col - row
    never = 2 * BLOCK
    sink = _sink_column(sink_ref, g, BLOCK)
    k_meta = _pad_rows(km_ref[0], BLOCK)
    v_meta = _pad_rows(vm_ref[0], BLOCK)
    meta_ok = col < n_meta

    def band(prev_ref, cur_ref, next_ref, meta, qi):
        lo, hi = (qi - 1) * BLOCK, (qi + 2) * BLOCK
        if 0 <= lo and hi <= qb * BLOCK:
            return jnp.concatenate([cur_ref[0, lo:hi], meta], axis=0)
        parts = [prev_ref[0] if qi == 0 else cur_ref[0, lo:lo + BLOCK],
                 cur_ref[0, qi * BLOCK:(qi + 1) * BLOCK],
                 next_ref[0] if qi == qb - 1 else cur_ref[0, hi - BLOCK:hi],
                 meta]
        return jnp.concatenate(parts, axis=0)

    for qi in range(qb):
        q4 = _stack_heads(q_ref[0, qi * BLOCK:(qi + 1) * BLOCK, :])
        off_p = jnp.where(jt >= 1, 0, never) if qi == 0 else 0
        off_n = jnp.where(jt <= ntile - 2, 0, -never) if qi == qb - 1 else 0
        masks = [diff >= off_p, None, diff <= off_n, meta_ok]
        out = _softmax_pv(q4, band(kp_ref, kc_ref, kn_ref, k_meta, qi), band(vp_ref, vc_ref, vn_ref, v_meta, qi),
                          masks, sink)
        for h in range(Q_PER_KV):
            o_ref[0, qi * BLOCK:(qi + 1) * BLOCK, h * HEAD_DIM:(h + 1) * HEAD_DIM] = (
                out[h * BLOCK:(h + 1) * BLOCK].astype(_BF16))


def _attn_meta_body(sink_ref, q_ref, k0_ref, km_ref, v0_ref, vm_ref, prev_out_ref, o_ref, *, n_meta):
    del prev_out_ref
    g = pl.program_id(1)
    rows = Q_PER_KV * n_meta
    row = lax.broadcasted_iota(jnp.int32, (rows, BLOCK), 0) % n_meta
    col = lax.broadcasted_iota(jnp.int32, (rows, BLOCK), 1)
    q4 = _stack_heads(q_ref[0])
    masks = [col <= row + (WINDOW - n_meta), col < n_meta]
    k_all = jnp.concatenate([k0_ref[0], _pad_rows(km_ref[0], BLOCK)], axis=0)
    v_all = jnp.concatenate([v0_ref[0], _pad_rows(vm_ref[0], BLOCK)], axis=0)
    out = _softmax_pv(q4, k_all, v_all, masks, _sink_column(sink_ref, g, n_meta))
    for h in range(Q_PER_KV):
        o_ref[0, :, h * HEAD_DIM:(h + 1) * HEAD_DIM] = out[h * n_meta:(h + 1) * n_meta].astype(_BF16)


def _attention(qkv, sink, *, n_real, n_meta, qb):
    B, T, _ = qkv.shape
    nblk = n_real // BLOCK
    ntile = nblk // qb
    kcol = Q_DIM // HEAD_DIM
    vcol = (Q_DIM + KV_DIM) // HEAD_DIM
    mrow = n_real // n_meta

    def edge_spec(rowf, col0):
        return pl.BlockSpec((1, BLOCK, HEAD_DIM), lambda b, g, j, s: (b, rowf(j), col0 + g))

    def main_spec(col0):
        return pl.BlockSpec((1, qb * BLOCK, HEAD_DIM), lambda b, g, j, s: (b, j, col0 + g))

    def meta_spec(col0):
        return pl.BlockSpec((1, n_meta, HEAD_DIM), lambda b, g, j, s: (b, mrow, col0 + g))

    def prev_i(j):
        return jnp.maximum(qb * j - 1, 0)

    def next_i(j):
        return jnp.minimum(qb * j + qb, nblk - 1)

    attn = pl.pallas_call(
        functools.partial(_attn_body, qb=qb, ntile=ntile, n_meta=n_meta),
        grid_spec=pltpu.PrefetchScalarGridSpec(
            num_scalar_prefetch=1,
            grid=(B, N_KV_HEADS, ntile),
            in_specs=[
                pl.BlockSpec((1, qb * BLOCK, PROJ_CHUNK), lambda b, g, j, s: (b, j, g)),
                edge_spec(prev_i, kcol), main_spec(kcol), edge_spec(next_i, kcol), meta_spec(kcol),
                edge_spec(prev_i, vcol), main_spec(vcol), edge_spec(next_i, vcol), meta_spec(vcol),
            ],
            out_specs=pl.BlockSpec((1, qb * BLOCK, PROJ_CHUNK), lambda b, g, j, s: (b, j, g)),
        ),
        out_shape=jax.ShapeDtypeStruct((B, T, Q_DIM), _BF16),
        compiler_params=_cparams(("parallel", "parallel", "arbitrary"), 32 << 20),
        name="windowed_gqa",
    )(sink, qkv, qkv, qkv, qkv, qkv, qkv, qkv, qkv, qkv)

    def blk0_spec(col0):
        return pl.BlockSpec((1, BLOCK, HEAD_DIM), lambda b, g, s: (b, 0, col0 + g))

    def meta2_spec(col0):
        return pl.BlockSpec((1, n_meta, HEAD_DIM), lambda b, g, s: (b, mrow, col0 + g))

    return pl.pallas_call(
        functools.partial(_attn_meta_body, n_meta=n_meta),
        grid_spec=pltpu.PrefetchScalarGridSpec(
            num_scalar_prefetch=1,
            grid=(B, N_KV_HEADS),
            in_specs=[
                pl.BlockSpec((1, n_meta, PROJ_CHUNK), lambda b, g, s: (b, mrow, g)),
                blk0_spec(kcol), meta2_spec(kcol), blk0_spec(vcol), meta2_spec(vcol),
                pl.BlockSpec(memory_space=pl.ANY),
            ],
            out_specs=pl.BlockSpec((1, n_meta, PROJ_CHUNK), lambda b, g, s: (b, mrow, g)),
        ),
        out_shape=jax.ShapeDtypeStruct((B, T, Q_DIM), _BF16),
        input_output_aliases={6: 0},
        compiler_params=_cparams(("parallel", "arbitrary"), 32 << 20),
        name="windowed_gqa_meta",
    )(sink, qkv, qkv, qkv, qkv, qkv, attn)


def _fft_factors(T):
    best = None
    for n2 in range(16, T + 1, 16):
        if T % n2:
            continue
        n1 = T // n2
        cost = 4 * n1 + 2 * n2
        if best is None or cost < best[0]:
            best = (cost, n1, n2)
    assert best is not None, "sequence length must be a multiple of 16"
    return best[1], best[2]


def _fft1_body(z_ref, mc_ref, ms_ref, twc_ref, tws_ref, a_ref, *, n1, R, C):
    rows = n1 * R
    rhs = jnp.concatenate([z_ref[0, 0].reshape(rows, C), z_ref[0, 1].reshape(rows, C)], axis=1)
    xc = jnp.dot(mc_ref[...], rhs, preferred_element_type=_F32)
    xs = jnp.dot(ms_ref[...], rhs, preferred_element_type=_F32)
    ar = xc[:, :C] + xs[:, C:]
    ai = xc[:, C:] - xs[:, :C]
    twc = jnp.concatenate([twc_ref[0]] * (C // LANES), axis=1)
    tws = jnp.concatenate([tws_ref[0]] * (C // LANES), axis=1)
    a_ref[0, 0] = (ar * twc + ai * tws).reshape(n1, R, C).astype(_BF16)
    a_ref[0, 1] = (ai * twc - ar * tws).reshape(n1, R, C).astype(_BF16)


def _fft2_body(ar_ref, ai_ref, mc_ref, ms_ref, f_ref):
    res = (jnp.dot(mc_ref[0], ar_ref[0, 0, 0], preferred_element_type=_F32)
           + jnp.dot(ms_ref[0], ai_ref[0, 0, 0], preferred_element_type=_F32))
    f_ref[0] = res.astype(_BF16)


def _fft_tables(T, n1, n2, n_meta, R):
    two_pi = 2.0 * np.pi
    k1 = jnp.arange(n1, dtype=jnp.int32)
    ang1 = ((k1[:, None] * k1[None, :]) % n1).astype(_F32) * (two_pi / n1)
    eye = jnp.eye(R, dtype=_F32)
    m1c = jnp.kron(jnp.cos(ang1) * (n1 ** -0.5), eye).astype(_BF16)
    m1s = jnp.kron(jnp.sin(ang1) * (n1 ** -0.5), eye).astype(_BF16)
    nn2 = jnp.arange(n2, dtype=jnp.int32).reshape(n2 // R, 1, R)
    angt = ((k1[None, :, None] * (nn2 + n_meta)) % T).astype(_F32) * (two_pi / T)
    angt = jnp.broadcast_to(angt.reshape(n2 // R, n1 * R, 1), (n2 // R, n1 * R, LANES))
    twc, tws = jnp.cos(angt), jnp.sin(angt)
    k2 = jnp.arange(n2, dtype=jnp.int32)[:, None]
    mm2 = jnp.arange(n2, dtype=jnp.int32)[None, :]
    outs_c, outs_s = [], []
    for rot in (n_meta // n1, n_meta // n1 + 1):
        ph2 = (((k2 + rot) % n2) * (mm2 + n_meta)) % n2
        ang2 = ph2.astype(_F32) * (two_pi / n2)
        outs_c.append(jnp.cos(ang2) * (n2 ** -0.5))
        outs_s.append(jnp.sin(ang2) * (n2 ** -0.5))
    return m1c, m1s, twc, tws, jnp.stack(outs_c).astype(_BF16), jnp.stack(outs_s).astype(_BF16)


def _position_dft(z, tables, *, n1, n2, R, n_meta):
    B, _, T, C = z.shape
    m1c, m1s, twc, tws, m2c, m2s = tables
    rows = n1 * R
    zv = z.reshape(B, 2, n1, n2, C)
    av = pl.pallas_call(
        functools.partial(_fft1_body, n1=n1, R=R, C=C),
        grid=(B, n2 // R),
        in_specs=[
            pl.BlockSpec((1, 2, n1, R, C), lambda b, t: (b, 0, 0, t, 0)),
            _const_spec((rows, rows)),
            _const_spec((rows, rows)),
            pl.BlockSpec((1, rows, LANES), lambda b, t: (t, 0, 0)),
            pl.BlockSpec((1, rows, LANES), lambda b, t: (t, 0, 0)),
        ],
        out_specs=pl.BlockSpec((1, 2, n1, R, C), lambda b, t: (b, 0, 0, t, 0)),
        out_shape=jax.ShapeDtypeStruct((B, 2, n1, n2, C), _BF16),
        compiler_params=_cparams(("parallel", "arbitrary"), 48 << 20),
        name="dft_stage1",
    )(zv, m1c, m1s, twc, tws)
    shift = n1 - (n_meta % n1)

    f = pl.pallas_call(
        _fft2_body,
        grid=(B, n1),
        in_specs=[
            pl.BlockSpec((1, 1, 1, n2, C), lambda b, k: (b, 0, k, 0, 0)),
            pl.BlockSpec((1, 1, 1, n2, C), lambda b, k: (b, 1, k, 0, 0)),
            pl.BlockSpec((1, n2, n2), lambda b, k: (jnp.where(k < n_meta % n1, 1, 0), 0, 0)),
            pl.BlockSpec((1, n2, n2), lambda b, k: (jnp.where(k < n_meta % n1, 1, 0), 0, 0)),
        ],
        out_specs=pl.BlockSpec((1, n2, C), lambda b, k: (b, 0, (k + shift) % n1)),
        out_shape=jax.ShapeDtypeStruct((B, n2, n1 * C), _BF16),
        compiler_params=_cparams(("parallel", "arbitrary"), 32 << 20),
        name="dft_stage2",
    )(av, av, m2c, m2s)
    return f.reshape(B, T, C)


def _mix_body(*refs, tn, n_alias):
    h_ref, gain_ref, a_ref, f_ref, wpa_ref, wpf_ref, wg_ref, wo_ref = refs[:8]
    o_ref, xn_ref = refs[8 + n_alias:]
    D = h_ref.shape[-1]
    x = h_ref[...].reshape(-1, D)
    xn_ref[...] = _rms(x, gain_ref[...]).astype(_BF16)
    o_ref[...] = x.reshape(o_ref.shape)
    a = a_ref[...].reshape(-1, Q_DIM)
    f = f_ref[...].reshape(-1, FOURIER_DIM)
    for j in range(D // tn):
        cols = slice(j * tn, (j + 1) * tn)
        av = jnp.dot(a, wpa_ref[:, cols], preferred_element_type=_F32)
        fv = jnp.dot(f, wpf_ref[:, cols], preferred_element_type=_F32)
        ga = jax.nn.sigmoid(jnp.dot(xn_ref[...], wg_ref[:, cols], preferred_element_type=_F32))
        gf = jax.nn.sigmoid(jnp.dot(xn_ref[...], wg_ref[:, D + j * tn:D + (j + 1) * tn],
                                    preferred_element_type=_F32))
        mixed = (ga * av + gf * fv).astype(_BF16)
        o_ref[...] += jnp.dot(mixed, wo_ref[cols, :], preferred_element_type=_F32).reshape(o_ref.shape)


def _mix_out(rows, h, gain, attn, four, wpa, wpf, wg, wo, *, layer, tn, prev_out=None):
    B, T, D = h.shape

    def resident(k, n):
        return pl.BlockSpec((None, k, n), lambda *_: (layer, 0, 0), pipeline_mode=pl.Buffered(1))

    in_specs = [rows.spec(D), _const_spec((1, D)), rows.spec(Q_DIM), rows.spec(FOURIER_DIM),
                resident(Q_DIM, D), resident(FOURIER_DIM, D), resident(D, 2 * D), resident(D, D)]
    args = [h, gain, attn, four, wpa, wpf, wg, wo]
    aliases = {}
    if prev_out is not None:
        extra, aliases = _alias_tail(len(args), [prev_out])
        in_specs += extra
        args.append(prev_out)
    m = rows.m
    vmem = ((Q_DIM + FOURIER_DIM + 3 * D) * D * 2 + 4 * m * D * 4 + m * D * 2
            + 2 * m * (Q_DIM + FOURIER_DIM) * 2 + 10 * m * tn * 4 + (4 << 20))
    return pl.pallas_call(
        functools.partial(_mix_body, tn=tn, n_alias=len(aliases)),
        grid=rows.grid,
        in_specs=in_specs,
        out_specs=rows.spec(D),
        out_shape=jax.ShapeDtypeStruct((B, T, D), _F32),
        scratch_shapes=[pltpu.VMEM((m, D), _BF16)],
        input_output_aliases=aliases,
        compiler_params=_cparams(("parallel", "arbitrary"), vmem),
        name="mixer_out" + ("_meta" if rows.meta else ""),
    )(*args)


def _rope_tables(n_real, n_meta):
    T = n_real + n_meta
    half = ROT_DIM // 2
    inv = 1.0 / (ROPE_THETA ** (jnp.arange(0, ROT_DIM, 2, dtype=_F32) / ROT_DIM))
    ang = jnp.arange(T, dtype=_F32)[:, None] * inv[None, :]
    cos, sin = jnp.cos(ang), jnp.sin(ang)
    cos = jnp.concatenate([cos[n_meta:], cos[:n_meta]], axis=0)
    sin = jnp.concatenate([sin[n_meta:], sin[:n_meta]], axis=0)
    ones = jnp.ones((T, HEAD_DIM - ROT_DIM), _F32)
    zeros_h = jnp.zeros((T, half), _F32)
    zeros_r = jnp.zeros((T, HEAD_DIM - ROT_DIM), _F32)
    c = jnp.concatenate([cos, cos, ones], axis=1)
    s1 = jnp.concatenate([-sin, zeros_h, zeros_r], axis=1)
    s2 = jnp.concatenate([zeros_h, sin, zeros_r], axis=1)
    return c, s1, s2


def _channel_dft_table():
    n = FOURIER_GROUP_DIM
    c = jnp.arange(n, dtype=jnp.int32)[:, None]
    k = jnp.arange(n, dtype=jnp.int32)[None, :]
    ang = ((c * k) % n).astype(_F32) * (2.0 * np.pi / n)
    return (jnp.concatenate([jnp.cos(ang), -jnp.sin(ang)], axis=1) * (n ** -0.5)).astype(_BF16)


def _pick_tile(n, pref):
    t = pref
    while n % t:
        t //= 2
    return t


def kernel(x, meta_tokens, ffn1_norm, ffn1_w_gate, ffn1_w_up, ffn1_w_down, mix_norm, w_in, w_gate, sink,
           w_attn_branch, w_fourier_branch, w_out, ffn2_norm, ffn2_w_gate, ffn2_w_up, ffn2_w_down, final_norm):
    B, S, D = x.shape
    n_meta = meta_tokens.shape[0]
    depth = ffn1_norm.shape[0]
    d_ff = ffn1_w_gate.shape[-1]
    T = S + n_meta
    assert S % BLOCK == 0 and S % n_meta == 0 and BLOCK % n_meta == 0
    assert w_in.shape[-1] == IN_DIM and KV_DIM == PROJ_CHUNK

    tm_ffn = _pick_tile(S, 1024)
    tm = _pick_tile(S, 512)
    tf = _pick_tile(d_ff, 512)
    tn = _pick_tile(D, 512)
    qb = _pick_tile(S // BLOCK, 8)
    rows_ffn = _Rows(B, S, n_meta, tm_ffn, meta=False)
    rows_main = _Rows(B, S, n_meta, tm, meta=False)
    rows_mix = _Rows(B, S, n_meta, _pick_tile(S, 256), meta=False)
    rows_meta = _Rows(B, S, n_meta, tm, meta=True)

    n1, n2 = _fft_factors(T)
    fft_tables = _fft_tables(T, n1, n2, n_meta, BF16_SUBLANES)
    rope_c, rope_s1, rope_s2 = _rope_tables(S, n_meta)
    dft_c = _channel_dft_table()

    ffn1_w = tuple(_to_bf16(w) for w in (ffn1_w_gate, ffn1_w_up, ffn1_w_down))
    ffn2_w = tuple(_to_bf16(w) for w in (ffn2_w_gate, ffn2_w_up, ffn2_w_down))
    w_in_b = _to_bf16(w_in)
    mix_w = tuple(_to_bf16(w) for w in (w_attn_branch, w_fourier_branch, w_gate, w_out))

    def ffn(h, h_meta, gain, wts, layer, final_gain=None):
        out = _ffn(rows_ffn, h, gain, *wts, layer=layer, tf=tf, T=T, final_gain=final_gain)
        if final_gain is None:
            src, blk0 = h_meta
            out = _ffn(rows_meta, src, gain, *wts, layer=layer, tf=tf, T=T, h_row_block0=blk0, prev_out=out)
        return out

    meta = jnp.broadcast_to(meta_tokens[None].astype(x.dtype), (B, n_meta, D))
    h = None
    for l in range(depth):
        if l == 0:
            h = ffn(x, (meta, 0), ffn1_norm[l][None], ffn1_w, l)
        else:
            h = ffn(h, (h, None), ffn1_norm[l][None], ffn1_w, l)

        gain = mix_norm[l][None]
        proj_args = (gain, w_in_b, rope_c, rope_s1, rope_s2, dft_c)
        qkv, z = _proj(rows_main, h, *proj_args, layer=l)
        qkv, z = _proj(rows_meta, h, *proj_args, layer=l, prev_out=(qkv, z))
        attn = _attention(qkv, sink[l], n_real=S, n_meta=n_meta, qb=qb)
        four = _position_dft(z, fft_tables, n1=n1, n2=n2, R=BF16_SUBLANES, n_meta=n_meta)
        mix_args = (gain, attn, four) + mix_w
        h2 = _mix_out(rows_mix, h, *mix_args, layer=l, tn=tn)
        h = _mix_out(rows_meta, h, *mix_args, layer=l, tn=tn, prev_out=h2)

        last = l == depth - 1
        h = ffn(h, (h, None), ffn2_norm[l][None], ffn2_w, l, final_gain=final_norm[None] if last else None)
    return h
```

```python
import functools

import numpy as np
import jax
import jax.numpy as jnp
from jax import lax
from jax.experimental import pallas as pl
from jax.experimental.pallas import tpu as pltpu

_F32 = jnp.float32
_BF16 = jnp.bfloat16

HEAD_DIM = 128
N_Q_HEADS = 16
N_KV_HEADS = 4
Q_PER_KV = N_Q_HEADS // N_KV_HEADS
WINDOW = 128
BLOCK = 128
ROPE_THETA = 500000.0
ROT_DIM = HEAD_DIM // 4
N_FOURIER_GROUPS = 8
FOURIER_GROUP_DIM = 128
FOURIER_DIM = N_FOURIER_GROUPS * FOURIER_GROUP_DIM
Q_DIM = N_Q_HEADS * HEAD_DIM
KV_DIM = N_KV_HEADS * HEAD_DIM
IN_DIM = Q_DIM + 2 * KV_DIM + FOURIER_DIM
RMS_EPS = 1e-6
NEG_INF = -1e30

LANES = 128
BF16_SUBLANES = 16
PROJ_CHUNK = Q_PER_KV * HEAD_DIM
V7X_VMEM_BYTES = 64 * 1024 * 1024


def _round_up(a, m):
    return (a + m - 1) // m * m


def _cparams(sem, vmem_bytes):
    return pltpu.CompilerParams(dimension_semantics=sem,
                                vmem_limit_bytes=int(min(vmem_bytes, V7X_VMEM_BYTES - (4 << 20))))


def _rms(x, g):
    return x * lax.rsqrt(jnp.mean(x * x, axis=-1, keepdims=True) + RMS_EPS) * g


class _Rows:
    def __init__(self, B, n_real, n_meta, tm, meta):
        self.B, self.n_real, self.n_meta, self.meta = B, n_real, n_meta, meta
        self.nb = B if meta else 1
        self.rows = n_meta if meta else tm
        self.grid = (1, 1) if meta else (B, n_real // tm)
        self.m = self.nb * self.rows

    def spec(self, width, lead=(), row_block0=None):
        nl = len(lead)
        if self.meta:
            r0 = self.n_real // self.n_meta if row_block0 is None else row_block0
            return pl.BlockSpec((self.nb,) + tuple(lead) + (self.rows, width),
                                lambda b, i, *_: (0,) + (0,) * nl + (r0, 0))
        return pl.BlockSpec((1,) + tuple(lead) + (self.rows, width),
                            lambda b, i, *_: (b,) + (0,) * nl + (i, 0))

    def table_spec(self, width):
        if self.meta:
            r0 = self.n_real // self.n_meta
            return pl.BlockSpec((self.rows, width), lambda b, i, *_: (r0, 0))
        return pl.BlockSpec((self.rows, width), lambda b, i, *_: (i, 0))


def _const_spec(shape, single_buffer=False):
    nd = len(shape)
    if single_buffer:
        return pl.BlockSpec(shape, lambda *_: (0,) * nd, pipeline_mode=pl.Buffered(1))
    return pl.BlockSpec(shape, lambda *_: (0,) * nd)


def _alias_tail(n_in, outs):
    specs = [pl.BlockSpec(memory_space=pl.ANY) for _ in outs]
    return specs, {n_in + k: k for k in range(len(outs))}


def _cast_body(w_ref, o_ref):
    o_ref[...] = w_ref[...].astype(_BF16)


def _to_bf16(w):
    L, K, N = w.shape
    tk = _pick_tile(K, 512)
    while tk > 8 and tk * N * 4 > (6 << 20):
        tk //= 2
    return pl.pallas_call(
        _cast_body,
        grid=(L, K // tk),
        in_specs=[pl.BlockSpec((1, tk, N), lambda l, i: (l, i, 0))],
        out_specs=pl.BlockSpec((1, tk, N), lambda l, i: (l, i, 0)),
        out_shape=jax.ShapeDtypeStruct(w.shape, _BF16),
        compiler_params=_cparams(("parallel", "parallel"), 6 * tk * N * 2 + (4 << 20)),
        name="cast_bf16",
    )(w)


def _ffn_body(*refs, nj, final, n_alias):
    h_ref, gain_ref, wg_ref, wu_ref, wd_ref = refs[:5]
    rest = refs[5:]
    fgain_ref = None
    if final:
        fgain_ref, rest = rest[0], rest[1:]
    o_ref, xn_ref = rest[n_alias:]
    j = pl.program_id(2)
    D = h_ref.shape[-1]

    @pl.when(j == 0)
    def _():
        x = h_ref[...].reshape(-1, D)
        xn_ref[...] = _rms(x, gain_ref[...]).astype(_BF16)
        o_ref[...] = x.reshape(o_ref.shape)

    g = jnp.dot(xn_ref[...], wg_ref[...], preferred_element_type=_F32)
    u = jnp.dot(xn_ref[...], wu_ref[...], preferred_element_type=_F32)
    act = ((g * jax.nn.sigmoid(g)) * u * 0.5).astype(_BF16)
    o_ref[...] += jnp.dot(act, wd_ref[...], preferred_element_type=_F32).reshape(o_ref.shape)

    if final:
        @pl.when(j == nj - 1)
        def _():
            o_ref[...] = _rms(o_ref[...].reshape(-1, D), fgain_ref[...]).reshape(o_ref.shape)


def _ffn(rows, h, gain, wg, wu, wd, *, layer, tf, T, final_gain=None, h_row_block0=None, prev_out=None):
    D = h.shape[-1]
    nj = wg.shape[2] // tf
    final = final_gain is not None
    in_specs = [rows.spec(D, row_block0=h_row_block0), _const_spec((1, D)),
                pl.BlockSpec((None, D, tf), lambda b, i, j: (layer, 0, j)),
                pl.BlockSpec((None, D, tf), lambda b, i, j: (layer, 0, j)),
                pl.BlockSpec((None, tf, D), lambda b, i, j: (layer, j, 0))]
    args = [h, gain, wg, wu, wd]
    if final:
        in_specs.append(_const_spec((1, D)))
        args.append(final_gain)
    aliases = {}
    if prev_out is not None:
        extra, aliases = _alias_tail(len(args), [prev_out])
        in_specs += extra
        args.append(prev_out)
    m = rows.m
    vmem = (4 * m * D * 4 + m * D * 2 + 2 * (D * 2 * tf + tf * D) * 2 + m * 2 * tf * 4 + m * tf * 2 + (4 << 20))
    out_rows = rows.n_real if final else T
    return pl.pallas_call(
        functools.partial(_ffn_body, nj=nj, final=final, n_alias=len(aliases)),
        grid=rows.grid + (nj,),
        in_specs=in_specs,
        out_specs=rows.spec(D),
        out_shape=jax.ShapeDtypeStruct((rows.B, out_rows, D), _F32),
        scratch_shapes=[pltpu.VMEM((m, D), _BF16)],
        input_output_aliases=aliases,
        compiler_params=_cparams(("parallel", "parallel", "arbitrary"), vmem),
        name=("ffn_final" if final else "ffn") + ("_meta" if rows.meta else ""),
    )(*args)


def _proj_body(*refs, n_rope, n_qkv, n_chunks, nb, n_alias):
    h_ref, gain_ref, w_ref, c_ref, s1_ref, s2_ref, dft_ref = refs[:7]
    qkv_ref, z_ref, xn_ref = refs[7 + n_alias:]
    D = h_ref.shape[-1]
    rows = h_ref.shape[-2]
    xn_ref[...] = _rms(h_ref[...].reshape(-1, D), gain_ref[...]).astype(_BF16)
    c = jnp.concatenate([c_ref[...]] * nb, axis=0)
    s1 = jnp.concatenate([s1_ref[...]] * nb, axis=0)
    s2 = jnp.concatenate([s2_ref[...]] * nb, axis=0)
    half = ROT_DIM // 2
    for ch in range(n_chunks):
        cols = slice(ch * PROJ_CHUNK, (ch + 1) * PROJ_CHUNK)
        y = jnp.dot(xn_ref[...], w_ref[:, cols], preferred_element_type=_F32)
        if ch < n_rope:
            for hh in range(PROJ_CHUNK // HEAD_DIM):
                xh = y[:, hh * HEAD_DIM:(hh + 1) * HEAD_DIM]
                up = pltpu.roll(xh, HEAD_DIM - half, 1)
                dn = pltpu.roll(xh, half, 1)
                r = (xh * c + up * s1 + dn * s2).astype(_BF16)
                lo = ch * PROJ_CHUNK + hh * HEAD_DIM
                qkv_ref[:, :, lo:lo + HEAD_DIM] = r.reshape(nb, rows, HEAD_DIM)
        elif ch < n_qkv:
            qkv_ref[:, :, cols] = y.astype(_BF16).reshape(nb, rows, PROJ_CHUNK)
        else:
            ub = y.astype(_BF16)
            for gg in range(PROJ_CHUNK // FOURIER_GROUP_DIM):
                zz = jnp.dot(ub[:, gg * FOURIER_GROUP_DIM:(gg + 1) * FOURIER_GROUP_DIM], dft_ref[...],
                             preferred_element_type=_F32)
                lo = (ch - n_qkv) * PROJ_CHUNK + gg * FOURIER_GROUP_DIM
                z_ref[:, 0, :, lo:lo + FOURIER_GROUP_DIM] = (
                    zz[:, :FOURIER_GROUP_DIM].astype(_BF16).reshape(nb, rows, FOURIER_GROUP_DIM))
                z_ref[:, 1, :, lo:lo + FOURIER_GROUP_DIM] = (
                    zz[:, FOURIER_GROUP_DIM:].astype(_BF16).reshape(nb, rows, FOURIER_GROUP_DIM))


def _proj(rows, h, gain, w_in, rope_c, rope_s1, rope_s2, dft_c, *, layer, prev_out=None):
    B, T, D = h.shape
    n_rope = (Q_DIM + KV_DIM) // PROJ_CHUNK
    n_qkv = (Q_DIM + 2 * KV_DIM) // PROJ_CHUNK
    n_chunks = IN_DIM // PROJ_CHUNK
    w_spec = pl.BlockSpec((None, D, IN_DIM), lambda *_: (layer, 0, 0), pipeline_mode=pl.Buffered(1))
    in_specs = [rows.spec(D), _const_spec((1, D)), w_spec,
                rows.table_spec(LANES), rows.table_spec(LANES), rows.table_spec(LANES),
                _const_spec((FOURIER_GROUP_DIM, 2 * FOURIER_GROUP_DIM))]
    args = [h, gain, w_in, rope_c, rope_s1, rope_s2, dft_c]
    aliases = {}
    if prev_out is not None:
        extra, aliases = _alias_tail(len(args), prev_out)
        in_specs += extra
        args += list(prev_out)
    m = rows.m
    vmem = (2 * m * D * 4 + m * D * 2 + D * IN_DIM * 2 + 6 * m * LANES * 4
            + 2 * m * (Q_DIM + 2 * KV_DIM) * 2 + 4 * m * FOURIER_DIM * 2 + 8 * m * PROJ_CHUNK * 4 + (6 << 20))
    return pl.pallas_call(
        functools.partial(_proj_body, n_rope=n_rope, n_qkv=n_qkv, n_chunks=n_chunks, nb=rows.nb,
                          n_alias=len(aliases)),
        grid=rows.grid,
        in_specs=in_specs,
        out_specs=[rows.spec(Q_DIM + 2 * KV_DIM), rows.spec(FOURIER_DIM, lead=(2,))],
        out_shape=[jax.ShapeDtypeStruct((B, T, Q_DIM + 2 * KV_DIM), _BF16),
                   jax.ShapeDtypeStruct((B, 2, T, FOURIER_DIM), _BF16)],
        scratch_shapes=[pltpu.VMEM((m, D), _BF16)],
        input_output_aliases=aliases,
        compiler_params=_cparams(("parallel", "arbitrary"), vmem),
        name="mixer_proj" + ("_meta" if rows.meta else ""),
    )(*args)


def _stack_heads(q):
    return jnp.concatenate([q[:, h * HEAD_DIM:(h + 1) * HEAD_DIM] for h in range(Q_PER_KV)], axis=0)


def _sink_column(sink_ref, g, rows_per_head):
    return jnp.concatenate(
        [jnp.full((rows_per_head, 1), sink_ref[g * Q_PER_KV + h], _F32) for h in range(Q_PER_KV)], axis=0)


def _pad_rows(a, rows):
    return jnp.concatenate([a, jnp.zeros((rows - a.shape[0], a.shape[1]), a.dtype)], axis=0)


def _softmax_pv(q4, k_all, v_all, masks, sink):
    rows = q4.shape[0]
    s = lax.dot_general(q4, k_all, (((1,), (1,)), ((), ())), preferred_element_type=_F32) * (HEAD_DIM ** -0.5)
    blocks = []
    for i, mk in enumerate(masks):
        sb = s[:, i * BLOCK:(i + 1) * BLOCK]
        blocks.append(sb if mk is None else jnp.where(mk, sb, NEG_INF))
    mx = jnp.maximum(jnp.max(functools.reduce(jnp.maximum, blocks), axis=-1, keepdims=True), sink)
    p = jnp.concatenate([jnp.exp(sb - mx).astype(_BF16) for sb in blocks], axis=1)
    v_ext = jnp.concatenate([v_all, jnp.ones(v_all.shape, _BF16)], axis=1)
    o = jnp.dot(p, v_ext, preferred_element_type=_F32)
    den = o[:, HEAD_DIM:] + jnp.exp(sink - mx)
    return o[:, :HEAD_DIM] / den


def _attn_body(sink_ref, q_ref, kp_ref, kc_ref, kn_ref, km_ref, vp_ref, vc_ref, vn_ref, vm_ref, o_ref,
               kbuf, vbuf, s_buf, p_buf, es_buf, *, qb, ntile, n_meta):
    g = pl.program_id(1)
    jt = pl.program_id(2)
    rows = Q_PER_KV * BLOCK
    band = 3 * BLOCK
    meta0 = (qb + 2) * BLOCK
    scale = HEAD_DIM ** -0.5
    c_exp2 = scale * 1.4426950408889634

    kbuf[0:BLOCK] = kp_ref[0]
    kbuf[BLOCK:(qb + 1) * BLOCK] = kc_ref[0]
    kbuf[(qb + 1) * BLOCK:meta0] = kn_ref[0]
    kbuf[meta0:meta0 + BLOCK] = _pad_rows(km_ref[0], BLOCK)
    vbuf[0:BLOCK, 0:HEAD_DIM] = vp_ref[0]
    vbuf[BLOCK:(qb + 1) * BLOCK, 0:HEAD_DIM] = vc_ref[0]
    vbuf[(qb + 1) * BLOCK:meta0, 0:HEAD_DIM] = vn_ref[0]
    vbuf[meta0:meta0 + BLOCK, 0:HEAD_DIM] = _pad_rows(vm_ref[0], BLOCK)
    vbuf[:, HEAD_DIM:] = jnp.ones((meta0 + BLOCK, HEAD_DIM), _BF16)

    row = lax.broadcasted_iota(jnp.int32, (rows, BLOCK), 0) % BLOCK
    col = lax.broadcasted_iota(jnp.int32, (rows, BLOCK), 1)
    diff = col - row
    meta_ok = col < n_meta
    never = 2 * BLOCK
    sink_s = _sink_column(sink_ref, g, BLOCK) * (1.0 / scale)

    def dot_t(a, b):
        return lax.dot_general(a, b, (((1,), (1,)), ((), ())), preferred_element_type=_F32)

    def stage_a(i, slot):
        r0 = pl.multiple_of(i * BLOCK, BLOCK)
        q4 = _stack_heads(q_ref[0, pl.ds(r0, BLOCK), :])
        s_buf[slot, :, 0:band] = dot_t(q4, kbuf[pl.ds(r0, band), :])
        s_buf[slot, :, band:] = dot_t(q4, kbuf[meta0:meta0 + BLOCK, :])

    def stage_b(i, slot):
        first = jnp.logical_and(jt == 0, i == 0)
        last = jnp.logical_and(jt == ntile - 1, i == qb - 1)
        off_p = jnp.where(first, never, 0)
        off_n = jnp.where(last, -never, 0)
        blocks = [jnp.where(diff >= off_p, s_buf[slot, :, 0:BLOCK], NEG_INF),
                  s_buf[slot, :, BLOCK:2 * BLOCK],
                  jnp.where(diff <= off_n, s_buf[slot, :, 2 * BLOCK:band], NEG_INF),
                  jnp.where(meta_ok, s_buf[slot, :, band:], NEG_INF)]
        mx = jnp.maximum(jnp.max(functools.reduce(jnp.maximum, blocks), axis=-1, keepdims=True), sink_s)
        for n, sb in enumerate(blocks):
            p_buf[slot, :, n * BLOCK:(n + 1) * BLOCK] = jnp.exp2((sb - mx) * c_exp2).astype(_BF16)
        es_buf[slot] = jnp.broadcast_to(jnp.exp2((sink_s - mx) * c_exp2), (rows, HEAD_DIM))

    def stage_c(i, slot):
        r0 = pl.multiple_of(i * BLOCK, BLOCK)
        o = (jnp.dot(p_buf[slot, :, 0:band], vbuf[pl.ds(r0, band), :], preferred_element_type=_F32)
             + jnp.dot(p_buf[slot, :, band:], vbuf[meta0:meta0 + BLOCK, :], preferred_element_type=_F32))
        out = o[:, :HEAD_DIM] / (o[:, HEAD_DIM:] + es_buf[slot])
        for h in range(Q_PER_KV):
            o_ref[0, pl.ds(r0, BLOCK), h * HEAD_DIM:(h + 1) * HEAD_DIM] = (
                out[h * BLOCK:(h + 1) * BLOCK].astype(_BF16))

    stage_a(0, 0)
    stage_a(1, 1)
    stage_b(0, 0)

    def pair(t, carry):
        i = 2 * t
        stage_c(i, 0)
        stage_b(i + 1, 1)
        stage_a(i + 2, 0)
        stage_c(i + 1, 1)
        stage_b(i + 2, 0)
        stage_a(i + 3, 1)
        return carry

    lax.fori_loop(0, (qb - 2) // 2, pair, 0)
    stage_c(qb - 2, 0)
    stage_b(qb - 1, 1)
    stage_c(qb - 1, 1)


def _attn_meta_body(sink_ref, q_ref, k0_ref, km_ref, v0_ref, vm_ref, prev_out_ref, o_ref, *, n_meta):
    del prev_out_ref
    g = pl.program_id(1)
    rows = Q_PER_KV * n_meta
    row = lax.broadcasted_iota(jnp.int32, (rows, BLOCK), 0) % n_meta
    col = lax.broadcasted_iota(jnp.int32, (rows, BLOCK), 1)
    q4 = _stack_heads(q_ref[0])
    masks = [col <= row + (WINDOW - n_meta), col < n_meta]
    k_all = jnp.concatenate([k0_ref[0], _pad_rows(km_ref[0], BLOCK)], axis=0)
    v_all = jnp.concatenate([v0_ref[0], _pad_rows(vm_ref[0], BLOCK)], axis=0)
    out = _softmax_pv(q4, k_all, v_all, masks, _sink_column(sink_ref, g, n_meta))
    for h in range(Q_PER_KV):
        o_ref[0, :, h * HEAD_DIM:(h + 1) * HEAD_DIM] = out[h * n_meta:(h + 1) * n_meta].astype(_BF16)


def _attention(qkv, sink, *, n_real, n_meta, qb):
    B, T, _ = qkv.shape
    nblk = n_real // BLOCK
    ntile = nblk // qb
    kcol = Q_DIM // HEAD_DIM
    vcol = (Q_DIM + KV_DIM) // HEAD_DIM
    mrow = n_real // n_meta

    def edge_spec(rowf, col0):
        return pl.BlockSpec((1, BLOCK, HEAD_DIM), lambda b, g, j, s: (b, rowf(j), col0 + g))

    def main_spec(col0):
        return pl.BlockSpec((1, qb * BLOCK, HEAD_DIM), lambda b, g, j, s: (b, j, col0 + g))

    def meta_spec(col0):
        return pl.BlockSpec((1, n_meta, HEAD_DIM), lambda b, g, j, s: (b, mrow, col0 + g))

    def prev_i(j):
        return jnp.maximum(qb * j - 1, 0)

    def next_i(j):
        return jnp.minimum(qb * j + qb, nblk - 1)

    attn = pl.pallas_call(
        functools.partial(_attn_body, qb=qb, ntile=ntile, n_meta=n_meta),
        grid_spec=pltpu.PrefetchScalarGridSpec(
            num_scalar_prefetch=1,
            grid=(B, N_KV_HEADS, ntile),
            in_specs=[
                pl.BlockSpec((1, qb * BLOCK, PROJ_CHUNK), lambda b, g, j, s: (b, j, g)),
                edge_spec(prev_i, kcol), main_spec(kcol), edge_spec(next_i, kcol), meta_spec(kcol),
                edge_spec(prev_i, vcol), main_spec(vcol), edge_spec(next_i, vcol), meta_spec(vcol),
            ],
            out_specs=pl.BlockSpec((1, qb * BLOCK, PROJ_CHUNK), lambda b, g, j, s: (b, j, g)),
            scratch_shapes=[
                pltpu.VMEM(((qb + 3) * BLOCK, HEAD_DIM), _BF16),
                pltpu.VMEM(((qb + 3) * BLOCK, 2 * HEAD_DIM), _BF16),
                pltpu.VMEM((2, Q_PER_KV * BLOCK, 4 * BLOCK), _F32),
                pltpu.VMEM((2, Q_PER_KV * BLOCK, 4 * BLOCK), _BF16),
                pltpu.VMEM((2, Q_PER_KV * BLOCK, HEAD_DIM), _F32),
            ],
        ),
        out_shape=jax.ShapeDtypeStruct((B, T, Q_DIM), _BF16),
        compiler_params=_cparams(("parallel", "parallel", "arbitrary"), 32 << 20),
        name="windowed_gqa",
    )(sink, qkv, qkv, qkv, qkv, qkv, qkv, qkv, qkv, qkv)

    def blk0_spec(col0):
        return pl.BlockSpec((1, BLOCK, HEAD_DIM), lambda b, g, s: (b, 0, col0 + g))

    def meta2_spec(col0):
        return pl.BlockSpec((1, n_meta, HEAD_DIM), lambda b, g, s: (b, mrow, col0 + g))

    return pl.pallas_call(
        functools.partial(_attn_meta_body, n_meta=n_meta),
        grid_spec=pltpu.PrefetchScalarGridSpec(
            num_scalar_prefetch=1,
            grid=(B, N_KV_HEADS),
            in_specs=[
                pl.BlockSpec((1, n_meta, PROJ_CHUNK), lambda b, g, s: (b, mrow, g)),
                blk0_spec(kcol), meta2_spec(kcol), blk0_spec(vcol), meta2_spec(vcol),
                pl.BlockSpec(memory_space=pl.ANY),
            ],
            out_specs=pl.BlockSpec((1, n_meta, PROJ_CHUNK), lambda b, g, s: (b, mrow, g)),
        ),
        out_shape=jax.ShapeDtypeStruct((B, T, Q_DIM), _BF16),
        input_output_aliases={6: 0},
        compiler_params=_cparams(("parallel", "arbitrary"), 32 << 20),
        name="windowed_gqa_meta",
    )(sink, qkv, qkv, qkv, qkv, qkv, attn)


def _fft_factors(T):
    best = None
    for n2 in range(16, T + 1, 16):
        if T % n2:
            continue
        n1 = T // n2
        cost = 4 * n1 + 2 * n2
        if best is None or cost < best[0]:
            best = (cost, n1, n2)
    assert best is not None, "sequence length must be a multiple of 16"
    return best[1], best[2]


def _fft1_body(z_ref, mc_ref, ms_ref, twc_ref, tws_ref, a_ref, *, n1, R, C):
    rows = n1 * R
    rhs = jnp.concatenate([z_ref[0, 0].reshape(rows, C), z_ref[0, 1].reshape(rows, C)], axis=1)
    xc = jnp.dot(mc_ref[...], rhs, preferred_element_type=_F32)
    xs = jnp.dot(ms_ref[...], rhs, preferred_element_type=_F32)
    ar = xc[:, :C] + xs[:, C:]
    ai = xc[:, C:] - xs[:, :C]
    twc = jnp.concatenate([twc_ref[0]] * (C // LANES), axis=1)
    tws = jnp.concatenate([tws_ref[0]] * (C // LANES), axis=1)
    a_ref[0, 0] = (ar * twc + ai * tws).reshape(n1, R, C).astype(_BF16)
    a_ref[0, 1] = (ai * twc - ar * tws).reshape(n1, R, C).astype(_BF16)


def _fft2_body(ar_ref, ai_ref, mc_ref, ms_ref, f_ref):
    res = (jnp.dot(mc_ref[0], ar_ref[0, 0, 0], preferred_element_type=_F32)
           + jnp.dot(ms_ref[0], ai_ref[0, 0, 0], preferred_element_type=_F32))
    f_ref[0] = res.astype(_BF16)


def _fft_tables(T, n1, n2, n_meta, R):
    two_pi = 2.0 * np.pi
    k1 = jnp.arange(n1, dtype=jnp.int32)
    ang1 = ((k1[:, None] * k1[None, :]) % n1).astype(_F32) * (two_pi / n1)
    eye = jnp.eye(R, dtype=_F32)
    m1c = jnp.kron(jnp.cos(ang1) * (n1 ** -0.5), eye).astype(_BF16)
    m1s = jnp.kron(jnp.sin(ang1) * (n1 ** -0.5), eye).astype(_BF16)
    nn2 = jnp.arange(n2, dtype=jnp.int32).reshape(n2 // R, 1, R)
    angt = ((k1[None, :, None] * (nn2 + n_meta)) % T).astype(_F32) * (two_pi / T)
    angt = jnp.broadcast_to(angt.reshape(n2 // R, n1 * R, 1), (n2 // R, n1 * R, LANES))
    twc, tws = jnp.cos(angt), jnp.sin(angt)
    k2 = jnp.arange(n2, dtype=jnp.int32)[:, None]
    mm2 = jnp.arange(n2, dtype=jnp.int32)[None, :]
    outs_c, outs_s = [], []
    for rot in (n_meta // n1, n_meta // n1 + 1):
        ph2 = (((k2 + rot) % n2) * (mm2 + n_meta)) % n2
        ang2 = ph2.astype(_F32) * (two_pi / n2)
        outs_c.append(jnp.cos(ang2) * (n2 ** -0.5))
        outs_s.append(jnp.sin(ang2) * (n2 ** -0.5))
    return m1c, m1s, twc, tws, jnp.stack(outs_c).astype(_BF16), jnp.stack(outs_s).astype(_BF16)


def _position_dft(z, tables, *, n1, n2, R, n_meta):
    B, _, T, C = z.shape
    m1c, m1s, twc, tws, m2c, m2s = tables
    rows = n1 * R
    zv = z.reshape(B, 2, n1, n2, C)
    av = pl.pallas_call(
        functools.partial(_fft1_body, n1=n1, R=R, C=C),
        grid=(B, n2 // R),
        in_specs=[
            pl.BlockSpec((1, 2, n1, R, C), lambda b, t: (b, 0, 0, t, 0)),
            _const_spec((rows, rows)),
            _const_spec((rows, rows)),
            pl.BlockSpec((1, rows, LANES), lambda b, t: (t, 0, 0)),
            pl.BlockSpec((1, rows, LANES), lambda b, t: (t, 0, 0)),
        ],
        out_specs=pl.BlockSpec((1, 2, n1, R, C), lambda b, t: (b, 0, 0, t, 0)),
        out_shape=jax.ShapeDtypeStruct((B, 2, n1, n2, C), _BF16),
        compiler_params=_cparams(("parallel", "arbitrary"), 48 << 20),
        name="dft_stage1",
    )(zv, m1c, m1s, twc, tws)
    shift = n1 - (n_meta % n1)

    f = pl.pallas_call(
        _fft2_body,
        grid=(B, n1),
        in_specs=[
            pl.BlockSpec((1, 1, 1, n2, C), lambda b, k: (b, 0, k, 0, 0)),
            pl.BlockSpec((1, 1, 1, n2, C), lambda b, k: (b, 1, k, 0, 0)),
            pl.BlockSpec((1, n2, n2), lambda b, k: (jnp.where(k < n_meta % n1, 1, 0), 0, 0)),
            pl.BlockSpec((1, n2, n2), lambda b, k: (jnp.where(k < n_meta % n1, 1, 0), 0, 0)),
        ],
        out_specs=pl.BlockSpec((1, n2, C), lambda b, k: (b, 0, (k + shift) % n1)),
        out_shape=jax.ShapeDtypeStruct((B, n2, n1 * C), _BF16),
        compiler_params=_cparams(("parallel", "arbitrary"), 32 << 20),
        name="dft_stage2",
    )(av, av, m2c, m2s)
    return f.reshape(B, T, C)


def _mix_body(*refs, tn, n_alias):
    h_ref, gain_ref, a_ref, f_ref, wpa_ref, wpf_ref, wg_ref, wo_ref = refs[:8]
    o_ref, xn_ref = refs[8 + n_alias:]
    D = h_ref.shape[-1]
    x = h_ref[...].reshape(-1, D)
    xn_ref[...] = _rms(x, gain_ref[...]).astype(_BF16)
    o_ref[...] = x.reshape(o_ref.shape)
    a = a_ref[...].reshape(-1, Q_DIM)
    f = f_ref[...].reshape(-1, FOURIER_DIM)
    for j in range(D // tn):
        cols = slice(j * tn, (j + 1) * tn)
        av = jnp.dot(a, wpa_ref[:, cols], preferred_element_type=_F32)
        fv = jnp.dot(f, wpf_ref[:, cols], preferred_element_type=_F32)
        ga = jax.nn.sigmoid(jnp.dot(xn_ref[...], wg_ref[:, cols], preferred_element_type=_F32))
        gf = jax.nn.sigmoid(jnp.dot(xn_ref[...], wg_ref[:, D + j * tn:D + (j + 1) * tn],
                                    preferred_element_type=_F32))
        mixed = (ga * av + gf * fv).astype(_BF16)
        o_ref[...] += jnp.dot(mixed, wo_ref[cols, :], preferred_element_type=_F32).reshape(o_ref.shape)


def _mix_out(rows, h, gain, attn, four, wpa, wpf, wg, wo, *, layer, tn, prev_out=None):
    B, T, D = h.shape

    def resident(k, n):
        return pl.BlockSpec((None, k, n), lambda *_: (layer, 0, 0), pipeline_mode=pl.Buffered(1))

    in_specs = [rows.spec(D), _const_spec((1, D)), rows.spec(Q_DIM), rows.spec(FOURIER_DIM),
                resident(Q_DIM, D), resident(FOURIER_DIM, D), resident(D, 2 * D), resident(D, D)]
    args = [h, gain, attn, four, wpa, wpf, wg, wo]
    aliases = {}
    if prev_out is not None:
        extra, aliases = _alias_tail(len(args), [prev_out])
        in_specs += extra
        args.append(prev_out)
    m = rows.m
    vmem = ((Q_DIM + FOURIER_DIM + 3 * D) * D * 2 + 4 * m * D * 4 + m * D * 2
            + 2 * m * (Q_DIM + FOURIER_DIM) * 2 + 10 * m * tn * 4 + (4 << 20))
    return pl.pallas_call(
        functools.partial(_mix_body, tn=tn, n_alias=len(aliases)),
        grid=rows.grid,
        in_specs=in_specs,
        out_specs=rows.spec(D),
        out_shape=jax.ShapeDtypeStruct((B, T, D), _F32),
        scratch_shapes=[pltpu.VMEM((m, D), _BF16)],
        input_output_aliases=aliases,
        compiler_params=_cparams(("parallel", "arbitrary"), vmem),
        name="mixer_out" + ("_meta" if rows.meta else ""),
    )(*args)


def _rope_tables(n_real, n_meta):
    T = n_real + n_meta
    half = ROT_DIM // 2
    inv = 1.0 / (ROPE_THETA ** (jnp.arange(0, ROT_DIM, 2, dtype=_F32) / ROT_DIM))
    ang = jnp.arange(T, dtype=_F32)[:, None] * inv[None, :]
    cos, sin = jnp.cos(ang), jnp.sin(ang)
    cos = jnp.concatenate([cos[n_meta:], cos[:n_meta]], axis=0)
    sin = jnp.concatenate([sin[n_meta:], sin[:n_meta]], axis=0)
    ones = jnp.ones((T, HEAD_DIM - ROT_DIM), _F32)
    zeros_h = jnp.zeros((T, half), _F32)
    zeros_r = jnp.zeros((T, HEAD_DIM - ROT_DIM), _F32)
    c = jnp.concatenate([cos, cos, ones], axis=1)
    s1 = jnp.concatenate([-sin, zeros_h, zeros_r], axis=1)
    s2 = jnp.concatenate([zeros_h, sin, zeros_r], axis=1)
    return c, s1, s2


def _channel_dft_table():
    n = FOURIER_GROUP_DIM
    c = jnp.arange(n, dtype=jnp.int32)[:, None]
    k = jnp.arange(n, dtype=jnp.int32)[None, :]
    ang = ((c * k) % n).astype(_F32) * (2.0 * np.pi / n)
    return (jnp.concatenate([jnp.cos(ang), -jnp.sin(ang)], axis=1) * (n ** -0.5)).astype(_BF16)


def _pick_tile(n, pref):
    t = pref
    while n % t:
        t //= 2
    return t


def kernel(x, meta_tokens, ffn1_norm, ffn1_w_gate, ffn1_w_up, ffn1_w_down, mix_norm, w_in, w_gate, sink,
           w_attn_branch, w_fourier_branch, w_out, ffn2_norm, ffn2_w_gate, ffn2_w_up, ffn2_w_down, final_norm):
    B, S, D = x.shape
    n_meta = meta_tokens.shape[0]
    depth = ffn1_norm.shape[0]
    d_ff = ffn1_w_gate.shape[-1]
    T = S + n_meta
    assert S % (4 * BLOCK) == 0 and S % n_meta == 0 and BLOCK % n_meta == 0
    assert w_in.shape[-1] == IN_DIM and KV_DIM == PROJ_CHUNK

    tm_ffn = _pick_tile(S, 1024)
    tm = _pick_tile(S, 512)
    tf = _pick_tile(d_ff, 512)
    tn = _pick_tile(D, 512)
    qb = _pick_tile(S // BLOCK, 16)
    rows_ffn = _Rows(B, S, n_meta, tm_ffn, meta=False)
    rows_main = _Rows(B, S, n_meta, tm, meta=False)
    rows_mix = _Rows(B, S, n_meta, _pick_tile(S, 256), meta=False)
    rows_meta = _Rows(B, S, n_meta, tm, meta=True)

    n1, n2 = _fft_factors(T)
    fft_tables = _fft_tables(T, n1, n2, n_meta, BF16_SUBLANES)
    rope_c, rope_s1, rope_s2 = _rope_tables(S, n_meta)
    dft_c = _channel_dft_table()

    ffn1_w = tuple(_to_bf16(w) for w in (ffn1_w_gate, ffn1_w_up, ffn1_w_down))
    ffn2_w = tuple(_to_bf16(w) for w in (ffn2_w_gate, ffn2_w_up, ffn2_w_down))
    w_in_b = _to_bf16(w_in)
    mix_w = tuple(_to_bf16(w) for w in (w_attn_branch, w_fourier_branch, w_gate, w_out))

    def ffn(h, h_meta, gain, wts, layer, final_gain=None):
        out = _ffn(rows_ffn, h, gain, *wts, layer=layer, tf=tf, T=T, final_gain=final_gain)
        if final_gain is None:
            src, blk0 = h_meta
            out = _ffn(rows_meta, src, gain, *wts, layer=layer, tf=tf, T=T, h_row_block0=blk0, prev_out=out)
        return out

    meta = jnp.broadcast_to(meta_tokens[None].astype(x.dtype), (B, n_meta, D))
    h = None
    for l in range(depth):
        if l == 0:
            h = ffn(x, (meta, 0), ffn1_norm[l][None], ffn1_w, l)
        else:
            h = ffn(h, (h, None), ffn1_norm[l][None], ffn1_w, l)

        gain = mix_norm[l][None]
        proj_args = (gain, w_in_b, rope_c, rope_s1, rope_s2, dft_c)
        qkv, z = _proj(rows_main, h, *proj_args, layer=l)
        qkv, z = _proj(rows_meta, h, *proj_args, layer=l, prev_out=(qkv, z))
        attn = _attention(qkv, sink[l], n_real=S, n_meta=n_meta, qb=qb)
        four = _position_dft(z, fft_tables, n1=n1, n2=n2, R=BF16_SUBLANES, n_meta=n_meta)
        mix_args = (gain, attn, four) + mix_w
        h2 = _mix_out(rows_mix, h, *mix_args, layer=l, tn=tn)
        h = _mix_out(rows_meta, h, *mix_args, layer=l, tn=tn, prev_out=h2)

        last = l == depth - 1
        h = ffn(h, (h, None), ffn2_norm[l][None], ffn2_w, l, final_gain=final_norm[None] if last else None)
    return h
```

```python
import functools

import numpy as np
import jax
import jax.numpy as jnp
from jax import lax
from jax.experimental import pallas as pl
from jax.experimental.pallas import tpu as pltpu

_F32 = jnp.float32
_BF16 = jnp.bfloat16

HEAD_DIM = 128
N_Q_HEADS = 16
N_KV_HEADS = 4
Q_PER_KV = N_Q_HEADS // N_KV_HEADS
WINDOW = 128
BLOCK = 128
ROPE_THETA = 500000.0
ROT_DIM = HEAD_DIM // 4
N_FOURIER_GROUPS = 8
FOURIER_GROUP_DIM = 128
FOURIER_DIM = N_FOURIER_GROUPS * FOURIER_GROUP_DIM
Q_DIM = N_Q_HEADS * HEAD_DIM
KV_DIM = N_KV_HEADS * HEAD_DIM
IN_DIM = Q_DIM + 2 * KV_DIM + FOURIER_DIM
RMS_EPS = 1e-6
NEG_INF = -1e30

LANES = 128
BF16_SUBLANES = 16
PROJ_CHUNK = Q_PER_KV * HEAD_DIM
V7X_VMEM_BYTES = 64 * 1024 * 1024


def _round_up(a, m):
    return (a + m - 1) // m * m


def _cparams(sem, vmem_bytes):
    return pltpu.CompilerParams(dimension_semantics=sem,
                                vmem_limit_bytes=int(min(vmem_bytes, V7X_VMEM_BYTES - (4 << 20))))


def _rms(x, g):
    return x * lax.rsqrt(jnp.mean(x * x, axis=-1, keepdims=True) + RMS_EPS) * g


class _Rows:
    def __init__(self, B, n_real, n_meta, tm, meta):
        self.B, self.n_real, self.n_meta, self.meta = B, n_real, n_meta, meta
        self.nb = B if meta else 1
        self.rows = n_meta if meta else tm
        self.grid = (1, 1) if meta else (B, n_real // tm)
        self.m = self.nb * self.rows

    def spec(self, width, lead=(), row_block0=None):
        nl = len(lead)
        if self.meta:
            r0 = self.n_real // self.n_meta if row_block0 is None else row_block0
            return pl.BlockSpec((self.nb,) + tuple(lead) + (self.rows, width),
                                lambda b, i, *_: (0,) + (0,) * nl + (r0, 0))
        return pl.BlockSpec((1,) + tuple(lead) + (self.rows, width),
                            lambda b, i, *_: (b,) + (0,) * nl + (i, 0))

    def table_spec(self, width):
        if self.meta:
            r0 = self.n_real // self.n_meta
            return pl.BlockSpec((self.rows, width), lambda b, i, *_: (r0, 0))
        return pl.BlockSpec((self.rows, width), lambda b, i, *_: (i, 0))


def _const_spec(shape, single_buffer=False):
    nd = len(shape)
    if single_buffer:
        return pl.BlockSpec(shape, lambda *_: (0,) * nd, pipeline_mode=pl.Buffered(1))
    return pl.BlockSpec(shape, lambda *_: (0,) * nd)


def _alias_tail(n_in, outs):
    specs = [pl.BlockSpec(memory_space=pl.ANY) for _ in outs]
    return specs, {n_in + k: k for k in range(len(outs))}


def _cast_body(w_ref, o_ref):
    o_ref[...] = w_ref[...].astype(_BF16)


def _to_bf16(w):
    L, K, N = w.shape
    tk = _pick_tile(K, 512)
    while tk > 8 and tk * N * 4 > (6 << 20):
        tk //= 2
    return pl.pallas_call(
        _cast_body,
        grid=(L, K // tk),
        in_specs=[pl.BlockSpec((1, tk, N), lambda l, i: (l, i, 0))],
        out_specs=pl.BlockSpec((1, tk, N), lambda l, i: (l, i, 0)),
        out_shape=jax.ShapeDtypeStruct(w.shape, _BF16),
        compiler_params=_cparams(("parallel", "parallel"), 6 * tk * N * 2 + (4 << 20)),
        name="cast_bf16",
    )(w)


def _ffn_body(*refs, nj, final, n_alias):
    h_ref, gain_ref, wg_ref, wu_ref, wd_ref = refs[:5]
    rest = refs[5:]
    fgain_ref = None
    if final:
        fgain_ref, rest = rest[0], rest[1:]
    o_ref, xn_ref = rest[n_alias:]
    j = pl.program_id(2)
    D = h_ref.shape[-1]

    @pl.when(j == 0)
    def _():
        x = h_ref[...].reshape(-1, D)
        xn_ref[...] = _rms(x, gain_ref[...]).astype(_BF16)
        o_ref[...] = x.reshape(o_ref.shape)

    g = jnp.dot(xn_ref[...], wg_ref[...], preferred_element_type=_F32)
    u = jnp.dot(xn_ref[...], wu_ref[...], preferred_element_type=_F32)
    act = ((g * jax.nn.sigmoid(g)) * u * 0.5).astype(_BF16)
    o_ref[...] += jnp.dot(act, wd_ref[...], preferred_element_type=_F32).reshape(o_ref.shape)

    if final:
        @pl.when(j == nj - 1)
        def _():
            o_ref[...] = _rms(o_ref[...].reshape(-1, D), fgain_ref[...]).reshape(o_ref.shape)


def _ffn(rows, h, gain, wg, wu, wd, *, layer, tf, T, final_gain=None, h_row_block0=None, prev_out=None):
    D = h.shape[-1]
    nj = wg.shape[2] // tf
    final = final_gain is not None
    in_specs = [rows.spec(D, row_block0=h_row_block0), _const_spec((1, D)),
                pl.BlockSpec((None, D, tf), lambda b, i, j: (layer, 0, j)),
                pl.BlockSpec((None, D, tf), lambda b, i, j: (layer, 0, j)),
                pl.BlockSpec((None, tf, D), lambda b, i, j: (layer, j, 0))]
    args = [h, gain, wg, wu, wd]
    if final:
        in_specs.append(_const_spec((1, D)))
        args.append(final_gain)
    aliases = {}
    if prev_out is not None:
        extra, aliases = _alias_tail(len(args), [prev_out])
        in_specs += extra
        args.append(prev_out)
    m = rows.m
    vmem = (4 * m * D * 4 + m * D * 2 + 2 * (D * 2 * tf + tf * D) * 2 + m * 2 * tf * 4 + m * tf * 2 + (4 << 20))
    out_rows = rows.n_real if final else T
    return pl.pallas_call(
        functools.partial(_ffn_body, nj=nj, final=final, n_alias=len(aliases)),
        grid=rows.grid + (nj,),
        in_specs=in_specs,
        out_specs=rows.spec(D),
        out_shape=jax.ShapeDtypeStruct((rows.B, out_rows, D), _F32),
        scratch_shapes=[pltpu.VMEM((m, D), _BF16)],
        input_output_aliases=aliases,
        compiler_params=_cparams(("parallel", "parallel", "arbitrary"), vmem),
        name=("ffn_final" if final else "ffn") + ("_meta" if rows.meta else ""),
    )(*args)


def _proj_body(*refs, n_rope, n_qkv, n_chunks, nb, n_alias):
    h_ref, gain_ref, w_ref, c_ref, s1_ref, s2_ref, dft_ref = refs[:7]
    qkv_ref, z_ref, xno_ref = refs[7 + n_alias:]
    D = h_ref.shape[-1]
    rows = h_ref.shape[-2]
    xno_ref[...] = _rms(h_ref[...].reshape(-1, D), gain_ref[...]).astype(_BF16).reshape(xno_ref.shape)
    c = jnp.concatenate([c_ref[...]] * nb, axis=0)
    s1 = jnp.concatenate([s1_ref[...]] * nb, axis=0)
    s2 = jnp.concatenate([s2_ref[...]] * nb, axis=0)
    half = ROT_DIM // 2
    for ch in range(n_chunks):
        cols = slice(ch * PROJ_CHUNK, (ch + 1) * PROJ_CHUNK)
        y = jnp.dot(xno_ref[...].reshape(-1, D), w_ref[:, cols], preferred_element_type=_F32)
        if ch < n_rope:
            for hh in range(PROJ_CHUNK // HEAD_DIM):
                xh = y[:, hh * HEAD_DIM:(hh + 1) * HEAD_DIM]
                up = pltpu.roll(xh, HEAD_DIM - half, 1)
                dn = pltpu.roll(xh, half, 1)
                r = (xh * c + up * s1 + dn * s2).astype(_BF16)
                lo = ch * PROJ_CHUNK + hh * HEAD_DIM
                qkv_ref[:, :, lo:lo + HEAD_DIM] = r.reshape(nb, rows, HEAD_DIM)
        elif ch < n_qkv:
            qkv_ref[:, :, cols] = y.astype(_BF16).reshape(nb, rows, PROJ_CHUNK)
        else:
            ub = y.astype(_BF16)
            for gg in range(PROJ_CHUNK // FOURIER_GROUP_DIM):
                zz = jnp.dot(ub[:, gg * FOURIER_GROUP_DIM:(gg + 1) * FOURIER_GROUP_DIM], dft_ref[...],
                             preferred_element_type=_F32)
                lo = (ch - n_qkv) * PROJ_CHUNK + gg * FOURIER_GROUP_DIM
                z_ref[:, 0, :, lo:lo + FOURIER_GROUP_DIM] = (
                    zz[:, :FOURIER_GROUP_DIM].astype(_BF16).reshape(nb, rows, FOURIER_GROUP_DIM))
                z_ref[:, 1, :, lo:lo + FOURIER_GROUP_DIM] = (
                    zz[:, FOURIER_GROUP_DIM:].astype(_BF16).reshape(nb, rows, FOURIER_GROUP_DIM))


def _proj(rows, h, gain, w_in, rope_c, rope_s1, rope_s2, dft_c, *, layer, prev_out=None):
    B, T, D = h.shape
    n_rope = (Q_DIM + KV_DIM) // PROJ_CHUNK
    n_qkv = (Q_DIM + 2 * KV_DIM) // PROJ_CHUNK
    n_chunks = IN_DIM // PROJ_CHUNK
    w_spec = pl.BlockSpec((None, D, IN_DIM), lambda *_: (layer, 0, 0), pipeline_mode=pl.Buffered(1))
    in_specs = [rows.spec(D), _const_spec((1, D)), w_spec,
                rows.table_spec(LANES), rows.table_spec(LANES), rows.table_spec(LANES),
                _const_spec((FOURIER_GROUP_DIM, 2 * FOURIER_GROUP_DIM))]
    args = [h, gain, w_in, rope_c, rope_s1, rope_s2, dft_c]
    aliases = {}
    if prev_out is not None:
        extra, aliases = _alias_tail(len(args), prev_out)
        in_specs += extra
        args += list(prev_out)
    m = rows.m
    vmem = (2 * m * D * 4 + 2 * m * D * 2 + D * IN_DIM * 2 + 6 * m * LANES * 4
            + 2 * m * (Q_DIM + 2 * KV_DIM) * 2 + 4 * m * FOURIER_DIM * 2 + 8 * m * PROJ_CHUNK * 4 + (6 << 20))
    return pl.pallas_call(
        functools.partial(_proj_body, n_rope=n_rope, n_qkv=n_qkv, n_chunks=n_chunks, nb=rows.nb,
                          n_alias=len(aliases)),
        grid=rows.grid,
        in_specs=in_specs,
        out_specs=[rows.spec(Q_DIM + 2 * KV_DIM), rows.spec(FOURIER_DIM, lead=(2,)), rows.spec(D)],
        out_shape=[jax.ShapeDtypeStruct((B, T, Q_DIM + 2 * KV_DIM), _BF16),
                   jax.ShapeDtypeStruct((B, 2, T, FOURIER_DIM), _BF16),
                   jax.ShapeDtypeStruct((B, T, D), _BF16)],
        input_output_aliases=aliases,
        compiler_params=_cparams(("parallel", "arbitrary"), vmem),
        name="mixer_proj" + ("_meta" if rows.meta else ""),
    )(*args)


def _stack_heads(q):
    return jnp.concatenate([q[:, h * HEAD_DIM:(h + 1) * HEAD_DIM] for h in range(Q_PER_KV)], axis=0)


def _sink_column(sink_ref, g, rows_per_head):
    return jnp.concatenate(
        [jnp.full((rows_per_head, 1), sink_ref[g * Q_PER_KV + h], _F32) for h in range(Q_PER_KV)], axis=0)


def _pad_rows(a, rows):
    return jnp.concatenate([a, jnp.zeros((rows - a.shape[0], a.shape[1]), a.dtype)], axis=0)


def _softmax_pv(q4, k_all, v_all, masks, sink):
    rows = q4.shape[0]
    s = lax.dot_general(q4, k_all, (((1,), (1,)), ((), ())), preferred_element_type=_F32) * (HEAD_DIM ** -0.5)
    blocks = []
    for i, mk in enumerate(masks):
        sb = s[:, i * BLOCK:(i + 1) * BLOCK]
        blocks.append(sb if mk is None else jnp.where(mk, sb, NEG_INF))
    mx = jnp.maximum(jnp.max(functools.reduce(jnp.maximum, blocks), axis=-1, keepdims=True), sink)
    p = jnp.concatenate([jnp.exp(sb - mx).astype(_BF16) for sb in blocks], axis=1)
    v_ext = jnp.concatenate([v_all, jnp.ones(v_all.shape, _BF16)], axis=1)
    o = jnp.dot(p, v_ext, preferred_element_type=_F32)
    den = o[:, HEAD_DIM:] + jnp.exp(sink - mx)
    return o[:, :HEAD_DIM] / den


def _attn_body(sink_ref, q_ref, kp_ref, kc_ref, kn_ref, km_ref, vp_ref, vc_ref, vn_ref, vm_ref, o_ref,
               kbuf, vbuf, s_buf, p_buf, es_buf, *, qb, ntile, n_meta):
    g = pl.program_id(1)
    jt = pl.program_id(2)
    rows = Q_PER_KV * BLOCK
    band = 3 * BLOCK
    meta0 = (qb + 2) * BLOCK
    scale = HEAD_DIM ** -0.5
    c_exp2 = scale * 1.4426950408889634

    kbuf[0:BLOCK] = kp_ref[0]
    kbuf[BLOCK:(qb + 1) * BLOCK] = kc_ref[0]
    kbuf[(qb + 1) * BLOCK:meta0] = kn_ref[0]
    kbuf[meta0:meta0 + BLOCK] = _pad_rows(km_ref[0], BLOCK)
    vbuf[0:BLOCK, 0:HEAD_DIM] = vp_ref[0]
    vbuf[BLOCK:(qb + 1) * BLOCK, 0:HEAD_DIM] = vc_ref[0]
    vbuf[(qb + 1) * BLOCK:meta0, 0:HEAD_DIM] = vn_ref[0]
    vbuf[meta0:meta0 + BLOCK, 0:HEAD_DIM] = _pad_rows(vm_ref[0], BLOCK)
    vbuf[:, HEAD_DIM:] = jnp.ones((meta0 + BLOCK, HEAD_DIM), _BF16)

    row = lax.broadcasted_iota(jnp.int32, (rows, BLOCK), 0) % BLOCK
    col = lax.broadcasted_iota(jnp.int32, (rows, BLOCK), 1)
    diff = col - row
    meta_ok = col < n_meta
    never = 2 * BLOCK
    sink_s = _sink_column(sink_ref, g, BLOCK) * (1.0 / scale)

    def dot_t(a, b):
        return lax.dot_general(a, b, (((1,), (1,)), ((), ())), preferred_element_type=_F32)

    def stage_a(i, slot):
        r0 = pl.multiple_of(i * BLOCK, BLOCK)
        q4 = _stack_heads(q_ref[0, pl.ds(r0, BLOCK), :])
        s_buf[slot, :, 0:band] = dot_t(q4, kbuf[pl.ds(r0, band), :])
        s_buf[slot, :, band:] = dot_t(q4, kbuf[meta0:meta0 + BLOCK, :])

    def stage_b(i, slot):
        first = jnp.logical_and(jt == 0, i == 0)
        last = jnp.logical_and(jt == ntile - 1, i == qb - 1)
        off_p = jnp.where(first, never, 0)
        off_n = jnp.where(last, -never, 0)
        blocks = [jnp.where(diff >= off_p, s_buf[slot, :, 0:BLOCK], NEG_INF),
                  s_buf[slot, :, BLOCK:2 * BLOCK],
                  jnp.where(diff <= off_n, s_buf[slot, :, 2 * BLOCK:band], NEG_INF),
                  jnp.where(meta_ok, s_buf[slot, :, band:], NEG_INF)]
        mx = jnp.maximum(jnp.max(functools.reduce(jnp.maximum, blocks), axis=-1, keepdims=True), sink_s)
        for n, sb in enumerate(blocks):
            p_buf[slot, :, n * BLOCK:(n + 1) * BLOCK] = jnp.exp2((sb - mx) * c_exp2).astype(_BF16)
        es_buf[slot] = jnp.broadcast_to(jnp.exp2((sink_s - mx) * c_exp2), (rows, HEAD_DIM))

    def stage_c(i, slot):
        r0 = pl.multiple_of(i * BLOCK, BLOCK)
        o = (jnp.dot(p_buf[slot, :, 0:band], vbuf[pl.ds(r0, band), :], preferred_element_type=_F32)
             + jnp.dot(p_buf[slot, :, band:], vbuf[meta0:meta0 + BLOCK, :], preferred_element_type=_F32))
        out = o[:, :HEAD_DIM] / (o[:, HEAD_DIM:] + es_buf[slot])
        for h in range(Q_PER_KV):
            o_ref[0, pl.ds(r0, BLOCK), h * HEAD_DIM:(h + 1) * HEAD_DIM] = (
                out[h * BLOCK:(h + 1) * BLOCK].astype(_BF16))

    stage_a(0, 0)
    stage_a(1, 1)
    stage_b(0, 0)

    def pair(t, carry):
        i = 2 * t
        stage_c(i, 0)
        stage_b(i + 1, 1)
        stage_a(i + 2, 0)
        stage_c(i + 1, 1)
        stage_b(i + 2, 0)
        stage_a(i + 3, 1)
        return carry

    lax.fori_loop(0, (qb - 2) // 2, pair, 0)
    stage_c(qb - 2, 0)
    stage_b(qb - 1, 1)
    stage_c(qb - 1, 1)


def _attn_meta_body(sink_ref, q_ref, k0_ref, km_ref, v0_ref, vm_ref, prev_out_ref, o_ref, *, n_meta):
    del prev_out_ref
    g = pl.program_id(1)
    rows = Q_PER_KV * n_meta
    row = lax.broadcasted_iota(jnp.int32, (rows, BLOCK), 0) % n_meta
    col = lax.broadcasted_iota(jnp.int32, (rows, BLOCK), 1)
    q4 = _stack_heads(q_ref[0])
    masks = [col <= row + (WINDOW - n_meta), col < n_meta]
    k_all = jnp.concatenate([k0_ref[0], _pad_rows(km_ref[0], BLOCK)], axis=0)
    v_all = jnp.concatenate([v0_ref[0], _pad_rows(vm_ref[0], BLOCK)], axis=0)
    out = _softmax_pv(q4, k_all, v_all, masks, _sink_column(sink_ref, g, n_meta))
    for h in range(Q_PER_KV):
        o_ref[0, :, h * HEAD_DIM:(h + 1) * HEAD_DIM] = out[h * n_meta:(h + 1) * n_meta].astype(_BF16)


def _attention(qkv, sink, *, n_real, n_meta, qb):
    B, T, _ = qkv.shape
    nblk = n_real // BLOCK
    ntile = nblk // qb
    kcol = Q_DIM // HEAD_DIM
    vcol = (Q_DIM + KV_DIM) // HEAD_DIM
    mrow = n_real // n_meta

    def edge_spec(rowf, col0):
        return pl.BlockSpec((1, BLOCK, HEAD_DIM), lambda b, g, j, s: (b, rowf(j), col0 + g))

    def main_spec(col0):
        return pl.BlockSpec((1, qb * BLOCK, HEAD_DIM), lambda b, g, j, s: (b, j, col0 + g))

    def meta_spec(col0):
        return pl.BlockSpec((1, n_meta, HEAD_DIM), lambda b, g, j, s: (b, mrow, col0 + g))

    def prev_i(j):
        return jnp.maximum(qb * j - 1, 0)

    def next_i(j):
        return jnp.minimum(qb * j + qb, nblk - 1)

    attn = pl.pallas_call(
        functools.partial(_attn_body, qb=qb, ntile=ntile, n_meta=n_meta),
        grid_spec=pltpu.PrefetchScalarGridSpec(
            num_scalar_prefetch=1,
            grid=(B, N_KV_HEADS, ntile),
            in_specs=[
                pl.BlockSpec((1, qb * BLOCK, PROJ_CHUNK), lambda b, g, j, s: (b, j, g)),
                edge_spec(prev_i, kcol), main_spec(kcol), edge_spec(next_i, kcol), meta_spec(kcol),
                edge_spec(prev_i, vcol), main_spec(vcol), edge_spec(next_i, vcol), meta_spec(vcol),
            ],
            out_specs=pl.BlockSpec((1, qb * BLOCK, PROJ_CHUNK), lambda b, g, j, s: (b, j, g)),
            scratch_shapes=[
                pltpu.VMEM(((qb + 3) * BLOCK, HEAD_DIM), _BF16),
                pltpu.VMEM(((qb + 3) * BLOCK, 2 * HEAD_DIM), _BF16),
                pltpu.VMEM((2, Q_PER_KV * BLOCK, 4 * BLOCK), _F32),
                pltpu.VMEM((2, Q_PER_KV * BLOCK, 4 * BLOCK), _BF16),
                pltpu.VMEM((2, Q_PER_KV * BLOCK, HEAD_DIM), _F32),
            ],
        ),
        out_shape=jax.ShapeDtypeStruct((B, T, Q_DIM), _BF16),
        compiler_params=_cparams(("parallel", "parallel", "arbitrary"), 32 << 20),
        name="windowed_gqa",
    )(sink, qkv, qkv, qkv, qkv, qkv, qkv, qkv, qkv, qkv)

    def blk0_spec(col0):
        return pl.BlockSpec((1, BLOCK, HEAD_DIM), lambda b, g, s: (b, 0, col0 + g))

    def meta2_spec(col0):
        return pl.BlockSpec((1, n_meta, HEAD_DIM), lambda b, g, s: (b, mrow, col0 + g))

    return pl.pallas_call(
        functools.partial(_attn_meta_body, n_meta=n_meta),
        grid_spec=pltpu.PrefetchScalarGridSpec(
            num_scalar_prefetch=1,
            grid=(B, N_KV_HEADS),
            in_specs=[
                pl.BlockSpec((1, n_meta, PROJ_CHUNK), lambda b, g, s: (b, mrow, g)),
                blk0_spec(kcol), meta2_spec(kcol), blk0_spec(vcol), meta2_spec(vcol),
                pl.BlockSpec(memory_space=pl.ANY),
            ],
            out_specs=pl.BlockSpec((1, n_meta, PROJ_CHUNK), lambda b, g, s: (b, mrow, g)),
        ),
        out_shape=jax.ShapeDtypeStruct((B, T, Q_DIM), _BF16),
        input_output_aliases={6: 0},
        compiler_params=_cparams(("parallel", "arbitrary"), 32 << 20),
        name="windowed_gqa_meta",
    )(sink, qkv, qkv, qkv, qkv, qkv, attn)


def _fft_factors(T):
    best = None
    for n2 in range(16, T + 1, 16):
        if T % n2:
            continue
        n1 = T // n2
        cost = 4 * n1 + 2 * n2
        if best is None or cost < best[0]:
            best = (cost, n1, n2)
    assert best is not None, "sequence length must be a multiple of 16"
    return best[1], best[2]


def _fft1_body(z_ref, mc_ref, ms_ref, twc_ref, tws_ref, a_ref, *, n1, R, C):
    rows = n1 * R
    rhs = jnp.concatenate([z_ref[0, 0].reshape(rows, C), z_ref[0, 1].reshape(rows, C)], axis=1)
    xc = jnp.dot(mc_ref[...], rhs, preferred_element_type=_F32)
    xs = jnp.dot(ms_ref[...], rhs, preferred_element_type=_F32)
    ar = xc[:, :C] + xs[:, C:]
    ai = xc[:, C:] - xs[:, :C]
    twc = jnp.concatenate([twc_ref[0]] * (C // LANES), axis=1)
    tws = jnp.concatenate([tws_ref[0]] * (C // LANES), axis=1)
    a_ref[0, 0] = (ar * twc + ai * tws).reshape(n1, R, C).astype(_BF16)
    a_ref[0, 1] = (ai * twc - ar * tws).reshape(n1, R, C).astype(_BF16)


def _fft2_body(ar_ref, ai_ref, mc_ref, ms_ref, f_ref):
    res = (jnp.dot(mc_ref[0], ar_ref[0, 0, 0], preferred_element_type=_F32)
           + jnp.dot(ms_ref[0], ai_ref[0, 0, 0], preferred_element_type=_F32))
    f_ref[0] = res.astype(_BF16)


def _fft_tables(T, n1, n2, n_meta, R):
    two_pi = 2.0 * np.pi
    k1 = jnp.arange(n1, dtype=jnp.int32)
    ang1 = ((k1[:, None] * k1[None, :]) % n1).astype(_F32) * (two_pi / n1)
    eye = jnp.eye(R, dtype=_F32)
    m1c = jnp.kron(jnp.cos(ang1) * (n1 ** -0.5), eye).astype(_BF16)
    m1s = jnp.kron(jnp.sin(ang1) * (n1 ** -0.5), eye).astype(_BF16)
    nn2 = jnp.arange(n2, dtype=jnp.int32).reshape(n2 // R, 1, R)
    angt = ((k1[None, :, None] * (nn2 + n_meta)) % T).astype(_F32) * (two_pi / T)
    angt = jnp.broadcast_to(angt.reshape(n2 // R, n1 * R, 1), (n2 // R, n1 * R, LANES))
    twc, tws = jnp.cos(angt), jnp.sin(angt)
    k2 = jnp.arange(n2, dtype=jnp.int32)[:, None]
    mm2 = jnp.arange(n2, dtype=jnp.int32)[None, :]
    outs_c, outs_s = [], []
    for rot in (n_meta // n1, n_meta // n1 + 1):
        ph2 = (((k2 + rot) % n2) * (mm2 + n_meta)) % n2
        ang2 = ph2.astype(_F32) * (two_pi / n2)
        outs_c.append(jnp.cos(ang2) * (n2 ** -0.5))
        outs_s.append(jnp.sin(ang2) * (n2 ** -0.5))
    return m1c, m1s, twc, tws, jnp.stack(outs_c).astype(_BF16), jnp.stack(outs_s).astype(_BF16)


def _position_dft(z, tables, *, n1, n2, R, n_meta):
    B, _, T, C = z.shape
    m1c, m1s, twc, tws, m2c, m2s = tables
    rows = n1 * R
    zv = z.reshape(B, 2, n1, n2, C)
    av = pl.pallas_call(
        functools.partial(_fft1_body, n1=n1, R=R, C=C),
        grid=(B, n2 // R),
        in_specs=[
            pl.BlockSpec((1, 2, n1, R, C), lambda b, t: (b, 0, 0, t, 0)),
            _const_spec((rows, rows)),
            _const_spec((rows, rows)),
            pl.BlockSpec((1, rows, LANES), lambda b, t: (t, 0, 0)),
            pl.BlockSpec((1, rows, LANES), lambda b, t: (t, 0, 0)),
        ],
        out_specs=pl.BlockSpec((1, 2, n1, R, C), lambda b, t: (b, 0, 0, t, 0)),
        out_shape=jax.ShapeDtypeStruct((B, 2, n1, n2, C), _BF16),
        compiler_params=_cparams(("parallel", "arbitrary"), 48 << 20),
        name="dft_stage1",
    )(zv, m1c, m1s, twc, tws)
    shift = n1 - (n_meta % n1)

    f = pl.pallas_call(
        _fft2_body,
        grid=(B, n1),
        in_specs=[
            pl.BlockSpec((1, 1, 1, n2, C), lambda b, k: (b, 0, k, 0, 0)),
            pl.BlockSpec((1, 1, 1, n2, C), lambda b, k: (b, 1, k, 0, 0)),
            pl.BlockSpec((1, n2, n2), lambda b, k: (jnp.where(k < n_meta % n1, 1, 0), 0, 0)),
            pl.BlockSpec((1, n2, n2), lambda b, k: (jnp.where(k < n_meta % n1, 1, 0), 0, 0)),
        ],
        out_specs=pl.BlockSpec((1, n2, C), lambda b, k: (b, 0, (k + shift) % n1)),
        out_shape=jax.ShapeDtypeStruct((B, n2, n1 * C), _BF16),
        compiler_params=_cparams(("parallel", "arbitrary"), 32 << 20),
        name="dft_stage2",
    )(av, av, m2c, m2s)
    return f.reshape(B, T, C)


def _mix_body(*refs, tn, n_alias):
    h_ref, xn_ref, a_ref, f_ref, wpa_ref, wpf_ref, wg_ref, wo_ref = refs[:8]
    (o_ref,) = refs[8 + n_alias:]
    D = h_ref.shape[-1]
    o_ref[...] = h_ref[...]
    xn = xn_ref[...].reshape(-1, D)
    a = a_ref[...].reshape(-1, Q_DIM)
    f = f_ref[...].reshape(-1, FOURIER_DIM)
    for j in range(D // tn):
        cols = slice(j * tn, (j + 1) * tn)
        av = jnp.dot(a, wpa_ref[:, cols], preferred_element_type=_F32)
        fv = jnp.dot(f, wpf_ref[:, cols], preferred_element_type=_F32)
        ga = jax.nn.sigmoid(jnp.dot(xn, wg_ref[:, cols], preferred_element_type=_F32))
        gf = jax.nn.sigmoid(jnp.dot(xn, wg_ref[:, D + j * tn:D + (j + 1) * tn], preferred_element_type=_F32))
        mixed = (ga * av + gf * fv).astype(_BF16)
        o_ref[...] += jnp.dot(mixed, wo_ref[cols, :], preferred_element_type=_F32).reshape(o_ref.shape)


def _mix_out(rows, h, xn, attn, four, wpa, wpf, wg, wo, *, layer, tn, prev_out=None):
    B, T, D = h.shape

    def resident(k, n):
        return pl.BlockSpec((None, k, n), lambda *_: (layer, 0, 0), pipeline_mode=pl.Buffered(1))

    in_specs = [rows.spec(D), rows.spec(D), rows.spec(Q_DIM), rows.spec(FOURIER_DIM),
                resident(Q_DIM, D), resident(FOURIER_DIM, D), resident(D, 2 * D), resident(D, D)]
    args = [h, xn, attn, four, wpa, wpf, wg, wo]
    aliases = {}
    if prev_out is not None:
        extra, aliases = _alias_tail(len(args), [prev_out])
        in_specs += extra
        args.append(prev_out)
    m = rows.m
    vmem = ((Q_DIM + FOURIER_DIM + 3 * D) * D * 2 + 4 * m * D * 4 + 2 * m * D * 2
            + 2 * m * (Q_DIM + FOURIER_DIM) * 2 + 10 * m * tn * 4 + (4 << 20))
    return pl.pallas_call(
        functools.partial(_mix_body, tn=tn, n_alias=len(aliases)),
        grid=rows.grid,
        in_specs=in_specs,
        out_specs=rows.spec(D),
        out_shape=jax.ShapeDtypeStruct((B, T, D), _F32),
        input_output_aliases=aliases,
        compiler_params=_cparams(("parallel", "arbitrary"), vmem),
        name="mixer_out" + ("_meta" if rows.meta else ""),
    )(*args)


def _rope_tables(n_real, n_meta):
    T = n_real + n_meta
    half = ROT_DIM // 2
    inv = 1.0 / (ROPE_THETA ** (jnp.arange(0, ROT_DIM, 2, dtype=_F32) / ROT_DIM))
    ang = jnp.arange(T, dtype=_F32)[:, None] * inv[None, :]
    cos, sin = jnp.cos(ang), jnp.sin(ang)
    cos = jnp.concatenate([cos[n_meta:], cos[:n_meta]], axis=0)
    sin = jnp.concatenate([sin[n_meta:], sin[:n_meta]], axis=0)
    ones = jnp.ones((T, HEAD_DIM - ROT_DIM), _F32)
    zeros_h = jnp.zeros((T, half), _F32)
    zeros_r = jnp.zeros((T, HEAD_DIM - ROT_DIM), _F32)
    c = jnp.concatenate([cos, cos, ones], axis=1)
    s1 = jnp.concatenate([-sin, zeros_h, zeros_r], axis=1)
    s2 = jnp.concatenate([zeros_h, sin, zeros_r], axis=1)
    return c, s1, s2


def _channel_dft_table():
    n = FOURIER_GROUP_DIM
    c = jnp.arange(n, dtype=jnp.int32)[:, None]
    k = jnp.arange(n, dtype=jnp.int32)[None, :]
    ang = ((c * k) % n).astype(_F32) * (2.0 * np.pi / n)
    return (jnp.concatenate([jnp.cos(ang), -jnp.sin(ang)], axis=1) * (n ** -0.5)).astype(_BF16)


def _pick_tile(n, pref):
    t = pref
    while n % t:
        t //= 2
    return t


def kernel(x, meta_tokens, ffn1_norm, ffn1_w_gate, ffn1_w_up, ffn1_w_down, mix_norm, w_in, w_gate, sink,
           w_attn_branch, w_fourier_branch, w_out, ffn2_norm, ffn2_w_gate, ffn2_w_up, ffn2_w_down, final_norm):
    B, S, D = x.shape
    n_meta = meta_tokens.shape[0]
    depth = ffn1_norm.shape[0]
    d_ff = ffn1_w_gate.shape[-1]
    T = S + n_meta
    assert S % (4 * BLOCK) == 0 and S % n_meta == 0 and BLOCK % n_meta == 0
    assert w_in.shape[-1] == IN_DIM and KV_DIM == PROJ_CHUNK

    tm_ffn = _pick_tile(S, 1024)
    tm = _pick_tile(S, 512)
    tf = _pick_tile(d_ff, 512)
    tn = _pick_tile(D, 512)
    qb = _pick_tile(S // BLOCK, 32)
    rows_ffn = _Rows(B, S, n_meta, tm_ffn, meta=False)
    rows_main = _Rows(B, S, n_meta, tm, meta=False)
    rows_mix = _Rows(B, S, n_meta, _pick_tile(S, 256), meta=False)
    rows_meta = _Rows(B, S, n_meta, tm, meta=True)

    n1, n2 = _fft_factors(T)
    fft_tables = _fft_tables(T, n1, n2, n_meta, BF16_SUBLANES)
    rope_c, rope_s1, rope_s2 = _rope_tables(S, n_meta)
    dft_c = _channel_dft_table()

    ffn1_w = tuple(_to_bf16(w) for w in (ffn1_w_gate, ffn1_w_up, ffn1_w_down))
    ffn2_w = tuple(_to_bf16(w) for w in (ffn2_w_gate, ffn2_w_up, ffn2_w_down))
    w_in_b = _to_bf16(w_in)
    mix_w = tuple(_to_bf16(w) for w in (w_attn_branch, w_fourier_branch, w_gate, w_out))

    def ffn(h, h_meta, gain, wts, layer, final_gain=None):
        out = _ffn(rows_ffn, h, gain, *wts, layer=layer, tf=tf, T=T, final_gain=final_gain)
        if final_gain is None:
            src, blk0 = h_meta
            out = _ffn(rows_meta, src, gain, *wts, layer=layer, tf=tf, T=T, h_row_block0=blk0, prev_out=out)
        return out

    meta = jnp.broadcast_to(meta_tokens[None].astype(x.dtype), (B, n_meta, D))
    h = None
    for l in range(depth):
        if l == 0:
            h = ffn(x, (meta, 0), ffn1_norm[l][None], ffn1_w, l)
        else:
            h = ffn(h, (h, None), ffn1_norm[l][None], ffn1_w, l)

        gain = mix_norm[l][None]
        proj_args = (gain, w_in_b, rope_c, rope_s1, rope_s2, dft_c)
        qkv, z, xn = _proj(rows_main, h, *proj_args, layer=l)
        qkv, z, xn = _proj(rows_meta, h, *proj_args, layer=l, prev_out=(qkv, z, xn))
        attn = _attention(qkv, sink[l], n_real=S, n_meta=n_meta, qb=qb)
        four = _position_dft(z, fft_tables, n1=n1, n2=n2, R=BF16_SUBLANES, n_meta=n_meta)
        mix_args = (xn, attn, four) + mix_w
        h2 = _mix_out(rows_mix, h, *mix_args, layer=l, tn=tn)
        h = _mix_out(rows_meta, h, *mix_args, layer=l, tn=tn, prev_out=h2)

        last = l == depth - 1
        h = ffn(h, (h, None), ffn2_norm[l][None], ffn2_w, l, final_gain=final_norm[None] if last else None)
    return h
```

```python
import functools

import numpy as np
import jax
import jax.numpy as jnp
from jax import lax
from jax.experimental import pallas as pl
from jax.experimental.pallas import tpu as pltpu

_F32 = jnp.float32
_BF16 = jnp.bfloat16

HEAD_DIM = 128
N_Q_HEADS = 16
N_KV_HEADS = 4
Q_PER_KV = N_Q_HEADS // N_KV_HEADS
WINDOW = 128
BLOCK = 128
ROPE_THETA = 500000.0
ROT_DIM = HEAD_DIM // 4
N_FOURIER_GROUPS = 8
FOURIER_GROUP_DIM = 128
FOURIER_DIM = N_FOURIER_GROUPS * FOURIER_GROUP_DIM
Q_DIM = N_Q_HEADS * HEAD_DIM
KV_DIM = N_KV_HEADS * HEAD_DIM
IN_DIM = Q_DIM + 2 * KV_DIM + FOURIER_DIM
RMS_EPS = 1e-6
NEG_INF = -1e30

LANES = 128
BF16_SUBLANES = 16
PROJ_CHUNK = Q_PER_KV * HEAD_DIM
V7X_VMEM_BYTES = 64 * 1024 * 1024
VMEM_RESERVE_BYTES = 4 << 20
SMALL_CALL_VMEM_BYTES = 32 << 20
CAST_BLOCK_BYTES = 6 << 20
LOG2_E = 1.4426950408889634

FFN_ROW_TILE = 1024
FF_CHUNK = 512
PROJ_ROW_TILE = 512
MIX_ROW_TILE = 256
MIX_CHUNK = 512
ATTN_BLOCKS_PER_STEP = 32


def _cparams(sem, vmem_bytes):
    return pltpu.CompilerParams(dimension_semantics=sem,
                                vmem_limit_bytes=int(min(vmem_bytes, V7X_VMEM_BYTES - VMEM_RESERVE_BYTES)))


def _rms(x, g):
    return x * lax.rsqrt(jnp.mean(x * x, axis=-1, keepdims=True) + RMS_EPS) * g


class _Rows:
    def __init__(self, B, n_real, n_meta, tm, meta):
        self.B, self.n_real, self.n_meta, self.meta = B, n_real, n_meta, meta
        self.nb = B if meta else 1
        self.rows = n_meta if meta else tm
        self.grid = (1, 1) if meta else (B, n_real // tm)
        self.m = self.nb * self.rows

    def spec(self, width, lead=(), row_block0=None):
        nl = len(lead)
        if self.meta:
            r0 = self.n_real // self.n_meta if row_block0 is None else row_block0
            return pl.BlockSpec((self.nb,) + tuple(lead) + (self.rows, width),
                                lambda b, i, *_: (0,) + (0,) * nl + (r0, 0))
        return pl.BlockSpec((1,) + tuple(lead) + (self.rows, width),
                            lambda b, i, *_: (b,) + (0,) * nl + (i, 0))

    def table_spec(self, width):
        if self.meta:
            r0 = self.n_real // self.n_meta
            return pl.BlockSpec((self.rows, width), lambda b, i, *_: (r0, 0))
        return pl.BlockSpec((self.rows, width), lambda b, i, *_: (i, 0))


def _const_spec(shape, single_buffer=False):
    nd = len(shape)
    if single_buffer:
        return pl.BlockSpec(shape, lambda *_: (0,) * nd, pipeline_mode=pl.Buffered(1))
    return pl.BlockSpec(shape, lambda *_: (0,) * nd)


def _alias_tail(n_in, outs):
    specs = [pl.BlockSpec(memory_space=pl.ANY) for _ in outs]
    return specs, {n_in + k: k for k in range(len(outs))}


def _cast_body(w_ref, o_ref):
    o_ref[...] = w_ref[...].astype(_BF16)


def _to_bf16(w):
    L, K, N = w.shape
    tk = _pick_tile(K, 512)
    while tk > 8 and tk * N * 4 > CAST_BLOCK_BYTES:
        tk //= 2
    return pl.pallas_call(
        _cast_body,
        grid=(L, K // tk),
        in_specs=[pl.BlockSpec((1, tk, N), lambda l, i: (l, i, 0))],
        out_specs=pl.BlockSpec((1, tk, N), lambda l, i: (l, i, 0)),
        out_shape=jax.ShapeDtypeStruct(w.shape, _BF16),
        compiler_params=_cparams(("parallel", "parallel"), 2 * tk * N * (4 + 2) + VMEM_RESERVE_BYTES),
        name="cast_bf16",
    )(w)


def _ffn_body(*refs, nj, final, n_alias):
    h_ref, gain_ref, wg_ref, wu_ref, wd_ref = refs[:5]
    rest = refs[5:]
    fgain_ref = None
    if final:
        fgain_ref, rest = rest[0], rest[1:]
    o_ref, xn_ref = rest[n_alias:]
    j = pl.program_id(2)
    D = h_ref.shape[-1]

    @pl.when(j == 0)
    def _():
        x = h_ref[...].reshape(-1, D)
        xn_ref[...] = _rms(x, gain_ref[...]).astype(_BF16)
        o_ref[...] = x.reshape(o_ref.shape)

    g = jnp.dot(xn_ref[...], wg_ref[...], preferred_element_type=_F32)
    u = jnp.dot(xn_ref[...], wu_ref[...], preferred_element_type=_F32)
    act = ((g * jax.nn.sigmoid(g)) * u * 0.5).astype(_BF16)
    o_ref[...] += jnp.dot(act, wd_ref[...], preferred_element_type=_F32).reshape(o_ref.shape)

    if final:
        @pl.when(j == nj - 1)
        def _():
            o_ref[...] = _rms(o_ref[...].reshape(-1, D), fgain_ref[...]).reshape(o_ref.shape)


def _ffn(rows, h, gain, wg, wu, wd, *, layer, tf, T, final_gain=None, h_row_block0=None, prev_out=None):
    D = h.shape[-1]
    nj = wg.shape[2] // tf
    final = final_gain is not None
    in_specs = [rows.spec(D, row_block0=h_row_block0), _const_spec((1, D)),
                pl.BlockSpec((None, D, tf), lambda b, i, j: (layer, 0, j)),
                pl.BlockSpec((None, D, tf), lambda b, i, j: (layer, 0, j)),
                pl.BlockSpec((None, tf, D), lambda b, i, j: (layer, j, 0))]
    args = [h, gain, wg, wu, wd]
    if final:
        in_specs.append(_const_spec((1, D)))
        args.append(final_gain)
    aliases = {}
    if prev_out is not None:
        extra, aliases = _alias_tail(len(args), [prev_out])
        in_specs += extra
        args.append(prev_out)
    m = rows.m
    vmem = (4 * m * D * 4 + m * D * 2 + 2 * (D * 2 * tf + tf * D) * 2 + m * 2 * tf * 4 + m * tf * 2
            + VMEM_RESERVE_BYTES)
    out_rows = rows.n_real if final else T
    return pl.pallas_call(
        functools.partial(_ffn_body, nj=nj, final=final, n_alias=len(aliases)),
        grid=rows.grid + (nj,),
        in_specs=in_specs,
        out_specs=rows.spec(D),
        out_shape=jax.ShapeDtypeStruct((rows.B, out_rows, D), _F32),
        scratch_shapes=[pltpu.VMEM((m, D), _BF16)],
        input_output_aliases=aliases,
        compiler_params=_cparams(("parallel", "parallel", "arbitrary"), vmem),
        name=("ffn_final" if final else "ffn") + ("_meta" if rows.meta else ""),
    )(*args)


def _proj_body(*refs, n_rope, n_qkv, n_chunks, nb, n_alias):
    h_ref, gain_ref, w_ref, c_ref, s1_ref, s2_ref, dft_ref = refs[:7]
    qkv_ref, z_ref, xno_ref = refs[7 + n_alias:]
    D = h_ref.shape[-1]
    rows = h_ref.shape[-2]
    xno_ref[...] = _rms(h_ref[...].reshape(-1, D), gain_ref[...]).astype(_BF16).reshape(xno_ref.shape)
    c = jnp.concatenate([c_ref[...]] * nb, axis=0)
    s1 = jnp.concatenate([s1_ref[...]] * nb, axis=0)
    s2 = jnp.concatenate([s2_ref[...]] * nb, axis=0)
    half = ROT_DIM // 2
    for ch in range(n_chunks):
        cols = slice(ch * PROJ_CHUNK, (ch + 1) * PROJ_CHUNK)
        y = jnp.dot(xno_ref[...].reshape(-1, D), w_ref[:, cols], preferred_element_type=_F32)
        if ch < n_rope:
            for hh in range(PROJ_CHUNK // HEAD_DIM):
                xh = y[:, hh * HEAD_DIM:(hh + 1) * HEAD_DIM]
                up = pltpu.roll(xh, HEAD_DIM - half, 1)
                dn = pltpu.roll(xh, half, 1)
                r = (xh * c + up * s1 + dn * s2).astype(_BF16)
                lo = ch * PROJ_CHUNK + hh * HEAD_DIM
                qkv_ref[:, :, lo:lo + HEAD_DIM] = r.reshape(nb, rows, HEAD_DIM)
        elif ch < n_qkv:
            qkv_ref[:, :, cols] = y.astype(_BF16).reshape(nb, rows, PROJ_CHUNK)
        else:
            ub = y.astype(_BF16)
            for gg in range(PROJ_CHUNK // FOURIER_GROUP_DIM):
                zz = jnp.dot(ub[:, gg * FOURIER_GROUP_DIM:(gg + 1) * FOURIER_GROUP_DIM], dft_ref[...],
                             preferred_element_type=_F32)
                lo = (ch - n_qkv) * PROJ_CHUNK + gg * FOURIER_GROUP_DIM
                z_ref[:, 0, :, lo:lo + FOURIER_GROUP_DIM] = (
                    zz[:, :FOURIER_GROUP_DIM].astype(_BF16).reshape(nb, rows, FOURIER_GROUP_DIM))
                z_ref[:, 1, :, lo:lo + FOURIER_GROUP_DIM] = (
                    zz[:, FOURIER_GROUP_DIM:].astype(_BF16).reshape(nb, rows, FOURIER_GROUP_DIM))


def _proj(rows, h, gain, w_in, rope_c, rope_s1, rope_s2, dft_c, *, layer, prev_out=None):
    B, T, D = h.shape
    n_rope = (Q_DIM + KV_DIM) // PROJ_CHUNK
    n_qkv = (Q_DIM + 2 * KV_DIM) // PROJ_CHUNK
    n_chunks = IN_DIM // PROJ_CHUNK
    w_spec = pl.BlockSpec((None, D, IN_DIM), lambda *_: (layer, 0, 0), pipeline_mode=pl.Buffered(1))
    in_specs = [rows.spec(D), _const_spec((1, D)), w_spec,
                rows.table_spec(LANES), rows.table_spec(LANES), rows.table_spec(LANES),
                _const_spec((FOURIER_GROUP_DIM, 2 * FOURIER_GROUP_DIM))]
    args = [h, gain, w_in, rope_c, rope_s1, rope_s2, dft_c]
    aliases = {}
    if prev_out is not None:
        extra, aliases = _alias_tail(len(args), prev_out)
        in_specs += extra
        args += list(prev_out)
    m = rows.m
    vmem = (2 * m * D * 4 + 2 * m * D * 2 + D * IN_DIM * 2 + 6 * m * LANES * 4
            + 2 * m * (Q_DIM + 2 * KV_DIM) * 2 + 4 * m * FOURIER_DIM * 2 + 8 * m * PROJ_CHUNK * 4
            + VMEM_RESERVE_BYTES)
    return pl.pallas_call(
        functools.partial(_proj_body, n_rope=n_rope, n_qkv=n_qkv, n_chunks=n_chunks, nb=rows.nb,
                          n_alias=len(aliases)),
        grid=rows.grid,
        in_specs=in_specs,
        out_specs=[rows.spec(Q_DIM + 2 * KV_DIM), rows.spec(FOURIER_DIM, lead=(2,)), rows.spec(D)],
        out_shape=[jax.ShapeDtypeStruct((B, T, Q_DIM + 2 * KV_DIM), _BF16),
                   jax.ShapeDtypeStruct((B, 2, T, FOURIER_DIM), _BF16),
                   jax.ShapeDtypeStruct((B, T, D), _BF16)],
        input_output_aliases=aliases,
        compiler_params=_cparams(("parallel", "arbitrary"), vmem),
        name="mixer_proj" + ("_meta" if rows.meta else ""),
    )(*args)


def _stack_heads(q):
    return jnp.concatenate([q[:, h * HEAD_DIM:(h + 1) * HEAD_DIM] for h in range(Q_PER_KV)], axis=0)


def _sink_column(sink_ref, g, rows_per_head):
    return jnp.concatenate(
        [jnp.full((rows_per_head, 1), sink_ref[g * Q_PER_KV + h], _F32) for h in range(Q_PER_KV)], axis=0)


def _pad_rows(a, rows):
    return jnp.concatenate([a, jnp.zeros((rows - a.shape[0], a.shape[1]), a.dtype)], axis=0)


def _softmax_pv(q4, k_all, v_all, masks, sink):
    s = lax.dot_general(q4, k_all, (((1,), (1,)), ((), ())), preferred_element_type=_F32) * (HEAD_DIM ** -0.5)
    blocks = []
    for i, mk in enumerate(masks):
        sb = s[:, i * BLOCK:(i + 1) * BLOCK]
        blocks.append(sb if mk is None else jnp.where(mk, sb, NEG_INF))
    mx = jnp.maximum(jnp.max(functools.reduce(jnp.maximum, blocks), axis=-1, keepdims=True), sink)
    p = jnp.concatenate([jnp.exp(sb - mx).astype(_BF16) for sb in blocks], axis=1)
    v_ext = jnp.concatenate([v_all, jnp.ones(v_all.shape, _BF16)], axis=1)
    o = jnp.dot(p, v_ext, preferred_element_type=_F32)
    den = o[:, HEAD_DIM:] + jnp.exp(sink - mx)
    return o[:, :HEAD_DIM] / den


def _attn_body(sink_ref, q_ref, kp_ref, kc_ref, kn_ref, km_ref, vp_ref, vc_ref, vn_ref, vm_ref, o_ref,
               kbuf, vbuf, s_buf, p_buf, es_buf, *, qb, ntile, n_meta):
    g = pl.program_id(1)
    jt = pl.program_id(2)
    rows = Q_PER_KV * BLOCK
    band = 3 * BLOCK
    meta0 = (qb + 2) * BLOCK
    scale = HEAD_DIM ** -0.5
    c_exp2 = scale * LOG2_E

    kbuf[0:BLOCK] = kp_ref[0]
    kbuf[BLOCK:(qb + 1) * BLOCK] = kc_ref[0]
    kbuf[(qb + 1) * BLOCK:meta0] = kn_ref[0]
    kbuf[meta0:meta0 + BLOCK] = _pad_rows(km_ref[0], BLOCK)
    vbuf[0:BLOCK, 0:HEAD_DIM] = vp_ref[0]
    vbuf[BLOCK:(qb + 1) * BLOCK, 0:HEAD_DIM] = vc_ref[0]
    vbuf[(qb + 1) * BLOCK:meta0, 0:HEAD_DIM] = vn_ref[0]
    vbuf[meta0:meta0 + BLOCK, 0:HEAD_DIM] = _pad_rows(vm_ref[0], BLOCK)
    vbuf[:, HEAD_DIM:] = jnp.ones((meta0 + BLOCK, HEAD_DIM), _BF16)

    row = lax.broadcasted_iota(jnp.int32, (rows, BLOCK), 0) % BLOCK
    col = lax.broadcasted_iota(jnp.int32, (rows, BLOCK), 1)
    diff = col - row
    meta_ok = col < n_meta
    never = 2 * BLOCK
    sink_s = _sink_column(sink_ref, g, BLOCK) * (1.0 / scale)

    def dot_t(a, b):
        return lax.dot_general(a, b, (((1,), (1,)), ((), ())), preferred_element_type=_F32)

    def stage_a(i, slot):
        r0 = pl.multiple_of(i * BLOCK, BLOCK)
        q4 = _stack_heads(q_ref[0, pl.ds(r0, BLOCK), :])
        s_buf[slot, :, 0:band] = dot_t(q4, kbuf[pl.ds(r0, band), :])
        s_buf[slot, :, band:] = dot_t(q4, kbuf[meta0:meta0 + BLOCK, :])

    def stage_b(i, slot):
        first = jnp.logical_and(jt == 0, i == 0)
        last = jnp.logical_and(jt == ntile - 1, i == qb - 1)
        off_p = jnp.where(first, never, 0)
        off_n = jnp.where(last, -never, 0)
        blocks = [jnp.where(diff >= off_p, s_buf[slot, :, 0:BLOCK], NEG_INF),
                  s_buf[slot, :, BLOCK:2 * BLOCK],
                  jnp.where(diff <= off_n, s_buf[slot, :, 2 * BLOCK:band], NEG_INF),
                  jnp.where(meta_ok, s_buf[slot, :, band:], NEG_INF)]
        mx = jnp.maximum(jnp.max(functools.reduce(jnp.maximum, blocks), axis=-1, keepdims=True), sink_s)
        for n, sb in enumerate(blocks):
            p_buf[slot, :, n * BLOCK:(n + 1) * BLOCK] = jnp.exp2((sb - mx) * c_exp2).astype(_BF16)
        es_buf[slot] = jnp.broadcast_to(jnp.exp2((sink_s - mx) * c_exp2), (rows, HEAD_DIM))

    def stage_c(i, slot):
        r0 = pl.multiple_of(i * BLOCK, BLOCK)
        o = (jnp.dot(p_buf[slot, :, 0:band], vbuf[pl.ds(r0, band), :], preferred_element_type=_F32)
             + jnp.dot(p_buf[slot, :, band:], vbuf[meta0:meta0 + BLOCK, :], preferred_element_type=_F32))
        out = o[:, :HEAD_DIM] / (o[:, HEAD_DIM:] + es_buf[slot])
        for h in range(Q_PER_KV):
            o_ref[0, pl.ds(r0, BLOCK), h * HEAD_DIM:(h + 1) * HEAD_DIM] = (
                out[h * BLOCK:(h + 1) * BLOCK].astype(_BF16))

    stage_a(0, 0)
    stage_a(1, 1)
    stage_b(0, 0)

    def pair(t, carry):
        i = 2 * t
        stage_c(i, 0)
        stage_b(i + 1, 1)
        stage_a(i + 2, 0)
        stage_c(i + 1, 1)
        stage_b(i + 2, 0)
        stage_a(i + 3, 1)
        return carry

    lax.fori_loop(0, (qb - 2) // 2, pair, 0)
    stage_c(qb - 2, 0)
    stage_b(qb - 1, 1)
    stage_c(qb - 1, 1)


def _attn_meta_body(sink_ref, q_ref, k0_ref, km_ref, v0_ref, vm_ref, prev_out_ref, o_ref, *, n_meta):
    del prev_out_ref
    g = pl.program_id(1)
    rows = Q_PER_KV * n_meta
    row = lax.broadcasted_iota(jnp.int32, (rows, BLOCK), 0) % n_meta
    col = lax.broadcasted_iota(jnp.int32, (rows, BLOCK), 1)
    q4 = _stack_heads(q_ref[0])
    masks = [col <= row + (WINDOW - n_meta), col < n_meta]
    k_all = jnp.concatenate([k0_ref[0], _pad_rows(km_ref[0], BLOCK)], axis=0)
    v_all = jnp.concatenate([v0_ref[0], _pad_rows(vm_ref[0], BLOCK)], axis=0)
    out = _softmax_pv(q4, k_all, v_all, masks, _sink_column(sink_ref, g, n_meta))
    for h in range(Q_PER_KV):
        o_ref[0, :, h * HEAD_DIM:(h + 1) * HEAD_DIM] = out[h * n_meta:(h + 1) * n_meta].astype(_BF16)


def _attention(qkv, sink, *, n_real, n_meta, qb):
    B, T, _ = qkv.shape
    nblk = n_real // BLOCK
    ntile = nblk // qb
    kcol = Q_DIM // HEAD_DIM
    vcol = (Q_DIM + KV_DIM) // HEAD_DIM
    mrow = n_real // n_meta

    def edge_spec(rowf, col0):
        return pl.BlockSpec((1, BLOCK, HEAD_DIM), lambda b, g, j, s: (b, rowf(j), col0 + g))

    def main_spec(col0):
        return pl.BlockSpec((1, qb * BLOCK, HEAD_DIM), lambda b, g, j, s: (b, j, col0 + g))

    def meta_spec(col0):
        return pl.BlockSpec((1, n_meta, HEAD_DIM), lambda b, g, j, s: (b, mrow, col0 + g))

    def prev_i(j):
        return jnp.maximum(qb * j - 1, 0)

    def next_i(j):
        return jnp.minimum(qb * j + qb, nblk - 1)

    attn = pl.pallas_call(
        functools.partial(_attn_body, qb=qb, ntile=ntile, n_meta=n_meta),
        grid_spec=pltpu.PrefetchScalarGridSpec(
            num_scalar_prefetch=1,
            grid=(B, N_KV_HEADS, ntile),
            in_specs=[
                pl.BlockSpec((1, qb * BLOCK, PROJ_CHUNK), lambda b, g, j, s: (b, j, g)),
                edge_spec(prev_i, kcol), main_spec(kcol), edge_spec(next_i, kcol), meta_spec(kcol),
                edge_spec(prev_i, vcol), main_spec(vcol), edge_spec(next_i, vcol), meta_spec(vcol),
            ],
            out_specs=pl.BlockSpec((1, qb * BLOCK, PROJ_CHUNK), lambda b, g, j, s: (b, j, g)),
            scratch_shapes=[
                pltpu.VMEM(((qb + 3) * BLOCK, HEAD_DIM), _BF16),
                pltpu.VMEM(((qb + 3) * BLOCK, 2 * HEAD_DIM), _BF16),
                pltpu.VMEM((2, Q_PER_KV * BLOCK, 4 * BLOCK), _F32),
                pltpu.VMEM((2, Q_PER_KV * BLOCK, 4 * BLOCK), _BF16),
                pltpu.VMEM((2, Q_PER_KV * BLOCK, HEAD_DIM), _F32),
            ],
        ),
        out_shape=jax.ShapeDtypeStruct((B, T, Q_DIM), _BF16),
        compiler_params=_cparams(("parallel", "parallel", "arbitrary"), SMALL_CALL_VMEM_BYTES),
        name="windowed_gqa",
    )(sink, qkv, qkv, qkv, qkv, qkv, qkv, qkv, qkv, qkv)

    def blk0_spec(col0):
        return pl.BlockSpec((1, BLOCK, HEAD_DIM), lambda b, g, s: (b, 0, col0 + g))

    def meta2_spec(col0):
        return pl.BlockSpec((1, n_meta, HEAD_DIM), lambda b, g, s: (b, mrow, col0 + g))

    return pl.pallas_call(
        functools.partial(_attn_meta_body, n_meta=n_meta),
        grid_spec=pltpu.PrefetchScalarGridSpec(
            num_scalar_prefetch=1,
            grid=(B, N_KV_HEADS),
            in_specs=[
                pl.BlockSpec((1, n_meta, PROJ_CHUNK), lambda b, g, s: (b, mrow, g)),
                blk0_spec(kcol), meta2_spec(kcol), blk0_spec(vcol), meta2_spec(vcol),
                pl.BlockSpec(memory_space=pl.ANY),
            ],
            out_specs=pl.BlockSpec((1, n_meta, PROJ_CHUNK), lambda b, g, s: (b, mrow, g)),
        ),
        out_shape=jax.ShapeDtypeStruct((B, T, Q_DIM), _BF16),
        input_output_aliases={6: 0},
        compiler_params=_cparams(("parallel", "arbitrary"), SMALL_CALL_VMEM_BYTES),
        name="windowed_gqa_meta",
    )(sink, qkv, qkv, qkv, qkv, qkv, attn)


def _fft_factors(T):
    best = None
    for n2 in range(16, T + 1, 16):
        if T % n2:
            continue
        n1 = T // n2
        cost = 4 * n1 + 2 * n2
        if best is None or cost < best[0]:
            best = (cost, n1, n2)
    assert best is not None, "sequence length must be a multiple of 16"
    return best[1], best[2]


def _fft1_body(z_ref, mc_ref, ms_ref, twc_ref, tws_ref, a_ref, *, n1, R, C):
    rows = n1 * R
    rhs = jnp.concatenate([z_ref[0, 0].reshape(rows, C), z_ref[0, 1].reshape(rows, C)], axis=1)
    xc = jnp.dot(mc_ref[...], rhs, preferred_element_type=_F32)
    xs = jnp.dot(ms_ref[...], rhs, preferred_element_type=_F32)
    ar = xc[:, :C] + xs[:, C:]
    ai = xc[:, C:] - xs[:, :C]
    twc = jnp.concatenate([twc_ref[0]] * (C // LANES), axis=1)
    tws = jnp.concatenate([tws_ref[0]] * (C // LANES), axis=1)
    a_ref[0, 0] = (ar * twc + ai * tws).reshape(n1, R, C).astype(_BF16)
    a_ref[0, 1] = (ai * twc - ar * tws).reshape(n1, R, C).astype(_BF16)


def _fft2_body(ar_ref, ai_ref, mc_ref, ms_ref, f_ref):
    for b in range(ar_ref.shape[0]):
        res = (jnp.dot(mc_ref[0], ar_ref[b, 0, 0], preferred_element_type=_F32)
               + jnp.dot(ms_ref[0], ai_ref[b, 0, 0], preferred_element_type=_F32))
        f_ref[b] = res.astype(_BF16)


def _fft_tables(T, n1, n2, n_meta, R):
    two_pi = 2.0 * np.pi
    k1 = jnp.arange(n1, dtype=jnp.int32)
    ang1 = ((k1[:, None] * k1[None, :]) % n1).astype(_F32) * (two_pi / n1)
    eye = jnp.eye(R, dtype=_F32)
    m1c = jnp.kron(jnp.cos(ang1) * (n1 ** -0.5), eye).astype(_BF16)
    m1s = jnp.kron(jnp.sin(ang1) * (n1 ** -0.5), eye).astype(_BF16)
    nn2 = jnp.arange(n2, dtype=jnp.int32).reshape(n2 // R, 1, R)
    angt = ((k1[None, :, None] * (nn2 + n_meta)) % T).astype(_F32) * (two_pi / T)
    angt = jnp.broadcast_to(angt.reshape(n2 // R, n1 * R, 1), (n2 // R, n1 * R, LANES))
    twc, tws = jnp.cos(angt), jnp.sin(angt)
    k2 = jnp.arange(n2, dtype=jnp.int32)[:, None]
    mm2 = jnp.arange(n2, dtype=jnp.int32)[None, :]
    outs_c, outs_s = [], []
    for rot in (n_meta // n1, n_meta // n1 + 1):
        ph2 = (((k2 + rot) % n2) * (mm2 + n_meta)) % n2
        ang2 = ph2.astype(_F32) * (two_pi / n2)
        outs_c.append(jnp.cos(ang2) * (n2 ** -0.5))
        outs_s.append(jnp.sin(ang2) * (n2 ** -0.5))
    return m1c, m1s, twc, tws, jnp.stack(outs_c).astype(_BF16), jnp.stack(outs_s).astype(_BF16)


def _position_dft(z, tables, *, n1, n2, R, n_meta):
    B, _, T, C = z.shape
    m1c, m1s, twc, tws, m2c, m2s = tables
    rows = n1 * R
    zv = z.reshape(B, 2, n1, n2, C)
    av = pl.pallas_call(
        functools.partial(_fft1_body, n1=n1, R=R, C=C),
        grid=(B, n2 // R),
        in_specs=[
            pl.BlockSpec((1, 2, n1, R, C), lambda b, t: (b, 0, 0, t, 0)),
            _const_spec((rows, rows)),
            _const_spec((rows, rows)),
            pl.BlockSpec((1, rows, LANES), lambda b, t: (t, 0, 0)),
            pl.BlockSpec((1, rows, LANES), lambda b, t: (t, 0, 0)),
        ],
        out_specs=pl.BlockSpec((1, 2, n1, R, C), lambda b, t: (b, 0, 0, t, 0)),
        out_shape=jax.ShapeDtypeStruct((B, 2, n1, n2, C), _BF16),
        compiler_params=_cparams(("parallel", "arbitrary"),
                                 4 * 2 * rows * C * 2 + 2 * 2 * rows * rows * 2 + 6 * rows * C * 4 + VMEM_RESERVE_BYTES),
        name="dft_stage1",
    )(zv, m1c, m1s, twc, tws)
    shift = n1 - (n_meta % n1)

    f = pl.pallas_call(
        _fft2_body,
        grid=(n1,),
        in_specs=[
            pl.BlockSpec((B, 1, 1, n2, C), lambda k: (0, 0, k, 0, 0)),
            pl.BlockSpec((B, 1, 1, n2, C), lambda k: (0, 1, k, 0, 0)),
            pl.BlockSpec((1, n2, n2), lambda k: (jnp.where(k < n_meta % n1, 1, 0), 0, 0)),
            pl.BlockSpec((1, n2, n2), lambda k: (jnp.where(k < n_meta % n1, 1, 0), 0, 0)),
        ],
        out_specs=pl.BlockSpec((B, n2, C), lambda k: (0, 0, (k + shift) % n1)),
        out_shape=jax.ShapeDtypeStruct((B, n2, n1 * C), _BF16),
        compiler_params=_cparams(("arbitrary",), SMALL_CALL_VMEM_BYTES),
        name="dft_stage2",
    )(av, av, m2c, m2s)
    return f.reshape(B, T, C)


def _mix_body(*refs, tn, n_alias):
    h_ref, xn_ref, a_ref, f_ref, wpa_ref, wpf_ref, wg_ref, wo_ref = refs[:8]
    (o_ref,) = refs[8 + n_alias:]
    D = h_ref.shape[-1]
    o_ref[...] = h_ref[...]
    xn = xn_ref[...].reshape(-1, D)
    a = a_ref[...].reshape(-1, Q_DIM)
    f = f_ref[...].reshape(-1, FOURIER_DIM)
    for j in range(D // tn):
        cols = slice(j * tn, (j + 1) * tn)
        av = jnp.dot(a, wpa_ref[:, cols], preferred_element_type=_F32)
        fv = jnp.dot(f, wpf_ref[:, cols], preferred_element_type=_F32)
        ga = jax.nn.sigmoid(jnp.dot(xn, wg_ref[:, cols], preferred_element_type=_F32))
        gf = jax.nn.sigmoid(jnp.dot(xn, wg_ref[:, D + j * tn:D + (j + 1) * tn], preferred_element_type=_F32))
        mixed = (ga * av + gf * fv).astype(_BF16)
        o_ref[...] += jnp.dot(mixed, wo_ref[cols, :], preferred_element_type=_F32).reshape(o_ref.shape)


def _mix_out(rows, h, xn, attn, four, wpa, wpf, wg, wo, *, layer, tn, prev_out=None):
    B, T, D = h.shape

    def resident(k, n):
        return pl.BlockSpec((None, k, n), lambda *_: (layer, 0, 0), pipeline_mode=pl.Buffered(1))

    in_specs = [rows.spec(D), rows.spec(D), rows.spec(Q_DIM), rows.spec(FOURIER_DIM),
                resident(Q_DIM, D), resident(FOURIER_DIM, D), resident(D, 2 * D), resident(D, D)]
    args = [h, xn, attn, four, wpa, wpf, wg, wo]
    aliases = {}
    if prev_out is not None:
        extra, aliases = _alias_tail(len(args), [prev_out])
        in_specs += extra
        args.append(prev_out)
    m = rows.m
    vmem = ((Q_DIM + FOURIER_DIM + 3 * D) * D * 2 + 4 * m * D * 4 + 2 * m * D * 2
            + 2 * m * (Q_DIM + FOURIER_DIM) * 2 + 10 * m * tn * 4 + VMEM_RESERVE_BYTES)
    return pl.pallas_call(
        functools.partial(_mix_body, tn=tn, n_alias=len(aliases)),
        grid=rows.grid,
        in_specs=in_specs,
        out_specs=rows.spec(D),
        out_shape=jax.ShapeDtypeStruct((B, T, D), _F32),
        input_output_aliases=aliases,
        compiler_params=_cparams(("parallel", "arbitrary"), vmem),
        name="mixer_out" + ("_meta" if rows.meta else ""),
    )(*args)


def _rope_tables(n_real, n_meta):
    T = n_real + n_meta
    half = ROT_DIM // 2
    inv = 1.0 / (ROPE_THETA ** (jnp.arange(0, ROT_DIM, 2, dtype=_F32) / ROT_DIM))
    ang = jnp.arange(T, dtype=_F32)[:, None] * inv[None, :]
    cos, sin = jnp.cos(ang), jnp.sin(ang)
    cos = jnp.concatenate([cos[n_meta:], cos[:n_meta]], axis=0)
    sin = jnp.concatenate([sin[n_meta:], sin[:n_meta]], axis=0)
    ones = jnp.ones((T, HEAD_DIM - ROT_DIM), _F32)
    zeros_h = jnp.zeros((T, half), _F32)
    zeros_r = jnp.zeros((T, HEAD_DIM - ROT_DIM), _F32)
    c = jnp.concatenate([cos, cos, ones], axis=1)
    s1 = jnp.concatenate([-sin, zeros_h, zeros_r], axis=1)
    s2 = jnp.concatenate([zeros_h, sin, zeros_r], axis=1)
    return c, s1, s2


def _channel_dft_table():
    n = FOURIER_GROUP_DIM
    c = jnp.arange(n, dtype=jnp.int32)[:, None]
    k = jnp.arange(n, dtype=jnp.int32)[None, :]
    ang = ((c * k) % n).astype(_F32) * (2.0 * np.pi / n)
    return (jnp.concatenate([jnp.cos(ang), -jnp.sin(ang)], axis=1) * (n ** -0.5)).astype(_BF16)


def _pick_tile(n, pref):
    t = pref
    while n % t:
        t //= 2
    return t


def kernel(x, meta_tokens, ffn1_norm, ffn1_w_gate, ffn1_w_up, ffn1_w_down, mix_norm, w_in, w_gate, sink,
           w_attn_branch, w_fourier_branch, w_out, ffn2_norm, ffn2_w_gate, ffn2_w_up, ffn2_w_down, final_norm):
    B, S, D = x.shape
    n_meta = meta_tokens.shape[0]
    depth = ffn1_norm.shape[0]
    d_ff = ffn1_w_gate.shape[-1]
    T = S + n_meta
    assert S % (4 * BLOCK) == 0 and S % n_meta == 0 and BLOCK % n_meta == 0
    assert w_in.shape[-1] == IN_DIM and KV_DIM == PROJ_CHUNK

    tf = _pick_tile(d_ff, FF_CHUNK)
    tn = _pick_tile(D, MIX_CHUNK)
    qb = _pick_tile(S // BLOCK, ATTN_BLOCKS_PER_STEP)
    rows_ffn = _Rows(B, S, n_meta, _pick_tile(S, FFN_ROW_TILE), meta=False)
    rows_main = _Rows(B, S, n_meta, _pick_tile(S, PROJ_ROW_TILE), meta=False)
    rows_mix = _Rows(B, S, n_meta, _pick_tile(S, MIX_ROW_TILE), meta=False)
    rows_meta = _Rows(B, S, n_meta, 0, meta=True)

    n1, n2 = _fft_factors(T)
    fft_tables = _fft_tables(T, n1, n2, n_meta, BF16_SUBLANES)
    rope_c, rope_s1, rope_s2 = _rope_tables(S, n_meta)
    dft_c = _channel_dft_table()

    ffn1_w = tuple(_to_bf16(w) for w in (ffn1_w_gate, ffn1_w_up, ffn1_w_down))
    ffn2_w = tuple(_to_bf16(w) for w in (ffn2_w_gate, ffn2_w_up, ffn2_w_down))
    w_in_b = _to_bf16(w_in)
    mix_w = tuple(_to_bf16(w) for w in (w_attn_branch, w_fourier_branch, w_gate, w_out))

    def ffn(h, h_meta, gain, wts, layer, final_gain=None):
        out = _ffn(rows_ffn, h, gain, *wts, layer=layer, tf=tf, T=T, final_gain=final_gain)
        if final_gain is None:
            src, blk0 = h_meta
            out = _ffn(rows_meta, src, gain, *wts, layer=layer, tf=tf, T=T, h_row_block0=blk0, prev_out=out)
        return out

    meta = jnp.broadcast_to(meta_tokens[None].astype(x.dtype), (B, n_meta, D))
    h = None
    for l in range(depth):
        if l == 0:
            h = ffn(x, (meta, 0), ffn1_norm[l][None], ffn1_w, l)
        else:
            h = ffn(h, (h, None), ffn1_norm[l][None], ffn1_w, l)

        gain = mix_norm[l][None]
        proj_args = (gain, w_in_b, rope_c, rope_s1, rope_s2, dft_c)
        qkv, z, xn = _proj(rows_main, h, *proj_args, layer=l)
        qkv, z, xn = _proj(rows_meta, h, *proj_args, layer=l, prev_out=(qkv, z, xn))
        attn = _attention(qkv, sink[l], n_real=S, n_meta=n_meta, qb=qb)
        four = _position_dft(z, fft_tables, n1=n1, n2=n2, R=BF16_SUBLANES, n_meta=n_meta)
        mix_args = (xn, attn, four) + mix_w
        h2 = _mix_out(rows_mix, h, *mix_args, layer=l, tn=tn)
        h = _mix_out(rows_meta, h, *mix_args, layer=l, tn=tn, prev_out=h2)

        last = l == depth - 1
        h = ffn(h, (h, None), ffn2_norm[l][None], ffn2_w, l, final_gain=final_norm[None] if last else None)
    return h
```

```python
import functools

import numpy as np
import jax
import jax.numpy as jnp
from jax import lax
from jax.experimental import pallas as pl
from jax.experimental.pallas import tpu as pltpu

_F32 = jnp.float32
_BF16 = jnp.bfloat16

HEAD_DIM = 128
N_Q_HEADS = 16
N_KV_HEADS = 4
Q_PER_KV = N_Q_HEADS // N_KV_HEADS
WINDOW = 128
BLOCK = 128
ROPE_THETA = 500000.0
ROT_DIM = HEAD_DIM // 4
N_FOURIER_GROUPS = 8
FOURIER_GROUP_DIM = 128
FOURIER_DIM = N_FOURIER_GROUPS * FOURIER_GROUP_DIM
Q_DIM = N_Q_HEADS * HEAD_DIM
KV_DIM = N_KV_HEADS * HEAD_DIM
IN_DIM = Q_DIM + 2 * KV_DIM + FOURIER_DIM
RMS_EPS = 1e-6
NEG_INF = -1e30

LANES = 128
BF16_SUBLANES = 16
PROJ_CHUNK = Q_PER_KV * HEAD_DIM
V7X_VMEM_BYTES = 64 * 1024 * 1024
VMEM_RESERVE_BYTES = 4 << 20
SMALL_CALL_VMEM_BYTES = 32 << 20
CAST_BLOCK_BYTES = 6 << 20
LOG2_E = 1.4426950408889634
MXU_DEPTH = 256

FFN_ROW_TILE = 1024
FF_CHUNK = 512
PROJ_ROW_TILE = 512
MIX_ROW_TILE = 256
MIX_CHUNK = 1024
ATTN_BLOCKS_PER_STEP = 32


def _round_up(a, m):
    return (a + m - 1) // m * m


def _cparams(sem, vmem_bytes):
    return pltpu.CompilerParams(dimension_semantics=sem,
                                vmem_limit_bytes=int(min(vmem_bytes, V7X_VMEM_BYTES - VMEM_RESERVE_BYTES)))


def _rms(x, g):
    return x * lax.rsqrt(jnp.mean(x * x, axis=-1, keepdims=True) + RMS_EPS) * g


class _Rows:
    def __init__(self, B, n_real, n_meta, tm, meta):
        self.B, self.n_real, self.n_meta, self.meta = B, n_real, n_meta, meta
        self.nb = B if meta else 1
        self.rows = n_meta if meta else tm
        self.grid = (1, 1) if meta else (B, n_real // tm)
        self.m = self.nb * self.rows

    def spec(self, width, lead=(), row_block0=None):
        nl = len(lead)
        if self.meta:
            r0 = self.n_real // self.n_meta if row_block0 is None else row_block0
            return pl.BlockSpec((self.nb,) + tuple(lead) + (self.rows, width),
                                lambda b, i, *_: (0,) + (0,) * nl + (r0, 0))
        return pl.BlockSpec((1,) + tuple(lead) + (self.rows, width),
                            lambda b, i, *_: (b,) + (0,) * nl + (i, 0))

    def table_spec(self, width):
        if self.meta:
            r0 = self.n_real // self.n_meta
            return pl.BlockSpec((self.rows, width), lambda b, i, *_: (r0, 0))
        return pl.BlockSpec((self.rows, width), lambda b, i, *_: (i, 0))


def _const_spec(shape, single_buffer=False):
    nd = len(shape)
    if single_buffer:
        return pl.BlockSpec(shape, lambda *_: (0,) * nd, pipeline_mode=pl.Buffered(1))
    return pl.BlockSpec(shape, lambda *_: (0,) * nd)


def _alias_tail(n_in, outs):
    specs = [pl.BlockSpec(memory_space=pl.ANY) for _ in outs]
    return specs, {n_in + k: k for k in range(len(outs))}


def _cast_body(w_ref, o_ref):
    o_ref[...] = w_ref[...].astype(_BF16)


def _to_bf16(w):
    L, K, N = w.shape
    tk = _pick_tile(K, 512)
    while tk > 8 and tk * N * 4 > CAST_BLOCK_BYTES:
        tk //= 2
    return pl.pallas_call(
        _cast_body,
        grid=(L, K // tk),
        in_specs=[pl.BlockSpec((1, tk, N), lambda l, i: (l, i, 0))],
        out_specs=pl.BlockSpec((1, tk, N), lambda l, i: (l, i, 0)),
        out_shape=jax.ShapeDtypeStruct(w.shape, _BF16),
        compiler_params=_cparams(("parallel", "parallel"), 2 * tk * N * (4 + 2) + VMEM_RESERVE_BYTES),
        name="cast_bf16",
    )(w)


def _ffn_body(*refs, nj, final, n_alias):
    h_ref, gain_ref, wg_ref, wu_ref, wd_ref = refs[:5]
    rest = refs[5:]
    fgain_ref = None
    if final:
        fgain_ref, rest = rest[0], rest[1:]
    o_ref, xn_ref = rest[n_alias:]
    j = pl.program_id(2)
    D = h_ref.shape[-1]

    @pl.when(j == 0)
    def _():
        x = h_ref[...].reshape(-1, D)
        xn_ref[...] = _rms(x, gain_ref[...]).astype(_BF16)
        o_ref[...] = x.reshape(o_ref.shape)

    g = jnp.dot(xn_ref[...], wg_ref[...], preferred_element_type=_F32)
    u = jnp.dot(xn_ref[...], wu_ref[...], preferred_element_type=_F32)
    act = ((g * jax.nn.sigmoid(g)) * u * 0.5).astype(_BF16)
    o_ref[...] += jnp.dot(act, wd_ref[...], preferred_element_type=_F32).reshape(o_ref.shape)

    if final:
        @pl.when(j == nj - 1)
        def _():
            o_ref[...] = _rms(o_ref[...].reshape(-1, D), fgain_ref[...]).reshape(o_ref.shape)


def _ffn(rows, h, gain, wg, wu, wd, *, layer, tf, T, final_gain=None, h_row_block0=None, prev_out=None):
    D = h.shape[-1]
    nj = wg.shape[2] // tf
    final = final_gain is not None
    in_specs = [rows.spec(D, row_block0=h_row_block0), _const_spec((1, D)),
                pl.BlockSpec((None, D, tf), lambda b, i, j: (layer, 0, j)),
                pl.BlockSpec((None, D, tf), lambda b, i, j: (layer, 0, j)),
                pl.BlockSpec((None, tf, D), lambda b, i, j: (layer, j, 0))]
    args = [h, gain, wg, wu, wd]
    if final:
        in_specs.append(_const_spec((1, D)))
        args.append(final_gain)
    aliases = {}
    if prev_out is not None:
        extra, aliases = _alias_tail(len(args), [prev_out])
        in_specs += extra
        args.append(prev_out)
    m = rows.m
    vmem = (4 * m * D * 4 + m * D * 2 + 2 * (D * 2 * tf + tf * D) * 2 + m * 2 * tf * 4 + m * tf * 2
            + VMEM_RESERVE_BYTES)
    out_rows = rows.n_real if final else T
    return pl.pallas_call(
        functools.partial(_ffn_body, nj=nj, final=final, n_alias=len(aliases)),
        grid=rows.grid + (nj,),
        in_specs=in_specs,
        out_specs=rows.spec(D),
        out_shape=jax.ShapeDtypeStruct((rows.B, out_rows, D), _F32),
        scratch_shapes=[pltpu.VMEM((m, D), _BF16)],
        input_output_aliases=aliases,
        compiler_params=_cparams(("parallel", "parallel", "arbitrary"), vmem),
        name=("ffn_final" if final else "ffn") + ("_meta" if rows.meta else ""),
    )(*args)


def _proj_body(*refs, n_rope, n_qkv, n_chunks, nb, n_alias):
    h_ref, gain_ref, w_ref, c_ref, s1_ref, s2_ref, dft_ref = refs[:7]
    qkv_ref, z_ref, xno_ref = refs[7 + n_alias:]
    D = h_ref.shape[-1]
    rows = h_ref.shape[-2]
    xno_ref[...] = _rms(h_ref[...].reshape(-1, D), gain_ref[...]).astype(_BF16).reshape(xno_ref.shape)
    c = jnp.concatenate([c_ref[...]] * nb, axis=0)
    s1 = jnp.concatenate([s1_ref[...]] * nb, axis=0)
    s2 = jnp.concatenate([s2_ref[...]] * nb, axis=0)
    half = ROT_DIM // 2
    for ch in range(n_chunks):
        cols = slice(ch * PROJ_CHUNK, (ch + 1) * PROJ_CHUNK)
        y = jnp.dot(xno_ref[...].reshape(-1, D), w_ref[:, cols], preferred_element_type=_F32)
        if ch < n_rope:
            for hh in range(PROJ_CHUNK // HEAD_DIM):
                xh = y[:, hh * HEAD_DIM:(hh + 1) * HEAD_DIM]
                up = pltpu.roll(xh, HEAD_DIM - half, 1)
                dn = pltpu.roll(xh, half, 1)
                r = (xh * c + up * s1 + dn * s2).astype(_BF16)
                lo = ch * PROJ_CHUNK + hh * HEAD_DIM
                qkv_ref[:, :, lo:lo + HEAD_DIM] = r.reshape(nb, rows, HEAD_DIM)
        elif ch < n_qkv:
            qkv_ref[:, :, cols] = y.astype(_BF16).reshape(nb, rows, PROJ_CHUNK)
        else:
            ub = y.astype(_BF16)
            for gg in range(PROJ_CHUNK // FOURIER_GROUP_DIM):
                zz = jnp.dot(ub[:, gg * FOURIER_GROUP_DIM:(gg + 1) * FOURIER_GROUP_DIM], dft_ref[...],
                             preferred_element_type=_F32)
                lo = (ch - n_qkv) * PROJ_CHUNK + gg * FOURIER_GROUP_DIM
                z_ref[:, 0, :, lo:lo + FOURIER_GROUP_DIM] = (
                    zz[:, :FOURIER_GROUP_DIM].astype(_BF16).reshape(nb, rows, FOURIER_GROUP_DIM))
                z_ref[:, 1, :, lo:lo + FOURIER_GROUP_DIM] = (
                    zz[:, FOURIER_GROUP_DIM:].astype(_BF16).reshape(nb, rows, FOURIER_GROUP_DIM))


def _proj(rows, h, gain, w_in, rope_c, rope_s1, rope_s2, dft_c, *, layer, prev_out=None):
    B, T, D = h.shape
    n_rope = (Q_DIM + KV_DIM) // PROJ_CHUNK
    n_qkv = (Q_DIM + 2 * KV_DIM) // PROJ_CHUNK
    n_chunks = IN_DIM // PROJ_CHUNK
    w_spec = pl.BlockSpec((None, D, IN_DIM), lambda *_: (layer, 0, 0), pipeline_mode=pl.Buffered(1))
    in_specs = [rows.spec(D), _const_spec((1, D)), w_spec,
                rows.table_spec(LANES), rows.table_spec(LANES), rows.table_spec(LANES),
                _const_spec((FOURIER_GROUP_DIM, 2 * FOURIER_GROUP_DIM))]
    args = [h, gain, w_in, rope_c, rope_s1, rope_s2, dft_c]
    aliases = {}
    if prev_out is not None:
        extra, aliases = _alias_tail(len(args), prev_out)
        in_specs += extra
        args += list(prev_out)
    m = rows.m
    vmem = (2 * m * D * 4 + 2 * m * D * 2 + D * IN_DIM * 2 + 6 * m * LANES * 4
            + 2 * m * (Q_DIM + 2 * KV_DIM) * 2 + 4 * m * FOURIER_DIM * 2 + 8 * m * PROJ_CHUNK * 4
            + VMEM_RESERVE_BYTES)
    return pl.pallas_call(
        functools.partial(_proj_body, n_rope=n_rope, n_qkv=n_qkv, n_chunks=n_chunks, nb=rows.nb,
                          n_alias=len(aliases)),
        grid=rows.grid,
        in_specs=in_specs,
        out_specs=[rows.spec(Q_DIM + 2 * KV_DIM), rows.spec(FOURIER_DIM, lead=(2,)), rows.spec(D)],
        out_shape=[jax.ShapeDtypeStruct((B, T, Q_DIM + 2 * KV_DIM), _BF16),
                   jax.ShapeDtypeStruct((B, 2, T, FOURIER_DIM), _BF16),
                   jax.ShapeDtypeStruct((B, T, D), _BF16)],
        input_output_aliases=aliases,
        compiler_params=_cparams(("parallel", "arbitrary"), vmem),
        name="mixer_proj" + ("_meta" if rows.meta else ""),
    )(*args)


def _stack_heads(q):
    return jnp.concatenate([q[:, h * HEAD_DIM:(h + 1) * HEAD_DIM] for h in range(Q_PER_KV)], axis=0)


def _sink_column(sink_ref, g, rows_per_head):
    return jnp.concatenate(
        [jnp.full((rows_per_head, 1), sink_ref[g * Q_PER_KV + h], _F32) for h in range(Q_PER_KV)], axis=0)


def _pad_rows(a, rows):
    return jnp.concatenate([a, jnp.zeros((rows - a.shape[0], a.shape[1]), a.dtype)], axis=0)


def _softmax_pv(q4, k_all, v_all, masks, sink):
    s = lax.dot_general(q4, k_all, (((1,), (1,)), ((), ())), preferred_element_type=_F32) * (HEAD_DIM ** -0.5)
    blocks = []
    for i, mk in enumerate(masks):
        sb = s[:, i * BLOCK:(i + 1) * BLOCK]
        blocks.append(sb if mk is None else jnp.where(mk, sb, NEG_INF))
    mx = jnp.maximum(jnp.max(functools.reduce(jnp.maximum, blocks), axis=-1, keepdims=True), sink)
    p = jnp.concatenate([jnp.exp(sb - mx).astype(_BF16) for sb in blocks], axis=1)
    v_ext = jnp.concatenate([v_all, jnp.ones(v_all.shape, _BF16)], axis=1)
    o = jnp.dot(p, v_ext, preferred_element_type=_F32)
    den = o[:, HEAD_DIM:] + jnp.exp(sink - mx)
    return o[:, :HEAD_DIM] / den


def _attn_body(sink_ref, q_ref, kp_ref, kc_ref, kn_ref, km_ref, vp_ref, vc_ref, vn_ref, vm_ref, o_ref,
               kbuf, vbuf, s_buf, p_buf, es_buf, *, qb, ntile, n_meta):
    g = pl.program_id(1)
    jt = pl.program_id(2)
    rows = Q_PER_KV * BLOCK
    band = 3 * BLOCK
    meta0 = (qb + 2) * BLOCK
    scale = HEAD_DIM ** -0.5
    c_exp2 = scale * LOG2_E

    kbuf[0:BLOCK] = kp_ref[0]
    kbuf[BLOCK:(qb + 1) * BLOCK] = kc_ref[0]
    kbuf[(qb + 1) * BLOCK:meta0] = kn_ref[0]
    kbuf[meta0:meta0 + BLOCK] = _pad_rows(km_ref[0], BLOCK)
    vbuf[0:BLOCK, 0:HEAD_DIM] = vp_ref[0]
    vbuf[BLOCK:(qb + 1) * BLOCK, 0:HEAD_DIM] = vc_ref[0]
    vbuf[(qb + 1) * BLOCK:meta0, 0:HEAD_DIM] = vn_ref[0]
    vbuf[meta0:meta0 + BLOCK, 0:HEAD_DIM] = _pad_rows(vm_ref[0], BLOCK)
    vbuf[:, HEAD_DIM:] = jnp.ones((meta0 + BLOCK, HEAD_DIM), _BF16)

    row = lax.broadcasted_iota(jnp.int32, (rows, BLOCK), 0) % BLOCK
    col = lax.broadcasted_iota(jnp.int32, (rows, BLOCK), 1)
    diff = col - row
    meta_ok = col < n_meta
    never = 2 * BLOCK
    sink_s = _sink_column(sink_ref, g, BLOCK) * (1.0 / scale)

    def dot_t(a, b):
        return lax.dot_general(a, b, (((1,), (1,)), ((), ())), preferred_element_type=_F32)

    def stage_a(i, slot):
        r0 = pl.multiple_of(i * BLOCK, BLOCK)
        q4 = _stack_heads(q_ref[0, pl.ds(r0, BLOCK), :])
        s_buf[slot, :, 0:band] = dot_t(q4, kbuf[pl.ds(r0, band), :])
        s_buf[slot, :, band:] = dot_t(q4, kbuf[meta0:meta0 + BLOCK, :])

    def stage_b(i, slot):
        first = jnp.logical_and(jt == 0, i == 0)
        last = jnp.logical_and(jt == ntile - 1, i == qb - 1)
        off_p = jnp.where(first, never, 0)
        off_n = jnp.where(last, -never, 0)
        blocks = [jnp.where(diff >= off_p, s_buf[slot, :, 0:BLOCK], NEG_INF),
                  s_buf[slot, :, BLOCK:2 * BLOCK],
                  jnp.where(diff <= off_n, s_buf[slot, :, 2 * BLOCK:band], NEG_INF),
                  jnp.where(meta_ok, s_buf[slot, :, band:], NEG_INF)]
        mx = jnp.maximum(jnp.max(functools.reduce(jnp.maximum, blocks), axis=-1, keepdims=True), sink_s)
        for n, sb in enumerate(blocks):
            p_buf[slot, :, n * BLOCK:(n + 1) * BLOCK] = jnp.exp2((sb - mx) * c_exp2).astype(_BF16)
        es_buf[slot] = jnp.broadcast_to(jnp.exp2((sink_s - mx) * c_exp2), (rows, HEAD_DIM))

    def stage_c(i, slot):
        r0 = pl.multiple_of(i * BLOCK, BLOCK)
        o = (jnp.dot(p_buf[slot, :, 0:band], vbuf[pl.ds(r0, band), :], preferred_element_type=_F32)
             + jnp.dot(p_buf[slot, :, band:], vbuf[meta0:meta0 + BLOCK, :], preferred_element_type=_F32))
        out = o[:, :HEAD_DIM] / (o[:, HEAD_DIM:] + es_buf[slot])
        for h in range(Q_PER_KV):
            o_ref[0, pl.ds(r0, BLOCK), h * HEAD_DIM:(h + 1) * HEAD_DIM] = (
                out[h * BLOCK:(h + 1) * BLOCK].astype(_BF16))

    stage_a(0, 0)
    stage_a(1, 1)
    stage_b(0, 0)

    def pair(t, carry):
        i = 2 * t
        stage_c(i, 0)
        stage_b(i + 1, 1)
        stage_a(i + 2, 0)
        stage_c(i + 1, 1)
        stage_b(i + 2, 0)
        stage_a(i + 3, 1)
        return carry

    lax.fori_loop(0, (qb - 2) // 2, pair, 0)
    stage_c(qb - 2, 0)
    stage_b(qb - 1, 1)
    stage_c(qb - 1, 1)


def _attn_meta_body(sink_ref, q_ref, k0_ref, km_ref, v0_ref, vm_ref, prev_out_ref, o_ref, *, n_meta):
    del prev_out_ref
    g = pl.program_id(1)
    rows = Q_PER_KV * n_meta
    row = lax.broadcasted_iota(jnp.int32, (rows, BLOCK), 0) % n_meta
    col = lax.broadcasted_iota(jnp.int32, (rows, BLOCK), 1)
    q4 = _stack_heads(q_ref[0])
    masks = [col <= row + (WINDOW - n_meta), col < n_meta]
    k_all = jnp.concatenate([k0_ref[0], _pad_rows(km_ref[0], BLOCK)], axis=0)
    v_all = jnp.concatenate([v0_ref[0], _pad_rows(vm_ref[0], BLOCK)], axis=0)
    out = _softmax_pv(q4, k_all, v_all, masks, _sink_column(sink_ref, g, n_meta))
    for h in range(Q_PER_KV):
        o_ref[0, :, h * HEAD_DIM:(h + 1) * HEAD_DIM] = out[h * n_meta:(h + 1) * n_meta].astype(_BF16)


def _attention(qkv, sink, *, n_real, n_meta, qb):
    B, T, _ = qkv.shape
    nblk = n_real // BLOCK
    ntile = nblk // qb
    kcol = Q_DIM // HEAD_DIM
    vcol = (Q_DIM + KV_DIM) // HEAD_DIM
    mrow = n_real // n_meta

    def edge_spec(rowf, col0):
        return pl.BlockSpec((1, BLOCK, HEAD_DIM), lambda b, g, j, s: (b, rowf(j), col0 + g))

    def main_spec(col0):
        return pl.BlockSpec((1, qb * BLOCK, HEAD_DIM), lambda b, g, j, s: (b, j, col0 + g))

    def meta_spec(col0):
        return pl.BlockSpec((1, n_meta, HEAD_DIM), lambda b, g, j, s: (b, mrow, col0 + g))

    def prev_i(j):
        return jnp.maximum(qb * j - 1, 0)

    def next_i(j):
        return jnp.minimum(qb * j + qb, nblk - 1)

    attn = pl.pallas_call(
        functools.partial(_attn_body, qb=qb, ntile=ntile, n_meta=n_meta),
        grid_spec=pltpu.PrefetchScalarGridSpec(
            num_scalar_prefetch=1,
            grid=(B, N_KV_HEADS, ntile),
            in_specs=[
                pl.BlockSpec((1, qb * BLOCK, PROJ_CHUNK), lambda b, g, j, s: (b, j, g)),
                edge_spec(prev_i, kcol), main_spec(kcol), edge_spec(next_i, kcol), meta_spec(kcol),
                edge_spec(prev_i, vcol), main_spec(vcol), edge_spec(next_i, vcol), meta_spec(vcol),
            ],
            out_specs=pl.BlockSpec((1, qb * BLOCK, PROJ_CHUNK), lambda b, g, j, s: (b, j, g)),
            scratch_shapes=[
                pltpu.VMEM(((qb + 3) * BLOCK, HEAD_DIM), _BF16),
                pltpu.VMEM(((qb + 3) * BLOCK, 2 * HEAD_DIM), _BF16),
                pltpu.VMEM((2, Q_PER_KV * BLOCK, 4 * BLOCK), _F32),
                pltpu.VMEM((2, Q_PER_KV * BLOCK, 4 * BLOCK), _BF16),
                pltpu.VMEM((2, Q_PER_KV * BLOCK, HEAD_DIM), _F32),
            ],
        ),
        out_shape=jax.ShapeDtypeStruct((B, T, Q_DIM), _BF16),
        compiler_params=_cparams(("parallel", "parallel", "arbitrary"), SMALL_CALL_VMEM_BYTES),
        name="windowed_gqa",
    )(sink, qkv, qkv, qkv, qkv, qkv, qkv, qkv, qkv, qkv)

    def blk0_spec(col0):
        return pl.BlockSpec((1, BLOCK, HEAD_DIM), lambda b, g, s: (b, 0, col0 + g))

    def meta2_spec(col0):
        return pl.BlockSpec((1, n_meta, HEAD_DIM), lambda b, g, s: (b, mrow, col0 + g))

    return pl.pallas_call(
        functools.partial(_attn_meta_body, n_meta=n_meta),
        grid_spec=pltpu.PrefetchScalarGridSpec(
            num_scalar_prefetch=1,
            grid=(B, N_KV_HEADS),
            in_specs=[
                pl.BlockSpec((1, n_meta, PROJ_CHUNK), lambda b, g, s: (b, mrow, g)),
                blk0_spec(kcol), meta2_spec(kcol), blk0_spec(vcol), meta2_spec(vcol),
                pl.BlockSpec(memory_space=pl.ANY),
            ],
            out_specs=pl.BlockSpec((1, n_meta, PROJ_CHUNK), lambda b, g, s: (b, mrow, g)),
        ),
        out_shape=jax.ShapeDtypeStruct((B, T, Q_DIM), _BF16),
        input_output_aliases={6: 0},
        compiler_params=_cparams(("parallel", "arbitrary"), SMALL_CALL_VMEM_BYTES),
        name="windowed_gqa_meta",
    )(sink, qkv, qkv, qkv, qkv, qkv, attn)


def _fft_factors(T):
    best = None
    for n2 in range(BF16_SUBLANES, T + 1, BF16_SUBLANES):
        if T % n2:
            continue
        n1 = T // n2
        r1 = n1 * BF16_SUBLANES
        cost = (4 * _round_up(r1, MXU_DEPTH) * r1 * (n2 // BF16_SUBLANES)
                + 2 * _round_up(n2, MXU_DEPTH) * n2 * n1)
        if best is None or cost < best[0]:
            best = (cost, n1, n2)
    assert best is not None, "sequence length must be a multiple of 16"
    return best[1], best[2]


def _fft1_body(z_ref, mc_ref, ms_ref, twc_ref, tws_ref, a_ref, *, n1, R, C):
    rows = n1 * R
    rhs = jnp.concatenate([z_ref[0, 0].reshape(rows, C), z_ref[0, 1].reshape(rows, C)], axis=1)
    xc = jnp.dot(mc_ref[...], rhs, preferred_element_type=_F32)
    xs = jnp.dot(ms_ref[...], rhs, preferred_element_type=_F32)
    ar = xc[:, :C] + xs[:, C:]
    ai = xc[:, C:] - xs[:, :C]
    twc = jnp.concatenate([twc_ref[0]] * (C // LANES), axis=1)
    tws = jnp.concatenate([tws_ref[0]] * (C // LANES), axis=1)
    a_ref[0, 0] = (ar * twc + ai * tws).reshape(n1, R, C).astype(_BF16)
    a_ref[0, 1] = (ai * twc - ar * tws).reshape(n1, R, C).astype(_BF16)


def _fft2_body(ar_ref, ai_ref, mc_ref, ms_ref, f_ref):
    for b in range(ar_ref.shape[0]):
        res = (jnp.dot(mc_ref[0], ar_ref[b, 0, 0], preferred_element_type=_F32)
               + jnp.dot(ms_ref[0], ai_ref[b, 0, 0], preferred_element_type=_F32))
        f_ref[b] = res.astype(_BF16)


def _fft_tables(T, n1, n2, n_meta, R):
    two_pi = 2.0 * np.pi
    k1 = jnp.arange(n1, dtype=jnp.int32)
    ang1 = ((k1[:, None] * k1[None, :]) % n1).astype(_F32) * (two_pi / n1)
    eye = jnp.eye(R, dtype=_F32)
    m1c = jnp.kron(jnp.cos(ang1) * (n1 ** -0.5), eye).astype(_BF16)
    m1s = jnp.kron(jnp.sin(ang1) * (n1 ** -0.5), eye).astype(_BF16)
    nn2 = jnp.arange(n2, dtype=jnp.int32).reshape(n2 // R, 1, R)
    angt = ((k1[None, :, None] * (nn2 + n_meta)) % T).astype(_F32) * (two_pi / T)
    angt = jnp.broadcast_to(angt.reshape(n2 // R, n1 * R, 1), (n2 // R, n1 * R, LANES))
    twc, tws = jnp.cos(angt), jnp.sin(angt)
    k2 = jnp.arange(n2, dtype=jnp.int32)[:, None]
    mm2 = jnp.arange(n2, dtype=jnp.int32)[None, :]
    outs_c, outs_s = [], []
    for rot in (n_meta // n1, n_meta // n1 + 1):
        ph2 = (((k2 + rot) % n2) * (mm2 + n_meta)) % n2
        ang2 = ph2.astype(_F32) * (two_pi / n2)
        outs_c.append(jnp.cos(ang2) * (n2 ** -0.5))
        outs_s.append(jnp.sin(ang2) * (n2 ** -0.5))
    return m1c, m1s, twc, tws, jnp.stack(outs_c).astype(_BF16), jnp.stack(outs_s).astype(_BF16)


def _position_dft(z, tables, *, n1, n2, R, n_meta):
    B, _, T, C = z.shape
    m1c, m1s, twc, tws, m2c, m2s = tables
    rows = n1 * R
    zv = z.reshape(B, 2, n1, n2, C)
    av = pl.pallas_call(
        functools.partial(_fft1_body, n1=n1, R=R, C=C),
        grid=(B, n2 // R),
        in_specs=[
            pl.BlockSpec((1, 2, n1, R, C), lambda b, t: (b, 0, 0, t, 0)),
            _const_spec((rows, rows)),
            _const_spec((rows, rows)),
            pl.BlockSpec((1, rows, LANES), lambda b, t: (t, 0, 0)),
            pl.BlockSpec((1, rows, LANES), lambda b, t: (t, 0, 0)),
        ],
        out_specs=pl.BlockSpec((1, 2, n1, R, C), lambda b, t: (b, 0, 0, t, 0)),
        out_shape=jax.ShapeDtypeStruct((B, 2, n1, n2, C), _BF16),
        compiler_params=_cparams(("parallel", "arbitrary"),
                                 4 * 2 * rows * C * 2 + 2 * 2 * rows * rows * 2 + 6 * rows * C * 4 + VMEM_RESERVE_BYTES),
        name="dft_stage1",
    )(zv, m1c, m1s, twc, tws)
    shift = n1 - (n_meta % n1)

    f = pl.pallas_call(
        _fft2_body,
        grid=(n1,),
        in_specs=[
            pl.BlockSpec((B, 1, 1, n2, C), lambda k: (0, 0, k, 0, 0)),
            pl.BlockSpec((B, 1, 1, n2, C), lambda k: (0, 1, k, 0, 0)),
            pl.BlockSpec((1, n2, n2), lambda k: (jnp.where(k < n_meta % n1, 1, 0), 0, 0)),
            pl.BlockSpec((1, n2, n2), lambda k: (jnp.where(k < n_meta % n1, 1, 0), 0, 0)),
        ],
        out_specs=pl.BlockSpec((B, n2, C), lambda k: (0, 0, (k + shift) % n1)),
        out_shape=jax.ShapeDtypeStruct((B, n2, n1 * C), _BF16),
        compiler_params=_cparams(("arbitrary",), SMALL_CALL_VMEM_BYTES),
        name="dft_stage2",
    )(av, av, m2c, m2s)
    return f.reshape(B, T, C)


def _mix_body(*refs, tn, n_alias):
    h_ref, xn_ref, a_ref, f_ref, wpa_ref, wpf_ref, wg_ref, wo_ref = refs[:8]
    (o_ref,) = refs[8 + n_alias:]
    D = h_ref.shape[-1]
    o_ref[...] = h_ref[...]
    xn = xn_ref[...].reshape(-1, D)
    a = a_ref[...].reshape(-1, Q_DIM)
    f = f_ref[...].reshape(-1, FOURIER_DIM)
    for j in range(D // tn):
        cols = slice(j * tn, (j + 1) * tn)
        av = jnp.dot(a, wpa_ref[:, cols], preferred_element_type=_F32)
        fv = jnp.dot(f, wpf_ref[:, cols], preferred_element_type=_F32)
        ga = jax.nn.sigmoid(jnp.dot(xn, wg_ref[:, cols], preferred_element_type=_F32))
        gf = jax.nn.sigmoid(jnp.dot(xn, wg_ref[:, D + j * tn:D + (j + 1) * tn], preferred_element_type=_F32))
        mixed = (ga * av + gf * fv).astype(_BF16)
        o_ref[...] += jnp.dot(mixed, wo_ref[cols, :], preferred_element_type=_F32).reshape(o_ref.shape)


def _mix_out(rows, h, xn, attn, four, wpa, wpf, wg, wo, *, layer, tn, prev_out=None):
    B, T, D = h.shape

    def resident(k, n):
        return pl.BlockSpec((None, k, n), lambda *_: (layer, 0, 0), pipeline_mode=pl.Buffered(1))

    in_specs = [rows.spec(D), rows.spec(D), rows.spec(Q_DIM), rows.spec(FOURIER_DIM),
                resident(Q_DIM, D), resident(FOURIER_DIM, D), resident(D, 2 * D), resident(D, D)]
    args = [h, xn, attn, four, wpa, wpf, wg, wo]
    aliases = {}
    if prev_out is not None:
        extra, aliases = _alias_tail(len(args), [prev_out])
        in_specs += extra
        args.append(prev_out)
    m = rows.m
    vmem = ((Q_DIM + FOURIER_DIM + 3 * D) * D * 2 + 4 * m * D * 4 + 2 * m * D * 2
            + 2 * m * (Q_DIM + FOURIER_DIM) * 2 + 10 * m * tn * 4 + VMEM_RESERVE_BYTES)
    return pl.pallas_call(
        functools.partial(_mix_body, tn=tn, n_alias=len(aliases)),
        grid=rows.grid,
        in_specs=in_specs,
        out_specs=rows.spec(D),
        out_shape=jax.ShapeDtypeStruct((B, T, D), _F32),
        input_output_aliases=aliases,
        compiler_params=_cparams(("parallel", "arbitrary"), vmem),
        name="mixer_out" + ("_meta" if rows.meta else ""),
    )(*args)


def _rope_tables(n_real, n_meta):
    T = n_real + n_meta
    half = ROT_DIM // 2
    inv = 1.0 / (ROPE_THETA ** (jnp.arange(0, ROT_DIM, 2, dtype=_F32) / ROT_DIM))
    ang = jnp.arange(T, dtype=_F32)[:, None] * inv[None, :]
    cos, sin = jnp.cos(ang), jnp.sin(ang)
    cos = jnp.concatenate([cos[n_meta:], cos[:n_meta]], axis=0)
    sin = jnp.concatenate([sin[n_meta:], sin[:n_meta]], axis=0)
    ones = jnp.ones((T, HEAD_DIM - ROT_DIM), _F32)
    zeros_h = jnp.zeros((T, half), _F32)
    zeros_r = jnp.zeros((T, HEAD_DIM - ROT_DIM), _F32)
    c = jnp.concatenate([cos, cos, ones], axis=1)
    s1 = jnp.concatenate([-sin, zeros_h, zeros_r], axis=1)
    s2 = jnp.concatenate([zeros_h, sin, zeros_r], axis=1)
    return c, s1, s2


def _channel_dft_table():
    n = FOURIER_GROUP_DIM
    c = jnp.arange(n, dtype=jnp.int32)[:, None]
    k = jnp.arange(n, dtype=jnp.int32)[None, :]
    ang = ((c * k) % n).astype(_F32) * (2.0 * np.pi / n)
    return (jnp.concatenate([jnp.cos(ang), -jnp.sin(ang)], axis=1) * (n ** -0.5)).astype(_BF16)


def _pick_tile(n, pref):
    t = pref
    while n % t:
        t //= 2
    return t


def kernel(x, meta_tokens, ffn1_norm, ffn1_w_gate, ffn1_w_up, ffn1_w_down, mix_norm, w_in, w_gate, sink,
           w_attn_branch, w_fourier_branch, w_out, ffn2_norm, ffn2_w_gate, ffn2_w_up, ffn2_w_down, final_norm):
    B, S, D = x.shape
    n_meta = meta_tokens.shape[0]
    depth = ffn1_norm.shape[0]
    d_ff = ffn1_w_gate.shape[-1]
    T = S + n_meta
    assert S % (4 * BLOCK) == 0 and S % n_meta == 0 and BLOCK % n_meta == 0
    assert w_in.shape[-1] == IN_DIM and KV_DIM == PROJ_CHUNK

    tf = _pick_tile(d_ff, FF_CHUNK)
    tn = _pick_tile(D, MIX_CHUNK)
    qb = _pick_tile(S // BLOCK, ATTN_BLOCKS_PER_STEP)
    rows_ffn = _Rows(B, S, n_meta, _pick_tile(S, FFN_ROW_TILE), meta=False)
    rows_main = _Rows(B, S, n_meta, _pick_tile(S, PROJ_ROW_TILE), meta=False)
    rows_mix = _Rows(B, S, n_meta, _pick_tile(S, MIX_ROW_TILE), meta=False)
    rows_meta = _Rows(B, S, n_meta, 0, meta=True)

    n1, n2 = _fft_factors(T)
    fft_tables = _fft_tables(T, n1, n2, n_meta, BF16_SUBLANES)
    rope_c, rope_s1, rope_s2 = _rope_tables(S, n_meta)
    dft_c = _channel_dft_table()

    ffn1_w = tuple(_to_bf16(w) for w in (ffn1_w_gate, ffn1_w_up, ffn1_w_down))
    ffn2_w = tuple(_to_bf16(w) for w in (ffn2_w_gate, ffn2_w_up, ffn2_w_down))
    w_in_b = _to_bf16(w_in)
    mix_w = tuple(_to_bf16(w) for w in (w_attn_branch, w_fourier_branch, w_gate, w_out))

    def ffn(h, h_meta, gain, wts, layer, final_gain=None):
        out = _ffn(rows_ffn, h, gain, *wts, layer=layer, tf=tf, T=T, final_gain=final_gain)
        if final_gain is None:
            src, blk0 = h_meta
            out = _ffn(rows_meta, src, gain, *wts, layer=layer, tf=tf, T=T, h_row_block0=blk0, prev_out=out)
        return out

    meta = jnp.broadcast_to(meta_tokens[None].astype(x.dtype), (B, n_meta, D))
    h = None
    for l in range(depth):
        if l == 0:
            h = ffn(x, (meta, 0), ffn1_norm[l][None], ffn1_w, l)
        else:
            h = ffn(h, (h, None), ffn1_norm[l][None], ffn1_w, l)

        gain = mix_norm[l][None]
        proj_args = (gain, w_in_b, rope_c, rope_s1, rope_s2, dft_c)
        qkv, z, xn = _proj(rows_main, h, *proj_args, layer=l)
        qkv, z, xn = _proj(rows_meta, h, *proj_args, layer=l, prev_out=(qkv, z, xn))
        attn = _attention(qkv, sink[l], n_real=S, n_meta=n_meta, qb=qb)
        four = _position_dft(z, fft_tables, n1=n1, n2=n2, R=BF16_SUBLANES, n_meta=n_meta)
        mix_args = (xn, attn, four) + mix_w
        h2 = _mix_out(rows_mix, h, *mix_args, layer=l, tn=tn)
        h = _mix_out(rows_meta, h, *mix_args, layer=l, tn=tn, prev_out=h2)

        last = l == depth - 1
        h = ffn(h, (h, None), ffn2_norm[l][None], ffn2_w, l, final_gain=final_norm[None] if last else None)
    return h
```

```python
import functools

import numpy as np
import jax
import jax.numpy as jnp
from jax import lax
from jax.experimental import pallas as pl
from jax.experimental.pallas import tpu as pltpu

_F32 = jnp.float32
_BF16 = jnp.bfloat16

HEAD_DIM = 128
N_Q_HEADS = 16
N_KV_HEADS = 4
Q_PER_KV = N_Q_HEADS // N_KV_HEADS
WINDOW = 128
BLOCK = 128
ROPE_THETA = 500000.0
ROT_DIM = HEAD_DIM // 4
N_FOURIER_GROUPS = 8
FOURIER_GROUP_DIM = 128
FOURIER_DIM = N_FOURIER_GROUPS * FOURIER_GROUP_DIM
Q_DIM = N_Q_HEADS * HEAD_DIM
KV_DIM = N_KV_HEADS * HEAD_DIM
IN_DIM = Q_DIM + 2 * KV_DIM + FOURIER_DIM
RMS_EPS = 1e-6
NEG_INF = -1e30

LANES = 128
BF16_SUBLANES = 16
PROJ_CHUNK = Q_PER_KV * HEAD_DIM
V7X_VMEM_BYTES = 64 * 1024 * 1024
VMEM_RESERVE_BYTES = 4 << 20
SMALL_CALL_VMEM_BYTES = 32 << 20
CAST_BLOCK_BYTES = 6 << 20
LOG2_E = 1.4426950408889634
MXU_DEPTH = 256

FFN_ROW_TILE = 1024
FF_CHUNK = 512
PROJ_ROW_TILE = 512
MIX_ROW_TILE = 256
MIX_CHUNK = 1024
ATTN_BLOCKS_PER_STEP = 32


def _round_up(a, m):
    return (a + m - 1) // m * m


def _cparams(sem, vmem_bytes):
    return pltpu.CompilerParams(dimension_semantics=sem,
                                vmem_limit_bytes=int(min(vmem_bytes, V7X_VMEM_BYTES - VMEM_RESERVE_BYTES)))


def _rms(x, g):
    return x * lax.rsqrt(jnp.mean(x * x, axis=-1, keepdims=True) + RMS_EPS) * g


class _Rows:
    def __init__(self, B, n_real, n_meta, tm, meta):
        self.B, self.n_real, self.n_meta, self.meta = B, n_real, n_meta, meta
        self.nb = B if meta else 1
        self.rows = n_meta if meta else tm
        self.grid = (1, 1) if meta else (B, n_real // tm)
        self.m = self.nb * self.rows

    def spec(self, width, lead=(), row_block0=None):
        nl = len(lead)
        if self.meta:
            r0 = self.n_real // self.n_meta if row_block0 is None else row_block0
            return pl.BlockSpec((self.nb,) + tuple(lead) + (self.rows, width),
                                lambda b, i, *_: (0,) + (0,) * nl + (r0, 0))
        return pl.BlockSpec((1,) + tuple(lead) + (self.rows, width),
                            lambda b, i, *_: (b,) + (0,) * nl + (i, 0))

    def table_spec(self, width):
        if self.meta:
            r0 = self.n_real // self.n_meta
            return pl.BlockSpec((self.rows, width), lambda b, i, *_: (r0, 0))
        return pl.BlockSpec((self.rows, width), lambda b, i, *_: (i, 0))


def _const_spec(shape, single_buffer=False):
    nd = len(shape)
    if single_buffer:
        return pl.BlockSpec(shape, lambda *_: (0,) * nd, pipeline_mode=pl.Buffered(1))
    return pl.BlockSpec(shape, lambda *_: (0,) * nd)


def _alias_tail(n_in, outs):
    specs = [pl.BlockSpec(memory_space=pl.ANY) for _ in outs]
    return specs, {n_in + k: k for k in range(len(outs))}


def _cast_body(w_ref, o_ref):
    o_ref[...] = w_ref[...].astype(_BF16)


def _to_bf16(w):
    L, K, N = w.shape
    tk = _pick_tile(K, 512)
    while tk > 8 and tk * N * 4 > CAST_BLOCK_BYTES:
        tk //= 2
    return pl.pallas_call(
        _cast_body,
        grid=(L, K // tk),
        in_specs=[pl.BlockSpec((1, tk, N), lambda l, i: (l, i, 0))],
        out_specs=pl.BlockSpec((1, tk, N), lambda l, i: (l, i, 0)),
        out_shape=jax.ShapeDtypeStruct(w.shape, _BF16),
        compiler_params=_cparams(("parallel", "parallel"), 2 * tk * N * (4 + 2) + VMEM_RESERVE_BYTES),
        name="cast_bf16",
    )(w)


def _ffn_body(*refs, nj, final, n_alias):
    h_ref, gain_ref, wg_ref, wu_ref, wd_ref = refs[:5]
    rest = refs[5:]
    fgain_ref = None
    if final:
        fgain_ref, rest = rest[0], rest[1:]
    o_ref, xn_ref = rest[n_alias:]
    j = pl.program_id(2)
    D = h_ref.shape[-1]

    @pl.when(j == 0)
    def _():
        x = h_ref[...].reshape(-1, D)
        xn_ref[...] = _rms(x, gain_ref[...]).astype(_BF16)
        o_ref[...] = x.reshape(o_ref.shape)

    g = jnp.dot(xn_ref[...], wg_ref[...], preferred_element_type=_F32)
    u = jnp.dot(xn_ref[...], wu_ref[...], preferred_element_type=_F32)
    act = ((g * jax.nn.sigmoid(g)) * u * 0.5).astype(_BF16)
    o_ref[...] += jnp.dot(act, wd_ref[...], preferred_element_type=_F32).reshape(o_ref.shape)

    if final:
        @pl.when(j == nj - 1)
        def _():
            o_ref[...] = _rms(o_ref[...].reshape(-1, D), fgain_ref[...]).reshape(o_ref.shape)


def _ffn(rows, h, gain, wg, wu, wd, *, layer, tf, T, final_gain=None, h_row_block0=None, prev_out=None):
    D = h.shape[-1]
    nj = wg.shape[2] // tf
    final = final_gain is not None
    in_specs = [rows.spec(D, row_block0=h_row_block0), _const_spec((1, D)),
                pl.BlockSpec((None, D, tf), lambda b, i, j: (layer, 0, j)),
                pl.BlockSpec((None, D, tf), lambda b, i, j: (layer, 0, j)),
                pl.BlockSpec((None, tf, D), lambda b, i, j: (layer, j, 0))]
    args = [h, gain, wg, wu, wd]
    if final:
        in_specs.append(_const_spec((1, D)))
        args.append(final_gain)
    aliases = {}
    if prev_out is not None:
        extra, aliases = _alias_tail(len(args), [prev_out])
        in_specs += extra
        args.append(prev_out)
    m = rows.m
    vmem = (4 * m * D * 4 + m * D * 2 + 2 * (D * 2 * tf + tf * D) * 2 + m * 2 * tf * 4 + m * tf * 2
            + VMEM_RESERVE_BYTES)
    out_rows = rows.n_real if final else T
    return pl.pallas_call(
        functools.partial(_ffn_body, nj=nj, final=final, n_alias=len(aliases)),
        grid=rows.grid + (nj,),
        in_specs=in_specs,
        out_specs=rows.spec(D),
        out_shape=jax.ShapeDtypeStruct((rows.B, out_rows, D), _F32),
        scratch_shapes=[pltpu.VMEM((m, D), _BF16)],
        input_output_aliases=aliases,
        compiler_params=_cparams(("parallel", "parallel", "arbitrary"), vmem),
        name=("ffn_final" if final else "ffn") + ("_meta" if rows.meta else ""),
    )(*args)


def _proj_body(*refs, n_rope, n_qkv, n_chunks, nb, n_alias, n_side):
    h_ref, gain_ref, w_ref, c_ref, s1_ref, s2_ref, dft_ref = refs[:7]
    side_in = refs[7:7 + n_side]
    qkv_ref, z_ref, xno_ref = refs[7 + n_side + n_alias:10 + n_side + n_alias]
    side_out = refs[10 + n_side + n_alias:]
    for src_ref, dst_ref in zip(side_in, side_out):
        dst_ref[...] = src_ref[...].astype(_BF16)
    D = h_ref.shape[-1]
    rows = h_ref.shape[-2]
    xno_ref[...] = _rms(h_ref[...].reshape(-1, D), gain_ref[...]).astype(_BF16).reshape(xno_ref.shape)
    c = jnp.concatenate([c_ref[...]] * nb, axis=0)
    s1 = jnp.concatenate([s1_ref[...]] * nb, axis=0)
    s2 = jnp.concatenate([s2_ref[...]] * nb, axis=0)
    half = ROT_DIM // 2
    for ch in range(n_chunks):
        cols = slice(ch * PROJ_CHUNK, (ch + 1) * PROJ_CHUNK)
        y = jnp.dot(xno_ref[...].reshape(-1, D), w_ref[:, cols], preferred_element_type=_F32)
        if ch < n_rope:
            for hh in range(PROJ_CHUNK // HEAD_DIM):
                xh = y[:, hh * HEAD_DIM:(hh + 1) * HEAD_DIM]
                up = pltpu.roll(xh, HEAD_DIM - half, 1)
                dn = pltpu.roll(xh, half, 1)
                r = (xh * c + up * s1 + dn * s2).astype(_BF16)
                lo = ch * PROJ_CHUNK + hh * HEAD_DIM
                qkv_ref[:, :, lo:lo + HEAD_DIM] = r.reshape(nb, rows, HEAD_DIM)
        elif ch < n_qkv:
            qkv_ref[:, :, cols] = y.astype(_BF16).reshape(nb, rows, PROJ_CHUNK)
        else:
            ub = y.astype(_BF16)
            for gg in range(PROJ_CHUNK // FOURIER_GROUP_DIM):
                zz = jnp.dot(ub[:, gg * FOURIER_GROUP_DIM:(gg + 1) * FOURIER_GROUP_DIM], dft_ref[...],
                             preferred_element_type=_F32)
                lo = (ch - n_qkv) * PROJ_CHUNK + gg * FOURIER_GROUP_DIM
                z_ref[:, 0, :, lo:lo + FOURIER_GROUP_DIM] = (
                    zz[:, :FOURIER_GROUP_DIM].astype(_BF16).reshape(nb, rows, FOURIER_GROUP_DIM))
                z_ref[:, 1, :, lo:lo + FOURIER_GROUP_DIM] = (
                    zz[:, FOURIER_GROUP_DIM:].astype(_BF16).reshape(nb, rows, FOURIER_GROUP_DIM))


def _proj(rows, h, gain, w_in, rope_c, rope_s1, rope_s2, dft_c, *, layer, prev_out=None, side_cast=()):
    B, T, D = h.shape
    n_rope = (Q_DIM + KV_DIM) // PROJ_CHUNK
    n_qkv = (Q_DIM + 2 * KV_DIM) // PROJ_CHUNK
    n_chunks = IN_DIM // PROJ_CHUNK
    w_spec = pl.BlockSpec((None, D, IN_DIM), lambda *_: (layer, 0, 0), pipeline_mode=pl.Buffered(1))
    in_specs = [rows.spec(D), _const_spec((1, D)), w_spec,
                rows.table_spec(LANES), rows.table_spec(LANES), rows.table_spec(LANES),
                _const_spec((FOURIER_GROUP_DIM, 2 * FOURIER_GROUP_DIM))]
    args = [h, gain, w_in, rope_c, rope_s1, rope_s2, dft_c]
    out_specs = [rows.spec(Q_DIM + 2 * KV_DIM), rows.spec(FOURIER_DIM, lead=(2,)), rows.spec(D)]
    out_shape = [jax.ShapeDtypeStruct((B, T, Q_DIM + 2 * KV_DIM), _BF16),
                 jax.ShapeDtypeStruct((B, 2, T, FOURIER_DIM), _BF16),
                 jax.ShapeDtypeStruct((B, T, D), _BF16)]
    m = rows.m
    vmem = (2 * m * D * 4 + 2 * m * D * 2 + D * IN_DIM * 2 + 6 * m * LANES * 4
            + 2 * m * (Q_DIM + 2 * KV_DIM) * 2 + 4 * m * FOURIER_DIM * 2 + 8 * m * PROJ_CHUNK * 4
            + VMEM_RESERVE_BYTES)
    n_steps = rows.grid[0] * rows.grid[1]
    for w in side_cast:
        L, K, N = w.shape
        rps = L * K // n_steps
        assert L * K % n_steps == 0 and rps % BF16_SUBLANES == 0
        spec = pl.BlockSpec((rps, N), lambda b, i: (b * rows.grid[1] + i, 0))
        in_specs.append(spec)
        args.append(w.reshape(L * K, N))
        out_specs.append(spec)
        out_shape.append(jax.ShapeDtypeStruct((L * K, N), _BF16))
        vmem += 2 * rps * N * (4 + 2)
    aliases = {}
    if prev_out is not None:
        extra, aliases = _alias_tail(len(args), prev_out)
        in_specs += extra
        args += list(prev_out)
    outs = pl.pallas_call(
        functools.partial(_proj_body, n_rope=n_rope, n_qkv=n_qkv, n_chunks=n_chunks, nb=rows.nb,
                          n_alias=len(aliases), n_side=len(side_cast)),
        grid=rows.grid,
        in_specs=in_specs,
        out_specs=out_specs,
        out_shape=out_shape,
        input_output_aliases=aliases,
        compiler_params=_cparams(("parallel", "arbitrary"), vmem),
        name="mixer_proj" + ("_meta" if rows.meta else ""),
    )(*args)
    return tuple(outs[:3]) + tuple(o.reshape(w.shape) for o, w in zip(outs[3:], side_cast))


def _stack_heads(q):
    return jnp.concatenate([q[:, h * HEAD_DIM:(h + 1) * HEAD_DIM] for h in range(Q_PER_KV)], axis=0)


def _sink_column(sink_ref, g, rows_per_head):
    return jnp.concatenate(
        [jnp.full((rows_per_head, 1), sink_ref[g * Q_PER_KV + h], _F32) for h in range(Q_PER_KV)], axis=0)


def _pad_rows(a, rows):
    return jnp.concatenate([a, jnp.zeros((rows - a.shape[0], a.shape[1]), a.dtype)], axis=0)


def _softmax_pv(q4, k_all, v_all, masks, sink):
    s = lax.dot_general(q4, k_all, (((1,), (1,)), ((), ())), preferred_element_type=_F32) * (HEAD_DIM ** -0.5)
    blocks = []
    for i, mk in enumerate(masks):
        sb = s[:, i * BLOCK:(i + 1) * BLOCK]
        blocks.append(sb if mk is None else jnp.where(mk, sb, NEG_INF))
    mx = jnp.maximum(jnp.max(functools.reduce(jnp.maximum, blocks), axis=-1, keepdims=True), sink)
    p = jnp.concatenate([jnp.exp(sb - mx).astype(_BF16) for sb in blocks], axis=1)
    v_ext = jnp.concatenate([v_all, jnp.ones(v_all.shape, _BF16)], axis=1)
    o = jnp.dot(p, v_ext, preferred_element_type=_F32)
    den = o[:, HEAD_DIM:] + jnp.exp(sink - mx)
    return o[:, :HEAD_DIM] / den


def _attn_body(sink_ref, q_ref, kp_ref, kc_ref, kn_ref, km_ref, vp_ref, vc_ref, vn_ref, vm_ref, o_ref,
               kbuf, vbuf, s_buf, p_buf, es_buf, *, qb, ntile, n_meta):
    g = pl.program_id(1)
    jt = pl.program_id(2)
    rows = Q_PER_KV * BLOCK
    band = 3 * BLOCK
    meta0 = (qb + 2) * BLOCK
    scale = HEAD_DIM ** -0.5
    c_exp2 = scale * LOG2_E

    kbuf[0:BLOCK] = kp_ref[0]
    kbuf[BLOCK:(qb + 1) * BLOCK] = kc_ref[0]
    kbuf[(qb + 1) * BLOCK:meta0] = kn_ref[0]
    kbuf[meta0:meta0 + BLOCK] = _pad_rows(km_ref[0], BLOCK)
    vbuf[0:BLOCK, 0:HEAD_DIM] = vp_ref[0]
    vbuf[BLOCK:(qb + 1) * BLOCK, 0:HEAD_DIM] = vc_ref[0]
    vbuf[(qb + 1) * BLOCK:meta0, 0:HEAD_DIM] = vn_ref[0]
    vbuf[meta0:meta0 + BLOCK, 0:HEAD_DIM] = _pad_rows(vm_ref[0], BLOCK)
    vbuf[:, HEAD_DIM:] = jnp.ones((meta0 + BLOCK, HEAD_DIM), _BF16)

    row = lax.broadcasted_iota(jnp.int32, (rows, BLOCK), 0) % BLOCK
    col = lax.broadcasted_iota(jnp.int32, (rows, BLOCK), 1)
    diff = col - row
    meta_ok = col < n_meta
    never = 2 * BLOCK
    sink_s = _sink_column(sink_ref, g, BLOCK) * (1.0 / scale)

    def dot_t(a, b):
        return lax.dot_general(a, b, (((1,), (1,)), ((), ())), preferred_element_type=_F32)

    def stage_a(i, slot):
        r0 = pl.multiple_of(i * BLOCK, BLOCK)
        q4 = _stack_heads(q_ref[0, pl.ds(r0, BLOCK), :])
        s_buf[slot, :, 0:band] = dot_t(q4, kbuf[pl.ds(r0, band), :])
        s_buf[slot, :, band:] = dot_t(q4, kbuf[meta0:meta0 + BLOCK, :])

    def stage_b(i, slot):
        first = jnp.logical_and(jt == 0, i == 0)
        last = jnp.logical_and(jt == ntile - 1, i == qb - 1)
        off_p = jnp.where(first, never, 0)
        off_n = jnp.where(last, -never, 0)
        blocks = [jnp.where(diff >= off_p, s_buf[slot, :, 0:BLOCK], NEG_INF),
                  s_buf[slot, :, BLOCK:2 * BLOCK],
                  jnp.where(diff <= off_n, s_buf[slot, :, 2 * BLOCK:band], NEG_INF),
                  jnp.where(meta_ok, s_buf[slot, :, band:], NEG_INF)]
        mx = jnp.maximum(jnp.max(functools.reduce(jnp.maximum, blocks), axis=-1, keepdims=True), sink_s)
        for n, sb in enumerate(blocks):
            p_buf[slot, :, n * BLOCK:(n + 1) * BLOCK] = jnp.exp2((sb - mx) * c_exp2).astype(_BF16)
        es_buf[slot] = jnp.broadcast_to(jnp.exp2((sink_s - mx) * c_exp2), (rows, HEAD_DIM))

    def stage_c(i, slot):
        r0 = pl.multiple_of(i * BLOCK, BLOCK)
        o = (jnp.dot(p_buf[slot, :, 0:band], vbuf[pl.ds(r0, band), :], preferred_element_type=_F32)
             + jnp.dot(p_buf[slot, :, band:], vbuf[meta0:meta0 + BLOCK, :], preferred_element_type=_F32))
        out = o[:, :HEAD_DIM] / (o[:, HEAD_DIM:] + es_buf[slot])
        for h in range(Q_PER_KV):
            o_ref[0, pl.ds(r0, BLOCK), h * HEAD_DIM:(h + 1) * HEAD_DIM] = (
                out[h * BLOCK:(h + 1) * BLOCK].astype(_BF16))

    stage_a(0, 0)
    stage_a(1, 1)
    stage_b(0, 0)

    def pair(t, carry):
        i = 2 * t
        stage_c(i, 0)
        stage_b(i + 1, 1)
        stage_a(i + 2, 0)
        stage_c(i + 1, 1)
        stage_b(i + 2, 0)
        stage_a(i + 3, 1)
        return carry

    lax.fori_loop(0, (qb - 2) // 2, pair, 0)
    stage_c(qb - 2, 0)
    stage_b(qb - 1, 1)
    stage_c(qb - 1, 1)


def _attn_meta_body(sink_ref, q_ref, k0_ref, km_ref, v0_ref, vm_ref, prev_out_ref, o_ref, *, n_meta):
    del prev_out_ref
    g = pl.program_id(1)
    rows = Q_PER_KV * n_meta
    row = lax.broadcasted_iota(jnp.int32, (rows, BLOCK), 0) % n_meta
    col = lax.broadcasted_iota(jnp.int32, (rows, BLOCK), 1)
    q4 = _stack_heads(q_ref[0])
    masks = [col <= row + (WINDOW - n_meta), col < n_meta]
    k_all = jnp.concatenate([k0_ref[0], _pad_rows(km_ref[0], BLOCK)], axis=0)
    v_all = jnp.concatenate([v0_ref[0], _pad_rows(vm_ref[0], BLOCK)], axis=0)
    out = _softmax_pv(q4, k_all, v_all, masks, _sink_column(sink_ref, g, n_meta))
    for h in range(Q_PER_KV):
        o_ref[0, :, h * HEAD_DIM:(h + 1) * HEAD_DIM] = out[h * n_meta:(h + 1) * n_meta].astype(_BF16)


def _attention(qkv, sink, *, n_real, n_meta, qb):
    B, T, _ = qkv.shape
    nblk = n_real // BLOCK
    ntile = nblk // qb
    kcol = Q_DIM // HEAD_DIM
    vcol = (Q_DIM + KV_DIM) // HEAD_DIM
    mrow = n_real // n_meta

    def edge_spec(rowf, col0):
        return pl.BlockSpec((1, BLOCK, HEAD_DIM), lambda b, g, j, s: (b, rowf(j), col0 + g))

    def main_spec(col0):
        return pl.BlockSpec((1, qb * BLOCK, HEAD_DIM), lambda b, g, j, s: (b, j, col0 + g))

    def meta_spec(col0):
        return pl.BlockSpec((1, n_meta, HEAD_DIM), lambda b, g, j, s: (b, mrow, col0 + g))

    def prev_i(j):
        return jnp.maximum(qb * j - 1, 0)

    def next_i(j):
        return jnp.minimum(qb * j + qb, nblk - 1)

    attn = pl.pallas_call(
        functools.partial(_attn_body, qb=qb, ntile=ntile, n_meta=n_meta),
        grid_spec=pltpu.PrefetchScalarGridSpec(
            num_scalar_prefetch=1,
            grid=(B, N_KV_HEADS, ntile),
            in_specs=[
                pl.BlockSpec((1, qb * BLOCK, PROJ_CHUNK), lambda b, g, j, s: (b, j, g)),
                edge_spec(prev_i, kcol), main_spec(kcol), edge_spec(next_i, kcol), meta_spec(kcol),
                edge_spec(prev_i, vcol), main_spec(vcol), edge_spec(next_i, vcol), meta_spec(vcol),
            ],
            out_specs=pl.BlockSpec((1, qb * BLOCK, PROJ_CHUNK), lambda b, g, j, s: (b, j, g)),
            scratch_shapes=[
                pltpu.VMEM(((qb + 3) * BLOCK, HEAD_DIM), _BF16),
                pltpu.VMEM(((qb + 3) * BLOCK, 2 * HEAD_DIM), _BF16),
                pltpu.VMEM((2, Q_PER_KV * BLOCK, 4 * BLOCK), _F32),
                pltpu.VMEM((2, Q_PER_KV * BLOCK, 4 * BLOCK), _BF16),
                pltpu.VMEM((2, Q_PER_KV * BLOCK, HEAD_DIM), _F32),
            ],
        ),
        out_shape=jax.ShapeDtypeStruct((B, T, Q_DIM), _BF16),
        compiler_params=_cparams(("parallel", "parallel", "arbitrary"), SMALL_CALL_VMEM_BYTES),
        name="windowed_gqa",
    )(sink, qkv, qkv, qkv, qkv, qkv, qkv, qkv, qkv, qkv)

    def blk0_spec(col0):
        return pl.BlockSpec((1, BLOCK, HEAD_DIM), lambda b, g, s: (b, 0, col0 + g))

    def meta2_spec(col0):
        return pl.BlockSpec((1, n_meta, HEAD_DIM), lambda b, g, s: (b, mrow, col0 + g))

    return pl.pallas_call(
        functools.partial(_attn_meta_body, n_meta=n_meta),
        grid_spec=pltpu.PrefetchScalarGridSpec(
            num_scalar_prefetch=1,
            grid=(B, N_KV_HEADS),
            in_specs=[
                pl.BlockSpec((1, n_meta, PROJ_CHUNK), lambda b, g, s: (b, mrow, g)),
                blk0_spec(kcol), meta2_spec(kcol), blk0_spec(vcol), meta2_spec(vcol),
                pl.BlockSpec(memory_space=pl.ANY),
            ],
            out_specs=pl.BlockSpec((1, n_meta, PROJ_CHUNK), lambda b, g, s: (b, mrow, g)),
        ),
        out_shape=jax.ShapeDtypeStruct((B, T, Q_DIM), _BF16),
        input_output_aliases={6: 0},
        compiler_params=_cparams(("parallel", "arbitrary"), SMALL_CALL_VMEM_BYTES),
        name="windowed_gqa_meta",
    )(sink, qkv, qkv, qkv, qkv, qkv, attn)


def _fft_factors(T):
    best = None
    for n2 in range(BF16_SUBLANES, T + 1, BF16_SUBLANES):
        if T % n2:
            continue
        n1 = T // n2
        r1 = n1 * BF16_SUBLANES
        cost = (4 * _round_up(r1, MXU_DEPTH) * r1 * (n2 // BF16_SUBLANES)
                + 2 * _round_up(n2, MXU_DEPTH) * n2 * n1)
        if best is None or cost < best[0]:
            best = (cost, n1, n2)
    assert best is not None, "sequence length must be a multiple of 16"
    return best[1], best[2]


def _fft1_body(z_ref, mc_ref, ms_ref, twc_ref, tws_ref, a_ref, *, n1, R, C):
    rows = n1 * R
    rhs = jnp.concatenate([z_ref[0, 0].reshape(rows, C), z_ref[0, 1].reshape(rows, C)], axis=1)
    xc = jnp.dot(mc_ref[...], rhs, preferred_element_type=_F32)
    xs = jnp.dot(ms_ref[...], rhs, preferred_element_type=_F32)
    ar = xc[:, :C] + xs[:, C:]
    ai = xc[:, C:] - xs[:, :C]
    twc = jnp.concatenate([twc_ref[0]] * (C // LANES), axis=1)
    tws = jnp.concatenate([tws_ref[0]] * (C // LANES), axis=1)
    a_ref[0, 0] = (ar * twc + ai * tws).reshape(n1, R, C).astype(_BF16)
    a_ref[0, 1] = (ai * twc - ar * tws).reshape(n1, R, C).astype(_BF16)


def _fft2_body(ar_ref, ai_ref, mc_ref, ms_ref, f_ref):
    for b in range(ar_ref.shape[0]):
        res = (jnp.dot(mc_ref[0], ar_ref[b, 0, 0], preferred_element_type=_F32)
               + jnp.dot(ms_ref[0], ai_ref[b, 0, 0], preferred_element_type=_F32))
        f_ref[b] = res.astype(_BF16)


def _fft_tables(T, n1, n2, n_meta, R):
    two_pi = 2.0 * np.pi
    k1 = jnp.arange(n1, dtype=jnp.int32)
    ang1 = ((k1[:, None] * k1[None, :]) % n1).astype(_F32) * (two_pi / n1)
    eye = jnp.eye(R, dtype=_F32)
    m1c = jnp.kron(jnp.cos(ang1) * (n1 ** -0.5), eye).astype(_BF16)
    m1s = jnp.kron(jnp.sin(ang1) * (n1 ** -0.5), eye).astype(_BF16)
    nn2 = jnp.arange(n2, dtype=jnp.int32).reshape(n2 // R, 1, R)
    angt = ((k1[None, :, None] * (nn2 + n_meta)) % T).astype(_F32) * (two_pi / T)
    angt = jnp.broadcast_to(angt.reshape(n2 // R, n1 * R, 1), (n2 // R, n1 * R, LANES))
    twc, tws = jnp.cos(angt), jnp.sin(angt)
    k2 = jnp.arange(n2, dtype=jnp.int32)[:, None]
    mm2 = jnp.arange(n2, dtype=jnp.int32)[None, :]
    outs_c, outs_s = [], []
    for rot in (n_meta // n1, n_meta // n1 + 1):
        ph2 = (((k2 + rot) % n2) * (mm2 + n_meta)) % n2
        ang2 = ph2.astype(_F32) * (two_pi / n2)
        outs_c.append(jnp.cos(ang2) * (n2 ** -0.5))
        outs_s.append(jnp.sin(ang2) * (n2 ** -0.5))
    return m1c, m1s, twc, tws, jnp.stack(outs_c).astype(_BF16), jnp.stack(outs_s).astype(_BF16)


def _position_dft(z, tables, *, n1, n2, R, n_meta):
    B, _, T, C = z.shape
    m1c, m1s, twc, tws, m2c, m2s = tables
    rows = n1 * R
    zv = z.reshape(B, 2, n1, n2, C)
    av = pl.pallas_call(
        functools.partial(_fft1_body, n1=n1, R=R, C=C),
        grid=(B, n2 // R),
        in_specs=[
            pl.BlockSpec((1, 2, n1, R, C), lambda b, t: (b, 0, 0, t, 0)),
            _const_spec((rows, rows)),
            _const_spec((rows, rows)),
            pl.BlockSpec((1, rows, LANES), lambda b, t: (t, 0, 0)),
            pl.BlockSpec((1, rows, LANES), lambda b, t: (t, 0, 0)),
        ],
        out_specs=pl.BlockSpec((1, 2, n1, R, C), lambda b, t: (b, 0, 0, t, 0)),
        out_shape=jax.ShapeDtypeStruct((B, 2, n1, n2, C), _BF16),
        compiler_params=_cparams(("parallel", "arbitrary"),
                                 4 * 2 * rows * C * 2 + 2 * 2 * rows * rows * 2 + 6 * rows * C * 4 + VMEM_RESERVE_BYTES),
        name="dft_stage1",
    )(zv, m1c, m1s, twc, tws)
    shift = n1 - (n_meta % n1)

    f = pl.pallas_call(
        _fft2_body,
        grid=(n1,),
        in_specs=[
            pl.BlockSpec((B, 1, 1, n2, C), lambda k: (0, 0, k, 0, 0)),
            pl.BlockSpec((B, 1, 1, n2, C), lambda k: (0, 1, k, 0, 0)),
            pl.BlockSpec((1, n2, n2), lambda k: (jnp.where(k < n_meta % n1, 1, 0), 0, 0)),
            pl.BlockSpec((1, n2, n2), lambda k: (jnp.where(k < n_meta % n1, 1, 0), 0, 0)),
        ],
        out_specs=pl.BlockSpec((B, n2, C), lambda k: (0, 0, (k + shift) % n1)),
        out_shape=jax.ShapeDtypeStruct((B, n2, n1 * C), _BF16),
        compiler_params=_cparams(("arbitrary",), SMALL_CALL_VMEM_BYTES),
        name="dft_stage2",
    )(av, av, m2c, m2s)
    return f.reshape(B, T, C)


def _mix_body(*refs, tn, n_alias):
    h_ref, xn_ref, a_ref, f_ref, wpa_ref, wpf_ref, wg_ref, wo_ref = refs[:8]
    (o_ref,) = refs[8 + n_alias:]
    D = h_ref.shape[-1]
    o_ref[...] = h_ref[...]
    xn = xn_ref[...].reshape(-1, D)
    a = a_ref[...].reshape(-1, Q_DIM)
    f = f_ref[...].reshape(-1, FOURIER_DIM)
    for j in range(D // tn):
        cols = slice(j * tn, (j + 1) * tn)
        av = jnp.dot(a, wpa_ref[:, cols], preferred_element_type=_F32)
        fv = jnp.dot(f, wpf_ref[:, cols], preferred_element_type=_F32)
        ga = jax.nn.sigmoid(jnp.dot(xn, wg_ref[:, cols], preferred_element_type=_F32))
        gf = jax.nn.sigmoid(jnp.dot(xn, wg_ref[:, D + j * tn:D + (j + 1) * tn], preferred_element_type=_F32))
        mixed = (ga * av + gf * fv).astype(_BF16)
        o_ref[...] += jnp.dot(mixed, wo_ref[cols, :], preferred_element_type=_F32).reshape(o_ref.shape)


def _mix_out(rows, h, xn, attn, four, wpa, wpf, wg, wo, *, layer, tn, prev_out=None):
    B, T, D = h.shape

    def resident(k, n):
        return pl.BlockSpec((None, k, n), lambda *_: (layer, 0, 0), pipeline_mode=pl.Buffered(1))

    in_specs = [rows.spec(D), rows.spec(D), rows.spec(Q_DIM), rows.spec(FOURIER_DIM),
                resident(Q_DIM, D), resident(FOURIER_DIM, D), resident(D, 2 * D), resident(D, D)]
    args = [h, xn, attn, four, wpa, wpf, wg, wo]
    aliases = {}
    if prev_out is not None:
        extra, aliases = _alias_tail(len(args), [prev_out])
        in_specs += extra
        args.append(prev_out)
    m = rows.m
    vmem = ((Q_DIM + FOURIER_DIM + 3 * D) * D * 2 + 4 * m * D * 4 + 2 * m * D * 2
            + 2 * m * (Q_DIM + FOURIER_DIM) * 2 + 10 * m * tn * 4 + VMEM_RESERVE_BYTES)
    return pl.pallas_call(
        functools.partial(_mix_body, tn=tn, n_alias=len(aliases)),
        grid=rows.grid,
        in_specs=in_specs,
        out_specs=rows.spec(D),
        out_shape=jax.ShapeDtypeStruct((B, T, D), _F32),
        input_output_aliases=aliases,
        compiler_params=_cparams(("parallel", "arbitrary"), vmem),
        name="mixer_out" + ("_meta" if rows.meta else ""),
    )(*args)


def _rope_tables(n_real, n_meta):
    T = n_real + n_meta
    half = ROT_DIM // 2
    inv = 1.0 / (ROPE_THETA ** (jnp.arange(0, ROT_DIM, 2, dtype=_F32) / ROT_DIM))
    ang = jnp.arange(T, dtype=_F32)[:, None] * inv[None, :]
    cos, sin = jnp.cos(ang), jnp.sin(ang)
    cos = jnp.concatenate([cos[n_meta:], cos[:n_meta]], axis=0)
    sin = jnp.concatenate([sin[n_meta:], sin[:n_meta]], axis=0)
    ones = jnp.ones((T, HEAD_DIM - ROT_DIM), _F32)
    zeros_h = jnp.zeros((T, half), _F32)
    zeros_r = jnp.zeros((T, HEAD_DIM - ROT_DIM), _F32)
    c = jnp.concatenate([cos, cos, ones], axis=1)
    s1 = jnp.concatenate([-sin, zeros_h, zeros_r], axis=1)
    s2 = jnp.concatenate([zeros_h, sin, zeros_r], axis=1)
    return c, s1, s2


def _channel_dft_table():
    n = FOURIER_GROUP_DIM
    c = jnp.arange(n, dtype=jnp.int32)[:, None]
    k = jnp.arange(n, dtype=jnp.int32)[None, :]
    ang = ((c * k) % n).astype(_F32) * (2.0 * np.pi / n)
    return (jnp.concatenate([jnp.cos(ang), -jnp.sin(ang)], axis=1) * (n ** -0.5)).astype(_BF16)


def _pick_tile(n, pref):
    t = pref
    while n % t:
        t //= 2
    return t


def kernel(x, meta_tokens, ffn1_norm, ffn1_w_gate, ffn1_w_up, ffn1_w_down, mix_norm, w_in, w_gate, sink,
           w_attn_branch, w_fourier_branch, w_out, ffn2_norm, ffn2_w_gate, ffn2_w_up, ffn2_w_down, final_norm):
    B, S, D = x.shape
    n_meta = meta_tokens.shape[0]
    depth = ffn1_norm.shape[0]
    d_ff = ffn1_w_gate.shape[-1]
    T = S + n_meta
    assert S % (4 * BLOCK) == 0 and S % n_meta == 0 and BLOCK % n_meta == 0
    assert w_in.shape[-1] == IN_DIM and KV_DIM == PROJ_CHUNK

    tf = _pick_tile(d_ff, FF_CHUNK)
    tn = _pick_tile(D, MIX_CHUNK)
    qb = _pick_tile(S // BLOCK, ATTN_BLOCKS_PER_STEP)
    rows_ffn = _Rows(B, S, n_meta, _pick_tile(S, FFN_ROW_TILE), meta=False)
    rows_main = _Rows(B, S, n_meta, _pick_tile(S, PROJ_ROW_TILE), meta=False)
    rows_mix = _Rows(B, S, n_meta, _pick_tile(S, MIX_ROW_TILE), meta=False)
    rows_meta = _Rows(B, S, n_meta, 0, meta=True)

    n1, n2 = _fft_factors(T)
    fft_tables = _fft_tables(T, n1, n2, n_meta, BF16_SUBLANES)
    rope_c, rope_s1, rope_s2 = _rope_tables(S, n_meta)
    dft_c = _channel_dft_table()

    ffn1_w = tuple(_to_bf16(w) for w in (ffn1_w_gate, ffn1_w_up, ffn1_w_down))
    w_in_b = _to_bf16(w_in)
    mix_w = tuple(_to_bf16(w) for w in (w_attn_branch, w_fourier_branch, w_gate, w_out))

    def ffn(h, h_meta, gain, wts, layer, final_gain=None):
        out = _ffn(rows_ffn, h, gain, *wts, layer=layer, tf=tf, T=T, final_gain=final_gain)
        if final_gain is None:
            src, blk0 = h_meta
            out = _ffn(rows_meta, src, gain, *wts, layer=layer, tf=tf, T=T, h_row_block0=blk0, prev_out=out)
        return out

    meta = jnp.broadcast_to(meta_tokens[None].astype(x.dtype), (B, n_meta, D))
    h = None
    for l in range(depth):
        if l == 0:
            h = ffn(x, (meta, 0), ffn1_norm[l][None], ffn1_w, l)
        else:
            h = ffn(h, (h, None), ffn1_norm[l][None], ffn1_w, l)

        gain = mix_norm[l][None]
        proj_args = (gain, w_in_b, rope_c, rope_s1, rope_s2, dft_c)
        if l == 0:
            qkv, z, xn, *ffn2_w = _proj(rows_main, h, *proj_args, layer=l,
                                        side_cast=(ffn2_w_gate, ffn2_w_up, ffn2_w_down))
        else:
            qkv, z, xn = _proj(rows_main, h, *proj_args, layer=l)
        qkv, z, xn = _proj(rows_meta, h, *proj_args, layer=l, prev_out=(qkv, z, xn))
        attn = _attention(qkv, sink[l], n_real=S, n_meta=n_meta, qb=qb)
        four = _position_dft(z, fft_tables, n1=n1, n2=n2, R=BF16_SUBLANES, n_meta=n_meta)
        mix_args = (xn, attn, four) + mix_w
        h2 = _mix_out(rows_mix, h, *mix_args, layer=l, tn=tn)
        h = _mix_out(rows_meta, h, *mix_args, layer=l, tn=tn, prev_out=h2)

        last = l == depth - 1
        h = ffn(h, (h, None), ffn2_norm[l][None], ffn2_w, l, final_gain=final_norm[None] if last else None)
    return h
```

```python
import functools

import numpy as np
import jax
import jax.numpy as jnp
from jax import lax
from jax.experimental import pallas as pl
from jax.experimental.pallas import tpu as pltpu

_F32 = jnp.float32
_BF16 = jnp.bfloat16

HEAD_DIM = 128
N_Q_HEADS = 16
N_KV_HEADS = 4
Q_PER_KV = N_Q_HEADS // N_KV_HEADS
WINDOW = 128
BLOCK = 128
ROPE_THETA = 500000.0
ROT_DIM = HEAD_DIM // 4
N_FOURIER_GROUPS = 8
FOURIER_GROUP_DIM = 128
FOURIER_DIM = N_FOURIER_GROUPS * FOURIER_GROUP_DIM
Q_DIM = N_Q_HEADS * HEAD_DIM
KV_DIM = N_KV_HEADS * HEAD_DIM
IN_DIM = Q_DIM + 2 * KV_DIM + FOURIER_DIM
RMS_EPS = 1e-6
NEG_INF = -1e30

LANES = 128
BF16_SUBLANES = 16
PROJ_CHUNK = Q_PER_KV * HEAD_DIM
V7X_VMEM_BYTES = 64 * 1024 * 1024
VMEM_RESERVE_BYTES = 4 << 20
SMALL_CALL_VMEM_BYTES = 32 << 20
CAST_BLOCK_BYTES = 6 << 20
LOG2_E = 1.4426950408889634
MXU_DEPTH = 256

FFN_ROW_TILE = 1024
FF_CHUNK = 512
PROJ_ROW_TILE = 512
MIX_ROW_TILE = 256
MIX_CHUNK = 1024
ATTN_BLOCKS_PER_STEP = 32


def _round_up(a, m):
    return (a + m - 1) // m * m


def _cparams(sem, vmem_bytes):
    return pltpu.CompilerParams(dimension_semantics=sem,
                                vmem_limit_bytes=int(min(vmem_bytes, V7X_VMEM_BYTES - VMEM_RESERVE_BYTES)))


def _rms(x, g):
    return x * lax.rsqrt(jnp.mean(x * x, axis=-1, keepdims=True) + RMS_EPS) * g


class _Rows:
    def __init__(self, B, n_real, n_meta, tm, meta):
        self.B, self.n_real, self.n_meta, self.meta = B, n_real, n_meta, meta
        self.nb = B if meta else 1
        self.rows = n_meta if meta else tm
        self.grid = (1, 1) if meta else (B, n_real // tm)
        self.m = self.nb * self.rows

    def spec(self, width, lead=(), row_block0=None):
        nl = len(lead)
        if self.meta:
            r0 = self.n_real // self.n_meta if row_block0 is None else row_block0
            return pl.BlockSpec((self.nb,) + tuple(lead) + (self.rows, width),
                                lambda b, i, *_: (0,) + (0,) * nl + (r0, 0))
        return pl.BlockSpec((1,) + tuple(lead) + (self.rows, width),
                            lambda b, i, *_: (b,) + (0,) * nl + (i, 0))

    def table_spec(self, width):
        if self.meta:
            r0 = self.n_real // self.n_meta
            return pl.BlockSpec((self.rows, width), lambda b, i, *_: (r0, 0))
        return pl.BlockSpec((self.rows, width), lambda b, i, *_: (i, 0))


def _const_spec(shape, single_buffer=False):
    nd = len(shape)
    if single_buffer:
        return pl.BlockSpec(shape, lambda *_: (0,) * nd, pipeline_mode=pl.Buffered(1))
    return pl.BlockSpec(shape, lambda *_: (0,) * nd)


def _alias_tail(n_in, outs):
    specs = [pl.BlockSpec(memory_space=pl.ANY) for _ in outs]
    return specs, {n_in + k: k for k in range(len(outs))}


def _side_cast(stacks, n_steps, step_index):
    specs, operands, shapes, vmem = [], [], [], 0
    for w in stacks:
        L, K, N = w.shape
        rps = L * K // n_steps
        assert L * K % n_steps == 0 and rps % BF16_SUBLANES == 0
        specs.append(pl.BlockSpec((rps, N), lambda *idx: (step_index(*idx), 0)))
        operands.append(w.reshape(L * K, N))
        shapes.append(jax.ShapeDtypeStruct((L * K, N), _BF16))
        vmem += 2 * rps * N * (4 + 2)
    return specs, operands, shapes, vmem


def _cast_blocks(src_refs, dst_refs):
    for src_ref, dst_ref in zip(src_refs, dst_refs):
        dst_ref[...] = src_ref[...].astype(_BF16)


def _cast_body(w_ref, o_ref):
    o_ref[...] = w_ref[...].astype(_BF16)


def _to_bf16(w):
    L, K, N = w.shape
    tk = _pick_tile(K, 512)
    while tk > 8 and tk * N * 4 > CAST_BLOCK_BYTES:
        tk //= 2
    return pl.pallas_call(
        _cast_body,
        grid=(L, K // tk),
        in_specs=[pl.BlockSpec((1, tk, N), lambda l, i: (l, i, 0))],
        out_specs=pl.BlockSpec((1, tk, N), lambda l, i: (l, i, 0)),
        out_shape=jax.ShapeDtypeStruct(w.shape, _BF16),
        compiler_params=_cparams(("parallel", "parallel"), 2 * tk * N * (4 + 2) + VMEM_RESERVE_BYTES),
        name="cast_bf16",
    )(w)


def _ffn_body(*refs, nj, final, n_alias):
    h_ref, gain_ref, wg_ref, wu_ref, wd_ref = refs[:5]
    rest = refs[5:]
    fgain_ref = None
    if final:
        fgain_ref, rest = rest[0], rest[1:]
    o_ref, xn_ref = rest[n_alias:]
    j = pl.program_id(2)
    D = h_ref.shape[-1]

    @pl.when(j == 0)
    def _():
        x = h_ref[...].reshape(-1, D)
        xn_ref[...] = _rms(x, gain_ref[...]).astype(_BF16)
        o_ref[...] = x.reshape(o_ref.shape)

    g = jnp.dot(xn_ref[...], wg_ref[...], preferred_element_type=_F32)
    u = jnp.dot(xn_ref[...], wu_ref[...], preferred_element_type=_F32)
    act = ((g * jax.nn.sigmoid(g)) * u * 0.5).astype(_BF16)
    o_ref[...] += jnp.dot(act, wd_ref[...], preferred_element_type=_F32).reshape(o_ref.shape)

    if final:
        @pl.when(j == nj - 1)
        def _():
            o_ref[...] = _rms(o_ref[...].reshape(-1, D), fgain_ref[...]).reshape(o_ref.shape)


def _ffn(rows, h, gain, wg, wu, wd, *, layer, tf, T, final_gain=None, h_row_block0=None, prev_out=None):
    D = h.shape[-1]
    nj = wg.shape[2] // tf
    final = final_gain is not None
    in_specs = [rows.spec(D, row_block0=h_row_block0), _const_spec((1, D)),
                pl.BlockSpec((None, D, tf), lambda b, i, j: (layer, 0, j)),
                pl.BlockSpec((None, D, tf), lambda b, i, j: (layer, 0, j)),
                pl.BlockSpec((None, tf, D), lambda b, i, j: (layer, j, 0))]
    args = [h, gain, wg, wu, wd]
    if final:
        in_specs.append(_const_spec((1, D)))
        args.append(final_gain)
    aliases = {}
    if prev_out is not None:
        extra, aliases = _alias_tail(len(args), [prev_out])
        in_specs += extra
        args.append(prev_out)
    m = rows.m
    vmem = (4 * m * D * 4 + m * D * 2 + 2 * (D * 2 * tf + tf * D) * 2 + m * 2 * tf * 4 + m * tf * 2
            + VMEM_RESERVE_BYTES)
    out_rows = rows.n_real if final else T
    return pl.pallas_call(
        functools.partial(_ffn_body, nj=nj, final=final, n_alias=len(aliases)),
        grid=rows.grid + (nj,),
        in_specs=in_specs,
        out_specs=rows.spec(D),
        out_shape=jax.ShapeDtypeStruct((rows.B, out_rows, D), _F32),
        scratch_shapes=[pltpu.VMEM((m, D), _BF16)],
        input_output_aliases=aliases,
        compiler_params=_cparams(("parallel", "parallel", "arbitrary"), vmem),
        name=("ffn_final" if final else "ffn") + ("_meta" if rows.meta else ""),
    )(*args)


def _proj_body(*refs, n_rope, n_qkv, n_chunks, nb, n_alias, n_side):
    h_ref, gain_ref, w_ref, c_ref, s1_ref, s2_ref, dft_ref = refs[:7]
    side_in = refs[7:7 + n_side]
    qkv_ref, z_ref, xno_ref = refs[7 + n_side + n_alias:10 + n_side + n_alias]
    side_out = refs[10 + n_side + n_alias:]
    _cast_blocks(side_in, side_out)
    D = h_ref.shape[-1]
    rows = h_ref.shape[-2]
    xno_ref[...] = _rms(h_ref[...].reshape(-1, D), gain_ref[...]).astype(_BF16).reshape(xno_ref.shape)
    c = jnp.concatenate([c_ref[...]] * nb, axis=0)
    s1 = jnp.concatenate([s1_ref[...]] * nb, axis=0)
    s2 = jnp.concatenate([s2_ref[...]] * nb, axis=0)
    half = ROT_DIM // 2
    for ch in range(n_chunks):
        cols = slice(ch * PROJ_CHUNK, (ch + 1) * PROJ_CHUNK)
        y = jnp.dot(xno_ref[...].reshape(-1, D), w_ref[:, cols], preferred_element_type=_F32)
        if ch < n_rope:
            for hh in range(PROJ_CHUNK // HEAD_DIM):
                xh = y[:, hh * HEAD_DIM:(hh + 1) * HEAD_DIM]
                up = pltpu.roll(xh, HEAD_DIM - half, 1)
                dn = pltpu.roll(xh, half, 1)
                r = (xh * c + up * s1 + dn * s2).astype(_BF16)
                lo = ch * PROJ_CHUNK + hh * HEAD_DIM
                qkv_ref[:, :, lo:lo + HEAD_DIM] = r.reshape(nb, rows, HEAD_DIM)
        elif ch < n_qkv:
            qkv_ref[:, :, cols] = y.astype(_BF16).reshape(nb, rows, PROJ_CHUNK)
        else:
            ub = y.astype(_BF16)
            for gg in range(PROJ_CHUNK // FOURIER_GROUP_DIM):
                zz = jnp.dot(ub[:, gg * FOURIER_GROUP_DIM:(gg + 1) * FOURIER_GROUP_DIM], dft_ref[...],
                             preferred_element_type=_F32)
                lo = (ch - n_qkv) * PROJ_CHUNK + gg * FOURIER_GROUP_DIM
                z_ref[:, 0, :, lo:lo + FOURIER_GROUP_DIM] = (
                    zz[:, :FOURIER_GROUP_DIM].astype(_BF16).reshape(nb, rows, FOURIER_GROUP_DIM))
                z_ref[:, 1, :, lo:lo + FOURIER_GROUP_DIM] = (
                    zz[:, FOURIER_GROUP_DIM:].astype(_BF16).reshape(nb, rows, FOURIER_GROUP_DIM))


def _proj(rows, h, gain, w_in, rope_c, rope_s1, rope_s2, dft_c, *, layer, prev_out=None, side_cast=()):
    B, T, D = h.shape
    n_rope = (Q_DIM + KV_DIM) // PROJ_CHUNK
    n_qkv = (Q_DIM + 2 * KV_DIM) // PROJ_CHUNK
    n_chunks = IN_DIM // PROJ_CHUNK
    w_spec = pl.BlockSpec((None, D, IN_DIM), lambda *_: (layer, 0, 0), pipeline_mode=pl.Buffered(1))
    in_specs = [rows.spec(D), _const_spec((1, D)), w_spec,
                rows.table_spec(LANES), rows.table_spec(LANES), rows.table_spec(LANES),
                _const_spec((FOURIER_GROUP_DIM, 2 * FOURIER_GROUP_DIM))]
    args = [h, gain, w_in, rope_c, rope_s1, rope_s2, dft_c]
    out_specs = [rows.spec(Q_DIM + 2 * KV_DIM), rows.spec(FOURIER_DIM, lead=(2,)), rows.spec(D)]
    out_shape = [jax.ShapeDtypeStruct((B, T, Q_DIM + 2 * KV_DIM), _BF16),
                 jax.ShapeDtypeStruct((B, 2, T, FOURIER_DIM), _BF16),
                 jax.ShapeDtypeStruct((B, T, D), _BF16)]
    m = rows.m
    vmem = (2 * m * D * 4 + 2 * m * D * 2 + D * IN_DIM * 2 + 6 * m * LANES * 4
            + 2 * m * (Q_DIM + 2 * KV_DIM) * 2 + 4 * m * FOURIER_DIM * 2 + 8 * m * PROJ_CHUNK * 4
            + VMEM_RESERVE_BYTES)
    side_specs, side_ops, side_shapes, side_vmem = _side_cast(
        side_cast, rows.grid[0] * rows.grid[1], lambda b, i: b * rows.grid[1] + i)
    in_specs += side_specs
    args += side_ops
    out_specs += side_specs
    out_shape += side_shapes
    vmem += side_vmem
    aliases = {}
    if prev_out is not None:
        extra, aliases = _alias_tail(len(args), prev_out)
        in_specs += extra
        args += list(prev_out)
    outs = pl.pallas_call(
        functools.partial(_proj_body, n_rope=n_rope, n_qkv=n_qkv, n_chunks=n_chunks, nb=rows.nb,
                          n_alias=len(aliases), n_side=len(side_cast)),
        grid=rows.grid,
        in_specs=in_specs,
        out_specs=out_specs,
        out_shape=out_shape,
        input_output_aliases=aliases,
        compiler_params=_cparams(("parallel", "arbitrary"), vmem),
        name="mixer_proj" + ("_meta" if rows.meta else ""),
    )(*args)
    return tuple(outs[:3]) + tuple(o.reshape(w.shape) for o, w in zip(outs[3:], side_cast))


def _stack_heads(q):
    return jnp.concatenate([q[:, h * HEAD_DIM:(h + 1) * HEAD_DIM] for h in range(Q_PER_KV)], axis=0)


def _sink_column(sink_ref, g, rows_per_head):
    return jnp.concatenate(
        [jnp.full((rows_per_head, 1), sink_ref[g * Q_PER_KV + h], _F32) for h in range(Q_PER_KV)], axis=0)


def _pad_rows(a, rows):
    return jnp.concatenate([a, jnp.zeros((rows - a.shape[0], a.shape[1]), a.dtype)], axis=0)


def _softmax_pv(q4, k_all, v_all, masks, sink):
    s = lax.dot_general(q4, k_all, (((1,), (1,)), ((), ())), preferred_element_type=_F32) * (HEAD_DIM ** -0.5)
    blocks = []
    for i, mk in enumerate(masks):
        sb = s[:, i * BLOCK:(i + 1) * BLOCK]
        blocks.append(sb if mk is None else jnp.where(mk, sb, NEG_INF))
    mx = jnp.maximum(jnp.max(functools.reduce(jnp.maximum, blocks), axis=-1, keepdims=True), sink)
    p = jnp.concatenate([jnp.exp(sb - mx).astype(_BF16) for sb in blocks], axis=1)
    v_ext = jnp.concatenate([v_all, jnp.ones(v_all.shape, _BF16)], axis=1)
    o = jnp.dot(p, v_ext, preferred_element_type=_F32)
    den = o[:, HEAD_DIM:] + jnp.exp(sink - mx)
    return o[:, :HEAD_DIM] / den


def _attn_body(sink_ref, *refs, qb, ntile, n_meta, n_side):
    q_ref, kp_ref, kc_ref, kn_ref, km_ref, vp_ref, vc_ref, vn_ref, vm_ref = refs[:9]
    o_ref = refs[9 + n_side]
    kbuf, vbuf, s_buf, p_buf, es_buf = refs[10 + 2 * n_side:]
    _cast_blocks(refs[9:9 + n_side], refs[10 + n_side:10 + 2 * n_side])
    g = pl.program_id(1)
    jt = pl.program_id(2)
    rows = Q_PER_KV * BLOCK
    band = 3 * BLOCK
    meta0 = (qb + 2) * BLOCK
    scale = HEAD_DIM ** -0.5
    c_exp2 = scale * LOG2_E

    kbuf[0:BLOCK] = kp_ref[0]
    kbuf[BLOCK:(qb + 1) * BLOCK] = kc_ref[0]
    kbuf[(qb + 1) * BLOCK:meta0] = kn_ref[0]
    kbuf[meta0:meta0 + BLOCK] = _pad_rows(km_ref[0], BLOCK)
    vbuf[0:BLOCK, 0:HEAD_DIM] = vp_ref[0]
    vbuf[BLOCK:(qb + 1) * BLOCK, 0:HEAD_DIM] = vc_ref[0]
    vbuf[(qb + 1) * BLOCK:meta0, 0:HEAD_DIM] = vn_ref[0]
    vbuf[meta0:meta0 + BLOCK, 0:HEAD_DIM] = _pad_rows(vm_ref[0], BLOCK)
    vbuf[:, HEAD_DIM:] = jnp.ones((meta0 + BLOCK, HEAD_DIM), _BF16)

    row = lax.broadcasted_iota(jnp.int32, (rows, BLOCK), 0) % BLOCK
    col = lax.broadcasted_iota(jnp.int32, (rows, BLOCK), 1)
    diff = col - row
    meta_ok = col < n_meta
    never = 2 * BLOCK
    sink_s = _sink_column(sink_ref, g, BLOCK) * (1.0 / scale)

    def dot_t(a, b):
        return lax.dot_general(a, b, (((1,), (1,)), ((), ())), preferred_element_type=_F32)

    def stage_a(i, slot):
        r0 = pl.multiple_of(i * BLOCK, BLOCK)
        q4 = _stack_heads(q_ref[0, pl.ds(r0, BLOCK), :])
        s_buf[slot, :, 0:band] = dot_t(q4, kbuf[pl.ds(r0, band), :])
        s_buf[slot, :, band:] = dot_t(q4, kbuf[meta0:meta0 + BLOCK, :])

    def stage_b(i, slot):
        first = jnp.logical_and(jt == 0, i == 0)
        last = jnp.logical_and(jt == ntile - 1, i == qb - 1)
        off_p = jnp.where(first, never, 0)
        off_n = jnp.where(last, -never, 0)
        blocks = [jnp.where(diff >= off_p, s_buf[slot, :, 0:BLOCK], NEG_INF),
                  s_buf[slot, :, BLOCK:2 * BLOCK],
                  jnp.where(diff <= off_n, s_buf[slot, :, 2 * BLOCK:band], NEG_INF),
                  jnp.where(meta_ok, s_buf[slot, :, band:], NEG_INF)]
        mx = jnp.maximum(jnp.max(functools.reduce(jnp.maximum, blocks), axis=-1, keepdims=True), sink_s)
        for n, sb in enumerate(blocks):
            p_buf[slot, :, n * BLOCK:(n + 1) * BLOCK] = jnp.exp2((sb - mx) * c_exp2).astype(_BF16)
        es_buf[slot] = jnp.broadcast_to(jnp.exp2((sink_s - mx) * c_exp2), (rows, HEAD_DIM))

    def stage_c(i, slot):
        r0 = pl.multiple_of(i * BLOCK, BLOCK)
        o = (jnp.dot(p_buf[slot, :, 0:band], vbuf[pl.ds(r0, band), :], preferred_element_type=_F32)
             + jnp.dot(p_buf[slot, :, band:], vbuf[meta0:meta0 + BLOCK, :], preferred_element_type=_F32))
        out = o[:, :HEAD_DIM] / (o[:, HEAD_DIM:] + es_buf[slot])
        for h in range(Q_PER_KV):
            o_ref[0, pl.ds(r0, BLOCK), h * HEAD_DIM:(h + 1) * HEAD_DIM] = (
                out[h * BLOCK:(h + 1) * BLOCK].astype(_BF16))

    stage_a(0, 0)
    stage_a(1, 1)
    stage_b(0, 0)

    def pair(t, carry):
        i = 2 * t
        stage_c(i, 0)
        stage_b(i + 1, 1)
        stage_a(i + 2, 0)
        stage_c(i + 1, 1)
        stage_b(i + 2, 0)
        stage_a(i + 3, 1)
        return carry

    lax.fori_loop(0, (qb - 2) // 2, pair, 0)
    stage_c(qb - 2, 0)
    stage_b(qb - 1, 1)
    stage_c(qb - 1, 1)


def _attn_meta_body(sink_ref, q_ref, k0_ref, km_ref, v0_ref, vm_ref, prev_out_ref, o_ref, *, n_meta):
    del prev_out_ref
    g = pl.program_id(1)
    rows = Q_PER_KV * n_meta
    row = lax.broadcasted_iota(jnp.int32, (rows, BLOCK), 0) % n_meta
    col = lax.broadcasted_iota(jnp.int32, (rows, BLOCK), 1)
    q4 = _stack_heads(q_ref[0])
    masks = [col <= row + (WINDOW - n_meta), col < n_meta]
    k_all = jnp.concatenate([k0_ref[0], _pad_rows(km_ref[0], BLOCK)], axis=0)
    v_all = jnp.concatenate([v0_ref[0], _pad_rows(vm_ref[0], BLOCK)], axis=0)
    out = _softmax_pv(q4, k_all, v_all, masks, _sink_column(sink_ref, g, n_meta))
    for h in range(Q_PER_KV):
        o_ref[0, :, h * HEAD_DIM:(h + 1) * HEAD_DIM] = out[h * n_meta:(h + 1) * n_meta].astype(_BF16)


def _attention(qkv, sink, *, n_real, n_meta, qb, side_cast=()):
    B, T, _ = qkv.shape
    nblk = n_real // BLOCK
    ntile = nblk // qb
    kcol = Q_DIM // HEAD_DIM
    vcol = (Q_DIM + KV_DIM) // HEAD_DIM
    mrow = n_real // n_meta

    def edge_spec(rowf, col0):
        return pl.BlockSpec((1, BLOCK, HEAD_DIM), lambda b, g, j, s: (b, rowf(j), col0 + g))

    def main_spec(col0):
        return pl.BlockSpec((1, qb * BLOCK, HEAD_DIM), lambda b, g, j, s: (b, j, col0 + g))

    def meta_spec(col0):
        return pl.BlockSpec((1, n_meta, HEAD_DIM), lambda b, g, j, s: (b, mrow, col0 + g))

    def prev_i(j):
        return jnp.maximum(qb * j - 1, 0)

    def next_i(j):
        return jnp.minimum(qb * j + qb, nblk - 1)

    side_specs, side_ops, side_shapes, side_vmem = _side_cast(
        side_cast, B * N_KV_HEADS * ntile, lambda b, g, j, s: (b * N_KV_HEADS + g) * ntile + j)
    attn, *cast = pl.pallas_call(
        functools.partial(_attn_body, qb=qb, ntile=ntile, n_meta=n_meta, n_side=len(side_cast)),
        grid_spec=pltpu.PrefetchScalarGridSpec(
            num_scalar_prefetch=1,
            grid=(B, N_KV_HEADS, ntile),
            in_specs=[
                pl.BlockSpec((1, qb * BLOCK, PROJ_CHUNK), lambda b, g, j, s: (b, j, g)),
                edge_spec(prev_i, kcol), main_spec(kcol), edge_spec(next_i, kcol), meta_spec(kcol),
                edge_spec(prev_i, vcol), main_spec(vcol), edge_spec(next_i, vcol), meta_spec(vcol),
            ] + side_specs,
            out_specs=[pl.BlockSpec((1, qb * BLOCK, PROJ_CHUNK), lambda b, g, j, s: (b, j, g))] + side_specs,
            scratch_shapes=[
                pltpu.VMEM(((qb + 3) * BLOCK, HEAD_DIM), _BF16),
                pltpu.VMEM(((qb + 3) * BLOCK, 2 * HEAD_DIM), _BF16),
                pltpu.VMEM((2, Q_PER_KV * BLOCK, 4 * BLOCK), _F32),
                pltpu.VMEM((2, Q_PER_KV * BLOCK, 4 * BLOCK), _BF16),
                pltpu.VMEM((2, Q_PER_KV * BLOCK, HEAD_DIM), _F32),
            ],
        ),
        out_shape=[jax.ShapeDtypeStruct((B, T, Q_DIM), _BF16)] + side_shapes,
        compiler_params=_cparams(("parallel", "parallel", "arbitrary"), SMALL_CALL_VMEM_BYTES + side_vmem),
        name="windowed_gqa",
    )(sink, qkv, qkv, qkv, qkv, qkv, qkv, qkv, qkv, qkv, *side_ops)
    cast = [c.reshape(w.shape) for c, w in zip(cast, side_cast)]

    def blk0_spec(col0):
        return pl.BlockSpec((1, BLOCK, HEAD_DIM), lambda b, g, s: (b, 0, col0 + g))

    def meta2_spec(col0):
        return pl.BlockSpec((1, n_meta, HEAD_DIM), lambda b, g, s: (b, mrow, col0 + g))

    return pl.pallas_call(
        functools.partial(_attn_meta_body, n_meta=n_meta),
        grid_spec=pltpu.PrefetchScalarGridSpec(
            num_scalar_prefetch=1,
            grid=(B, N_KV_HEADS),
            in_specs=[
                pl.BlockSpec((1, n_meta, PROJ_CHUNK), lambda b, g, s: (b, mrow, g)),
                blk0_spec(kcol), meta2_spec(kcol), blk0_spec(vcol), meta2_spec(vcol),
                pl.BlockSpec(memory_space=pl.ANY),
            ],
            out_specs=pl.BlockSpec((1, n_meta, PROJ_CHUNK), lambda b, g, s: (b, mrow, g)),
        ),
        out_shape=jax.ShapeDtypeStruct((B, T, Q_DIM), _BF16),
        input_output_aliases={6: 0},
        compiler_params=_cparams(("parallel", "arbitrary"), SMALL_CALL_VMEM_BYTES),
        name="windowed_gqa_meta",
    )(sink, qkv, qkv, qkv, qkv, qkv, attn), cast


def _fft_factors(T):
    best = None
    for n2 in range(BF16_SUBLANES, T + 1, BF16_SUBLANES):
        if T % n2:
            continue
        n1 = T // n2
        r1 = n1 * BF16_SUBLANES
        cost = (4 * _round_up(r1, MXU_DEPTH) * r1 * (n2 // BF16_SUBLANES)
                + 2 * _round_up(n2, MXU_DEPTH) * n2 * n1)
        if best is None or cost < best[0]:
            best = (cost, n1, n2)
    assert best is not None, "sequence length must be a multiple of 16"
    return best[1], best[2]


def _fft1_body(z_ref, mc_ref, ms_ref, twc_ref, tws_ref, a_ref, *, n1, R, C):
    rows = n1 * R
    rhs = jnp.concatenate([z_ref[0, 0].reshape(rows, C), z_ref[0, 1].reshape(rows, C)], axis=1)
    xc = jnp.dot(mc_ref[...], rhs, preferred_element_type=_F32)
    xs = jnp.dot(ms_ref[...], rhs, preferred_element_type=_F32)
    ar = xc[:, :C] + xs[:, C:]
    ai = xc[:, C:] - xs[:, :C]
    twc = jnp.concatenate([twc_ref[0]] * (C // LANES), axis=1)
    tws = jnp.concatenate([tws_ref[0]] * (C // LANES), axis=1)
    a_ref[0, 0] = (ar * twc + ai * tws).reshape(n1, R, C).astype(_BF16)
    a_ref[0, 1] = (ai * twc - ar * tws).reshape(n1, R, C).astype(_BF16)


def _fft2_body(ar_ref, ai_ref, mc_ref, ms_ref, f_ref):
    for b in range(ar_ref.shape[0]):
        res = (jnp.dot(mc_ref[0], ar_ref[b, 0, 0], preferred_element_type=_F32)
               + jnp.dot(ms_ref[0], ai_ref[b, 0, 0], preferred_element_type=_F32))
        f_ref[b] = res.astype(_BF16)


def _fft_tables(T, n1, n2, n_meta, R):
    two_pi = 2.0 * np.pi
    k1 = jnp.arange(n1, dtype=jnp.int32)
    ang1 = ((k1[:, None] * k1[None, :]) % n1).astype(_F32) * (two_pi / n1)
    eye = jnp.eye(R, dtype=_F32)
    m1c = jnp.kron(jnp.cos(ang1) * (n1 ** -0.5), eye).astype(_BF16)
    m1s = jnp.kron(jnp.sin(ang1) * (n1 ** -0.5), eye).astype(_BF16)
    nn2 = jnp.arange(n2, dtype=jnp.int32).reshape(n2 // R, 1, R)
    angt = ((k1[None, :, None] * (nn2 + n_meta)) % T).astype(_F32) * (two_pi / T)
    angt = jnp.broadcast_to(angt.reshape(n2 // R, n1 * R, 1), (n2 // R, n1 * R, LANES))
    twc, tws = jnp.cos(angt), jnp.sin(angt)
    k2 = jnp.arange(n2, dtype=jnp.int32)[:, None]
    mm2 = jnp.arange(n2, dtype=jnp.int32)[None, :]
    outs_c, outs_s = [], []
    for rot in (n_meta // n1, n_meta // n1 + 1):
        ph2 = (((k2 + rot) % n2) * (mm2 + n_meta)) % n2
        ang2 = ph2.astype(_F32) * (two_pi / n2)
        outs_c.append(jnp.cos(ang2) * (n2 ** -0.5))
        outs_s.append(jnp.sin(ang2) * (n2 ** -0.5))
    return m1c, m1s, twc, tws, jnp.stack(outs_c).astype(_BF16), jnp.stack(outs_s).astype(_BF16)


def _position_dft(z, tables, *, n1, n2, R, n_meta):
    B, _, T, C = z.shape
    m1c, m1s, twc, tws, m2c, m2s = tables
    rows = n1 * R
    zv = z.reshape(B, 2, n1, n2, C)
    av = pl.pallas_call(
        functools.partial(_fft1_body, n1=n1, R=R, C=C),
        grid=(B, n2 // R),
        in_specs=[
            pl.BlockSpec((1, 2, n1, R, C), lambda b, t: (b, 0, 0, t, 0)),
            _const_spec((rows, rows)),
            _const_spec((rows, rows)),
            pl.BlockSpec((1, rows, LANES), lambda b, t: (t, 0, 0)),
            pl.BlockSpec((1, rows, LANES), lambda b, t: (t, 0, 0)),
        ],
        out_specs=pl.BlockSpec((1, 2, n1, R, C), lambda b, t: (b, 0, 0, t, 0)),
        out_shape=jax.ShapeDtypeStruct((B, 2, n1, n2, C), _BF16),
        compiler_params=_cparams(("parallel", "arbitrary"),
                                 4 * 2 * rows * C * 2 + 2 * 2 * rows * rows * 2 + 6 * rows * C * 4 + VMEM_RESERVE_BYTES),
        name="dft_stage1",
    )(zv, m1c, m1s, twc, tws)
    shift = n1 - (n_meta % n1)

    f = pl.pallas_call(
        _fft2_body,
        grid=(n1,),
        in_specs=[
            pl.BlockSpec((B, 1, 1, n2, C), lambda k: (0, 0, k, 0, 0)),
            pl.BlockSpec((B, 1, 1, n2, C), lambda k: (0, 1, k, 0, 0)),
            pl.BlockSpec((1, n2, n2), lambda k: (jnp.where(k < n_meta % n1, 1, 0), 0, 0)),
            pl.BlockSpec((1, n2, n2), lambda k: (jnp.where(k < n_meta % n1, 1, 0), 0, 0)),
        ],
        out_specs=pl.BlockSpec((B, n2, C), lambda k: (0, 0, (k + shift) % n1)),
        out_shape=jax.ShapeDtypeStruct((B, n2, n1 * C), _BF16),
        compiler_params=_cparams(("arbitrary",), SMALL_CALL_VMEM_BYTES),
        name="dft_stage2",
    )(av, av, m2c, m2s)
    return f.reshape(B, T, C)


def _mix_body(*refs, tn, n_alias):
    h_ref, xn_ref, a_ref, f_ref, wpa_ref, wpf_ref, wg_ref, wo_ref = refs[:8]
    (o_ref,) = refs[8 + n_alias:]
    D = h_ref.shape[-1]
    o_ref[...] = h_ref[...]
    xn = xn_ref[...].reshape(-1, D)
    a = a_ref[...].reshape(-1, Q_DIM)
    f = f_ref[...].reshape(-1, FOURIER_DIM)
    for j in range(D // tn):
        cols = slice(j * tn, (j + 1) * tn)
        av = jnp.dot(a, wpa_ref[:, cols], preferred_element_type=_F32)
        fv = jnp.dot(f, wpf_ref[:, cols], preferred_element_type=_F32)
        ga = jax.nn.sigmoid(jnp.dot(xn, wg_ref[:, cols], preferred_element_type=_F32))
        gf = jax.nn.sigmoid(jnp.dot(xn, wg_ref[:, D + j * tn:D + (j + 1) * tn], preferred_element_type=_F32))
        mixed = (ga * av + gf * fv).astype(_BF16)
        o_ref[...] += jnp.dot(mixed, wo_ref[cols, :], preferred_element_type=_F32).reshape(o_ref.shape)


def _mix_out(rows, h, xn, attn, four, wpa, wpf, wg, wo, *, layer, tn, prev_out=None):
    B, T, D = h.shape

    def resident(k, n):
        return pl.BlockSpec((None, k, n), lambda *_: (layer, 0, 0), pipeline_mode=pl.Buffered(1))

    in_specs = [rows.spec(D), rows.spec(D), rows.spec(Q_DIM), rows.spec(FOURIER_DIM),
                resident(Q_DIM, D), resident(FOURIER_DIM, D), resident(D, 2 * D), resident(D, D)]
    args = [h, xn, attn, four, wpa, wpf, wg, wo]
    aliases = {}
    if prev_out is not None:
        extra, aliases = _alias_tail(len(args), [prev_out])
        in_specs += extra
        args.append(prev_out)
    m = rows.m
    vmem = ((Q_DIM + FOURIER_DIM + 3 * D) * D * 2 + 4 * m * D * 4 + 2 * m * D * 2
            + 2 * m * (Q_DIM + FOURIER_DIM) * 2 + 10 * m * tn * 4 + VMEM_RESERVE_BYTES)
    return pl.pallas_call(
        functools.partial(_mix_body, tn=tn, n_alias=len(aliases)),
        grid=rows.grid,
        in_specs=in_specs,
        out_specs=rows.spec(D),
        out_shape=jax.ShapeDtypeStruct((B, T, D), _F32),
        input_output_aliases=aliases,
        compiler_params=_cparams(("parallel", "arbitrary"), vmem),
        name="mixer_out" + ("_meta" if rows.meta else ""),
    )(*args)


def _rope_tables(n_real, n_meta):
    T = n_real + n_meta
    half = ROT_DIM // 2
    inv = 1.0 / (ROPE_THETA ** (jnp.arange(0, ROT_DIM, 2, dtype=_F32) / ROT_DIM))
    ang = jnp.arange(T, dtype=_F32)[:, None] * inv[None, :]
    cos, sin = jnp.cos(ang), jnp.sin(ang)
    cos = jnp.concatenate([cos[n_meta:], cos[:n_meta]], axis=0)
    sin = jnp.concatenate([sin[n_meta:], sin[:n_meta]], axis=0)
    ones = jnp.ones((T, HEAD_DIM - ROT_DIM), _F32)
    zeros_h = jnp.zeros((T, half), _F32)
    zeros_r = jnp.zeros((T, HEAD_DIM - ROT_DIM), _F32)
    c = jnp.concatenate([cos, cos, ones], axis=1)
    s1 = jnp.concatenate([-sin, zeros_h, zeros_r], axis=1)
    s2 = jnp.concatenate([zeros_h, sin, zeros_r], axis=1)
    return c, s1, s2


def _channel_dft_table():
    n = FOURIER_GROUP_DIM
    c = jnp.arange(n, dtype=jnp.int32)[:, None]
    k = jnp.arange(n, dtype=jnp.int32)[None, :]
    ang = ((c * k) % n).astype(_F32) * (2.0 * np.pi / n)
    return (jnp.concatenate([jnp.cos(ang), -jnp.sin(ang)], axis=1) * (n ** -0.5)).astype(_BF16)


def _pick_tile(n, pref):
    t = pref
    while n % t:
        t //= 2
    return t


def kernel(x, meta_tokens, ffn1_norm, ffn1_w_gate, ffn1_w_up, ffn1_w_down, mix_norm, w_in, w_gate, sink,
           w_attn_branch, w_fourier_branch, w_out, ffn2_norm, ffn2_w_gate, ffn2_w_up, ffn2_w_down, final_norm):
    B, S, D = x.shape
    n_meta = meta_tokens.shape[0]
    depth = ffn1_norm.shape[0]
    d_ff = ffn1_w_gate.shape[-1]
    T = S + n_meta
    assert S % (4 * BLOCK) == 0 and S % n_meta == 0 and BLOCK % n_meta == 0
    assert w_in.shape[-1] == IN_DIM and KV_DIM == PROJ_CHUNK

    tf = _pick_tile(d_ff, FF_CHUNK)
    tn = _pick_tile(D, MIX_CHUNK)
    qb = _pick_tile(S // BLOCK, ATTN_BLOCKS_PER_STEP)
    rows_ffn = _Rows(B, S, n_meta, _pick_tile(S, FFN_ROW_TILE), meta=False)
    rows_main = _Rows(B, S, n_meta, _pick_tile(S, PROJ_ROW_TILE), meta=False)
    rows_mix = _Rows(B, S, n_meta, _pick_tile(S, MIX_ROW_TILE), meta=False)
    rows_meta = _Rows(B, S, n_meta, 0, meta=True)

    n1, n2 = _fft_factors(T)
    fft_tables = _fft_tables(T, n1, n2, n_meta, BF16_SUBLANES)
    rope_c, rope_s1, rope_s2 = _rope_tables(S, n_meta)
    dft_c = _channel_dft_table()

    ffn1_w = tuple(_to_bf16(w) for w in (ffn1_w_gate, ffn1_w_up, ffn1_w_down))
    w_in_b = _to_bf16(w_in)

    def ffn(h, h_meta, gain, wts, layer, final_gain=None):
        out = _ffn(rows_ffn, h, gain, *wts, layer=layer, tf=tf, T=T, final_gain=final_gain)
        if final_gain is None:
            src, blk0 = h_meta
            out = _ffn(rows_meta, src, gain, *wts, layer=layer, tf=tf, T=T, h_row_block0=blk0, prev_out=out)
        return out

    meta = jnp.broadcast_to(meta_tokens[None].astype(x.dtype), (B, n_meta, D))
    h = None
    for l in range(depth):
        if l == 0:
            h = ffn(x, (meta, 0), ffn1_norm[l][None], ffn1_w, l)
        else:
            h = ffn(h, (h, None), ffn1_norm[l][None], ffn1_w, l)

        gain = mix_norm[l][None]
        proj_args = (gain, w_in_b, rope_c, rope_s1, rope_s2, dft_c)
        if l == 0:
            qkv, z, xn, *ffn2_w = _proj(rows_main, h, *proj_args, layer=l,
                                        side_cast=(ffn2_w_gate, ffn2_w_up, ffn2_w_down))
        else:
            qkv, z, xn = _proj(rows_main, h, *proj_args, layer=l)
        qkv, z, xn = _proj(rows_meta, h, *proj_args, layer=l, prev_out=(qkv, z, xn))
        if l == 0:
            attn, mix_w = _attention(qkv, sink[l], n_real=S, n_meta=n_meta, qb=qb,
                                     side_cast=(w_attn_branch, w_fourier_branch, w_gate, w_out))
        else:
            attn, _ = _attention(qkv, sink[l], n_real=S, n_meta=n_meta, qb=qb)
        four = _position_dft(z, fft_tables, n1=n1, n2=n2, R=BF16_SUBLANES, n_meta=n_meta)
        mix_args = (xn, attn, four) + tuple(mix_w)
        h2 = _mix_out(rows_mix, h, *mix_args, layer=l, tn=tn)
        h = _mix_out(rows_meta, h, *mix_args, layer=l, tn=tn, prev_out=h2)

        last = l == depth - 1
        h = ffn(h, (h, None), ffn2_norm[l][None], ffn2_w, l, final_gain=final_norm[None] if last else None)
    return h
```

```python
import functools

import numpy as np
import jax
import jax.numpy as jnp
from jax import lax
from jax.experimental import pallas as pl
from jax.experimental.pallas import tpu as pltpu

_F32 = jnp.float32
_BF16 = jnp.bfloat16

HEAD_DIM = 128
N_Q_HEADS = 16
N_KV_HEADS = 4
Q_PER_KV = N_Q_HEADS // N_KV_HEADS
WINDOW = 128
BLOCK = 128
ROPE_THETA = 500000.0
ROT_DIM = HEAD_DIM // 4
N_FOURIER_GROUPS = 8
FOURIER_GROUP_DIM = 128
FOURIER_DIM = N_FOURIER_GROUPS * FOURIER_GROUP_DIM
Q_DIM = N_Q_HEADS * HEAD_DIM
KV_DIM = N_KV_HEADS * HEAD_DIM
IN_DIM = Q_DIM + 2 * KV_DIM + FOURIER_DIM
RMS_EPS = 1e-6
NEG_INF = -1e30

LANES = 128
BF16_SUBLANES = 16
PROJ_CHUNK = Q_PER_KV * HEAD_DIM
V7X_VMEM_BYTES = 64 * 1024 * 1024
VMEM_RESERVE_BYTES = 4 << 20
SMALL_CALL_VMEM_BYTES = 32 << 20
CAST_BLOCK_BYTES = 6 << 20
LOG2_E = 1.4426950408889634
MXU_DEPTH = 256

FFN_ROW_TILE = 1024
FF_CHUNK = 512
PROJ_ROW_TILE = 512
MIX_ROW_TILE = 256
MIX_CHUNK = 1024
ATTN_BLOCKS_PER_STEP = 32


def _round_up(a, m):
    return (a + m - 1) // m * m


def _cparams(sem, vmem_bytes):
    return pltpu.CompilerParams(dimension_semantics=sem,
                                vmem_limit_bytes=int(min(vmem_bytes, V7X_VMEM_BYTES - VMEM_RESERVE_BYTES)))


def _rms(x, g):
    return x * lax.rsqrt(jnp.mean(x * x, axis=-1, keepdims=True) + RMS_EPS) * g


class _Rows:
    def __init__(self, B, n_real, n_meta, tm, meta):
        self.B, self.n_real, self.n_meta, self.meta = B, n_real, n_meta, meta
        self.nb = B if meta else 1
        self.rows = n_meta if meta else tm
        self.grid = (1, 1) if meta else (B, n_real // tm)
        self.m = self.nb * self.rows

    def spec(self, width, lead=(), row_block0=None):
        nl = len(lead)
        if self.meta:
            r0 = self.n_real // self.n_meta if row_block0 is None else row_block0
            return pl.BlockSpec((self.nb,) + tuple(lead) + (self.rows, width),
                                lambda b, i, *_: (0,) + (0,) * nl + (r0, 0))
        return pl.BlockSpec((1,) + tuple(lead) + (self.rows, width),
                            lambda b, i, *_: (b,) + (0,) * nl + (i, 0))

    def table_spec(self, width):
        if self.meta:
            r0 = self.n_real // self.n_meta
            return pl.BlockSpec((self.rows, width), lambda b, i, *_: (r0, 0))
        return pl.BlockSpec((self.rows, width), lambda b, i, *_: (i, 0))


def _const_spec(shape, single_buffer=False):
    nd = len(shape)
    if single_buffer:
        return pl.BlockSpec(shape, lambda *_: (0,) * nd, pipeline_mode=pl.Buffered(1))
    return pl.BlockSpec(shape, lambda *_: (0,) * nd)


def _alias_tail(n_in, outs):
    specs = [pl.BlockSpec(memory_space=pl.ANY) for _ in outs]
    return specs, {n_in + k: k for k in range(len(outs))}


def _side_cast(stacks, n_steps, step_index):
    specs, operands, shapes, vmem = [], [], [], 0
    for w in stacks:
        L, K, N = w.shape
        rps = L * K // n_steps
        assert L * K % n_steps == 0 and rps % BF16_SUBLANES == 0
        specs.append(pl.BlockSpec((rps, N), lambda *idx: (step_index(*idx), 0)))
        operands.append(w.reshape(L * K, N))
        shapes.append(jax.ShapeDtypeStruct((L * K, N), _BF16))
        vmem += 2 * rps * N * (4 + 2)
    return specs, operands, shapes, vmem


def _cast_blocks(src_refs, dst_refs):
    for src_ref, dst_ref in zip(src_refs, dst_refs):
        dst_ref[...] = src_ref[...].astype(_BF16)


def _cast_body(w_ref, o_ref):
    o_ref[...] = w_ref[...].astype(_BF16)


def _to_bf16(w):
    L, K, N = w.shape
    tk = _pick_tile(K, 512)
    while tk > 8 and tk * N * 4 > CAST_BLOCK_BYTES:
        tk //= 2
    return pl.pallas_call(
        _cast_body,
        grid=(L, K // tk),
        in_specs=[pl.BlockSpec((1, tk, N), lambda l, i: (l, i, 0))],
        out_specs=pl.BlockSpec((1, tk, N), lambda l, i: (l, i, 0)),
        out_shape=jax.ShapeDtypeStruct(w.shape, _BF16),
        compiler_params=_cparams(("parallel", "parallel"), 2 * tk * N * (4 + 2) + VMEM_RESERVE_BYTES),
        name="cast_bf16",
    )(w)


def _ffn_body(*refs, nj, final, n_alias):
    h_ref, gain_ref, wg_ref, wu_ref, wd_ref = refs[:5]
    rest = refs[5:]
    fgain_ref = None
    if final:
        fgain_ref, rest = rest[0], rest[1:]
    o_ref, xn_ref = rest[n_alias:]
    j = pl.program_id(2)
    D = h_ref.shape[-1]

    @pl.when(j == 0)
    def _():
        x = h_ref[...].reshape(-1, D)
        xn_ref[...] = _rms(x, gain_ref[...]).astype(_BF16)
        o_ref[...] = x.reshape(o_ref.shape)

    g = jnp.dot(xn_ref[...], wg_ref[...], preferred_element_type=_F32)
    u = jnp.dot(xn_ref[...], wu_ref[...], preferred_element_type=_F32)
    act = ((g * jax.nn.sigmoid(g)) * u * 0.5).astype(_BF16)
    o_ref[...] += jnp.dot(act, wd_ref[...], preferred_element_type=_F32).reshape(o_ref.shape)

    if final:
        @pl.when(j == nj - 1)
        def _():
            o_ref[...] = _rms(o_ref[...].reshape(-1, D), fgain_ref[...]).reshape(o_ref.shape)


def _ffn(rows, h, gain, wg, wu, wd, *, layer, tf, T, final_gain=None, h_row_block0=None, prev_out=None):
    D = h.shape[-1]
    nj = wg.shape[2] // tf
    final = final_gain is not None
    in_specs = [rows.spec(D, row_block0=h_row_block0), _const_spec((1, D)),
                pl.BlockSpec((None, D, tf), lambda b, i, j: (layer, 0, j)),
                pl.BlockSpec((None, D, tf), lambda b, i, j: (layer, 0, j)),
                pl.BlockSpec((None, tf, D), lambda b, i, j: (layer, j, 0))]
    args = [h, gain, wg, wu, wd]
    if final:
        in_specs.append(_const_spec((1, D)))
        args.append(final_gain)
    aliases = {}
    if prev_out is not None:
        extra, aliases = _alias_tail(len(args), [prev_out])
        in_specs += extra
        args.append(prev_out)
    m = rows.m
    vmem = (4 * m * D * 4 + m * D * 2 + 2 * (D * 2 * tf + tf * D) * 2 + m * 2 * tf * 4 + m * tf * 2
            + VMEM_RESERVE_BYTES)
    out_rows = rows.n_real if final else T
    return pl.pallas_call(
        functools.partial(_ffn_body, nj=nj, final=final, n_alias=len(aliases)),
        grid=rows.grid + (nj,),
        in_specs=in_specs,
        out_specs=rows.spec(D),
        out_shape=jax.ShapeDtypeStruct((rows.B, out_rows, D), _F32),
        scratch_shapes=[pltpu.VMEM((m, D), _BF16)],
        input_output_aliases=aliases,
        compiler_params=_cparams(("parallel", "parallel", "arbitrary"), vmem),
        name=("ffn_final" if final else "ffn") + ("_meta" if rows.meta else ""),
    )(*args)


def _proj_body(*refs, n_rope, n_qkv, n_chunks, nb, n_alias, n_side):
    h_ref, gain_ref, w_ref, c_ref, s1_ref, s2_ref, dft_ref = refs[:7]
    side_in = refs[7:7 + n_side]
    qkv_ref, z_ref, xno_ref = refs[7 + n_side + n_alias:10 + n_side + n_alias]
    side_out = refs[10 + n_side + n_alias:]
    _cast_blocks(side_in, side_out)
    D = h_ref.shape[-1]
    rows = h_ref.shape[-2]
    xno_ref[...] = _rms(h_ref[...].reshape(-1, D), gain_ref[...]).astype(_BF16).reshape(xno_ref.shape)
    c = jnp.concatenate([c_ref[...]] * nb, axis=0)
    s1 = jnp.concatenate([s1_ref[...]] * nb, axis=0)
    s2 = jnp.concatenate([s2_ref[...]] * nb, axis=0)
    half = ROT_DIM // 2
    for ch in range(n_chunks):
        cols = slice(ch * PROJ_CHUNK, (ch + 1) * PROJ_CHUNK)
        y = jnp.dot(xno_ref[...].reshape(-1, D), w_ref[:, cols], preferred_element_type=_F32)
        if ch < n_rope:
            for hh in range(PROJ_CHUNK // HEAD_DIM):
                xh = y[:, hh * HEAD_DIM:(hh + 1) * HEAD_DIM]
                up = pltpu.roll(xh, HEAD_DIM - half, 1)
                dn = pltpu.roll(xh, half, 1)
                r = (xh * c + up * s1 + dn * s2).astype(_BF16)
                lo = ch * PROJ_CHUNK + hh * HEAD_DIM
                qkv_ref[:, :, lo:lo + HEAD_DIM] = r.reshape(nb, rows, HEAD_DIM)
        elif ch < n_qkv:
            qkv_ref[:, :, cols] = y.astype(_BF16).reshape(nb, rows, PROJ_CHUNK)
        else:
            ub = y.astype(_BF16)
            for gg in range(PROJ_CHUNK // FOURIER_GROUP_DIM):
                zz = jnp.dot(ub[:, gg * FOURIER_GROUP_DIM:(gg + 1) * FOURIER_GROUP_DIM], dft_ref[...],
                             preferred_element_type=_F32)
                lo = (ch - n_qkv) * PROJ_CHUNK + gg * FOURIER_GROUP_DIM
                z_ref[:, 0, :, lo:lo + FOURIER_GROUP_DIM] = (
                    zz[:, :FOURIER_GROUP_DIM].astype(_BF16).reshape(nb, rows, FOURIER_GROUP_DIM))
                z_ref[:, 1, :, lo:lo + FOURIER_GROUP_DIM] = (
                    zz[:, FOURIER_GROUP_DIM:].astype(_BF16).reshape(nb, rows, FOURIER_GROUP_DIM))


def _proj(rows, h, gain, w_in, rope_c, rope_s1, rope_s2, dft_c, *, layer, prev_out=None, side_cast=()):
    B, T, D = h.shape
    n_rope = (Q_DIM + KV_DIM) // PROJ_CHUNK
    n_qkv = (Q_DIM + 2 * KV_DIM) // PROJ_CHUNK
    n_chunks = IN_DIM // PROJ_CHUNK
    w_spec = pl.BlockSpec((None, D, IN_DIM), lambda *_: (layer, 0, 0), pipeline_mode=pl.Buffered(1))
    in_specs = [rows.spec(D), _const_spec((1, D)), w_spec,
                rows.table_spec(LANES), rows.table_spec(LANES), rows.table_spec(LANES),
                _const_spec((FOURIER_GROUP_DIM, 2 * FOURIER_GROUP_DIM))]
    args = [h, gain, w_in, rope_c, rope_s1, rope_s2, dft_c]
    out_specs = [rows.spec(Q_DIM + 2 * KV_DIM), rows.spec(FOURIER_DIM, lead=(2,)), rows.spec(D)]
    out_shape = [jax.ShapeDtypeStruct((B, T, Q_DIM + 2 * KV_DIM), _BF16),
                 jax.ShapeDtypeStruct((B, 2, T, FOURIER_DIM), _BF16),
                 jax.ShapeDtypeStruct((B, T, D), _BF16)]
    m = rows.m
    vmem = (2 * m * D * 4 + 2 * m * D * 2 + D * IN_DIM * 2 + 6 * m * LANES * 4
            + 2 * m * (Q_DIM + 2 * KV_DIM) * 2 + 4 * m * FOURIER_DIM * 2 + 8 * m * PROJ_CHUNK * 4
            + VMEM_RESERVE_BYTES)
    side_specs, side_ops, side_shapes, side_vmem = _side_cast(
        side_cast, rows.grid[0] * rows.grid[1], lambda b, i: b * rows.grid[1] + i)
    in_specs += side_specs
    args += side_ops
    out_specs += side_specs
    out_shape += side_shapes
    vmem += side_vmem
    aliases = {}
    if prev_out is not None:
        extra, aliases = _alias_tail(len(args), prev_out)
        in_specs += extra
        args += list(prev_out)
    outs = pl.pallas_call(
        functools.partial(_proj_body, n_rope=n_rope, n_qkv=n_qkv, n_chunks=n_chunks, nb=rows.nb,
                          n_alias=len(aliases), n_side=len(side_cast)),
        grid=rows.grid,
        in_specs=in_specs,
        out_specs=out_specs,
        out_shape=out_shape,
        input_output_aliases=aliases,
        compiler_params=_cparams(("parallel", "arbitrary"), vmem),
        name="mixer_proj" + ("_meta" if rows.meta else ""),
    )(*args)
    return tuple(outs[:3]) + tuple(o.reshape(w.shape) for o, w in zip(outs[3:], side_cast))


def _stack_heads(q):
    return jnp.concatenate([q[:, h * HEAD_DIM:(h + 1) * HEAD_DIM] for h in range(Q_PER_KV)], axis=0)


def _sink_column(sink_ref, g, rows_per_head):
    return jnp.concatenate(
        [jnp.full((rows_per_head, 1), sink_ref[g * Q_PER_KV + h], _F32) for h in range(Q_PER_KV)], axis=0)


def _pad_rows(a, rows):
    return jnp.concatenate([a, jnp.zeros((rows - a.shape[0], a.shape[1]), a.dtype)], axis=0)


def _softmax_pv(q4, k_all, v_all, masks, sink):
    s = lax.dot_general(q4, k_all, (((1,), (1,)), ((), ())), preferred_element_type=_F32) * (HEAD_DIM ** -0.5)
    blocks = []
    for i, mk in enumerate(masks):
        sb = s[:, i * BLOCK:(i + 1) * BLOCK]
        blocks.append(sb if mk is None else jnp.where(mk, sb, NEG_INF))
    mx = jnp.maximum(jnp.max(functools.reduce(jnp.maximum, blocks), axis=-1, keepdims=True), sink)
    p = jnp.concatenate([jnp.exp(sb - mx).astype(_BF16) for sb in blocks], axis=1)
    v_ext = jnp.concatenate([v_all, jnp.ones(v_all.shape, _BF16)], axis=1)
    o = jnp.dot(p, v_ext, preferred_element_type=_F32)
    den = o[:, HEAD_DIM:] + jnp.exp(sink - mx)
    return o[:, :HEAD_DIM] / den


def _attn_body(sink_ref, *refs, qb, ntile, n_meta, n_side):
    q_ref, kp_ref, kc_ref, kn_ref, km_ref, vp_ref, vc_ref, vn_ref, vm_ref = refs[:9]
    o_ref = refs[9 + n_side]
    kbuf, vbuf, s_buf, p_buf = refs[10 + 2 * n_side:]
    _cast_blocks(refs[9:9 + n_side], refs[10 + n_side:10 + 2 * n_side])
    g = pl.program_id(1)
    jt = pl.program_id(2)
    rows = Q_PER_KV * BLOCK
    band = 3 * BLOCK
    meta0 = (qb + 2) * BLOCK
    scale = HEAD_DIM ** -0.5
    c_exp2 = scale * LOG2_E

    kbuf[0:BLOCK] = kp_ref[0]
    kbuf[BLOCK:(qb + 1) * BLOCK] = kc_ref[0]
    kbuf[(qb + 1) * BLOCK:meta0] = kn_ref[0]
    kbuf[meta0:meta0 + BLOCK] = _pad_rows(km_ref[0], BLOCK)
    vbuf[0:BLOCK, 0:HEAD_DIM] = vp_ref[0]
    vbuf[BLOCK:(qb + 1) * BLOCK, 0:HEAD_DIM] = vc_ref[0]
    vbuf[(qb + 1) * BLOCK:meta0, 0:HEAD_DIM] = vn_ref[0]
    vbuf[meta0:meta0 + BLOCK, 0:HEAD_DIM] = _pad_rows(vm_ref[0], BLOCK)
    vbuf[:, HEAD_DIM:] = jnp.ones((meta0 + BLOCK, HEAD_DIM), _BF16)

    row = lax.broadcasted_iota(jnp.int32, (rows, BLOCK), 0) % BLOCK
    col = lax.broadcasted_iota(jnp.int32, (rows, BLOCK), 1)
    diff = col - row
    never = 2 * BLOCK
    sink_s = _sink_column(sink_ref, g, BLOCK) * (1.0 / scale)
    meta_fill = jnp.where(col == n_meta, sink_s, NEG_INF)
    meta_ok = col < n_meta

    def dot_t(a, b):
        return lax.dot_general(a, b, (((1,), (1,)), ((), ())), preferred_element_type=_F32)

    def stage_a(i, slot):
        r0 = pl.multiple_of(i * BLOCK, BLOCK)
        q4 = _stack_heads(q_ref[0, pl.ds(r0, BLOCK), :])
        s_buf[slot, :, 0:band] = dot_t(q4, kbuf[pl.ds(r0, band), :])
        s_buf[slot, :, band:] = dot_t(q4, kbuf[meta0:meta0 + BLOCK, :])

    def stage_b(i, slot):
        first = jnp.logical_and(jt == 0, i == 0)
        last = jnp.logical_and(jt == ntile - 1, i == qb - 1)
        off_p = jnp.where(first, never, 0)
        off_n = jnp.where(last, -never, 0)
        blocks = [jnp.where(diff >= off_p, s_buf[slot, :, 0:BLOCK], NEG_INF),
                  s_buf[slot, :, BLOCK:2 * BLOCK],
                  jnp.where(diff <= off_n, s_buf[slot, :, 2 * BLOCK:band], NEG_INF),
                  jnp.where(meta_ok, s_buf[slot, :, band:], meta_fill)]
        mx = jnp.max(functools.reduce(jnp.maximum, blocks), axis=-1, keepdims=True)
        for n, sb in enumerate(blocks):
            p_buf[slot, :, n * BLOCK:(n + 1) * BLOCK] = jnp.exp2((sb - mx) * c_exp2).astype(_BF16)

    def stage_c(i, slot):
        r0 = pl.multiple_of(i * BLOCK, BLOCK)
        o = (jnp.dot(p_buf[slot, :, 0:band], vbuf[pl.ds(r0, band), :], preferred_element_type=_F32)
             + jnp.dot(p_buf[slot, :, band:], vbuf[meta0:meta0 + BLOCK, :], preferred_element_type=_F32))
        out = o[:, :HEAD_DIM] / o[:, HEAD_DIM:]
        for h in range(Q_PER_KV):
            o_ref[0, pl.ds(r0, BLOCK), h * HEAD_DIM:(h + 1) * HEAD_DIM] = (
                out[h * BLOCK:(h + 1) * BLOCK].astype(_BF16))

    stage_a(0, 0)
    stage_a(1, 1)
    stage_b(0, 0)

    def pair(t, carry):
        i = 2 * t
        stage_c(i, 0)
        stage_b(i + 1, 1)
        stage_a(i + 2, 0)
        stage_c(i + 1, 1)
        stage_b(i + 2, 0)
        stage_a(i + 3, 1)
        return carry

    lax.fori_loop(0, (qb - 2) // 2, pair, 0)
    stage_c(qb - 2, 0)
    stage_b(qb - 1, 1)
    stage_c(qb - 1, 1)


def _attn_meta_body(sink_ref, q_ref, k0_ref, km_ref, v0_ref, vm_ref, prev_out_ref, o_ref, *, n_meta):
    del prev_out_ref
    g = pl.program_id(1)
    rows = Q_PER_KV * n_meta
    row = lax.broadcasted_iota(jnp.int32, (rows, BLOCK), 0) % n_meta
    col = lax.broadcasted_iota(jnp.int32, (rows, BLOCK), 1)
    q4 = _stack_heads(q_ref[0])
    masks = [col <= row + (WINDOW - n_meta), col < n_meta]
    k_all = jnp.concatenate([k0_ref[0], _pad_rows(km_ref[0], BLOCK)], axis=0)
    v_all = jnp.concatenate([v0_ref[0], _pad_rows(vm_ref[0], BLOCK)], axis=0)
    out = _softmax_pv(q4, k_all, v_all, masks, _sink_column(sink_ref, g, n_meta))
    for h in range(Q_PER_KV):
        o_ref[0, :, h * HEAD_DIM:(h + 1) * HEAD_DIM] = out[h * n_meta:(h + 1) * n_meta].astype(_BF16)


def _attention(qkv, sink, *, n_real, n_meta, qb, side_cast=()):
    B, T, _ = qkv.shape
    nblk = n_real // BLOCK
    ntile = nblk // qb
    kcol = Q_DIM // HEAD_DIM
    vcol = (Q_DIM + KV_DIM) // HEAD_DIM
    mrow = n_real // n_meta

    def edge_spec(rowf, col0):
        return pl.BlockSpec((1, BLOCK, HEAD_DIM), lambda b, g, j, s: (b, rowf(j), col0 + g))

    def main_spec(col0):
        return pl.BlockSpec((1, qb * BLOCK, HEAD_DIM), lambda b, g, j, s: (b, j, col0 + g))

    def meta_spec(col0):
        return pl.BlockSpec((1, n_meta, HEAD_DIM), lambda b, g, j, s: (b, mrow, col0 + g))

    def prev_i(j):
        return jnp.maximum(qb * j - 1, 0)

    def next_i(j):
        return jnp.minimum(qb * j + qb, nblk - 1)

    side_specs, side_ops, side_shapes, side_vmem = _side_cast(
        side_cast, B * N_KV_HEADS * ntile, lambda b, g, j, s: (b * N_KV_HEADS + g) * ntile + j)
    attn, *cast = pl.pallas_call(
        functools.partial(_attn_body, qb=qb, ntile=ntile, n_meta=n_meta, n_side=len(side_cast)),
        grid_spec=pltpu.PrefetchScalarGridSpec(
            num_scalar_prefetch=1,
            grid=(B, N_KV_HEADS, ntile),
            in_specs=[
                pl.BlockSpec((1, qb * BLOCK, PROJ_CHUNK), lambda b, g, j, s: (b, j, g)),
                edge_spec(prev_i, kcol), main_spec(kcol), edge_spec(next_i, kcol), meta_spec(kcol),
                edge_spec(prev_i, vcol), main_spec(vcol), edge_spec(next_i, vcol), meta_spec(vcol),
            ] + side_specs,
            out_specs=[pl.BlockSpec((1, qb * BLOCK, PROJ_CHUNK), lambda b, g, j, s: (b, j, g))] + side_specs,
            scratch_shapes=[
                pltpu.VMEM(((qb + 3) * BLOCK, HEAD_DIM), _BF16),
                pltpu.VMEM(((qb + 3) * BLOCK, 2 * HEAD_DIM), _BF16),
                pltpu.VMEM((2, Q_PER_KV * BLOCK, 4 * BLOCK), _F32),
                pltpu.VMEM((2, Q_PER_KV * BLOCK, 4 * BLOCK), _BF16),
            ],
        ),
        out_shape=[jax.ShapeDtypeStruct((B, T, Q_DIM), _BF16)] + side_shapes,
        compiler_params=_cparams(("parallel", "parallel", "arbitrary"), SMALL_CALL_VMEM_BYTES + side_vmem),
        name="windowed_gqa",
    )(sink, qkv, qkv, qkv, qkv, qkv, qkv, qkv, qkv, qkv, *side_ops)
    cast = [c.reshape(w.shape) for c, w in zip(cast, side_cast)]

    def blk0_spec(col0):
        return pl.BlockSpec((1, BLOCK, HEAD_DIM), lambda b, g, s: (b, 0, col0 + g))

    def meta2_spec(col0):
        return pl.BlockSpec((1, n_meta, HEAD_DIM), lambda b, g, s: (b, mrow, col0 + g))

    return pl.pallas_call(
        functools.partial(_attn_meta_body, n_meta=n_meta),
        grid_spec=pltpu.PrefetchScalarGridSpec(
            num_scalar_prefetch=1,
            grid=(B, N_KV_HEADS),
            in_specs=[
                pl.BlockSpec((1, n_meta, PROJ_CHUNK), lambda b, g, s: (b, mrow, g)),
                blk0_spec(kcol), meta2_spec(kcol), blk0_spec(vcol), meta2_spec(vcol),
                pl.BlockSpec(memory_space=pl.ANY),
            ],
            out_specs=pl.BlockSpec((1, n_meta, PROJ_CHUNK), lambda b, g, s: (b, mrow, g)),
        ),
        out_shape=jax.ShapeDtypeStruct((B, T, Q_DIM), _BF16),
        input_output_aliases={6: 0},
        compiler_params=_cparams(("parallel", "arbitrary"), SMALL_CALL_VMEM_BYTES),
        name="windowed_gqa_meta",
    )(sink, qkv, qkv, qkv, qkv, qkv, attn), cast


def _fft_factors(T):
    best = None
    for n2 in range(BF16_SUBLANES, T + 1, BF16_SUBLANES):
        if T % n2:
            continue
        n1 = T // n2
        r1 = n1 * BF16_SUBLANES
        cost = (4 * _round_up(r1, MXU_DEPTH) * r1 * (n2 // BF16_SUBLANES)
                + 2 * _round_up(n2, MXU_DEPTH) * n2 * n1)
        if best is None or cost < best[0]:
            best = (cost, n1, n2)
    assert best is not None, "sequence length must be a multiple of 16"
    return best[1], best[2]


def _fft1_body(z_ref, mc_ref, ms_ref, twc_ref, tws_ref, a_ref, *, n1, R, C):
    rows = n1 * R
    rhs = jnp.concatenate([z_ref[0, 0].reshape(rows, C), z_ref[0, 1].reshape(rows, C)], axis=1)
    xc = jnp.dot(mc_ref[...], rhs, preferred_element_type=_F32)
    xs = jnp.dot(ms_ref[...], rhs, preferred_element_type=_F32)
    ar = xc[:, :C] + xs[:, C:]
    ai = xc[:, C:] - xs[:, :C]
    twc = jnp.concatenate([twc_ref[0]] * (C // LANES), axis=1)
    tws = jnp.concatenate([tws_ref[0]] * (C // LANES), axis=1)
    a_ref[0, 0] = (ar * twc + ai * tws).reshape(n1, R, C).astype(_BF16)
    a_ref[0, 1] = (ai * twc - ar * tws).reshape(n1, R, C).astype(_BF16)


def _fft2_body(ar_ref, ai_ref, mc_ref, ms_ref, f_ref):
    for b in range(ar_ref.shape[0]):
        res = (jnp.dot(mc_ref[0], ar_ref[b, 0, 0], preferred_element_type=_F32)
               + jnp.dot(ms_ref[0], ai_ref[b, 0, 0], preferred_element_type=_F32))
        f_ref[b] = res.astype(_BF16)


def _fft_tables(T, n1, n2, n_meta, R):
    two_pi = 2.0 * np.pi
    k1 = jnp.arange(n1, dtype=jnp.int32)
    ang1 = ((k1[:, None] * k1[None, :]) % n1).astype(_F32) * (two_pi / n1)
    eye = jnp.eye(R, dtype=_F32)
    m1c = jnp.kron(jnp.cos(ang1) * (n1 ** -0.5), eye).astype(_BF16)
    m1s = jnp.kron(jnp.sin(ang1) * (n1 ** -0.5), eye).astype(_BF16)
    nn2 = jnp.arange(n2, dtype=jnp.int32).reshape(n2 // R, 1, R)
    angt = ((k1[None, :, None] * (nn2 + n_meta)) % T).astype(_F32) * (two_pi / T)
    angt = jnp.broadcast_to(angt.reshape(n2 // R, n1 * R, 1), (n2 // R, n1 * R, LANES))
    twc, tws = jnp.cos(angt), jnp.sin(angt)
    k2 = jnp.arange(n2, dtype=jnp.int32)[:, None]
    mm2 = jnp.arange(n2, dtype=jnp.int32)[None, :]
    outs_c, outs_s = [], []
    for rot in (n_meta // n1, n_meta // n1 + 1):
        ph2 = (((k2 + rot) % n2) * (mm2 + n_meta)) % n2
        ang2 = ph2.astype(_F32) * (two_pi / n2)
        outs_c.append(jnp.cos(ang2) * (n2 ** -0.5))
        outs_s.append(jnp.sin(ang2) * (n2 ** -0.5))
    return m1c, m1s, twc, tws, jnp.stack(outs_c).astype(_BF16), jnp.stack(outs_s).astype(_BF16)


def _position_dft(z, tables, *, n1, n2, R, n_meta):
    B, _, T, C = z.shape
    m1c, m1s, twc, tws, m2c, m2s = tables
    rows = n1 * R
    zv = z.reshape(B, 2, n1, n2, C)
    av = pl.pallas_call(
        functools.partial(_fft1_body, n1=n1, R=R, C=C),
        grid=(B, n2 // R),
        in_specs=[
            pl.BlockSpec((1, 2, n1, R, C), lambda b, t: (b, 0, 0, t, 0)),
            _const_spec((rows, rows)),
            _const_spec((rows, rows)),
            pl.BlockSpec((1, rows, LANES), lambda b, t: (t, 0, 0)),
            pl.BlockSpec((1, rows, LANES), lambda b, t: (t, 0, 0)),
        ],
        out_specs=pl.BlockSpec((1, 2, n1, R, C), lambda b, t: (b, 0, 0, t, 0)),
        out_shape=jax.ShapeDtypeStruct((B, 2, n1, n2, C), _BF16),
        compiler_params=_cparams(("parallel", "arbitrary"),
                                 4 * 2 * rows * C * 2 + 2 * 2 * rows * rows * 2 + 6 * rows * C * 4 + VMEM_RESERVE_BYTES),
        name="dft_stage1",
    )(zv, m1c, m1s, twc, tws)
    shift = n1 - (n_meta % n1)

    f = pl.pallas_call(
        _fft2_body,
        grid=(n1,),
        in_specs=[
            pl.BlockSpec((B, 1, 1, n2, C), lambda k: (0, 0, k, 0, 0)),
            pl.BlockSpec((B, 1, 1, n2, C), lambda k: (0, 1, k, 0, 0)),
            pl.BlockSpec((1, n2, n2), lambda k: (jnp.where(k < n_meta % n1, 1, 0), 0, 0)),
            pl.BlockSpec((1, n2, n2), lambda k: (jnp.where(k < n_meta % n1, 1, 0), 0, 0)),
        ],
        out_specs=pl.BlockSpec((B, n2, C), lambda k: (0, 0, (k + shift) % n1)),
        out_shape=jax.ShapeDtypeStruct((B, n2, n1 * C), _BF16),
        compiler_params=_cparams(("arbitrary",), SMALL_CALL_VMEM_BYTES),
        name="dft_stage2",
    )(av, av, m2c, m2s)
    return f.reshape(B, T, C)


def _mix_body(*refs, tn, n_alias):
    h_ref, xn_ref, a_ref, f_ref, wpa_ref, wpf_ref, wg_ref, wo_ref = refs[:8]
    (o_ref,) = refs[8 + n_alias:]
    D = h_ref.shape[-1]
    o_ref[...] = h_ref[...]
    xn = xn_ref[...].reshape(-1, D)
    a = a_ref[...].reshape(-1, Q_DIM)
    f = f_ref[...].reshape(-1, FOURIER_DIM)
    for j in range(D // tn):
        cols = slice(j * tn, (j + 1) * tn)
        av = jnp.dot(a, wpa_ref[:, cols], preferred_element_type=_F32)
        fv = jnp.dot(f, wpf_ref[:, cols], preferred_element_type=_F32)
        ga = jax.nn.sigmoid(jnp.dot(xn, wg_ref[:, cols], preferred_element_type=_F32))
        gf = jax.nn.sigmoid(jnp.dot(xn, wg_ref[:, D + j * tn:D + (j + 1) * tn], preferred_element_type=_F32))
        mixed = (ga * av + gf * fv).astype(_BF16)
        o_ref[...] += jnp.dot(mixed, wo_ref[cols, :], preferred_element_type=_F32).reshape(o_ref.shape)


def _mix_out(rows, h, xn, attn, four, wpa, wpf, wg, wo, *, layer, tn, prev_out=None):
    B, T, D = h.shape

    def resident(k, n):
        return pl.BlockSpec((None, k, n), lambda *_: (layer, 0, 0), pipeline_mode=pl.Buffered(1))

    in_specs = [rows.spec(D), rows.spec(D), rows.spec(Q_DIM), rows.spec(FOURIER_DIM),
                resident(Q_DIM, D), resident(FOURIER_DIM, D), resident(D, 2 * D), resident(D, D)]
    args = [h, xn, attn, four, wpa, wpf, wg, wo]
    aliases = {}
    if prev_out is not None:
        extra, aliases = _alias_tail(len(args), [prev_out])
        in_specs += extra
        args.append(prev_out)
    m = rows.m
    vmem = ((Q_DIM + FOURIER_DIM + 3 * D) * D * 2 + 4 * m * D * 4 + 2 * m * D * 2
            + 2 * m * (Q_DIM + FOURIER_DIM) * 2 + 10 * m * tn * 4 + VMEM_RESERVE_BYTES)
    return pl.pallas_call(
        functools.partial(_mix_body, tn=tn, n_alias=len(aliases)),
        grid=rows.grid,
        in_specs=in_specs,
        out_specs=rows.spec(D),
        out_shape=jax.ShapeDtypeStruct((B, T, D), _F32),
        input_output_aliases=aliases,
        compiler_params=_cparams(("parallel", "arbitrary"), vmem),
        name="mixer_out" + ("_meta" if rows.meta else ""),
    )(*args)


def _rope_tables(n_real, n_meta):
    T = n_real + n_meta
    half = ROT_DIM // 2
    inv = 1.0 / (ROPE_THETA ** (jnp.arange(0, ROT_DIM, 2, dtype=_F32) / ROT_DIM))
    ang = jnp.arange(T, dtype=_F32)[:, None] * inv[None, :]
    cos, sin = jnp.cos(ang), jnp.sin(ang)
    cos = jnp.concatenate([cos[n_meta:], cos[:n_meta]], axis=0)
    sin = jnp.concatenate([sin[n_meta:], sin[:n_meta]], axis=0)
    ones = jnp.ones((T, HEAD_DIM - ROT_DIM), _F32)
    zeros_h = jnp.zeros((T, half), _F32)
    zeros_r = jnp.zeros((T, HEAD_DIM - ROT_DIM), _F32)
    c = jnp.concatenate([cos, cos, ones], axis=1)
    s1 = jnp.concatenate([-sin, zeros_h, zeros_r], axis=1)
    s2 = jnp.concatenate([zeros_h, sin, zeros_r], axis=1)
    return c, s1, s2


def _channel_dft_table():
    n = FOURIER_GROUP_DIM
    c = jnp.arange(n, dtype=jnp.int32)[:, None]
    k = jnp.arange(n, dtype=jnp.int32)[None, :]
    ang = ((c * k) % n).astype(_F32) * (2.0 * np.pi / n)
    return (jnp.concatenate([jnp.cos(ang), -jnp.sin(ang)], axis=1) * (n ** -0.5)).astype(_BF16)


def _pick_tile(n, pref):
    t = pref
    while n % t:
        t //= 2
    return t


def kernel(x, meta_tokens, ffn1_norm, ffn1_w_gate, ffn1_w_up, ffn1_w_down, mix_norm, w_in, w_gate, sink,
           w_attn_branch, w_fourier_branch, w_out, ffn2_norm, ffn2_w_gate, ffn2_w_up, ffn2_w_down, final_norm):
    B, S, D = x.shape
    n_meta = meta_tokens.shape[0]
    depth = ffn1_norm.shape[0]
    d_ff = ffn1_w_gate.shape[-1]
    T = S + n_meta
    assert S % (4 * BLOCK) == 0 and S % n_meta == 0 and BLOCK % n_meta == 0 and n_meta < BLOCK
    assert w_in.shape[-1] == IN_DIM and KV_DIM == PROJ_CHUNK

    tf = _pick_tile(d_ff, FF_CHUNK)
    tn = _pick_tile(D, MIX_CHUNK)
    qb = _pick_tile(S // BLOCK, ATTN_BLOCKS_PER_STEP)
    rows_ffn = _Rows(B, S, n_meta, _pick_tile(S, FFN_ROW_TILE), meta=False)
    rows_main = _Rows(B, S, n_meta, _pick_tile(S, PROJ_ROW_TILE), meta=False)
    rows_mix = _Rows(B, S, n_meta, _pick_tile(S, MIX_ROW_TILE), meta=False)
    rows_meta = _Rows(B, S, n_meta, 0, meta=True)

    n1, n2 = _fft_factors(T)
    fft_tables = _fft_tables(T, n1, n2, n_meta, BF16_SUBLANES)
    rope_c, rope_s1, rope_s2 = _rope_tables(S, n_meta)
    dft_c = _channel_dft_table()

    ffn1_w = tuple(_to_bf16(w) for w in (ffn1_w_gate, ffn1_w_up, ffn1_w_down))
    w_in_b = _to_bf16(w_in)

    def ffn(h, h_meta, gain, wts, layer, final_gain=None):
        out = _ffn(rows_ffn, h, gain, *wts, layer=layer, tf=tf, T=T, final_gain=final_gain)
        if final_gain is None:
            src, blk0 = h_meta
            out = _ffn(rows_meta, src, gain, *wts, layer=layer, tf=tf, T=T, h_row_block0=blk0, prev_out=out)
        return out

    meta = jnp.broadcast_to(meta_tokens[None].astype(x.dtype), (B, n_meta, D))
    h = None
    for l in range(depth):
        if l == 0:
            h = ffn(x, (meta, 0), ffn1_norm[l][None], ffn1_w, l)
        else:
            h = ffn(h, (h, None), ffn1_norm[l][None], ffn1_w, l)

        gain = mix_norm[l][None]
        proj_args = (gain, w_in_b, rope_c, rope_s1, rope_s2, dft_c)
        if l == 0:
            qkv, z, xn, *ffn2_w = _proj(rows_main, h, *proj_args, layer=l,
                                        side_cast=(ffn2_w_gate, ffn2_w_up, ffn2_w_down))
        else:
            qkv, z, xn = _proj(rows_main, h, *proj_args, layer=l)
        qkv, z, xn = _proj(rows_meta, h, *proj_args, layer=l, prev_out=(qkv, z, xn))
        if l == 0:
            attn, mix_w = _attention(qkv, sink[l], n_real=S, n_meta=n_meta, qb=qb,
                                     side_cast=(w_attn_branch, w_fourier_branch, w_gate, w_out))
        else:
            attn, _ = _attention(qkv, sink[l], n_real=S, n_meta=n_meta, qb=qb)
        four = _position_dft(z, fft_tables, n1=n1, n2=n2, R=BF16_SUBLANES, n_meta=n_meta)
        mix_args = (xn, attn, four) + tuple(mix_w)
        h2 = _mix_out(rows_mix, h, *mix_args, layer=l, tn=tn)
        h = _mix_out(rows_meta, h, *mix_args, layer=l, tn=tn, prev_out=h2)

        last = l == depth - 1
        h = ffn(h, (h, None), ffn2_norm[l][None], ffn2_w, l, final_gain=final_norm[None] if last else None)
    return h
```

```python
import functools

import numpy as np
import jax
import jax.numpy as jnp
from jax import lax
from jax.experimental import pallas as pl
from jax.experimental.pallas import tpu as pltpu

_F32 = jnp.float32
_BF16 = jnp.bfloat16

HEAD_DIM = 128
N_Q_HEADS = 16
N_KV_HEADS = 4
Q_PER_KV = N_Q_HEADS // N_KV_HEADS
WINDOW = 128
BLOCK = 128
ROPE_THETA = 500000.0
ROT_DIM = HEAD_DIM // 4
N_FOURIER_GROUPS = 8
FOURIER_GROUP_DIM = 128
FOURIER_DIM = N_FOURIER_GROUPS * FOURIER_GROUP_DIM
Q_DIM = N_Q_HEADS * HEAD_DIM
KV_DIM = N_KV_HEADS * HEAD_DIM
IN_DIM = Q_DIM + 2 * KV_DIM + FOURIER_DIM
RMS_EPS = 1e-6
NEG_INF = -1e30

LANES = 128
BF16_SUBLANES = 16
PROJ_CHUNK = Q_PER_KV * HEAD_DIM
V7X_VMEM_BYTES = 64 * 1024 * 1024
VMEM_RESERVE_BYTES = 4 << 20
SMALL_CALL_VMEM_BYTES = 32 << 20
CAST_BLOCK_BYTES = 6 << 20
LOG2_E = 1.4426950408889634
MXU_DEPTH = 256

FFN_ROW_TILE = 1024
FF_CHUNK = 512
FFN_COL_TILE = 256
PROJ_ROW_TILE = 512
MIX_ROW_TILE = 256
MIX_CHUNK = 1024
ATTN_BLOCKS_PER_STEP = 32


def _round_up(a, m):
    return (a + m - 1) // m * m


def _cparams(sem, vmem_bytes):
    return pltpu.CompilerParams(dimension_semantics=sem,
                                vmem_limit_bytes=int(min(vmem_bytes, V7X_VMEM_BYTES - VMEM_RESERVE_BYTES)))


def _rms(x, g):
    return x * lax.rsqrt(jnp.mean(x * x, axis=-1, keepdims=True) + RMS_EPS) * g


class _Rows:
    def __init__(self, B, n_real, n_meta, tm, meta):
        self.B, self.n_real, self.n_meta, self.meta = B, n_real, n_meta, meta
        self.nb = B if meta else 1
        self.rows = n_meta if meta else tm
        self.grid = (1, 1) if meta else (B, n_real // tm)
        self.m = self.nb * self.rows

    def spec(self, width, lead=(), row_block0=None):
        nl = len(lead)
        if self.meta:
            r0 = self.n_real // self.n_meta if row_block0 is None else row_block0
            return pl.BlockSpec((self.nb,) + tuple(lead) + (self.rows, width),
                                lambda b, i, *_: (0,) + (0,) * nl + (r0, 0))
        return pl.BlockSpec((1,) + tuple(lead) + (self.rows, width),
                            lambda b, i, *_: (b,) + (0,) * nl + (i, 0))

    def table_spec(self, width):
        if self.meta:
            r0 = self.n_real // self.n_meta
            return pl.BlockSpec((self.rows, width), lambda b, i, *_: (r0, 0))
        return pl.BlockSpec((self.rows, width), lambda b, i, *_: (i, 0))


def _const_spec(shape, single_buffer=False):
    nd = len(shape)
    if single_buffer:
        return pl.BlockSpec(shape, lambda *_: (0,) * nd, pipeline_mode=pl.Buffered(1))
    return pl.BlockSpec(shape, lambda *_: (0,) * nd)


def _alias_tail(n_in, outs):
    specs = [pl.BlockSpec(memory_space=pl.ANY) for _ in outs]
    return specs, {n_in + k: k for k in range(len(outs))}


def _side_cast(stacks, n_steps, step_index):
    specs, operands, shapes, vmem = [], [], [], 0
    for w in stacks:
        L, K, N = w.shape
        rps = L * K // n_steps
        assert L * K % n_steps == 0 and rps % BF16_SUBLANES == 0
        specs.append(pl.BlockSpec((rps, N), lambda *idx: (step_index(*idx), 0)))
        operands.append(w.reshape(L * K, N))
        shapes.append(jax.ShapeDtypeStruct((L * K, N), _BF16))
        vmem += 2 * rps * N * (4 + 2)
    return specs, operands, shapes, vmem


def _cast_blocks(src_refs, dst_refs):
    for src_ref, dst_ref in zip(src_refs, dst_refs):
        dst_ref[...] = src_ref[...].astype(_BF16)


def _cast_body(w_ref, o_ref):
    o_ref[...] = w_ref[...].astype(_BF16)


def _to_bf16(w):
    L, K, N = w.shape
    tk = _pick_tile(K, 512)
    while tk > 8 and tk * N * 4 > CAST_BLOCK_BYTES:
        tk //= 2
    return pl.pallas_call(
        _cast_body,
        grid=(L, K // tk),
        in_specs=[pl.BlockSpec((1, tk, N), lambda l, i: (l, i, 0))],
        out_specs=pl.BlockSpec((1, tk, N), lambda l, i: (l, i, 0)),
        out_shape=jax.ShapeDtypeStruct(w.shape, _BF16),
        compiler_params=_cparams(("parallel", "parallel"), 2 * tk * N * (4 + 2) + VMEM_RESERVE_BYTES),
        name="cast_bf16",
    )(w)


def _ffn_body(*refs, nj, final, n_alias):
    h_ref, gain_ref, wg_ref, wu_ref, wd_ref = refs[:5]
    rest = refs[5:]
    fgain_ref = None
    if final:
        fgain_ref, rest = rest[0], rest[1:]
    o_ref, xn_ref = rest[n_alias:]
    j = pl.program_id(2)
    D = h_ref.shape[-1]

    @pl.when(j == 0)
    def _():
        x = h_ref[...].reshape(-1, D)
        xn_ref[...] = _rms(x, gain_ref[...]).astype(_BF16)
        o_ref[...] = x.reshape(o_ref.shape)

    tf = wg_ref.shape[-1]
    ct = min(FFN_COL_TILE, tf)
    acts = []
    for t in range(tf // ct):
        cols = slice(t * ct, (t + 1) * ct)
        g = jnp.dot(xn_ref[...], wg_ref[:, cols], preferred_element_type=_F32)
        u = jnp.dot(xn_ref[...], wu_ref[:, cols], preferred_element_type=_F32)
        acts.append(((g * jax.nn.sigmoid(g)) * u * 0.5).astype(_BF16))
    act = jnp.concatenate(acts, axis=1)
    o_ref[...] += jnp.dot(act, wd_ref[...], preferred_element_type=_F32).reshape(o_ref.shape)

    if final:
        @pl.when(j == nj - 1)
        def _():
            o_ref[...] = _rms(o_ref[...].reshape(-1, D), fgain_ref[...]).reshape(o_ref.shape)


def _ffn(rows, h, gain, wg, wu, wd, *, layer, tf, T, final_gain=None, h_row_block0=None, prev_out=None):
    D = h.shape[-1]
    nj = wg.shape[2] // tf
    final = final_gain is not None
    in_specs = [rows.spec(D, row_block0=h_row_block0), _const_spec((1, D)),
                pl.BlockSpec((None, D, tf), lambda b, i, j: (layer, 0, j)),
                pl.BlockSpec((None, D, tf), lambda b, i, j: (layer, 0, j)),
                pl.BlockSpec((None, tf, D), lambda b, i, j: (layer, j, 0))]
    args = [h, gain, wg, wu, wd]
    if final:
        in_specs.append(_const_spec((1, D)))
        args.append(final_gain)
    aliases = {}
    if prev_out is not None:
        extra, aliases = _alias_tail(len(args), [prev_out])
        in_specs += extra
        args.append(prev_out)
    m = rows.m
    vmem = (4 * m * D * 4 + m * D * 2 + 2 * (D * 2 * tf + tf * D) * 2 + m * 2 * tf * 4 + m * tf * 2
            + VMEM_RESERVE_BYTES)
    out_rows = rows.n_real if final else T
    return pl.pallas_call(
        functools.partial(_ffn_body, nj=nj, final=final, n_alias=len(aliases)),
        grid=rows.grid + (nj,),
        in_specs=in_specs,
        out_specs=rows.spec(D),
        out_shape=jax.ShapeDtypeStruct((rows.B, out_rows, D), _F32),
        scratch_shapes=[pltpu.VMEM((m, D), _BF16)],
        input_output_aliases=aliases,
        compiler_params=_cparams(("parallel", "parallel", "arbitrary"), vmem),
        name=("ffn_final" if final else "ffn") + ("_meta" if rows.meta else ""),
    )(*args)


def _proj_body(*refs, n_rope, n_qkv, n_chunks, nb, n_alias, n_side):
    h_ref, gain_ref, w_ref, c_ref, s1_ref, s2_ref, dft_ref = refs[:7]
    side_in = refs[7:7 + n_side]
    qkv_ref, z_ref, xno_ref = refs[7 + n_side + n_alias:10 + n_side + n_alias]
    side_out = refs[10 + n_side + n_alias:]
    _cast_blocks(side_in, side_out)
    D = h_ref.shape[-1]
    rows = h_ref.shape[-2]
    xno_ref[...] = _rms(h_ref[...].reshape(-1, D), gain_ref[...]).astype(_BF16).reshape(xno_ref.shape)
    c = jnp.concatenate([c_ref[...]] * nb, axis=0)
    s1 = jnp.concatenate([s1_ref[...]] * nb, axis=0)
    s2 = jnp.concatenate([s2_ref[...]] * nb, axis=0)
    half = ROT_DIM // 2
    for ch in range(n_chunks):
        cols = slice(ch * PROJ_CHUNK, (ch + 1) * PROJ_CHUNK)
        y = jnp.dot(xno_ref[...].reshape(-1, D), w_ref[:, cols], preferred_element_type=_F32)
        if ch < n_rope:
            for hh in range(PROJ_CHUNK // HEAD_DIM):
                xh = y[:, hh * HEAD_DIM:(hh + 1) * HEAD_DIM]
                up = pltpu.roll(xh, HEAD_DIM - half, 1)
                dn = pltpu.roll(xh, half, 1)
                r = (xh * c + up * s1 + dn * s2).astype(_BF16)
                lo = ch * PROJ_CHUNK + hh * HEAD_DIM
                qkv_ref[:, :, lo:lo + HEAD_DIM] = r.reshape(nb, rows, HEAD_DIM)
        elif ch < n_qkv:
            qkv_ref[:, :, cols] = y.astype(_BF16).reshape(nb, rows, PROJ_CHUNK)
        else:
            ub = y.astype(_BF16)
            for gg in range(PROJ_CHUNK // FOURIER_GROUP_DIM):
                zz = jnp.dot(ub[:, gg * FOURIER_GROUP_DIM:(gg + 1) * FOURIER_GROUP_DIM], dft_ref[...],
                             preferred_element_type=_F32)
                lo = (ch - n_qkv) * PROJ_CHUNK + gg * FOURIER_GROUP_DIM
                z_ref[:, 0, :, lo:lo + FOURIER_GROUP_DIM] = (
                    zz[:, :FOURIER_GROUP_DIM].astype(_BF16).reshape(nb, rows, FOURIER_GROUP_DIM))
                z_ref[:, 1, :, lo:lo + FOURIER_GROUP_DIM] = (
                    zz[:, FOURIER_GROUP_DIM:].astype(_BF16).reshape(nb, rows, FOURIER_GROUP_DIM))


def _proj(rows, h, gain, w_in, rope_c, rope_s1, rope_s2, dft_c, *, layer, prev_out=None, side_cast=()):
    B, T, D = h.shape
    n_rope = (Q_DIM + KV_DIM) // PROJ_CHUNK
    n_qkv = (Q_DIM + 2 * KV_DIM) // PROJ_CHUNK
    n_chunks = IN_DIM // PROJ_CHUNK
    w_spec = pl.BlockSpec((None, D, IN_DIM), lambda *_: (layer, 0, 0), pipeline_mode=pl.Buffered(1))
    in_specs = [rows.spec(D), _const_spec((1, D)), w_spec,
                rows.table_spec(LANES), rows.table_spec(LANES), rows.table_spec(LANES),
                _const_spec((FOURIER_GROUP_DIM, 2 * FOURIER_GROUP_DIM))]
    args = [h, gain, w_in, rope_c, rope_s1, rope_s2, dft_c]
    out_specs = [rows.spec(Q_DIM + 2 * KV_DIM), rows.spec(FOURIER_DIM, lead=(2,)), rows.spec(D)]
    out_shape = [jax.ShapeDtypeStruct((B, T, Q_DIM + 2 * KV_DIM), _BF16),
                 jax.ShapeDtypeStruct((B, 2, T, FOURIER_DIM), _BF16),
                 jax.ShapeDtypeStruct((B, T, D), _BF16)]
    m = rows.m
    vmem = (2 * m * D * 4 + 2 * m * D * 2 + D * IN_DIM * 2 + 6 * m * LANES * 4
            + 2 * m * (Q_DIM + 2 * KV_DIM) * 2 + 4 * m * FOURIER_DIM * 2 + 8 * m * PROJ_CHUNK * 4
            + VMEM_RESERVE_BYTES)
    side_specs, side_ops, side_shapes, side_vmem = _side_cast(
        side_cast, rows.grid[0] * rows.grid[1], lambda b, i: b * rows.grid[1] + i)
    in_specs += side_specs
    args += side_ops
    out_specs += side_specs
    out_shape += side_shapes
    vmem += side_vmem
    aliases = {}
    if prev_out is not None:
        extra, aliases = _alias_tail(len(args), prev_out)
        in_specs += extra
        args += list(prev_out)
    outs = pl.pallas_call(
        functools.partial(_proj_body, n_rope=n_rope, n_qkv=n_qkv, n_chunks=n_chunks, nb=rows.nb,
                          n_alias=len(aliases), n_side=len(side_cast)),
        grid=rows.grid,
        in_specs=in_specs,
        out_specs=out_specs,
        out_shape=out_shape,
        input_output_aliases=aliases,
        compiler_params=_cparams(("parallel", "arbitrary"), vmem),
        name="mixer_proj" + ("_meta" if rows.meta else ""),
    )(*args)
    return tuple(outs[:3]) + tuple(o.reshape(w.shape) for o, w in zip(outs[3:], side_cast))


def _stack_heads(q):
    return jnp.concatenate([q[:, h * HEAD_DIM:(h + 1) * HEAD_DIM] for h in range(Q_PER_KV)], axis=0)


def _sink_column(sink_ref, g, rows_per_head):
    return jnp.concatenate(
        [jnp.full((rows_per_head, 1), sink_ref[g * Q_PER_KV + h], _F32) for h in range(Q_PER_KV)], axis=0)


def _pad_rows(a, rows):
    return jnp.concatenate([a, jnp.zeros((rows - a.shape[0], a.shape[1]), a.dtype)], axis=0)


def _softmax_pv(q4, k_all, v_all, masks, sink):
    s = lax.dot_general(q4, k_all, (((1,), (1,)), ((), ())), preferred_element_type=_F32) * (HEAD_DIM ** -0.5)
    blocks = []
    for i, mk in enumerate(masks):
        sb = s[:, i * BLOCK:(i + 1) * BLOCK]
        blocks.append(sb if mk is None else jnp.where(mk, sb, NEG_INF))
    mx = jnp.maximum(jnp.max(functools.reduce(jnp.maximum, blocks), axis=-1, keepdims=True), sink)
    p = jnp.concatenate([jnp.exp(sb - mx).astype(_BF16) for sb in blocks], axis=1)
    v_ext = jnp.concatenate([v_all, jnp.ones(v_all.shape, _BF16)], axis=1)
    o = jnp.dot(p, v_ext, preferred_element_type=_F32)
    den = o[:, HEAD_DIM:] + jnp.exp(sink - mx)
    return o[:, :HEAD_DIM] / den


def _attn_body(sink_ref, *refs, qb, ntile, n_meta, n_side):
    q_ref, kp_ref, kc_ref, kn_ref, km_ref, vp_ref, vc_ref, vn_ref, vm_ref = refs[:9]
    o_ref = refs[9 + n_side]
    kbuf, vbuf, s_buf, p_buf = refs[10 + 2 * n_side:]
    _cast_blocks(refs[9:9 + n_side], refs[10 + n_side:10 + 2 * n_side])
    g = pl.program_id(1)
    jt = pl.program_id(2)
    rows = Q_PER_KV * BLOCK
    band = 3 * BLOCK
    meta0 = (qb + 2) * BLOCK
    scale = HEAD_DIM ** -0.5
    c_exp2 = scale * LOG2_E

    kbuf[0:BLOCK] = kp_ref[0]
    kbuf[BLOCK:(qb + 1) * BLOCK] = kc_ref[0]
    kbuf[(qb + 1) * BLOCK:meta0] = kn_ref[0]
    kbuf[meta0:meta0 + BLOCK] = _pad_rows(km_ref[0], BLOCK)
    vbuf[0:BLOCK, 0:HEAD_DIM] = vp_ref[0]
    vbuf[BLOCK:(qb + 1) * BLOCK, 0:HEAD_DIM] = vc_ref[0]
    vbuf[(qb + 1) * BLOCK:meta0, 0:HEAD_DIM] = vn_ref[0]
    vbuf[meta0:meta0 + BLOCK, 0:HEAD_DIM] = _pad_rows(vm_ref[0], BLOCK)
    vbuf[:, HEAD_DIM:] = jnp.ones((meta0 + BLOCK, HEAD_DIM), _BF16)

    row = lax.broadcasted_iota(jnp.int32, (rows, BLOCK), 0) % BLOCK
    col = lax.broadcasted_iota(jnp.int32, (rows, BLOCK), 1)
    diff = col - row
    never = 2 * BLOCK
    sink_s = _sink_column(sink_ref, g, BLOCK) * (1.0 / scale)
    meta_fill = jnp.where(col == n_meta, sink_s, NEG_INF)
    meta_ok = col < n_meta

    def dot_t(a, b):
        return lax.dot_general(a, b, (((1,), (1,)), ((), ())), preferred_element_type=_F32)

    def stage_a(i, slot):
        r0 = pl.multiple_of(i * BLOCK, BLOCK)
        q4 = _stack_heads(q_ref[0, pl.ds(r0, BLOCK), :])
        s_buf[slot, :, 0:band] = dot_t(q4, kbuf[pl.ds(r0, band), :])
        s_buf[slot, :, band:] = dot_t(q4, kbuf[meta0:meta0 + BLOCK, :])

    def stage_b(i, slot):
        first = jnp.logical_and(jt == 0, i == 0)
        last = jnp.logical_and(jt == ntile - 1, i == qb - 1)
        off_p = jnp.where(first, never, 0)
        off_n = jnp.where(last, -never, 0)
        blocks = [jnp.where(diff >= off_p, s_buf[slot, :, 0:BLOCK], NEG_INF),
                  s_buf[slot, :, BLOCK:2 * BLOCK],
                  jnp.where(diff <= off_n, s_buf[slot, :, 2 * BLOCK:band], NEG_INF),
                  jnp.where(meta_ok, s_buf[slot, :, band:], meta_fill)]
        mx = jnp.max(functools.reduce(jnp.maximum, blocks), axis=-1, keepdims=True)
        for n, sb in enumerate(blocks):
            p_buf[slot, :, n * BLOCK:(n + 1) * BLOCK] = jnp.exp2((sb - mx) * c_exp2).astype(_BF16)

    def stage_c(i, slot):
        r0 = pl.multiple_of(i * BLOCK, BLOCK)
        o = (jnp.dot(p_buf[slot, :, 0:band], vbuf[pl.ds(r0, band), :], preferred_element_type=_F32)
             + jnp.dot(p_buf[slot, :, band:], vbuf[meta0:meta0 + BLOCK, :], preferred_element_type=_F32))
        out = o[:, :HEAD_DIM] / o[:, HEAD_DIM:]
        for h in range(Q_PER_KV):
            o_ref[0, pl.ds(r0, BLOCK), h * HEAD_DIM:(h + 1) * HEAD_DIM] = (
                out[h * BLOCK:(h + 1) * BLOCK].astype(_BF16))

    stage_a(0, 0)
    stage_a(1, 1)
    stage_b(0, 0)

    def pair(t, carry):
        i = 2 * t
        stage_c(i, 0)
        stage_b(i + 1, 1)
        stage_a(i + 2, 0)
        stage_c(i + 1, 1)
        stage_b(i + 2, 0)
        stage_a(i + 3, 1)
        return carry

    lax.fori_loop(0, (qb - 2) // 2, pair, 0)
    stage_c(qb - 2, 0)
    stage_b(qb - 1, 1)
    stage_c(qb - 1, 1)


def _attn_meta_body(sink_ref, q_ref, k0_ref, km_ref, v0_ref, vm_ref, prev_out_ref, o_ref, *, n_meta):
    del prev_out_ref
    g = pl.program_id(1)
    rows = Q_PER_KV * n_meta
    row = lax.broadcasted_iota(jnp.int32, (rows, BLOCK), 0) % n_meta
    col = lax.broadcasted_iota(jnp.int32, (rows, BLOCK), 1)
    q4 = _stack_heads(q_ref[0])
    masks = [col <= row + (WINDOW - n_meta), col < n_meta]
    k_all = jnp.concatenate([k0_ref[0], _pad_rows(km_ref[0], BLOCK)], axis=0)
    v_all = jnp.concatenate([v0_ref[0], _pad_rows(vm_ref[0], BLOCK)], axis=0)
    out = _softmax_pv(q4, k_all, v_all, masks, _sink_column(sink_ref, g, n_meta))
    for h in range(Q_PER_KV):
        o_ref[0, :, h * HEAD_DIM:(h + 1) * HEAD_DIM] = out[h * n_meta:(h + 1) * n_meta].astype(_BF16)


def _attention(qkv, sink, *, n_real, n_meta, qb, side_cast=()):
    B, T, _ = qkv.shape
    nblk = n_real // BLOCK
    ntile = nblk // qb
    kcol = Q_DIM // HEAD_DIM
    vcol = (Q_DIM + KV_DIM) // HEAD_DIM
    mrow = n_real // n_meta

    def edge_spec(rowf, col0):
        return pl.BlockSpec((1, BLOCK, HEAD_DIM), lambda b, g, j, s: (b, rowf(j), col0 + g))

    def main_spec(col0):
        return pl.BlockSpec((1, qb * BLOCK, HEAD_DIM), lambda b, g, j, s: (b, j, col0 + g))

    def meta_spec(col0):
        return pl.BlockSpec((1, n_meta, HEAD_DIM), lambda b, g, j, s: (b, mrow, col0 + g))

    def prev_i(j):
        return jnp.maximum(qb * j - 1, 0)

    def next_i(j):
        return jnp.minimum(qb * j + qb, nblk - 1)

    side_specs, side_ops, side_shapes, side_vmem = _side_cast(
        side_cast, B * N_KV_HEADS * ntile, lambda b, g, j, s: (b * N_KV_HEADS + g) * ntile + j)
    attn, *cast = pl.pallas_call(
        functools.partial(_attn_body, qb=qb, ntile=ntile, n_meta=n_meta, n_side=len(side_cast)),
        grid_spec=pltpu.PrefetchScalarGridSpec(
            num_scalar_prefetch=1,
            grid=(B, N_KV_HEADS, ntile),
            in_specs=[
                pl.BlockSpec((1, qb * BLOCK, PROJ_CHUNK), lambda b, g, j, s: (b, j, g)),
                edge_spec(prev_i, kcol), main_spec(kcol), edge_spec(next_i, kcol), meta_spec(kcol),
                edge_spec(prev_i, vcol), main_spec(vcol), edge_spec(next_i, vcol), meta_spec(vcol),
            ] + side_specs,
            out_specs=[pl.BlockSpec((1, qb * BLOCK, PROJ_CHUNK), lambda b, g, j, s: (b, j, g))] + side_specs,
            scratch_shapes=[
                pltpu.VMEM(((qb + 3) * BLOCK, HEAD_DIM), _BF16),
                pltpu.VMEM(((qb + 3) * BLOCK, 2 * HEAD_DIM), _BF16),
                pltpu.VMEM((2, Q_PER_KV * BLOCK, 4 * BLOCK), _F32),
                pltpu.VMEM((2, Q_PER_KV * BLOCK, 4 * BLOCK), _BF16),
            ],
        ),
        out_shape=[jax.ShapeDtypeStruct((B, T, Q_DIM), _BF16)] + side_shapes,
        compiler_params=_cparams(("parallel", "parallel", "arbitrary"), SMALL_CALL_VMEM_BYTES + side_vmem),
        name="windowed_gqa",
    )(sink, qkv, qkv, qkv, qkv, qkv, qkv, qkv, qkv, qkv, *side_ops)
    cast = [c.reshape(w.shape) for c, w in zip(cast, side_cast)]

    def blk0_spec(col0):
        return pl.BlockSpec((1, BLOCK, HEAD_DIM), lambda b, g, s: (b, 0, col0 + g))

    def meta2_spec(col0):
        return pl.BlockSpec((1, n_meta, HEAD_DIM), lambda b, g, s: (b, mrow, col0 + g))

    return pl.pallas_call(
        functools.partial(_attn_meta_body, n_meta=n_meta),
        grid_spec=pltpu.PrefetchScalarGridSpec(
            num_scalar_prefetch=1,
            grid=(B, N_KV_HEADS),
            in_specs=[
                pl.BlockSpec((1, n_meta, PROJ_CHUNK), lambda b, g, s: (b, mrow, g)),
                blk0_spec(kcol), meta2_spec(kcol), blk0_spec(vcol), meta2_spec(vcol),
                pl.BlockSpec(memory_space=pl.ANY),
            ],
            out_specs=pl.BlockSpec((1, n_meta, PROJ_CHUNK), lambda b, g, s: (b, mrow, g)),
        ),
        out_shape=jax.ShapeDtypeStruct((B, T, Q_DIM), _BF16),
        input_output_aliases={6: 0},
        compiler_params=_cparams(("parallel", "arbitrary"), SMALL_CALL_VMEM_BYTES),
        name="windowed_gqa_meta",
    )(sink, qkv, qkv, qkv, qkv, qkv, attn), cast


def _fft_factors(T):
    best = None
    for n2 in range(BF16_SUBLANES, T + 1, BF16_SUBLANES):
        if T % n2:
            continue
        n1 = T // n2
        r1 = n1 * BF16_SUBLANES
        cost = (4 * _round_up(r1, MXU_DEPTH) * r1 * (n2 // BF16_SUBLANES)
                + 2 * _round_up(n2, MXU_DEPTH) * n2 * n1)
        if best is None or cost < best[0]:
            best = (cost, n1, n2)
    assert best is not None, "sequence length must be a multiple of 16"
    return best[1], best[2]


def _fft1_body(z_ref, mc_ref, ms_ref, twc_ref, tws_ref, a_ref, *, n1, R, C):
    rows = n1 * R
    rhs = jnp.concatenate([z_ref[0, 0].reshape(rows, C), z_ref[0, 1].reshape(rows, C)], axis=1)
    xc = jnp.dot(mc_ref[...], rhs, preferred_element_type=_F32)
    xs = jnp.dot(ms_ref[...], rhs, preferred_element_type=_F32)
    ar = xc[:, :C] + xs[:, C:]
    ai = xc[:, C:] - xs[:, :C]
    twc = jnp.concatenate([twc_ref[0]] * (C // LANES), axis=1)
    tws = jnp.concatenate([tws_ref[0]] * (C // LANES), axis=1)
    a_ref[0, 0] = (ar * twc + ai * tws).reshape(n1, R, C).astype(_BF16)
    a_ref[0, 1] = (ai * twc - ar * tws).reshape(n1, R, C).astype(_BF16)


def _fft2_body(ar_ref, ai_ref, mc_ref, ms_ref, f_ref):
    for b in range(ar_ref.shape[0]):
        res = (jnp.dot(mc_ref[0], ar_ref[b, 0, 0], preferred_element_type=_F32)
               + jnp.dot(ms_ref[0], ai_ref[b, 0, 0], preferred_element_type=_F32))
        f_ref[b] = res.astype(_BF16)


def _fft_tables(T, n1, n2, n_meta, R):
    two_pi = 2.0 * np.pi
    k1 = jnp.arange(n1, dtype=jnp.int32)
    ang1 = ((k1[:, None] * k1[None, :]) % n1).astype(_F32) * (two_pi / n1)
    eye = jnp.eye(R, dtype=_F32)
    m1c = jnp.kron(jnp.cos(ang1) * (n1 ** -0.5), eye).astype(_BF16)
    m1s = jnp.kron(jnp.sin(ang1) * (n1 ** -0.5), eye).astype(_BF16)
    nn2 = jnp.arange(n2, dtype=jnp.int32).reshape(n2 // R, 1, R)
    angt = ((k1[None, :, None] * (nn2 + n_meta)) % T).astype(_F32) * (two_pi / T)
    angt = jnp.broadcast_to(angt.reshape(n2 // R, n1 * R, 1), (n2 // R, n1 * R, LANES))
    twc, tws = jnp.cos(angt), jnp.sin(angt)
    k2 = jnp.arange(n2, dtype=jnp.int32)[:, None]
    mm2 = jnp.arange(n2, dtype=jnp.int32)[None, :]
    outs_c, outs_s = [], []
    for rot in (n_meta // n1, n_meta // n1 + 1):
        ph2 = (((k2 + rot) % n2) * (mm2 + n_meta)) % n2
        ang2 = ph2.astype(_F32) * (two_pi / n2)
        outs_c.append(jnp.cos(ang2) * (n2 ** -0.5))
        outs_s.append(jnp.sin(ang2) * (n2 ** -0.5))
    return m1c, m1s, twc, tws, jnp.stack(outs_c).astype(_BF16), jnp.stack(outs_s).astype(_BF16)


def _position_dft(z, tables, *, n1, n2, R, n_meta):
    B, _, T, C = z.shape
    m1c, m1s, twc, tws, m2c, m2s = tables
    rows = n1 * R
    zv = z.reshape(B, 2, n1, n2, C)
    av = pl.pallas_call(
        functools.partial(_fft1_body, n1=n1, R=R, C=C),
        grid=(B, n2 // R),
        in_specs=[
            pl.BlockSpec((1, 2, n1, R, C), lambda b, t: (b, 0, 0, t, 0)),
            _const_spec((rows, rows)),
            _const_spec((rows, rows)),
            pl.BlockSpec((1, rows, LANES), lambda b, t: (t, 0, 0)),
            pl.BlockSpec((1, rows, LANES), lambda b, t: (t, 0, 0)),
        ],
        out_specs=pl.BlockSpec((1, 2, n1, R, C), lambda b, t: (b, 0, 0, t, 0)),
        out_shape=jax.ShapeDtypeStruct((B, 2, n1, n2, C), _BF16),
        compiler_params=_cparams(("parallel", "arbitrary"),
                                 4 * 2 * rows * C * 2 + 2 * 2 * rows * rows * 2 + 6 * rows * C * 4 + VMEM_RESERVE_BYTES),
        name="dft_stage1",
    )(zv, m1c, m1s, twc, tws)
    shift = n1 - (n_meta % n1)

    f = pl.pallas_call(
        _fft2_body,
        grid=(n1,),
        in_specs=[
            pl.BlockSpec((B, 1, 1, n2, C), lambda k: (0, 0, k, 0, 0)),
            pl.BlockSpec((B, 1, 1, n2, C), lambda k: (0, 1, k, 0, 0)),
            pl.BlockSpec((1, n2, n2), lambda k: (jnp.where(k < n_meta % n1, 1, 0), 0, 0)),
            pl.BlockSpec((1, n2, n2), lambda k: (jnp.where(k < n_meta % n1, 1, 0), 0, 0)),
        ],
        out_specs=pl.BlockSpec((B, n2, C), lambda k: (0, 0, (k + shift) % n1)),
        out_shape=jax.ShapeDtypeStruct((B, n2, n1 * C), _BF16),
        compiler_params=_cparams(("arbitrary",), SMALL_CALL_VMEM_BYTES),
        name="dft_stage2",
    )(av, av, m2c, m2s)
    return f.reshape(B, T, C)


def _mix_body(*refs, tn, n_alias):
    h_ref, xn_ref, a_ref, f_ref, wpa_ref, wpf_ref, wg_ref, wo_ref = refs[:8]
    (o_ref,) = refs[8 + n_alias:]
    D = h_ref.shape[-1]
    o_ref[...] = h_ref[...]
    xn = xn_ref[...].reshape(-1, D)
    a = a_ref[...].reshape(-1, Q_DIM)
    f = f_ref[...].reshape(-1, FOURIER_DIM)
    for j in range(D // tn):
        cols = slice(j * tn, (j + 1) * tn)
        av = jnp.dot(a, wpa_ref[:, cols], preferred_element_type=_F32)
        fv = jnp.dot(f, wpf_ref[:, cols], preferred_element_type=_F32)
        ga = jax.nn.sigmoid(jnp.dot(xn, wg_ref[:, cols], preferred_element_type=_F32))
        gf = jax.nn.sigmoid(jnp.dot(xn, wg_ref[:, D + j * tn:D + (j + 1) * tn], preferred_element_type=_F32))
        mixed = (ga * av + gf * fv).astype(_BF16)
        o_ref[...] += jnp.dot(mixed, wo_ref[cols, :], preferred_element_type=_F32).reshape(o_ref.shape)


def _mix_out(rows, h, xn, attn, four, wpa, wpf, wg, wo, *, layer, tn, prev_out=None):
    B, T, D = h.shape

    def resident(k, n):
        return pl.BlockSpec((None, k, n), lambda *_: (layer, 0, 0), pipeline_mode=pl.Buffered(1))

    in_specs = [rows.spec(D), rows.spec(D), rows.spec(Q_DIM), rows.spec(FOURIER_DIM),
                resident(Q_DIM, D), resident(FOURIER_DIM, D), resident(D, 2 * D), resident(D, D)]
    args = [h, xn, attn, four, wpa, wpf, wg, wo]
    aliases = {}
    if prev_out is not None:
        extra, aliases = _alias_tail(len(args), [prev_out])
        in_specs += extra
        args.append(prev_out)
    m = rows.m
    vmem = ((Q_DIM + FOURIER_DIM + 3 * D) * D * 2 + 4 * m * D * 4 + 2 * m * D * 2
            + 2 * m * (Q_DIM + FOURIER_DIM) * 2 + 10 * m * tn * 4 + VMEM_RESERVE_BYTES)
    return pl.pallas_call(
        functools.partial(_mix_body, tn=tn, n_alias=len(aliases)),
        grid=rows.grid,
        in_specs=in_specs,
        out_specs=rows.spec(D),
        out_shape=jax.ShapeDtypeStruct((B, T, D), _F32),
        input_output_aliases=aliases,
        compiler_params=_cparams(("parallel", "arbitrary"), vmem),
        name="mixer_out" + ("_meta" if rows.meta else ""),
    )(*args)


def _rope_tables(n_real, n_meta):
    T = n_real + n_meta
    half = ROT_DIM // 2
    inv = 1.0 / (ROPE_THETA ** (jnp.arange(0, ROT_DIM, 2, dtype=_F32) / ROT_DIM))
    ang = jnp.arange(T, dtype=_F32)[:, None] * inv[None, :]
    cos, sin = jnp.cos(ang), jnp.sin(ang)
    cos = jnp.concatenate([cos[n_meta:], cos[:n_meta]], axis=0)
    sin = jnp.concatenate([sin[n_meta:], sin[:n_meta]], axis=0)
    ones = jnp.ones((T, HEAD_DIM - ROT_DIM), _F32)
    zeros_h = jnp.zeros((T, half), _F32)
    zeros_r = jnp.zeros((T, HEAD_DIM - ROT_DIM), _F32)
    c = jnp.concatenate([cos, cos, ones], axis=1)
    s1 = jnp.concatenate([-sin, zeros_h, zeros_r], axis=1)
    s2 = jnp.concatenate([zeros_h, sin, zeros_r], axis=1)
    return c, s1, s2


def _channel_dft_table():
    n = FOURIER_GROUP_DIM
    c = jnp.arange(n, dtype=jnp.int32)[:, None]
    k = jnp.arange(n, dtype=jnp.int32)[None, :]
    ang = ((c * k) % n).astype(_F32) * (2.0 * np.pi / n)
    return (jnp.concatenate([jnp.cos(ang), -jnp.sin(ang)], axis=1) * (n ** -0.5)).astype(_BF16)


def _pick_tile(n, pref):
    t = pref
    while n % t:
        t //= 2
    return t


def kernel(x, meta_tokens, ffn1_norm, ffn1_w_gate, ffn1_w_up, ffn1_w_down, mix_norm, w_in, w_gate, sink,
           w_attn_branch, w_fourier_branch, w_out, ffn2_norm, ffn2_w_gate, ffn2_w_up, ffn2_w_down, final_norm):
    B, S, D = x.shape
    n_meta = meta_tokens.shape[0]
    depth = ffn1_norm.shape[0]
    d_ff = ffn1_w_gate.shape[-1]
    T = S + n_meta
    assert S % (4 * BLOCK) == 0 and S % n_meta == 0 and BLOCK % n_meta == 0 and n_meta < BLOCK
    assert w_in.shape[-1] == IN_DIM and KV_DIM == PROJ_CHUNK

    tf = _pick_tile(d_ff, FF_CHUNK)
    tn = _pick_tile(D, MIX_CHUNK)
    qb = _pick_tile(S // BLOCK, ATTN_BLOCKS_PER_STEP)
    rows_ffn = _Rows(B, S, n_meta, _pick_tile(S, FFN_ROW_TILE), meta=False)
    rows_main = _Rows(B, S, n_meta, _pick_tile(S, PROJ_ROW_TILE), meta=False)
    rows_mix = _Rows(B, S, n_meta, _pick_tile(S, MIX_ROW_TILE), meta=False)
    rows_meta = _Rows(B, S, n_meta, 0, meta=True)

    n1, n2 = _fft_factors(T)
    fft_tables = _fft_tables(T, n1, n2, n_meta, BF16_SUBLANES)
    rope_c, rope_s1, rope_s2 = _rope_tables(S, n_meta)
    dft_c = _channel_dft_table()

    ffn1_w = tuple(_to_bf16(w) for w in (ffn1_w_gate, ffn1_w_up, ffn1_w_down))
    w_in_b = _to_bf16(w_in)

    def ffn(h, h_meta, gain, wts, layer, final_gain=None):
        out = _ffn(rows_ffn, h, gain, *wts, layer=layer, tf=tf, T=T, final_gain=final_gain)
        if final_gain is None:
            src, blk0 = h_meta
            out = _ffn(rows_meta, src, gain, *wts, layer=layer, tf=tf, T=T, h_row_block0=blk0, prev_out=out)
        return out

    meta = jnp.broadcast_to(meta_tokens[None].astype(x.dtype), (B, n_meta, D))
    h = None
    for l in range(depth):
        if l == 0:
            h = ffn(x, (meta, 0), ffn1_norm[l][None], ffn1_w, l)
        else:
            h = ffn(h, (h, None), ffn1_norm[l][None], ffn1_w, l)

        gain = mix_norm[l][None]
        proj_args = (gain, w_in_b, rope_c, rope_s1, rope_s2, dft_c)
        if l == 0:
            qkv, z, xn, *ffn2_w = _proj(rows_main, h, *proj_args, layer=l,
                                        side_cast=(ffn2_w_gate, ffn2_w_up, ffn2_w_down))
        else:
            qkv, z, xn = _proj(rows_main, h, *proj_args, layer=l)
        qkv, z, xn = _proj(rows_meta, h, *proj_args, layer=l, prev_out=(qkv, z, xn))
        if l == 0:
            attn, mix_w = _attention(qkv, sink[l], n_real=S, n_meta=n_meta, qb=qb,
                                     side_cast=(w_attn_branch, w_fourier_branch, w_gate, w_out))
        else:
            attn, _ = _attention(qkv, sink[l], n_real=S, n_meta=n_meta, qb=qb)
        four = _position_dft(z, fft_tables, n1=n1, n2=n2, R=BF16_SUBLANES, n_meta=n_meta)
        mix_args = (xn, attn, four) + tuple(mix_w)
        h2 = _mix_out(rows_mix, h, *mix_args, layer=l, tn=tn)
        h = _mix_out(rows_meta, h, *mix_args, layer=l, tn=tn, prev_out=h2)

        last = l == depth - 1
        h = ffn(h, (h, None), ffn2_norm[l][None], ffn2_w, l, final_gain=final_norm[None] if last else None)
    return h
```

```python
import functools

import numpy as np
import jax
import jax.numpy as jnp
from jax import lax
from jax.experimental import pallas as pl
from jax.experimental.pallas import tpu as pltpu

_F32 = jnp.float32
_BF16 = jnp.bfloat16

HEAD_DIM = 128
N_Q_HEADS = 16
N_KV_HEADS = 4
Q_PER_KV = N_Q_HEADS // N_KV_HEADS
WINDOW = 128
BLOCK = 128
ROPE_THETA = 500000.0
ROT_DIM = HEAD_DIM // 4
N_FOURIER_GROUPS = 8
FOURIER_GROUP_DIM = 128
FOURIER_DIM = N_FOURIER_GROUPS * FOURIER_GROUP_DIM
Q_DIM = N_Q_HEADS * HEAD_DIM
KV_DIM = N_KV_HEADS * HEAD_DIM
IN_DIM = Q_DIM + 2 * KV_DIM + FOURIER_DIM
RMS_EPS = 1e-6
NEG_INF = -1e30

LANES = 128
BF16_SUBLANES = 16
PROJ_CHUNK = Q_PER_KV * HEAD_DIM
V7X_VMEM_BYTES = 64 * 1024 * 1024
VMEM_RESERVE_BYTES = 4 << 20
SMALL_CALL_VMEM_BYTES = 32 << 20
CAST_BLOCK_BYTES = 6 << 20
LOG2_E = 1.4426950408889634
MXU_DEPTH = 256

FFN_ROW_TILE = 1024
FF_CHUNK = 512
FFN_COL_TILE = 256
DFT_COL_TILE = 256
PROJ_ROW_TILE = 512
MIX_ROW_TILE = 256
MIX_CHUNK = 1024
ATTN_BLOCKS_PER_STEP = 32


def _round_up(a, m):
    return (a + m - 1) // m * m


def _cparams(sem, vmem_bytes):
    return pltpu.CompilerParams(dimension_semantics=sem,
                                vmem_limit_bytes=int(min(vmem_bytes, V7X_VMEM_BYTES - VMEM_RESERVE_BYTES)))


def _rms(x, g):
    return x * lax.rsqrt(jnp.mean(x * x, axis=-1, keepdims=True) + RMS_EPS) * g


class _Rows:
    def __init__(self, B, n_real, n_meta, tm, meta):
        self.B, self.n_real, self.n_meta, self.meta = B, n_real, n_meta, meta
        self.nb = B if meta else 1
        self.rows = n_meta if meta else tm
        self.grid = (1, 1) if meta else (B, n_real // tm)
        self.m = self.nb * self.rows

    def spec(self, width, lead=(), row_block0=None):
        nl = len(lead)
        if self.meta:
            r0 = self.n_real // self.n_meta if row_block0 is None else row_block0
            return pl.BlockSpec((self.nb,) + tuple(lead) + (self.rows, width),
                                lambda b, i, *_: (0,) + (0,) * nl + (r0, 0))
        return pl.BlockSpec((1,) + tuple(lead) + (self.rows, width),
                            lambda b, i, *_: (b,) + (0,) * nl + (i, 0))

    def table_spec(self, width):
        if self.meta:
            r0 = self.n_real // self.n_meta
            return pl.BlockSpec((self.rows, width), lambda b, i, *_: (r0, 0))
        return pl.BlockSpec((self.rows, width), lambda b, i, *_: (i, 0))


def _const_spec(shape, single_buffer=False):
    nd = len(shape)
    if single_buffer:
        return pl.BlockSpec(shape, lambda *_: (0,) * nd, pipeline_mode=pl.Buffered(1))
    return pl.BlockSpec(shape, lambda *_: (0,) * nd)


def _alias_tail(n_in, outs):
    specs = [pl.BlockSpec(memory_space=pl.ANY) for _ in outs]
    return specs, {n_in + k: k for k in range(len(outs))}


def _side_cast(stacks, n_steps, step_index):
    specs, operands, shapes, vmem = [], [], [], 0
    for w in stacks:
        L, K, N = w.shape
        rps = L * K // n_steps
        assert L * K % n_steps == 0 and rps % BF16_SUBLANES == 0
        specs.append(pl.BlockSpec((rps, N), lambda *idx: (step_index(*idx), 0)))
        operands.append(w.reshape(L * K, N))
        shapes.append(jax.ShapeDtypeStruct((L * K, N), _BF16))
        vmem += 2 * rps * N * (4 + 2)
    return specs, operands, shapes, vmem


def _cast_blocks(src_refs, dst_refs):
    for src_ref, dst_ref in zip(src_refs, dst_refs):
        dst_ref[...] = src_ref[...].astype(_BF16)


def _cast_body(w_ref, o_ref):
    o_ref[...] = w_ref[...].astype(_BF16)


def _to_bf16(w):
    L, K, N = w.shape
    tk = _pick_tile(K, 512)
    while tk > 8 and tk * N * 4 > CAST_BLOCK_BYTES:
        tk //= 2
    return pl.pallas_call(
        _cast_body,
        grid=(L, K // tk),
        in_specs=[pl.BlockSpec((1, tk, N), lambda l, i: (l, i, 0))],
        out_specs=pl.BlockSpec((1, tk, N), lambda l, i: (l, i, 0)),
        out_shape=jax.ShapeDtypeStruct(w.shape, _BF16),
        compiler_params=_cparams(("parallel", "parallel"), 2 * tk * N * (4 + 2) + VMEM_RESERVE_BYTES),
        name="cast_bf16",
    )(w)


def _ffn_body(*refs, nj, final, n_alias):
    h_ref, gain_ref, wg_ref, wu_ref, wd_ref = refs[:5]
    rest = refs[5:]
    fgain_ref = None
    if final:
        fgain_ref, rest = rest[0], rest[1:]
    o_ref, xn_ref = rest[n_alias:]
    j = pl.program_id(2)
    D = h_ref.shape[-1]

    @pl.when(j == 0)
    def _():
        x = h_ref[...].reshape(-1, D)
        xn_ref[...] = _rms(x, gain_ref[...]).astype(_BF16)
        o_ref[...] = x.reshape(o_ref.shape)

    tf = wg_ref.shape[-1]
    ct = min(FFN_COL_TILE, tf)
    acts = []
    for t in range(tf // ct):
        cols = slice(t * ct, (t + 1) * ct)
        g = jnp.dot(xn_ref[...], wg_ref[:, cols], preferred_element_type=_F32)
        u = jnp.dot(xn_ref[...], wu_ref[:, cols], preferred_element_type=_F32)
        acts.append(((g * jax.nn.sigmoid(g)) * u * 0.5).astype(_BF16))
    act = jnp.concatenate(acts, axis=1)
    o_ref[...] += jnp.dot(act, wd_ref[...], preferred_element_type=_F32).reshape(o_ref.shape)

    if final:
        @pl.when(j == nj - 1)
        def _():
            o_ref[...] = _rms(o_ref[...].reshape(-1, D), fgain_ref[...]).reshape(o_ref.shape)


def _ffn(rows, h, gain, wg, wu, wd, *, layer, tf, T, final_gain=None, h_row_block0=None, prev_out=None):
    D = h.shape[-1]
    nj = wg.shape[2] // tf
    final = final_gain is not None
    in_specs = [rows.spec(D, row_block0=h_row_block0), _const_spec((1, D)),
                pl.BlockSpec((None, D, tf), lambda b, i, j: (layer, 0, j)),
                pl.BlockSpec((None, D, tf), lambda b, i, j: (layer, 0, j)),
                pl.BlockSpec((None, tf, D), lambda b, i, j: (layer, j, 0))]
    args = [h, gain, wg, wu, wd]
    if final:
        in_specs.append(_const_spec((1, D)))
        args.append(final_gain)
    aliases = {}
    if prev_out is not None:
        extra, aliases = _alias_tail(len(args), [prev_out])
        in_specs += extra
        args.append(prev_out)
    m = rows.m
    vmem = (4 * m * D * 4 + m * D * 2 + 2 * (D * 2 * tf + tf * D) * 2 + m * 2 * tf * 4 + m * tf * 2
            + VMEM_RESERVE_BYTES)
    out_rows = rows.n_real if final else T
    return pl.pallas_call(
        functools.partial(_ffn_body, nj=nj, final=final, n_alias=len(aliases)),
        grid=rows.grid + (nj,),
        in_specs=in_specs,
        out_specs=rows.spec(D),
        out_shape=jax.ShapeDtypeStruct((rows.B, out_rows, D), _F32),
        scratch_shapes=[pltpu.VMEM((m, D), _BF16)],
        input_output_aliases=aliases,
        compiler_params=_cparams(("parallel", "parallel", "arbitrary"), vmem),
        name=("ffn_final" if final else "ffn") + ("_meta" if rows.meta else ""),
    )(*args)


def _proj_body(*refs, n_rope, n_qkv, n_chunks, nb, n_alias, n_side):
    h_ref, gain_ref, w_ref, c_ref, s1_ref, s2_ref, dft_ref = refs[:7]
    side_in = refs[7:7 + n_side]
    qkv_ref, z_ref, xno_ref = refs[7 + n_side + n_alias:10 + n_side + n_alias]
    side_out = refs[10 + n_side + n_alias:]
    _cast_blocks(side_in, side_out)
    D = h_ref.shape[-1]
    rows = h_ref.shape[-2]
    xno_ref[...] = _rms(h_ref[...].reshape(-1, D), gain_ref[...]).astype(_BF16).reshape(xno_ref.shape)
    c = jnp.concatenate([c_ref[...]] * nb, axis=0)
    s1 = jnp.concatenate([s1_ref[...]] * nb, axis=0)
    s2 = jnp.concatenate([s2_ref[...]] * nb, axis=0)
    half = ROT_DIM // 2
    for ch in range(n_chunks):
        cols = slice(ch * PROJ_CHUNK, (ch + 1) * PROJ_CHUNK)
        y = jnp.dot(xno_ref[...].reshape(-1, D), w_ref[:, cols], preferred_element_type=_F32)
        if ch < n_rope:
            for hh in range(PROJ_CHUNK // HEAD_DIM):
                xh = y[:, hh * HEAD_DIM:(hh + 1) * HEAD_DIM]
                up = pltpu.roll(xh, HEAD_DIM - half, 1)
                dn = pltpu.roll(xh, half, 1)
                r = (xh * c + up * s1 + dn * s2).astype(_BF16)
                lo = ch * PROJ_CHUNK + hh * HEAD_DIM
                qkv_ref[:, :, lo:lo + HEAD_DIM] = r.reshape(nb, rows, HEAD_DIM)
        elif ch < n_qkv:
            qkv_ref[:, :, cols] = y.astype(_BF16).reshape(nb, rows, PROJ_CHUNK)
        else:
            ub = y.astype(_BF16)
            for gg in range(PROJ_CHUNK // FOURIER_GROUP_DIM):
                zz = jnp.dot(ub[:, gg * FOURIER_GROUP_DIM:(gg + 1) * FOURIER_GROUP_DIM], dft_ref[...],
                             preferred_element_type=_F32)
                lo = (ch - n_qkv) * PROJ_CHUNK + gg * FOURIER_GROUP_DIM
                z_ref[:, 0, :, lo:lo + FOURIER_GROUP_DIM] = (
                    zz[:, :FOURIER_GROUP_DIM].astype(_BF16).reshape(nb, rows, FOURIER_GROUP_DIM))
                z_ref[:, 1, :, lo:lo + FOURIER_GROUP_DIM] = (
                    zz[:, FOURIER_GROUP_DIM:].astype(_BF16).reshape(nb, rows, FOURIER_GROUP_DIM))


def _proj(rows, h, gain, w_in, rope_c, rope_s1, rope_s2, dft_c, *, layer, prev_out=None, side_cast=()):
    B, T, D = h.shape
    n_rope = (Q_DIM + KV_DIM) // PROJ_CHUNK
    n_qkv = (Q_DIM + 2 * KV_DIM) // PROJ_CHUNK
    n_chunks = IN_DIM // PROJ_CHUNK
    w_spec = pl.BlockSpec((None, D, IN_DIM), lambda *_: (layer, 0, 0), pipeline_mode=pl.Buffered(1))
    in_specs = [rows.spec(D), _const_spec((1, D)), w_spec,
                rows.table_spec(LANES), rows.table_spec(LANES), rows.table_spec(LANES),
                _const_spec((FOURIER_GROUP_DIM, 2 * FOURIER_GROUP_DIM))]
    args = [h, gain, w_in, rope_c, rope_s1, rope_s2, dft_c]
    out_specs = [rows.spec(Q_DIM + 2 * KV_DIM), rows.spec(FOURIER_DIM, lead=(2,)), rows.spec(D)]
    out_shape = [jax.ShapeDtypeStruct((B, T, Q_DIM + 2 * KV_DIM), _BF16),
                 jax.ShapeDtypeStruct((B, 2, T, FOURIER_DIM), _BF16),
                 jax.ShapeDtypeStruct((B, T, D), _BF16)]
    m = rows.m
    vmem = (2 * m * D * 4 + 2 * m * D * 2 + D * IN_DIM * 2 + 6 * m * LANES * 4
            + 2 * m * (Q_DIM + 2 * KV_DIM) * 2 + 4 * m * FOURIER_DIM * 2 + 8 * m * PROJ_CHUNK * 4
            + VMEM_RESERVE_BYTES)
    side_specs, side_ops, side_shapes, side_vmem = _side_cast(
        side_cast, rows.grid[0] * rows.grid[1], lambda b, i: b * rows.grid[1] + i)
    in_specs += side_specs
    args += side_ops
    out_specs += side_specs
    out_shape += side_shapes
    vmem += side_vmem
    aliases = {}
    if prev_out is not None:
        extra, aliases = _alias_tail(len(args), prev_out)
        in_specs += extra
        args += list(prev_out)
    outs = pl.pallas_call(
        functools.partial(_proj_body, n_rope=n_rope, n_qkv=n_qkv, n_chunks=n_chunks, nb=rows.nb,
                          n_alias=len(aliases), n_side=len(side_cast)),
        grid=rows.grid,
        in_specs=in_specs,
        out_specs=out_specs,
        out_shape=out_shape,
        input_output_aliases=aliases,
        compiler_params=_cparams(("parallel", "arbitrary"), vmem),
        name="mixer_proj" + ("_meta" if rows.meta else ""),
    )(*args)
    return tuple(outs[:3]) + tuple(o.reshape(w.shape) for o, w in zip(outs[3:], side_cast))


def _stack_heads(q):
    return jnp.concatenate([q[:, h * HEAD_DIM:(h + 1) * HEAD_DIM] for h in range(Q_PER_KV)], axis=0)


def _sink_column(sink_ref, g, rows_per_head):
    return jnp.concatenate(
        [jnp.full((rows_per_head, 1), sink_ref[g * Q_PER_KV + h], _F32) for h in range(Q_PER_KV)], axis=0)


def _pad_rows(a, rows):
    return jnp.concatenate([a, jnp.zeros((rows - a.shape[0], a.shape[1]), a.dtype)], axis=0)


def _softmax_pv(q4, k_all, v_all, masks, sink):
    s = lax.dot_general(q4, k_all, (((1,), (1,)), ((), ())), preferred_element_type=_F32) * (HEAD_DIM ** -0.5)
    blocks = []
    for i, mk in enumerate(masks):
        sb = s[:, i * BLOCK:(i + 1) * BLOCK]
        blocks.append(sb if mk is None else jnp.where(mk, sb, NEG_INF))
    mx = jnp.maximum(jnp.max(functools.reduce(jnp.maximum, blocks), axis=-1, keepdims=True), sink)
    p = jnp.concatenate([jnp.exp(sb - mx).astype(_BF16) for sb in blocks], axis=1)
    v_ext = jnp.concatenate([v_all, jnp.ones(v_all.shape, _BF16)], axis=1)
    o = jnp.dot(p, v_ext, preferred_element_type=_F32)
    den = o[:, HEAD_DIM:] + jnp.exp(sink - mx)
    return o[:, :HEAD_DIM] / den


def _attn_body(sink_ref, *refs, qb, ntile, n_meta, n_side):
    q_ref, kp_ref, kc_ref, kn_ref, km_ref, vp_ref, vc_ref, vn_ref, vm_ref = refs[:9]
    o_ref = refs[9 + n_side]
    kbuf, vbuf, s_buf, p_buf = refs[10 + 2 * n_side:]
    _cast_blocks(refs[9:9 + n_side], refs[10 + n_side:10 + 2 * n_side])
    g = pl.program_id(1)
    jt = pl.program_id(2)
    rows = Q_PER_KV * BLOCK
    band = 3 * BLOCK
    meta0 = (qb + 2) * BLOCK
    scale = HEAD_DIM ** -0.5
    c_exp2 = scale * LOG2_E

    kbuf[0:BLOCK] = kp_ref[0]
    kbuf[BLOCK:(qb + 1) * BLOCK] = kc_ref[0]
    kbuf[(qb + 1) * BLOCK:meta0] = kn_ref[0]
    kbuf[meta0:meta0 + BLOCK] = _pad_rows(km_ref[0], BLOCK)
    vbuf[0:BLOCK, 0:HEAD_DIM] = vp_ref[0]
    vbuf[BLOCK:(qb + 1) * BLOCK, 0:HEAD_DIM] = vc_ref[0]
    vbuf[(qb + 1) * BLOCK:meta0, 0:HEAD_DIM] = vn_ref[0]
    vbuf[meta0:meta0 + BLOCK, 0:HEAD_DIM] = _pad_rows(vm_ref[0], BLOCK)
    vbuf[:, HEAD_DIM:] = jnp.ones((meta0 + BLOCK, HEAD_DIM), _BF16)

    row = lax.broadcasted_iota(jnp.int32, (rows, BLOCK), 0) % BLOCK
    col = lax.broadcasted_iota(jnp.int32, (rows, BLOCK), 1)
    diff = col - row
    never = 2 * BLOCK
    sink_s = _sink_column(sink_ref, g, BLOCK) * (1.0 / scale)
    meta_fill = jnp.where(col == n_meta, sink_s, NEG_INF)
    meta_ok = col < n_meta

    def dot_t(a, b):
        return lax.dot_general(a, b, (((1,), (1,)), ((), ())), preferred_element_type=_F32)

    def stage_a(i, slot):
        r0 = pl.multiple_of(i * BLOCK, BLOCK)
        q4 = _stack_heads(q_ref[0, pl.ds(r0, BLOCK), :])
        s_buf[slot, :, 0:band] = dot_t(q4, kbuf[pl.ds(r0, band), :])
        s_buf[slot, :, band:] = dot_t(q4, kbuf[meta0:meta0 + BLOCK, :])

    def stage_b(i, slot):
        first = jnp.logical_and(jt == 0, i == 0)
        last = jnp.logical_and(jt == ntile - 1, i == qb - 1)
        off_p = jnp.where(first, never, 0)
        off_n = jnp.where(last, -never, 0)
        blocks = [jnp.where(diff >= off_p, s_buf[slot, :, 0:BLOCK], NEG_INF),
                  s_buf[slot, :, BLOCK:2 * BLOCK],
                  jnp.where(diff <= off_n, s_buf[slot, :, 2 * BLOCK:band], NEG_INF),
                  jnp.where(meta_ok, s_buf[slot, :, band:], meta_fill)]
        mx = jnp.max(functools.reduce(jnp.maximum, blocks), axis=-1, keepdims=True)
        for n, sb in enumerate(blocks):
            p_buf[slot, :, n * BLOCK:(n + 1) * BLOCK] = jnp.exp2((sb - mx) * c_exp2).astype(_BF16)

    def stage_c(i, slot):
        r0 = pl.multiple_of(i * BLOCK, BLOCK)
        o = (jnp.dot(p_buf[slot, :, 0:band], vbuf[pl.ds(r0, band), :], preferred_element_type=_F32)
             + jnp.dot(p_buf[slot, :, band:], vbuf[meta0:meta0 + BLOCK, :], preferred_element_type=_F32))
        out = o[:, :HEAD_DIM] / o[:, HEAD_DIM:]
        for h in range(Q_PER_KV):
            o_ref[0, pl.ds(r0, BLOCK), h * HEAD_DIM:(h + 1) * HEAD_DIM] = (
                out[h * BLOCK:(h + 1) * BLOCK].astype(_BF16))

    stage_a(0, 0)
    stage_a(1, 1)
    stage_b(0, 0)

    def pair(t, carry):
        i = 2 * t
        stage_c(i, 0)
        stage_b(i + 1, 1)
        stage_a(i + 2, 0)
        stage_c(i + 1, 1)
        stage_b(i + 2, 0)
        stage_a(i + 3, 1)
        return carry

    lax.fori_loop(0, (qb - 2) // 2, pair, 0)
    stage_c(qb - 2, 0)
    stage_b(qb - 1, 1)
    stage_c(qb - 1, 1)


def _attn_meta_body(sink_ref, q_ref, k0_ref, km_ref, v0_ref, vm_ref, prev_out_ref, o_ref, *, n_meta):
    del prev_out_ref
    g = pl.program_id(1)
    rows = Q_PER_KV * n_meta
    row = lax.broadcasted_iota(jnp.int32, (rows, BLOCK), 0) % n_meta
    col = lax.broadcasted_iota(jnp.int32, (rows, BLOCK), 1)
    q4 = _stack_heads(q_ref[0])
    masks = [col <= row + (WINDOW - n_meta), col < n_meta]
    k_all = jnp.concatenate([k0_ref[0], _pad_rows(km_ref[0], BLOCK)], axis=0)
    v_all = jnp.concatenate([v0_ref[0], _pad_rows(vm_ref[0], BLOCK)], axis=0)
    out = _softmax_pv(q4, k_all, v_all, masks, _sink_column(sink_ref, g, n_meta))
    for h in range(Q_PER_KV):
        o_ref[0, :, h * HEAD_DIM:(h + 1) * HEAD_DIM] = out[h * n_meta:(h + 1) * n_meta].astype(_BF16)


def _attention(qkv, sink, *, n_real, n_meta, qb, side_cast=()):
    B, T, _ = qkv.shape
    nblk = n_real // BLOCK
    ntile = nblk // qb
    kcol = Q_DIM // HEAD_DIM
    vcol = (Q_DIM + KV_DIM) // HEAD_DIM
    mrow = n_real // n_meta

    def edge_spec(rowf, col0):
        return pl.BlockSpec((1, BLOCK, HEAD_DIM), lambda b, g, j, s: (b, rowf(j), col0 + g))

    def main_spec(col0):
        return pl.BlockSpec((1, qb * BLOCK, HEAD_DIM), lambda b, g, j, s: (b, j, col0 + g))

    def meta_spec(col0):
        return pl.BlockSpec((1, n_meta, HEAD_DIM), lambda b, g, j, s: (b, mrow, col0 + g))

    def prev_i(j):
        return jnp.maximum(qb * j - 1, 0)

    def next_i(j):
        return jnp.minimum(qb * j + qb, nblk - 1)

    side_specs, side_ops, side_shapes, side_vmem = _side_cast(
        side_cast, B * N_KV_HEADS * ntile, lambda b, g, j, s: (b * N_KV_HEADS + g) * ntile + j)
    attn, *cast = pl.pallas_call(
        functools.partial(_attn_body, qb=qb, ntile=ntile, n_meta=n_meta, n_side=len(side_cast)),
        grid_spec=pltpu.PrefetchScalarGridSpec(
            num_scalar_prefetch=1,
            grid=(B, N_KV_HEADS, ntile),
            in_specs=[
                pl.BlockSpec((1, qb * BLOCK, PROJ_CHUNK), lambda b, g, j, s: (b, j, g)),
                edge_spec(prev_i, kcol), main_spec(kcol), edge_spec(next_i, kcol), meta_spec(kcol),
                edge_spec(prev_i, vcol), main_spec(vcol), edge_spec(next_i, vcol), meta_spec(vcol),
            ] + side_specs,
            out_specs=[pl.BlockSpec((1, qb * BLOCK, PROJ_CHUNK), lambda b, g, j, s: (b, j, g))] + side_specs,
            scratch_shapes=[
                pltpu.VMEM(((qb + 3) * BLOCK, HEAD_DIM), _BF16),
                pltpu.VMEM(((qb + 3) * BLOCK, 2 * HEAD_DIM), _BF16),
                pltpu.VMEM((2, Q_PER_KV * BLOCK, 4 * BLOCK), _F32),
                pltpu.VMEM((2, Q_PER_KV * BLOCK, 4 * BLOCK), _BF16),
            ],
        ),
        out_shape=[jax.ShapeDtypeStruct((B, T, Q_DIM), _BF16)] + side_shapes,
        compiler_params=_cparams(("parallel", "parallel", "arbitrary"), SMALL_CALL_VMEM_BYTES + side_vmem),
        name="windowed_gqa",
    )(sink, qkv, qkv, qkv, qkv, qkv, qkv, qkv, qkv, qkv, *side_ops)
    cast = [c.reshape(w.shape) for c, w in zip(cast, side_cast)]

    def blk0_spec(col0):
        return pl.BlockSpec((1, BLOCK, HEAD_DIM), lambda b, g, s: (b, 0, col0 + g))

    def meta2_spec(col0):
        return pl.BlockSpec((1, n_meta, HEAD_DIM), lambda b, g, s: (b, mrow, col0 + g))

    return pl.pallas_call(
        functools.partial(_attn_meta_body, n_meta=n_meta),
        grid_spec=pltpu.PrefetchScalarGridSpec(
            num_scalar_prefetch=1,
            grid=(B, N_KV_HEADS),
            in_specs=[
                pl.BlockSpec((1, n_meta, PROJ_CHUNK), lambda b, g, s: (b, mrow, g)),
                blk0_spec(kcol), meta2_spec(kcol), blk0_spec(vcol), meta2_spec(vcol),
                pl.BlockSpec(memory_space=pl.ANY),
            ],
            out_specs=pl.BlockSpec((1, n_meta, PROJ_CHUNK), lambda b, g, s: (b, mrow, g)),
        ),
        out_shape=jax.ShapeDtypeStruct((B, T, Q_DIM), _BF16),
        input_output_aliases={6: 0},
        compiler_params=_cparams(("parallel", "arbitrary"), SMALL_CALL_VMEM_BYTES),
        name="windowed_gqa_meta",
    )(sink, qkv, qkv, qkv, qkv, qkv, attn), cast


def _fft_factors(T):
    best = None
    for n2 in range(BF16_SUBLANES, T + 1, BF16_SUBLANES):
        if T % n2:
            continue
        n1 = T // n2
        r1 = n1 * BF16_SUBLANES
        cost = (4 * _round_up(r1, MXU_DEPTH) * r1 * (n2 // BF16_SUBLANES)
                + 2 * _round_up(n2, MXU_DEPTH) * n2 * n1)
        if best is None or cost < best[0]:
            best = (cost, n1, n2)
    assert best is not None, "sequence length must be a multiple of 16"
    return best[1], best[2]


def _fft1_body(z_ref, mc_ref, ms_ref, twc_ref, tws_ref, a_ref, *, n1, R, C):
    rows = n1 * R
    ct = min(DFT_COL_TILE, C)
    twc = jnp.concatenate([twc_ref[0]] * (ct // LANES), axis=1)
    tws = jnp.concatenate([tws_ref[0]] * (ct // LANES), axis=1)
    for t in range(C // ct):
        cols = slice(t * ct, (t + 1) * ct)
        rhs = jnp.concatenate([z_ref[0, 0, :, :, cols].reshape(rows, ct), z_ref[0, 1, :, :, cols].reshape(rows, ct)],
                              axis=1)
        xc = jnp.dot(mc_ref[...], rhs, preferred_element_type=_F32)
        xs = jnp.dot(ms_ref[...], rhs, preferred_element_type=_F32)
        ar = xc[:, :ct] + xs[:, ct:]
        ai = xc[:, ct:] - xs[:, :ct]
        a_ref[0, 0, :, :, cols] = (ar * twc + ai * tws).reshape(n1, R, ct).astype(_BF16)
        a_ref[0, 1, :, :, cols] = (ai * twc - ar * tws).reshape(n1, R, ct).astype(_BF16)


def _fft2_body(ar_ref, ai_ref, mc_ref, ms_ref, f_ref):
    for b in range(ar_ref.shape[0]):
        res = (jnp.dot(mc_ref[0], ar_ref[b, 0, 0], preferred_element_type=_F32)
               + jnp.dot(ms_ref[0], ai_ref[b, 0, 0], preferred_element_type=_F32))
        f_ref[b] = res.astype(_BF16)


def _fft_tables(T, n1, n2, n_meta, R):
    two_pi = 2.0 * np.pi
    k1 = jnp.arange(n1, dtype=jnp.int32)
    ang1 = ((k1[:, None] * k1[None, :]) % n1).astype(_F32) * (two_pi / n1)
    eye = jnp.eye(R, dtype=_F32)
    m1c = jnp.kron(jnp.cos(ang1) * (n1 ** -0.5), eye).astype(_BF16)
    m1s = jnp.kron(jnp.sin(ang1) * (n1 ** -0.5), eye).astype(_BF16)
    nn2 = jnp.arange(n2, dtype=jnp.int32).reshape(n2 // R, 1, R)
    angt = ((k1[None, :, None] * (nn2 + n_meta)) % T).astype(_F32) * (two_pi / T)
    angt = jnp.broadcast_to(angt.reshape(n2 // R, n1 * R, 1), (n2 // R, n1 * R, LANES))
    twc, tws = jnp.cos(angt), jnp.sin(angt)
    k2 = jnp.arange(n2, dtype=jnp.int32)[:, None]
    mm2 = jnp.arange(n2, dtype=jnp.int32)[None, :]
    outs_c, outs_s = [], []
    for rot in (n_meta // n1, n_meta // n1 + 1):
        ph2 = (((k2 + rot) % n2) * (mm2 + n_meta)) % n2
        ang2 = ph2.astype(_F32) * (two_pi / n2)
        outs_c.append(jnp.cos(ang2) * (n2 ** -0.5))
        outs_s.append(jnp.sin(ang2) * (n2 ** -0.5))
    return m1c, m1s, twc, tws, jnp.stack(outs_c).astype(_BF16), jnp.stack(outs_s).astype(_BF16)


def _position_dft(z, tables, *, n1, n2, R, n_meta):
    B, _, T, C = z.shape
    m1c, m1s, twc, tws, m2c, m2s = tables
    rows = n1 * R
    zv = z.reshape(B, 2, n1, n2, C)
    av = pl.pallas_call(
        functools.partial(_fft1_body, n1=n1, R=R, C=C),
        grid=(B, n2 // R),
        in_specs=[
            pl.BlockSpec((1, 2, n1, R, C), lambda b, t: (b, 0, 0, t, 0)),
            _const_spec((rows, rows)),
            _const_spec((rows, rows)),
            pl.BlockSpec((1, rows, LANES), lambda b, t: (t, 0, 0)),
            pl.BlockSpec((1, rows, LANES), lambda b, t: (t, 0, 0)),
        ],
        out_specs=pl.BlockSpec((1, 2, n1, R, C), lambda b, t: (b, 0, 0, t, 0)),
        out_shape=jax.ShapeDtypeStruct((B, 2, n1, n2, C), _BF16),
        compiler_params=_cparams(("parallel", "arbitrary"),
                                 4 * 2 * rows * C * 2 + 2 * 2 * rows * rows * 2 + 6 * rows * C * 4 + VMEM_RESERVE_BYTES),
        name="dft_stage1",
    )(zv, m1c, m1s, twc, tws)
    shift = n1 - (n_meta % n1)

    f = pl.pallas_call(
        _fft2_body,
        grid=(n1,),
        in_specs=[
            pl.BlockSpec((B, 1, 1, n2, C), lambda k: (0, 0, k, 0, 0)),
            pl.BlockSpec((B, 1, 1, n2, C), lambda k: (0, 1, k, 0, 0)),
            pl.BlockSpec((1, n2, n2), lambda k: (jnp.where(k < n_meta % n1, 1, 0), 0, 0)),
            pl.BlockSpec((1, n2, n2), lambda k: (jnp.where(k < n_meta % n1, 1, 0), 0, 0)),
        ],
        out_specs=pl.BlockSpec((B, n2, C), lambda k: (0, 0, (k + shift) % n1)),
        out_shape=jax.ShapeDtypeStruct((B, n2, n1 * C), _BF16),
        compiler_params=_cparams(("arbitrary",), SMALL_CALL_VMEM_BYTES),
        name="dft_stage2",
    )(av, av, m2c, m2s)
    return f.reshape(B, T, C)


def _mix_body(*refs, tn, n_alias):
    h_ref, xn_ref, a_ref, f_ref, wpa_ref, wpf_ref, wg_ref, wo_ref = refs[:8]
    (o_ref,) = refs[8 + n_alias:]
    D = h_ref.shape[-1]
    o_ref[...] = h_ref[...]
    xn = xn_ref[...].reshape(-1, D)
    a = a_ref[...].reshape(-1, Q_DIM)
    f = f_ref[...].reshape(-1, FOURIER_DIM)
    for j in range(D // tn):
        cols = slice(j * tn, (j + 1) * tn)
        av = jnp.dot(a, wpa_ref[:, cols], preferred_element_type=_F32)
        fv = jnp.dot(f, wpf_ref[:, cols], preferred_element_type=_F32)
        ga = jax.nn.sigmoid(jnp.dot(xn, wg_ref[:, cols], preferred_element_type=_F32))
        gf = jax.nn.sigmoid(jnp.dot(xn, wg_ref[:, D + j * tn:D + (j + 1) * tn], preferred_element_type=_F32))
        mixed = (ga * av + gf * fv).astype(_BF16)
        o_ref[...] += jnp.dot(mixed, wo_ref[cols, :], preferred_element_type=_F32).reshape(o_ref.shape)


def _mix_out(rows, h, xn, attn, four, wpa, wpf, wg, wo, *, layer, tn, prev_out=None):
    B, T, D = h.shape

    def resident(k, n):
        return pl.BlockSpec((None, k, n), lambda *_: (layer, 0, 0), pipeline_mode=pl.Buffered(1))

    in_specs = [rows.spec(D), rows.spec(D), rows.spec(Q_DIM), rows.spec(FOURIER_DIM),
                resident(Q_DIM, D), resident(FOURIER_DIM, D), resident(D, 2 * D), resident(D, D)]
    args = [h, xn, attn, four, wpa, wpf, wg, wo]
    aliases = {}
    if prev_out is not None:
        extra, aliases = _alias_tail(len(args), [prev_out])
        in_specs += extra
        args.append(prev_out)
    m = rows.m
    vmem = ((Q_DIM + FOURIER_DIM + 3 * D) * D * 2 + 4 * m * D * 4 + 2 * m * D * 2
            + 2 * m * (Q_DIM + FOURIER_DIM) * 2 + 10 * m * tn * 4 + VMEM_RESERVE_BYTES)
    return pl.pallas_call(
        functools.partial(_mix_body, tn=tn, n_alias=len(aliases)),
        grid=rows.grid,
        in_specs=in_specs,
        out_specs=rows.spec(D),
        out_shape=jax.ShapeDtypeStruct((B, T, D), _F32),
        input_output_aliases=aliases,
        compiler_params=_cparams(("parallel", "arbitrary"), vmem),
        name="mixer_out" + ("_meta" if rows.meta else ""),
    )(*args)


def _rope_tables(n_real, n_meta):
    T = n_real + n_meta
    half = ROT_DIM // 2
    inv = 1.0 / (ROPE_THETA ** (jnp.arange(0, ROT_DIM, 2, dtype=_F32) / ROT_DIM))
    ang = jnp.arange(T, dtype=_F32)[:, None] * inv[None, :]
    cos, sin = jnp.cos(ang), jnp.sin(ang)
    cos = jnp.concatenate([cos[n_meta:], cos[:n_meta]], axis=0)
    sin = jnp.concatenate([sin[n_meta:], sin[:n_meta]], axis=0)
    ones = jnp.ones((T, HEAD_DIM - ROT_DIM), _F32)
    zeros_h = jnp.zeros((T, half), _F32)
    zeros_r = jnp.zeros((T, HEAD_DIM - ROT_DIM), _F32)
    c = jnp.concatenate([cos, cos, ones], axis=1)
    s1 = jnp.concatenate([-sin, zeros_h, zeros_r], axis=1)
    s2 = jnp.concatenate([zeros_h, sin, zeros_r], axis=1)
    return c, s1, s2


def _channel_dft_table():
    n = FOURIER_GROUP_DIM
    c = jnp.arange(n, dtype=jnp.int32)[:, None]
    k = jnp.arange(n, dtype=jnp.int32)[None, :]
    ang = ((c * k) % n).astype(_F32) * (2.0 * np.pi / n)
    return (jnp.concatenate([jnp.cos(ang), -jnp.sin(ang)], axis=1) * (n ** -0.5)).astype(_BF16)


def _pick_tile(n, pref):
    t = pref
    while n % t:
        t //= 2
    return t


def kernel(x, meta_tokens, ffn1_norm, ffn1_w_gate, ffn1_w_up, ffn1_w_down, mix_norm, w_in, w_gate, sink,
           w_attn_branch, w_fourier_branch, w_out, ffn2_norm, ffn2_w_gate, ffn2_w_up, ffn2_w_down, final_norm):
    B, S, D = x.shape
    n_meta = meta_tokens.shape[0]
    depth = ffn1_norm.shape[0]
    d_ff = ffn1_w_gate.shape[-1]
    T = S + n_meta
    assert S % (4 * BLOCK) == 0 and S % n_meta == 0 and BLOCK % n_meta == 0 and n_meta < BLOCK
    assert w_in.shape[-1] == IN_DIM and KV_DIM == PROJ_CHUNK

    tf = _pick_tile(d_ff, FF_CHUNK)
    tn = _pick_tile(D, MIX_CHUNK)
    qb = _pick_tile(S // BLOCK, ATTN_BLOCKS_PER_STEP)
    rows_ffn = _Rows(B, S, n_meta, _pick_tile(S, FFN_ROW_TILE), meta=False)
    rows_main = _Rows(B, S, n_meta, _pick_tile(S, PROJ_ROW_TILE), meta=False)
    rows_mix = _Rows(B, S, n_meta, _pick_tile(S, MIX_ROW_TILE), meta=False)
    rows_meta = _Rows(B, S, n_meta, 0, meta=True)

    n1, n2 = _fft_factors(T)
    fft_tables = _fft_tables(T, n1, n2, n_meta, BF16_SUBLANES)
    rope_c, rope_s1, rope_s2 = _rope_tables(S, n_meta)
    dft_c = _channel_dft_table()

    ffn1_w = tuple(_to_bf16(w) for w in (ffn1_w_gate, ffn1_w_up, ffn1_w_down))
    w_in_b = _to_bf16(w_in)

    def ffn(h, h_meta, gain, wts, layer, final_gain=None):
        out = _ffn(rows_ffn, h, gain, *wts, layer=layer, tf=tf, T=T, final_gain=final_gain)
        if final_gain is None:
            src, blk0 = h_meta
            out = _ffn(rows_meta, src, gain, *wts, layer=layer, tf=tf, T=T, h_row_block0=blk0, prev_out=out)
        return out

    meta = jnp.broadcast_to(meta_tokens[None].astype(x.dtype), (B, n_meta, D))
    h = None
    for l in range(depth):
        if l == 0:
            h = ffn(x, (meta, 0), ffn1_norm[l][None], ffn1_w, l)
        else:
            h = ffn(h, (h, None), ffn1_norm[l][None], ffn1_w, l)

        gain = mix_norm[l][None]
        proj_args = (gain, w_in_b, rope_c, rope_s1, rope_s2, dft_c)
        if l == 0:
            qkv, z, xn, *ffn2_w = _proj(rows_main, h, *proj_args, layer=l,
                                        side_cast=(ffn2_w_gate, ffn2_w_up, ffn2_w_down))
        else:
            qkv, z, xn = _proj(rows_main, h, *proj_args, layer=l)
        qkv, z, xn = _proj(rows_meta, h, *proj_args, layer=l, prev_out=(qkv, z, xn))
        if l == 0:
            attn, mix_w = _attention(qkv, sink[l], n_real=S, n_meta=n_meta, qb=qb,
                                     side_cast=(w_attn_branch, w_fourier_branch, w_gate, w_out))
        else:
            attn, _ = _attention(qkv, sink[l], n_real=S, n_meta=n_meta, qb=qb)
        four = _position_dft(z, fft_tables, n1=n1, n2=n2, R=BF16_SUBLANES, n_meta=n_meta)
        mix_args = (xn, attn, four) + tuple(mix_w)
        h2 = _mix_out(rows_mix, h, *mix_args, layer=l, tn=tn)
        h = _mix_out(rows_meta, h, *mix_args, layer=l, tn=tn, prev_out=h2)

        last = l == depth - 1
        h = ffn(h, (h, None), ffn2_norm[l][None], ffn2_w, l, final_gain=final_norm[None] if last else None)
    return h
```

```python
import functools

import numpy as np
import jax
import jax.numpy as jnp
from jax import lax
from jax.experimental import pallas as pl
from jax.experimental.pallas import tpu as pltpu

_F32 = jnp.float32
_BF16 = jnp.bfloat16

HEAD_DIM = 128
N_Q_HEADS = 16
N_KV_HEADS = 4
Q_PER_KV = N_Q_HEADS // N_KV_HEADS
WINDOW = 128
BLOCK = 128
ROPE_THETA = 500000.0
ROT_DIM = HEAD_DIM // 4
N_FOURIER_GROUPS = 8
FOURIER_GROUP_DIM = 128
FOURIER_DIM = N_FOURIER_GROUPS * FOURIER_GROUP_DIM
Q_DIM = N_Q_HEADS * HEAD_DIM
KV_DIM = N_KV_HEADS * HEAD_DIM
IN_DIM = Q_DIM + 2 * KV_DIM + FOURIER_DIM
RMS_EPS = 1e-6
NEG_INF = -1e30

LANES = 128
BF16_SUBLANES = 16
PROJ_CHUNK = Q_PER_KV * HEAD_DIM
V7X_VMEM_BYTES = 64 * 1024 * 1024
VMEM_RESERVE_BYTES = 4 << 20
SMALL_CALL_VMEM_BYTES = 32 << 20
CAST_BLOCK_BYTES = 6 << 20
LOG2_E = 1.4426950408889634
MXU_DEPTH = 256

FFN_ROW_TILE = 1024
FF_CHUNK = 512
FFN_COL_TILE = 256
DFT_COL_TILE = 256
PROJ_ROW_TILE = 512
MIX_ROW_TILE = 256
MIX_CHUNK = 1024
ATTN_BLOCKS_PER_STEP = 32


def _round_up(a, m):
    return (a + m - 1) // m * m


def _cparams(sem, vmem_bytes):
    return pltpu.CompilerParams(dimension_semantics=sem,
                                vmem_limit_bytes=int(min(vmem_bytes, V7X_VMEM_BYTES - VMEM_RESERVE_BYTES)))


def _rms(x, g):
    return x * lax.rsqrt(jnp.mean(x * x, axis=-1, keepdims=True) + RMS_EPS) * g


class _Rows:
    def __init__(self, B, n_real, n_meta, tm, meta):
        self.B, self.n_real, self.n_meta, self.meta = B, n_real, n_meta, meta
        self.nb = B if meta else 1
        self.rows = n_meta if meta else tm
        self.grid = (1, 1) if meta else (B, n_real // tm)
        self.m = self.nb * self.rows

    def spec(self, width, lead=(), row_block0=None):
        nl = len(lead)
        if self.meta:
            r0 = self.n_real // self.n_meta if row_block0 is None else row_block0
            return pl.BlockSpec((self.nb,) + tuple(lead) + (self.rows, width),
                                lambda b, i, *_: (0,) + (0,) * nl + (r0, 0))
        return pl.BlockSpec((1,) + tuple(lead) + (self.rows, width),
                            lambda b, i, *_: (b,) + (0,) * nl + (i, 0))

    def table_spec(self, width):
        if self.meta:
            r0 = self.n_real // self.n_meta
            return pl.BlockSpec((self.rows, width), lambda b, i, *_: (r0, 0))
        return pl.BlockSpec((self.rows, width), lambda b, i, *_: (i, 0))


def _const_spec(shape, single_buffer=False):
    nd = len(shape)
    if single_buffer:
        return pl.BlockSpec(shape, lambda *_: (0,) * nd, pipeline_mode=pl.Buffered(1))
    return pl.BlockSpec(shape, lambda *_: (0,) * nd)


def _alias_tail(n_in, outs):
    specs = [pl.BlockSpec(memory_space=pl.ANY) for _ in outs]
    return specs, {n_in + k: k for k in range(len(outs))}


def _side_cast(stacks, n_steps, step_index):
    specs, operands, shapes, vmem = [], [], [], 0
    for w in stacks:
        L, K, N = w.shape
        rps = L * K // n_steps
        assert L * K % n_steps == 0 and rps % BF16_SUBLANES == 0
        specs.append(pl.BlockSpec((rps, N), lambda *idx: (step_index(*idx), 0)))
        operands.append(w.reshape(L * K, N))
        shapes.append(jax.ShapeDtypeStruct((L * K, N), _BF16))
        vmem += 2 * rps * N * (4 + 2)
    return specs, operands, shapes, vmem


def _cast_blocks(src_refs, dst_refs):
    for src_ref, dst_ref in zip(src_refs, dst_refs):
        dst_ref[...] = src_ref[...].astype(_BF16)


def _cast_body(w_ref, o_ref):
    o_ref[...] = w_ref[...].astype(_BF16)


def _to_bf16(w):
    L, K, N = w.shape
    tk = _pick_tile(K, 512)
    while tk > 8 and tk * N * 4 > CAST_BLOCK_BYTES:
        tk //= 2
    return pl.pallas_call(
        _cast_body,
        grid=(L, K // tk),
        in_specs=[pl.BlockSpec((1, tk, N), lambda l, i: (l, i, 0))],
        out_specs=pl.BlockSpec((1, tk, N), lambda l, i: (l, i, 0)),
        out_shape=jax.ShapeDtypeStruct(w.shape, _BF16),
        compiler_params=_cparams(("parallel", "parallel"), 2 * tk * N * (4 + 2) + VMEM_RESERVE_BYTES),
        name="cast_bf16",
    )(w)


def _ffn_body(*refs, nj, final, n_alias):
    h_ref, gain_ref, wg_ref, wu_ref, wd_ref = refs[:5]
    rest = refs[5:]
    fgain_ref = None
    if final:
        fgain_ref, rest = rest[0], rest[1:]
    o_ref, xn_ref = rest[n_alias:]
    j = pl.program_id(2)
    D = h_ref.shape[-1]

    @pl.when(j == 0)
    def _():
        x = h_ref[...].reshape(-1, D)
        xn_ref[...] = _rms(x, gain_ref[...]).astype(_BF16)
        o_ref[...] = x.reshape(o_ref.shape)

    tf = wg_ref.shape[-1]
    ct = min(FFN_COL_TILE, tf)
    acts = []
    for t in range(tf // ct):
        cols = slice(t * ct, (t + 1) * ct)
        g = jnp.dot(xn_ref[...], wg_ref[:, cols], preferred_element_type=_F32)
        u = jnp.dot(xn_ref[...], wu_ref[:, cols], preferred_element_type=_F32)
        acts.append(((g * jax.nn.sigmoid(g)) * u * 0.5).astype(_BF16))
    act = jnp.concatenate(acts, axis=1)
    o_ref[...] += jnp.dot(act, wd_ref[...], preferred_element_type=_F32).reshape(o_ref.shape)

    if final:
        @pl.when(j == nj - 1)
        def _():
            o_ref[...] = _rms(o_ref[...].reshape(-1, D), fgain_ref[...]).reshape(o_ref.shape)


def _ffn(rows, h, gain, wg, wu, wd, *, layer, tf, T, final_gain=None, h_row_block0=None, prev_out=None):
    D = h.shape[-1]
    nj = wg.shape[2] // tf
    final = final_gain is not None
    in_specs = [rows.spec(D, row_block0=h_row_block0), _const_spec((1, D)),
                pl.BlockSpec((None, D, tf), lambda b, i, j: (layer, 0, j)),
                pl.BlockSpec((None, D, tf), lambda b, i, j: (layer, 0, j)),
                pl.BlockSpec((None, tf, D), lambda b, i, j: (layer, j, 0))]
    args = [h, gain, wg, wu, wd]
    if final:
        in_specs.append(_const_spec((1, D)))
        args.append(final_gain)
    aliases = {}
    if prev_out is not None:
        extra, aliases = _alias_tail(len(args), [prev_out])
        in_specs += extra
        args.append(prev_out)
    m = rows.m
    vmem = (4 * m * D * 4 + m * D * 2 + 2 * (D * 2 * tf + tf * D) * 2 + m * 2 * tf * 4 + m * tf * 2
            + VMEM_RESERVE_BYTES)
    out_rows = rows.n_real if final else T
    return pl.pallas_call(
        functools.partial(_ffn_body, nj=nj, final=final, n_alias=len(aliases)),
        grid=rows.grid + (nj,),
        in_specs=in_specs,
        out_specs=rows.spec(D),
        out_shape=jax.ShapeDtypeStruct((rows.B, out_rows, D), _F32),
        scratch_shapes=[pltpu.VMEM((m, D), _BF16)],
        input_output_aliases=aliases,
        compiler_params=_cparams(("parallel", "parallel", "arbitrary"), vmem),
        name=("ffn_final" if final else "ffn") + ("_meta" if rows.meta else ""),
    )(*args)


def _proj_body(*refs, n_rope, n_qkv, n_chunks, nb, n_alias, n_side):
    h_ref, gain_ref, w_ref, c_ref, s1_ref, s2_ref, dft_ref = refs[:7]
    side_in = refs[7:7 + n_side]
    qkv_ref, z_ref, xno_ref = refs[7 + n_side + n_alias:10 + n_side + n_alias]
    side_out = refs[10 + n_side + n_alias:]
    _cast_blocks(side_in, side_out)
    D = h_ref.shape[-1]
    rows = h_ref.shape[-2]
    xno_ref[...] = _rms(h_ref[...].reshape(-1, D), gain_ref[...]).astype(_BF16).reshape(xno_ref.shape)
    c = jnp.concatenate([c_ref[...]] * nb, axis=0)
    s1 = jnp.concatenate([s1_ref[...]] * nb, axis=0)
    s2 = jnp.concatenate([s2_ref[...]] * nb, axis=0)
    half = ROT_DIM // 2
    for ch in range(n_chunks):
        cols = slice(ch * PROJ_CHUNK, (ch + 1) * PROJ_CHUNK)
        y = jnp.dot(xno_ref[...].reshape(-1, D), w_ref[:, cols], preferred_element_type=_F32)
        if ch < n_rope:
            for hh in range(PROJ_CHUNK // HEAD_DIM):
                xh = y[:, hh * HEAD_DIM:(hh + 1) * HEAD_DIM]
                up = pltpu.roll(xh, HEAD_DIM - half, 1)
                dn = pltpu.roll(xh, half, 1)
                r = (xh * c + up * s1 + dn * s2).astype(_BF16)
                lo = ch * PROJ_CHUNK + hh * HEAD_DIM
                qkv_ref[:, :, lo:lo + HEAD_DIM] = r.reshape(nb, rows, HEAD_DIM)
        elif ch < n_qkv:
            qkv_ref[:, :, cols] = y.astype(_BF16).reshape(nb, rows, PROJ_CHUNK)
        else:
            ub = y.astype(_BF16)
            for gg in range(PROJ_CHUNK // FOURIER_GROUP_DIM):
                zz = jnp.dot(ub[:, gg * FOURIER_GROUP_DIM:(gg + 1) * FOURIER_GROUP_DIM], dft_ref[...],
                             preferred_element_type=_F32)
                lo = (ch - n_qkv) * PROJ_CHUNK + gg * FOURIER_GROUP_DIM
                z_ref[:, 0, :, lo:lo + FOURIER_GROUP_DIM] = (
                    zz[:, :FOURIER_GROUP_DIM].astype(_BF16).reshape(nb, rows, FOURIER_GROUP_DIM))
                z_ref[:, 1, :, lo:lo + FOURIER_GROUP_DIM] = (
                    zz[:, FOURIER_GROUP_DIM:].astype(_BF16).reshape(nb, rows, FOURIER_GROUP_DIM))


def _proj(rows, h, gain, w_in, rope_c, rope_s1, rope_s2, dft_c, *, layer, prev_out=None, side_cast=()):
    B, T, D = h.shape
    n_rope = (Q_DIM + KV_DIM) // PROJ_CHUNK
    n_qkv = (Q_DIM + 2 * KV_DIM) // PROJ_CHUNK
    n_chunks = IN_DIM // PROJ_CHUNK
    w_spec = pl.BlockSpec((None, D, IN_DIM), lambda *_: (layer, 0, 0), pipeline_mode=pl.Buffered(1))
    in_specs = [rows.spec(D), _const_spec((1, D)), w_spec,
                rows.table_spec(LANES), rows.table_spec(LANES), rows.table_spec(LANES),
                _const_spec((FOURIER_GROUP_DIM, 2 * FOURIER_GROUP_DIM))]
    args = [h, gain, w_in, rope_c, rope_s1, rope_s2, dft_c]
    out_specs = [rows.spec(Q_DIM + 2 * KV_DIM), rows.spec(FOURIER_DIM, lead=(2,)), rows.spec(D)]
    out_shape = [jax.ShapeDtypeStruct((B, T, Q_DIM + 2 * KV_DIM), _BF16),
                 jax.ShapeDtypeStruct((B, 2, T, FOURIER_DIM), _BF16),
                 jax.ShapeDtypeStruct((B, T, D), _BF16)]
    m = rows.m
    vmem = (2 * m * D * 4 + 2 * m * D * 2 + D * IN_DIM * 2 + 6 * m * LANES * 4
            + 2 * m * (Q_DIM + 2 * KV_DIM) * 2 + 4 * m * FOURIER_DIM * 2 + 8 * m * PROJ_CHUNK * 4
            + VMEM_RESERVE_BYTES)
    side_specs, side_ops, side_shapes, side_vmem = _side_cast(
        side_cast, rows.grid[0] * rows.grid[1], lambda b, i: b * rows.grid[1] + i)
    in_specs += side_specs
    args += side_ops
    out_specs += side_specs
    out_shape += side_shapes
    vmem += side_vmem
    aliases = {}
    if prev_out is not None:
        extra, aliases = _alias_tail(len(args), prev_out)
        in_specs += extra
        args += list(prev_out)
    outs = pl.pallas_call(
        functools.partial(_proj_body, n_rope=n_rope, n_qkv=n_qkv, n_chunks=n_chunks, nb=rows.nb,
                          n_alias=len(aliases), n_side=len(side_cast)),
        grid=rows.grid,
        in_specs=in_specs,
        out_specs=out_specs,
        out_shape=out_shape,
        input_output_aliases=aliases,
        compiler_params=_cparams(("parallel", "arbitrary"), vmem),
        name="mixer_proj" + ("_meta" if rows.meta else ""),
    )(*args)
    return tuple(outs[:3]) + tuple(o.reshape(w.shape) for o, w in zip(outs[3:], side_cast))


def _stack_heads(q):
    return jnp.concatenate([q[:, h * HEAD_DIM:(h + 1) * HEAD_DIM] for h in range(Q_PER_KV)], axis=0)


def _sink_column(sink_ref, g, rows_per_head):
    return jnp.concatenate(
        [jnp.full((rows_per_head, 1), sink_ref[g * Q_PER_KV + h], _F32) for h in range(Q_PER_KV)], axis=0)


def _pad_rows(a, rows):
    return jnp.concatenate([a, jnp.zeros((rows - a.shape[0], a.shape[1]), a.dtype)], axis=0)


def _softmax_pv(q4, k_all, v_all, masks, sink):
    s = lax.dot_general(q4, k_all, (((1,), (1,)), ((), ())), preferred_element_type=_F32) * (HEAD_DIM ** -0.5)
    blocks = []
    for i, mk in enumerate(masks):
        sb = s[:, i * BLOCK:(i + 1) * BLOCK]
        blocks.append(sb if mk is None else jnp.where(mk, sb, NEG_INF))
    mx = jnp.maximum(jnp.max(functools.reduce(jnp.maximum, blocks), axis=-1, keepdims=True), sink)
    p = jnp.concatenate([jnp.exp(sb - mx).astype(_BF16) for sb in blocks], axis=1)
    v_ext = jnp.concatenate([v_all, jnp.ones(v_all.shape, _BF16)], axis=1)
    o = jnp.dot(p, v_ext, preferred_element_type=_F32)
    den = o[:, HEAD_DIM:] + jnp.exp(sink - mx)
    return o[:, :HEAD_DIM] / den


def _attn_body(sink_ref, *refs, qb, ntile, n_meta, n_side):
    q_ref, kp_ref, kc_ref, kn_ref, km_ref, vp_ref, vc_ref, vn_ref, vm_ref = refs[:9]
    o_ref = refs[9 + n_side]
    kbuf, vbuf, s_buf, p_buf = refs[10 + 2 * n_side:]
    _cast_blocks(refs[9:9 + n_side], refs[10 + n_side:10 + 2 * n_side])
    g = pl.program_id(1)
    jt = pl.program_id(2)
    rows = Q_PER_KV * BLOCK
    band = 3 * BLOCK
    meta0 = (qb + 2) * BLOCK
    scale = HEAD_DIM ** -0.5
    c_exp2 = scale * LOG2_E

    kbuf[0:BLOCK] = kp_ref[0]
    kbuf[BLOCK:(qb + 1) * BLOCK] = kc_ref[0]
    kbuf[(qb + 1) * BLOCK:meta0] = kn_ref[0]
    kbuf[meta0:meta0 + BLOCK] = _pad_rows(km_ref[0], BLOCK)
    vbuf[0:BLOCK, 0:HEAD_DIM] = vp_ref[0]
    vbuf[BLOCK:(qb + 1) * BLOCK, 0:HEAD_DIM] = vc_ref[0]
    vbuf[(qb + 1) * BLOCK:meta0, 0:HEAD_DIM] = vn_ref[0]
    vbuf[meta0:meta0 + BLOCK, 0:HEAD_DIM] = _pad_rows(vm_ref[0], BLOCK)
    vbuf[:, HEAD_DIM:] = jnp.ones((meta0 + BLOCK, HEAD_DIM), _BF16)

    row = lax.broadcasted_iota(jnp.int32, (rows, BLOCK), 0) % BLOCK
    col = lax.broadcasted_iota(jnp.int32, (rows, BLOCK), 1)
    diff = col - row
    never = 2 * BLOCK
    sink_s = _sink_column(sink_ref, g, BLOCK) * (1.0 / scale)
    meta_fill = jnp.where(col == n_meta, sink_s, NEG_INF)
    meta_ok = col < n_meta

    def dot_t(a, b):
        return lax.dot_general(a, b, (((1,), (1,)), ((), ())), preferred_element_type=_F32)

    def stage_a(i, slot):
        r0 = pl.multiple_of(i * BLOCK, BLOCK)
        q4 = _stack_heads(q_ref[0, pl.ds(r0, BLOCK), :])
        s_buf[slot, :, 0:band] = dot_t(q4, kbuf[pl.ds(r0, band), :])
        s_buf[slot, :, band:] = dot_t(q4, kbuf[meta0:meta0 + BLOCK, :])

    def stage_b(i, slot):
        first = jnp.logical_and(jt == 0, i == 0)
        last = jnp.logical_and(jt == ntile - 1, i == qb - 1)
        off_p = jnp.where(first, never, 0)
        off_n = jnp.where(last, -never, 0)
        blocks = [jnp.where(diff >= off_p, s_buf[slot, :, 0:BLOCK], NEG_INF),
                  s_buf[slot, :, BLOCK:2 * BLOCK],
                  jnp.where(diff <= off_n, s_buf[slot, :, 2 * BLOCK:band], NEG_INF),
                  jnp.where(meta_ok, s_buf[slot, :, band:], meta_fill)]
        mx = jnp.max(functools.reduce(jnp.maximum, blocks), axis=-1, keepdims=True)
        for n, sb in enumerate(blocks):
            p_buf[slot, :, n * BLOCK:(n + 1) * BLOCK] = jnp.exp2((sb - mx) * c_exp2).astype(_BF16)

    def stage_c(i, slot):
        r0 = pl.multiple_of(i * BLOCK, BLOCK)
        o = (jnp.dot(p_buf[slot, :, 0:band], vbuf[pl.ds(r0, band), :], preferred_element_type=_F32)
             + jnp.dot(p_buf[slot, :, band:], vbuf[meta0:meta0 + BLOCK, :], preferred_element_type=_F32))
        out = o[:, :HEAD_DIM] / o[:, HEAD_DIM:]
        for h in range(Q_PER_KV):
            o_ref[0, pl.ds(r0, BLOCK), h * HEAD_DIM:(h + 1) * HEAD_DIM] = (
                out[h * BLOCK:(h + 1) * BLOCK].astype(_BF16))

    stage_a(0, 0)
    stage_a(1, 1)
    stage_b(0, 0)

    def pair(t, carry):
        i = 2 * t
        stage_c(i, 0)
        stage_b(i + 1, 1)
        stage_a(i + 2, 0)
        stage_c(i + 1, 1)
        stage_b(i + 2, 0)
        stage_a(i + 3, 1)
        return carry

    lax.fori_loop(0, (qb - 2) // 2, pair, 0)
    stage_c(qb - 2, 0)
    stage_b(qb - 1, 1)
    stage_c(qb - 1, 1)


def _attn_meta_body(sink_ref, q_ref, k0_ref, km_ref, v0_ref, vm_ref, prev_out_ref, o_ref, *, n_meta):
    del prev_out_ref
    g = pl.program_id(1)
    rows = Q_PER_KV * n_meta
    row = lax.broadcasted_iota(jnp.int32, (rows, BLOCK), 0) % n_meta
    col = lax.broadcasted_iota(jnp.int32, (rows, BLOCK), 1)
    q4 = _stack_heads(q_ref[0])
    masks = [col <= row + (WINDOW - n_meta), col < n_meta]
    k_all = jnp.concatenate([k0_ref[0], _pad_rows(km_ref[0], BLOCK)], axis=0)
    v_all = jnp.concatenate([v0_ref[0], _pad_rows(vm_ref[0], BLOCK)], axis=0)
    out = _softmax_pv(q4, k_all, v_all, masks, _sink_column(sink_ref, g, n_meta))
    for h in range(Q_PER_KV):
        o_ref[0, :, h * HEAD_DIM:(h + 1) * HEAD_DIM] = out[h * n_meta:(h + 1) * n_meta].astype(_BF16)


def _attention(qkv, sink, *, n_real, n_meta, qb, side_cast=()):
    B, T, _ = qkv.shape
    nblk = n_real // BLOCK
    ntile = nblk // qb
    kcol = Q_DIM // HEAD_DIM
    vcol = (Q_DIM + KV_DIM) // HEAD_DIM
    mrow = n_real // n_meta

    def edge_spec(rowf, col0):
        return pl.BlockSpec((1, BLOCK, HEAD_DIM), lambda b, g, j, s: (b, rowf(j), col0 + g))

    def main_spec(col0):
        return pl.BlockSpec((1, qb * BLOCK, HEAD_DIM), lambda b, g, j, s: (b, j, col0 + g))

    def meta_spec(col0):
        return pl.BlockSpec((1, n_meta, HEAD_DIM), lambda b, g, j, s: (b, mrow, col0 + g))

    def prev_i(j):
        return jnp.maximum(qb * j - 1, 0)

    def next_i(j):
        return jnp.minimum(qb * j + qb, nblk - 1)

    side_specs, side_ops, side_shapes, side_vmem = _side_cast(
        side_cast, B * N_KV_HEADS * ntile, lambda b, g, j, s: (b * N_KV_HEADS + g) * ntile + j)
    attn, *cast = pl.pallas_call(
        functools.partial(_attn_body, qb=qb, ntile=ntile, n_meta=n_meta, n_side=len(side_cast)),
        grid_spec=pltpu.PrefetchScalarGridSpec(
            num_scalar_prefetch=1,
            grid=(B, N_KV_HEADS, ntile),
            in_specs=[
                pl.BlockSpec((1, qb * BLOCK, PROJ_CHUNK), lambda b, g, j, s: (b, j, g)),
                edge_spec(prev_i, kcol), main_spec(kcol), edge_spec(next_i, kcol), meta_spec(kcol),
                edge_spec(prev_i, vcol), main_spec(vcol), edge_spec(next_i, vcol), meta_spec(vcol),
            ] + side_specs,
            out_specs=[pl.BlockSpec((1, qb * BLOCK, PROJ_CHUNK), lambda b, g, j, s: (b, j, g))] + side_specs,
            scratch_shapes=[
                pltpu.VMEM(((qb + 3) * BLOCK, HEAD_DIM), _BF16),
                pltpu.VMEM(((qb + 3) * BLOCK, 2 * HEAD_DIM), _BF16),
                pltpu.VMEM((2, Q_PER_KV * BLOCK, 4 * BLOCK), _F32),
                pltpu.VMEM((2, Q_PER_KV * BLOCK, 4 * BLOCK), _BF16),
            ],
        ),
        out_shape=[jax.ShapeDtypeStruct((B, T, Q_DIM), _BF16)] + side_shapes,
        compiler_params=_cparams(("parallel", "parallel", "arbitrary"), SMALL_CALL_VMEM_BYTES + side_vmem),
        name="windowed_gqa",
    )(sink, qkv, qkv, qkv, qkv, qkv, qkv, qkv, qkv, qkv, *side_ops)
    cast = [c.reshape(w.shape) for c, w in zip(cast, side_cast)]

    def blk0_spec(col0):
        return pl.BlockSpec((1, BLOCK, HEAD_DIM), lambda b, g, s: (b, 0, col0 + g))

    def meta2_spec(col0):
        return pl.BlockSpec((1, n_meta, HEAD_DIM), lambda b, g, s: (b, mrow, col0 + g))

    return pl.pallas_call(
        functools.partial(_attn_meta_body, n_meta=n_meta),
        grid_spec=pltpu.PrefetchScalarGridSpec(
            num_scalar_prefetch=1,
            grid=(B, N_KV_HEADS),
            in_specs=[
                pl.BlockSpec((1, n_meta, PROJ_CHUNK), lambda b, g, s: (b, mrow, g)),
                blk0_spec(kcol), meta2_spec(kcol), blk0_spec(vcol), meta2_spec(vcol),
                pl.BlockSpec(memory_space=pl.ANY),
            ],
            out_specs=pl.BlockSpec((1, n_meta, PROJ_CHUNK), lambda b, g, s: (b, mrow, g)),
        ),
        out_shape=jax.ShapeDtypeStruct((B, T, Q_DIM), _BF16),
        input_output_aliases={6: 0},
        compiler_params=_cparams(("parallel", "arbitrary"), SMALL_CALL_VMEM_BYTES),
        name="windowed_gqa_meta",
    )(sink, qkv, qkv, qkv, qkv, qkv, attn), cast


def _fft_factors(T):
    best = None
    for n2 in range(BF16_SUBLANES, T + 1, BF16_SUBLANES):
        if T % n2:
            continue
        n1 = T // n2
        r1 = n1 * BF16_SUBLANES
        cost = (4 * _round_up(r1, MXU_DEPTH) * r1 * (n2 // BF16_SUBLANES)
                + 2 * _round_up(n2, MXU_DEPTH) * n2 * n1)
        if best is None or cost < best[0]:
            best = (cost, n1, n2)
    assert best is not None, "sequence length must be a multiple of 16"
    return best[1], best[2]


def _fft1_body(z_ref, mc_ref, ms_ref, twc_ref, tws_ref, a_ref, *, n1, R, C):
    rows = n1 * R
    ct = min(DFT_COL_TILE, C)
    twc = jnp.concatenate([twc_ref[0]] * (ct // LANES), axis=1)
    tws = jnp.concatenate([tws_ref[0]] * (ct // LANES), axis=1)
    for t in range(C // ct):
        cols = slice(t * ct, (t + 1) * ct)
        rhs = jnp.concatenate([z_ref[0, 0, :, :, cols].reshape(rows, ct), z_ref[0, 1, :, :, cols].reshape(rows, ct)],
                              axis=1)
        xc = jnp.dot(mc_ref[...], rhs, preferred_element_type=_F32)
        xs = jnp.dot(ms_ref[...], rhs, preferred_element_type=_F32)
        ar = xc[:, :ct] + xs[:, ct:]
        ai = xc[:, ct:] - xs[:, :ct]
        a_ref[0, 0, :, :, cols] = (ar * twc + ai * tws).reshape(n1, R, ct).astype(_BF16)
        a_ref[0, 1, :, :, cols] = (ai * twc - ar * tws).reshape(n1, R, ct).astype(_BF16)


def _fft2_body(ar_ref, ai_ref, mc_ref, ms_ref, f_ref):
    for b in range(ar_ref.shape[0]):
        res = (jnp.dot(mc_ref[0], ar_ref[b, 0, 0], preferred_element_type=_F32)
               + jnp.dot(ms_ref[0], ai_ref[b, 0, 0], preferred_element_type=_F32))
        f_ref[b] = res.astype(_BF16).reshape(f_ref.shape[1:])


def _unpermute_body(p_ref, x_ref, o_ref):
    o_ref[0] = jnp.dot(p_ref[...], x_ref[0], preferred_element_type=_F32).astype(_BF16)


def _fft_tables(T, n1, n2, n_meta, R):
    two_pi = 2.0 * np.pi
    k1 = jnp.arange(n1, dtype=jnp.int32)
    ang1 = ((k1[:, None] * k1[None, :]) % n1).astype(_F32) * (two_pi / n1)
    eye = jnp.eye(R, dtype=_F32)
    m1c = jnp.kron(jnp.cos(ang1) * (n1 ** -0.5), eye).astype(_BF16)
    m1s = jnp.kron(jnp.sin(ang1) * (n1 ** -0.5), eye).astype(_BF16)
    nn2 = jnp.arange(n2, dtype=jnp.int32).reshape(n2 // R, 1, R)
    angt = ((k1[None, :, None] * (nn2 + n_meta)) % T).astype(_F32) * (two_pi / T)
    angt = jnp.broadcast_to(angt.reshape(n2 // R, n1 * R, 1), (n2 // R, n1 * R, LANES))
    twc, tws = jnp.cos(angt), jnp.sin(angt)
    k2 = jnp.arange(n2, dtype=jnp.int32)[:, None]
    mm2 = jnp.arange(n2, dtype=jnp.int32)[None, :]
    outs_c, outs_s = [], []
    for rot in (n_meta // n1, n_meta // n1 + 1):
        ph2 = (((k2 + rot) % n2) * (mm2 + n_meta)) % n2
        ang2 = ph2.astype(_F32) * (two_pi / n2)
        outs_c.append(jnp.cos(ang2) * (n2 ** -0.5))
        outs_s.append(jnp.sin(ang2) * (n2 ** -0.5))
    return m1c, m1s, twc, tws, jnp.stack(outs_c).astype(_BF16), jnp.stack(outs_s).astype(_BF16)


def _position_dft(z, tables, *, n1, n2, R, n_meta):
    B, _, T, C = z.shape
    m1c, m1s, twc, tws, m2c, m2s = tables
    rows = n1 * R
    zv = z.reshape(B, 2, n1, n2, C)
    av = pl.pallas_call(
        functools.partial(_fft1_body, n1=n1, R=R, C=C),
        grid=(B, n2 // R),
        in_specs=[
            pl.BlockSpec((1, 2, n1, R, C), lambda b, t: (b, 0, 0, t, 0)),
            _const_spec((rows, rows)),
            _const_spec((rows, rows)),
            pl.BlockSpec((1, rows, LANES), lambda b, t: (t, 0, 0)),
            pl.BlockSpec((1, rows, LANES), lambda b, t: (t, 0, 0)),
        ],
        out_specs=pl.BlockSpec((1, 2, n1, R, C), lambda b, t: (b, 0, 0, t, 0)),
        out_shape=jax.ShapeDtypeStruct((B, 2, n1, n2, C), _BF16),
        compiler_params=_cparams(("parallel", "arbitrary"),
                                 4 * 2 * rows * C * 2 + 2 * 2 * rows * rows * 2 + 6 * rows * C * 4 + VMEM_RESERVE_BYTES),
        name="dft_stage1",
    )(zv, m1c, m1s, twc, tws)
    shift = n1 - (n_meta % n1)

    f = pl.pallas_call(
        _fft2_body,
        grid=(n1,),
        in_specs=[
            pl.BlockSpec((B, 1, 1, n2, C), lambda k: (0, 0, k, 0, 0)),
            pl.BlockSpec((B, 1, 1, n2, C), lambda k: (0, 1, k, 0, 0)),
            pl.BlockSpec((1, n2, n2), lambda k: (jnp.where(k < n_meta % n1, 1, 0), 0, 0)),
            pl.BlockSpec((1, n2, n2), lambda k: (jnp.where(k < n_meta % n1, 1, 0), 0, 0)),
        ],
        out_specs=pl.BlockSpec((B, n2 // R, None, R, C), lambda k: (0, 0, (k + shift) % n1, 0, 0)),
        out_shape=jax.ShapeDtypeStruct((B, n2 // R, n1, R, C), _BF16),
        compiler_params=_cparams(("arbitrary",), SMALL_CALL_VMEM_BYTES),
        name="dft_stage2",
    )(av, av, m2c, m2s)
    q = jnp.arange(rows, dtype=jnp.int32)
    src = R * (q % n1) + q // n1
    perm = (jnp.arange(rows, dtype=jnp.int32)[None, :] == src[:, None]).astype(_BF16)
    return pl.pallas_call(
        _unpermute_body,
        grid=(B, n2 // R),
        in_specs=[_const_spec((rows, rows)), pl.BlockSpec((1, rows, C), lambda b, i: (b, i, 0))],
        out_specs=pl.BlockSpec((1, rows, C), lambda b, i: (b, i, 0)),
        out_shape=jax.ShapeDtypeStruct((B, T, C), _BF16),
        compiler_params=_cparams(("parallel", "parallel"), SMALL_CALL_VMEM_BYTES),
        name="dft_unpermute",
    )(perm, f.reshape(B, T, C))


def _mix_body(*refs, tn, n_alias):
    h_ref, xn_ref, a_ref, f_ref, wpa_ref, wpf_ref, wg_ref, wo_ref = refs[:8]
    (o_ref,) = refs[8 + n_alias:]
    D = h_ref.shape[-1]
    o_ref[...] = h_ref[...]
    xn = xn_ref[...].reshape(-1, D)
    a = a_ref[...].reshape(-1, Q_DIM)
    f = f_ref[...].reshape(-1, FOURIER_DIM)
    for j in range(D // tn):
        cols = slice(j * tn, (j + 1) * tn)
        av = jnp.dot(a, wpa_ref[:, cols], preferred_element_type=_F32)
        fv = jnp.dot(f, wpf_ref[:, cols], preferred_element_type=_F32)
        ga = jax.nn.sigmoid(jnp.dot(xn, wg_ref[:, cols], preferred_element_type=_F32))
        gf = jax.nn.sigmoid(jnp.dot(xn, wg_ref[:, D + j * tn:D + (j + 1) * tn], preferred_element_type=_F32))
        mixed = (ga * av + gf * fv).astype(_BF16)
        o_ref[...] += jnp.dot(mixed, wo_ref[cols, :], preferred_element_type=_F32).reshape(o_ref.shape)


def _mix_out(rows, h, xn, attn, four, wpa, wpf, wg, wo, *, layer, tn, prev_out=None):
    B, T, D = h.shape

    def resident(k, n):
        return pl.BlockSpec((None, k, n), lambda *_: (layer, 0, 0), pipeline_mode=pl.Buffered(1))

    in_specs = [rows.spec(D), rows.spec(D), rows.spec(Q_DIM), rows.spec(FOURIER_DIM),
                resident(Q_DIM, D), resident(FOURIER_DIM, D), resident(D, 2 * D), resident(D, D)]
    args = [h, xn, attn, four, wpa, wpf, wg, wo]
    aliases = {}
    if prev_out is not None:
        extra, aliases = _alias_tail(len(args), [prev_out])
        in_specs += extra
        args.append(prev_out)
    m = rows.m
    vmem = ((Q_DIM + FOURIER_DIM + 3 * D) * D * 2 + 4 * m * D * 4 + 2 * m * D * 2
            + 2 * m * (Q_DIM + FOURIER_DIM) * 2 + 10 * m * tn * 4 + VMEM_RESERVE_BYTES)
    return pl.pallas_call(
        functools.partial(_mix_body, tn=tn, n_alias=len(aliases)),
        grid=rows.grid,
        in_specs=in_specs,
        out_specs=rows.spec(D),
        out_shape=jax.ShapeDtypeStruct((B, T, D), _F32),
        input_output_aliases=aliases,
        compiler_params=_cparams(("parallel", "arbitrary"), vmem),
        name="mixer_out" + ("_meta" if rows.meta else ""),
    )(*args)


def _rope_tables(n_real, n_meta):
    T = n_real + n_meta
    half = ROT_DIM // 2
    inv = 1.0 / (ROPE_THETA ** (jnp.arange(0, ROT_DIM, 2, dtype=_F32) / ROT_DIM))
    ang = jnp.arange(T, dtype=_F32)[:, None] * inv[None, :]
    cos, sin = jnp.cos(ang), jnp.sin(ang)
    cos = jnp.concatenate([cos[n_meta:], cos[:n_meta]], axis=0)
    sin = jnp.concatenate([sin[n_meta:], sin[:n_meta]], axis=0)
    ones = jnp.ones((T, HEAD_DIM - ROT_DIM), _F32)
    zeros_h = jnp.zeros((T, half), _F32)
    zeros_r = jnp.zeros((T, HEAD_DIM - ROT_DIM), _F32)
    c = jnp.concatenate([cos, cos, ones], axis=1)
    s1 = jnp.concatenate([-sin, zeros_h, zeros_r], axis=1)
    s2 = jnp.concatenate([zeros_h, sin, zeros_r], axis=1)
    return c, s1, s2


def _channel_dft_table():
    n = FOURIER_GROUP_DIM
    c = jnp.arange(n, dtype=jnp.int32)[:, None]
    k = jnp.arange(n, dtype=jnp.int32)[None, :]
    ang = ((c * k) % n).astype(_F32) * (2.0 * np.pi / n)
    return (jnp.concatenate([jnp.cos(ang), -jnp.sin(ang)], axis=1) * (n ** -0.5)).astype(_BF16)


def _pick_tile(n, pref):
    t = pref
    while n % t:
        t //= 2
    return t


def kernel(x, meta_tokens, ffn1_norm, ffn1_w_gate, ffn1_w_up, ffn1_w_down, mix_norm, w_in, w_gate, sink,
           w_attn_branch, w_fourier_branch, w_out, ffn2_norm, ffn2_w_gate, ffn2_w_up, ffn2_w_down, final_norm):
    B, S, D = x.shape
    n_meta = meta_tokens.shape[0]
    depth = ffn1_norm.shape[0]
    d_ff = ffn1_w_gate.shape[-1]
    T = S + n_meta
    assert S % (4 * BLOCK) == 0 and S % n_meta == 0 and BLOCK % n_meta == 0 and n_meta < BLOCK
    assert w_in.shape[-1] == IN_DIM and KV_DIM == PROJ_CHUNK

    tf = _pick_tile(d_ff, FF_CHUNK)
    tn = _pick_tile(D, MIX_CHUNK)
    qb = _pick_tile(S // BLOCK, ATTN_BLOCKS_PER_STEP)
    rows_ffn = _Rows(B, S, n_meta, _pick_tile(S, FFN_ROW_TILE), meta=False)
    rows_main = _Rows(B, S, n_meta, _pick_tile(S, PROJ_ROW_TILE), meta=False)
    rows_mix = _Rows(B, S, n_meta, _pick_tile(S, MIX_ROW_TILE), meta=False)
    rows_meta = _Rows(B, S, n_meta, 0, meta=True)

    n1, n2 = _fft_factors(T)
    fft_tables = _fft_tables(T, n1, n2, n_meta, BF16_SUBLANES)
    rope_c, rope_s1, rope_s2 = _rope_tables(S, n_meta)
    dft_c = _channel_dft_table()

    ffn1_w = tuple(_to_bf16(w) for w in (ffn1_w_gate, ffn1_w_up, ffn1_w_down))
    w_in_b = _to_bf16(w_in)

    def ffn(h, h_meta, gain, wts, layer, final_gain=None):
        out = _ffn(rows_ffn, h, gain, *wts, layer=layer, tf=tf, T=T, final_gain=final_gain)
        if final_gain is None:
            src, blk0 = h_meta
            out = _ffn(rows_meta, src, gain, *wts, layer=layer, tf=tf, T=T, h_row_block0=blk0, prev_out=out)
        return out

    meta = jnp.broadcast_to(meta_tokens[None].astype(x.dtype), (B, n_meta, D))
    h = None
    for l in range(depth):
        if l == 0:
            h = ffn(x, (meta, 0), ffn1_norm[l][None], ffn1_w, l)
        else:
            h = ffn(h, (h, None), ffn1_norm[l][None], ffn1_w, l)

        gain = mix_norm[l][None]
        proj_args = (gain, w_in_b, rope_c, rope_s1, rope_s2, dft_c)
        if l == 0:
            qkv, z, xn, *ffn2_w = _proj(rows_main, h, *proj_args, layer=l,
                                        side_cast=(ffn2_w_gate, ffn2_w_up, ffn2_w_down))
        else:
            qkv, z, xn = _proj(rows_main, h, *proj_args, layer=l)
        qkv, z, xn = _proj(rows_meta, h, *proj_args, layer=l, prev_out=(qkv, z, xn))
        if l == 0:
            attn, mix_w = _attention(qkv, sink[l], n_real=S, n_meta=n_meta, qb=qb,
                                     side_cast=(w_attn_branch, w_fourier_branch, w_gate, w_out))
        else:
            attn, _ = _attention(qkv, sink[l], n_real=S, n_meta=n_meta, qb=qb)
        four = _position_dft(z, fft_tables, n1=n1, n2=n2, R=BF16_SUBLANES, n_meta=n_meta)
        mix_args = (xn, attn, four) + tuple(mix_w)
        h2 = _mix_out(rows_mix, h, *mix_args, layer=l, tn=tn)
        h = _mix_out(rows_meta, h, *mix_args, layer=l, tn=tn, prev_out=h2)

        last = l == depth - 1
        h = ffn(h, (h, None), ffn2_norm[l][None], ffn2_w, l, final_gain=final_norm[None] if last else None)
    return h
```

```python
import functools

import numpy as np
import jax
import jax.numpy as jnp
from jax import lax
from jax.experimental import pallas as pl
from jax.experimental.pallas import tpu as pltpu

_F32 = jnp.float32
_BF16 = jnp.bfloat16

HEAD_DIM = 128
N_Q_HEADS = 16
N_KV_HEADS = 4
Q_PER_KV = N_Q_HEADS // N_KV_HEADS
WINDOW = 128
BLOCK = 128
ROPE_THETA = 500000.0
ROT_DIM = HEAD_DIM // 4
N_FOURIER_GROUPS = 8
FOURIER_GROUP_DIM = 128
FOURIER_DIM = N_FOURIER_GROUPS * FOURIER_GROUP_DIM
Q_DIM = N_Q_HEADS * HEAD_DIM
KV_DIM = N_KV_HEADS * HEAD_DIM
IN_DIM = Q_DIM + 2 * KV_DIM + FOURIER_DIM
RMS_EPS = 1e-6
NEG_INF = -1e30

LANES = 128
BF16_SUBLANES = 16
PROJ_CHUNK = Q_PER_KV * HEAD_DIM
V7X_VMEM_BYTES = 64 * 1024 * 1024
VMEM_RESERVE_BYTES = 4 << 20
SMALL_CALL_VMEM_BYTES = 32 << 20
CAST_BLOCK_BYTES = 6 << 20
LOG2_E = 1.4426950408889634
MXU_DEPTH = 256

FFN_ROW_TILE = 1024
FF_CHUNK = 512
FFN_COL_TILE = 256
DFT_COL_TILE = 256
PROJ_ROW_TILE = 512
MIX_ROW_TILE = 256
MIX_CHUNK = 1024
ATTN_BLOCKS_PER_STEP = 32


def _round_up(a, m):
    return (a + m - 1) // m * m


def _cparams(sem, vmem_bytes):
    return pltpu.CompilerParams(dimension_semantics=sem,
                                vmem_limit_bytes=int(min(vmem_bytes, V7X_VMEM_BYTES - VMEM_RESERVE_BYTES)))


def _rms(x, g):
    return x * lax.rsqrt(jnp.mean(x * x, axis=-1, keepdims=True) + RMS_EPS) * g


class _Rows:
    def __init__(self, B, n_real, n_meta, tm, meta):
        self.B, self.n_real, self.n_meta, self.meta = B, n_real, n_meta, meta
        self.nb = B if meta else 1
        self.rows = n_meta if meta else tm
        self.grid = (1, 1) if meta else (B, n_real // tm)
        self.m = self.nb * self.rows

    def spec(self, width, lead=(), row_block0=None):
        nl = len(lead)
        if self.meta:
            r0 = self.n_real // self.n_meta if row_block0 is None else row_block0
            return pl.BlockSpec((self.nb,) + tuple(lead) + (self.rows, width),
                                lambda b, i, *_: (0,) + (0,) * nl + (r0, 0))
        return pl.BlockSpec((1,) + tuple(lead) + (self.rows, width),
                            lambda b, i, *_: (b,) + (0,) * nl + (i, 0))

    def table_spec(self, width):
        if self.meta:
            r0 = self.n_real // self.n_meta
            return pl.BlockSpec((self.rows, width), lambda b, i, *_: (r0, 0))
        return pl.BlockSpec((self.rows, width), lambda b, i, *_: (i, 0))


def _const_spec(shape, single_buffer=False):
    nd = len(shape)
    if single_buffer:
        return pl.BlockSpec(shape, lambda *_: (0,) * nd, pipeline_mode=pl.Buffered(1))
    return pl.BlockSpec(shape, lambda *_: (0,) * nd)


def _alias_tail(n_in, outs):
    specs = [pl.BlockSpec(memory_space=pl.ANY) for _ in outs]
    return specs, {n_in + k: k for k in range(len(outs))}


def _side_cast(stacks, n_steps, step_index):
    specs, operands, shapes, vmem = [], [], [], 0
    for w in stacks:
        L, K, N = w.shape
        rps = L * K // n_steps
        assert L * K % n_steps == 0 and rps % BF16_SUBLANES == 0
        specs.append(pl.BlockSpec((rps, N), lambda *idx: (step_index(*idx), 0)))
        operands.append(w.reshape(L * K, N))
        shapes.append(jax.ShapeDtypeStruct((L * K, N), _BF16))
        vmem += 2 * rps * N * (4 + 2)
    return specs, operands, shapes, vmem


def _cast_blocks(src_refs, dst_refs):
    for src_ref, dst_ref in zip(src_refs, dst_refs):
        dst_ref[...] = src_ref[...].astype(_BF16)


def _cast_body(w_ref, o_ref):
    o_ref[...] = w_ref[...].astype(_BF16)


def _to_bf16(w):
    L, K, N = w.shape
    tk = _pick_tile(K, 512)
    while tk > 8 and tk * N * 4 > CAST_BLOCK_BYTES:
        tk //= 2
    return pl.pallas_call(
        _cast_body,
        grid=(L, K // tk),
        in_specs=[pl.BlockSpec((1, tk, N), lambda l, i: (l, i, 0))],
        out_specs=pl.BlockSpec((1, tk, N), lambda l, i: (l, i, 0)),
        out_shape=jax.ShapeDtypeStruct(w.shape, _BF16),
        compiler_params=_cparams(("parallel", "parallel"), 2 * tk * N * (4 + 2) + VMEM_RESERVE_BYTES),
        name="cast_bf16",
    )(w)


def _ffn_body(*refs, nj, final, n_alias):
    h_ref, gain_ref, wg_ref, wu_ref, wd_ref = refs[:5]
    rest = refs[5:]
    fgain_ref = None
    if final:
        fgain_ref, rest = rest[0], rest[1:]
    o_ref, xn_ref = rest[n_alias:]
    j = pl.program_id(2)
    D = h_ref.shape[-1]

    @pl.when(j == 0)
    def _():
        x = h_ref[...].reshape(-1, D)
        xn_ref[...] = _rms(x, gain_ref[...]).astype(_BF16)
        o_ref[...] = x.reshape(o_ref.shape)

    tf = wg_ref.shape[-1]
    ct = min(FFN_COL_TILE, tf)
    acts = []
    for t in range(tf // ct):
        cols = slice(t * ct, (t + 1) * ct)
        g = jnp.dot(xn_ref[...], wg_ref[:, cols], preferred_element_type=_F32)
        u = jnp.dot(xn_ref[...], wu_ref[:, cols], preferred_element_type=_F32)
        acts.append(((g * jax.nn.sigmoid(g)) * u * 0.5).astype(_BF16))
    act = jnp.concatenate(acts, axis=1)
    o_ref[...] += jnp.dot(act, wd_ref[...], preferred_element_type=_F32).reshape(o_ref.shape)

    if final:
        @pl.when(j == nj - 1)
        def _():
            o_ref[...] = _rms(o_ref[...].reshape(-1, D), fgain_ref[...]).reshape(o_ref.shape)


def _ffn(rows, h, gain, wg, wu, wd, *, layer, tf, T, final_gain=None, h_row_block0=None, prev_out=None):
    D = h.shape[-1]
    nj = wg.shape[2] // tf
    final = final_gain is not None
    in_specs = [rows.spec(D, row_block0=h_row_block0), _const_spec((1, D)),
                pl.BlockSpec((None, D, tf), lambda b, i, j: (layer, 0, j)),
                pl.BlockSpec((None, D, tf), lambda b, i, j: (layer, 0, j)),
                pl.BlockSpec((None, tf, D), lambda b, i, j: (layer, j, 0))]
    args = [h, gain, wg, wu, wd]
    if final:
        in_specs.append(_const_spec((1, D)))
        args.append(final_gain)
    aliases = {}
    if prev_out is not None:
        extra, aliases = _alias_tail(len(args), [prev_out])
        in_specs += extra
        args.append(prev_out)
    m = rows.m
    vmem = (4 * m * D * 4 + m * D * 2 + 2 * (D * 2 * tf + tf * D) * 2 + m * 2 * tf * 4 + m * tf * 2
            + VMEM_RESERVE_BYTES)
    out_rows = rows.n_real if final else T
    return pl.pallas_call(
        functools.partial(_ffn_body, nj=nj, final=final, n_alias=len(aliases)),
        grid=rows.grid + (nj,),
        in_specs=in_specs,
        out_specs=rows.spec(D),
        out_shape=jax.ShapeDtypeStruct((rows.B, out_rows, D), _F32),
        scratch_shapes=[pltpu.VMEM((m, D), _BF16)],
        input_output_aliases=aliases,
        compiler_params=_cparams(("parallel", "parallel", "arbitrary"), vmem),
        name=("ffn_final" if final else "ffn") + ("_meta" if rows.meta else ""),
    )(*args)


def _proj_body(*refs, n_rope, n_qkv, n_chunks, nb, n_alias, n_side):
    h_ref, gain_ref, w_ref, c_ref, s1_ref, s2_ref, dft_ref = refs[:7]
    side_in = refs[7:7 + n_side]
    qkv_ref, z_ref, xno_ref = refs[7 + n_side + n_alias:10 + n_side + n_alias]
    side_out = refs[10 + n_side + n_alias:]
    _cast_blocks(side_in, side_out)
    D = h_ref.shape[-1]
    rows = h_ref.shape[-2]
    xno_ref[...] = _rms(h_ref[...].reshape(-1, D), gain_ref[...]).astype(_BF16).reshape(xno_ref.shape)
    c = jnp.concatenate([c_ref[...]] * nb, axis=0)
    s1 = jnp.concatenate([s1_ref[...]] * nb, axis=0)
    s2 = jnp.concatenate([s2_ref[...]] * nb, axis=0)
    half = ROT_DIM // 2
    for ch in range(n_chunks):
        cols = slice(ch * PROJ_CHUNK, (ch + 1) * PROJ_CHUNK)
        y = jnp.dot(xno_ref[...].reshape(-1, D), w_ref[:, cols], preferred_element_type=_F32)
        if ch < n_rope:
            for hh in range(PROJ_CHUNK // HEAD_DIM):
                xh = y[:, hh * HEAD_DIM:(hh + 1) * HEAD_DIM]
                up = pltpu.roll(xh, HEAD_DIM - half, 1)
                dn = pltpu.roll(xh, half, 1)
                r = (xh * c + up * s1 + dn * s2).astype(_BF16)
                lo = ch * PROJ_CHUNK + hh * HEAD_DIM
                qkv_ref[:, :, lo:lo + HEAD_DIM] = r.reshape(nb, rows, HEAD_DIM)
        elif ch < n_qkv:
            qkv_ref[:, :, cols] = y.astype(_BF16).reshape(nb, rows, PROJ_CHUNK)
        else:
            ub = y.astype(_BF16)
            for gg in range(PROJ_CHUNK // FOURIER_GROUP_DIM):
                zz = jnp.dot(ub[:, gg * FOURIER_GROUP_DIM:(gg + 1) * FOURIER_GROUP_DIM], dft_ref[...],
                             preferred_element_type=_F32)
                lo = (ch - n_qkv) * PROJ_CHUNK + gg * FOURIER_GROUP_DIM
                z_ref[:, 0, :, lo:lo + FOURIER_GROUP_DIM] = (
                    zz[:, :FOURIER_GROUP_DIM].astype(_BF16).reshape(nb, rows, FOURIER_GROUP_DIM))
                z_ref[:, 1, :, lo:lo + FOURIER_GROUP_DIM] = (
                    zz[:, FOURIER_GROUP_DIM:].astype(_BF16).reshape(nb, rows, FOURIER_GROUP_DIM))


def _proj(rows, h, gain, w_in, rope_c, rope_s1, rope_s2, dft_c, *, layer, prev_out=None, side_cast=()):
    B, T, D = h.shape
    n_rope = (Q_DIM + KV_DIM) // PROJ_CHUNK
    n_qkv = (Q_DIM + 2 * KV_DIM) // PROJ_CHUNK
    n_chunks = IN_DIM // PROJ_CHUNK
    w_spec = pl.BlockSpec((None, D, IN_DIM), lambda *_: (layer, 0, 0), pipeline_mode=pl.Buffered(1))
    in_specs = [rows.spec(D), _const_spec((1, D)), w_spec,
                rows.table_spec(LANES), rows.table_spec(LANES), rows.table_spec(LANES),
                _const_spec((FOURIER_GROUP_DIM, 2 * FOURIER_GROUP_DIM))]
    args = [h, gain, w_in, rope_c, rope_s1, rope_s2, dft_c]
    out_specs = [rows.spec(Q_DIM + 2 * KV_DIM), rows.spec(FOURIER_DIM, lead=(2,)), rows.spec(D)]
    out_shape = [jax.ShapeDtypeStruct((B, T, Q_DIM + 2 * KV_DIM), _BF16),
                 jax.ShapeDtypeStruct((B, 2, T, FOURIER_DIM), _BF16),
                 jax.ShapeDtypeStruct((B, T, D), _BF16)]
    m = rows.m
    vmem = (2 * m * D * 4 + 2 * m * D * 2 + D * IN_DIM * 2 + 6 * m * LANES * 4
            + 2 * m * (Q_DIM + 2 * KV_DIM) * 2 + 4 * m * FOURIER_DIM * 2 + 8 * m * PROJ_CHUNK * 4
            + VMEM_RESERVE_BYTES)
    side_specs, side_ops, side_shapes, side_vmem = _side_cast(
        side_cast, rows.grid[0] * rows.grid[1], lambda b, i: b * rows.grid[1] + i)
    in_specs += side_specs
    args += side_ops
    out_specs += side_specs
    out_shape += side_shapes
    vmem += side_vmem
    aliases = {}
    if prev_out is not None:
        extra, aliases = _alias_tail(len(args), prev_out)
        in_specs += extra
        args += list(prev_out)
    outs = pl.pallas_call(
        functools.partial(_proj_body, n_rope=n_rope, n_qkv=n_qkv, n_chunks=n_chunks, nb=rows.nb,
                          n_alias=len(aliases), n_side=len(side_cast)),
        grid=rows.grid,
        in_specs=in_specs,
        out_specs=out_specs,
        out_shape=out_shape,
        input_output_aliases=aliases,
        compiler_params=_cparams(("parallel", "arbitrary"), vmem),
        name="mixer_proj" + ("_meta" if rows.meta else ""),
    )(*args)
    return tuple(outs[:3]) + tuple(o.reshape(w.shape) for o, w in zip(outs[3:], side_cast))


def _stack_heads(q):
    return jnp.concatenate([q[:, h * HEAD_DIM:(h + 1) * HEAD_DIM] for h in range(Q_PER_KV)], axis=0)


def _sink_column(sink_ref, g, rows_per_head):
    return jnp.concatenate(
        [jnp.full((rows_per_head, 1), sink_ref[g * Q_PER_KV + h], _F32) for h in range(Q_PER_KV)], axis=0)


def _pad_rows(a, rows):
    return jnp.concatenate([a, jnp.zeros((rows - a.shape[0], a.shape[1]), a.dtype)], axis=0)


def _softmax_pv(q4, k_all, v_all, masks, sink):
    s = lax.dot_general(q4, k_all, (((1,), (1,)), ((), ())), preferred_element_type=_F32) * (HEAD_DIM ** -0.5)
    blocks = []
    for i, mk in enumerate(masks):
        sb = s[:, i * BLOCK:(i + 1) * BLOCK]
        blocks.append(sb if mk is None else jnp.where(mk, sb, NEG_INF))
    mx = jnp.maximum(jnp.max(functools.reduce(jnp.maximum, blocks), axis=-1, keepdims=True), sink)
    p = jnp.concatenate([jnp.exp(sb - mx).astype(_BF16) for sb in blocks], axis=1)
    v_ext = jnp.concatenate([v_all, jnp.ones(v_all.shape, _BF16)], axis=1)
    o = jnp.dot(p, v_ext, preferred_element_type=_F32)
    den = o[:, HEAD_DIM:] + jnp.exp(sink - mx)
    return o[:, :HEAD_DIM] / den


def _attn_body(sink_ref, *refs, qb, ntile, n_meta, n_side):
    q_ref, kp_ref, kc_ref, kn_ref, km_ref, vp_ref, vc_ref, vn_ref, vm_ref = refs[:9]
    o_ref = refs[9 + n_side]
    kbuf, vbuf, s_buf, p_buf = refs[10 + 2 * n_side:]
    _cast_blocks(refs[9:9 + n_side], refs[10 + n_side:10 + 2 * n_side])
    g = pl.program_id(1)
    jt = pl.program_id(2)
    rows = Q_PER_KV * BLOCK
    band = 3 * BLOCK
    meta0 = (qb + 2) * BLOCK
    scale = HEAD_DIM ** -0.5
    c_exp2 = scale * LOG2_E

    kbuf[0:BLOCK] = kp_ref[0]
    kbuf[BLOCK:(qb + 1) * BLOCK] = kc_ref[0]
    kbuf[(qb + 1) * BLOCK:meta0] = kn_ref[0]
    kbuf[meta0:meta0 + BLOCK] = _pad_rows(km_ref[0], BLOCK)
    vbuf[0:BLOCK, 0:HEAD_DIM] = vp_ref[0]
    vbuf[BLOCK:(qb + 1) * BLOCK, 0:HEAD_DIM] = vc_ref[0]
    vbuf[(qb + 1) * BLOCK:meta0, 0:HEAD_DIM] = vn_ref[0]
    vbuf[meta0:meta0 + BLOCK, 0:HEAD_DIM] = _pad_rows(vm_ref[0], BLOCK)
    vbuf[:, HEAD_DIM:] = jnp.ones((meta0 + BLOCK, HEAD_DIM), _BF16)

    row = lax.broadcasted_iota(jnp.int32, (rows, BLOCK), 0) % BLOCK
    col = lax.broadcasted_iota(jnp.int32, (rows, BLOCK), 1)
    diff = col - row
    never = 2 * BLOCK
    sink_s = _sink_column(sink_ref, g, BLOCK) * (1.0 / scale)
    meta_fill = jnp.where(col == n_meta, sink_s, NEG_INF)
    meta_ok = col < n_meta

    def dot_t(a, b):
        return lax.dot_general(a, b, (((1,), (1,)), ((), ())), preferred_element_type=_F32)

    def stage_a(i, slot):
        r0 = pl.multiple_of(i * BLOCK, BLOCK)
        q4 = _stack_heads(q_ref[0, pl.ds(r0, BLOCK), :])
        s_buf[slot, :, 0:band] = dot_t(q4, kbuf[pl.ds(r0, band), :])
        s_buf[slot, :, band:] = dot_t(q4, kbuf[meta0:meta0 + BLOCK, :])

    def stage_b(i, slot):
        first = jnp.logical_and(jt == 0, i == 0)
        last = jnp.logical_and(jt == ntile - 1, i == qb - 1)
        off_p = jnp.where(first, never, 0)
        off_n = jnp.where(last, -never, 0)
        blocks = [jnp.where(diff >= off_p, s_buf[slot, :, 0:BLOCK], NEG_INF),
                  s_buf[slot, :, BLOCK:2 * BLOCK],
                  jnp.where(diff <= off_n, s_buf[slot, :, 2 * BLOCK:band], NEG_INF),
                  jnp.where(meta_ok, s_buf[slot, :, band:], meta_fill)]
        mx = jnp.max(functools.reduce(jnp.maximum, blocks), axis=-1, keepdims=True)
        for n, sb in enumerate(blocks):
            p_buf[slot, :, n * BLOCK:(n + 1) * BLOCK] = jnp.exp2((sb - mx) * c_exp2).astype(_BF16)

    def stage_c(i, slot):
        r0 = pl.multiple_of(i * BLOCK, BLOCK)
        o = (jnp.dot(p_buf[slot, :, 0:band], vbuf[pl.ds(r0, band), :], preferred_element_type=_F32)
             + jnp.dot(p_buf[slot, :, band:], vbuf[meta0:meta0 + BLOCK, :], preferred_element_type=_F32))
        out = o[:, :HEAD_DIM] / o[:, HEAD_DIM:]
        for h in range(Q_PER_KV):
            o_ref[0, pl.ds(r0, BLOCK), h * HEAD_DIM:(h + 1) * HEAD_DIM] = (
                out[h * BLOCK:(h + 1) * BLOCK].astype(_BF16))

    stage_a(0, 0)
    stage_a(1, 1)
    stage_b(0, 0)

    def pair(t, carry):
        i = 2 * t
        stage_c(i, 0)
        stage_b(i + 1, 1)
        stage_a(i + 2, 0)
        stage_c(i + 1, 1)
        stage_b(i + 2, 0)
        stage_a(i + 3, 1)
        return carry

    lax.fori_loop(0, (qb - 2) // 2, pair, 0)
    stage_c(qb - 2, 0)
    stage_b(qb - 1, 1)
    stage_c(qb - 1, 1)


def _attn_meta_body(sink_ref, q_ref, k0_ref, km_ref, v0_ref, vm_ref, prev_out_ref, o_ref, *, n_meta):
    del prev_out_ref
    g = pl.program_id(1)
    rows = Q_PER_KV * n_meta
    row = lax.broadcasted_iota(jnp.int32, (rows, BLOCK), 0) % n_meta
    col = lax.broadcasted_iota(jnp.int32, (rows, BLOCK), 1)
    q4 = _stack_heads(q_ref[0])
    masks = [col <= row + (WINDOW - n_meta), col < n_meta]
    k_all = jnp.concatenate([k0_ref[0], _pad_rows(km_ref[0], BLOCK)], axis=0)
    v_all = jnp.concatenate([v0_ref[0], _pad_rows(vm_ref[0], BLOCK)], axis=0)
    out = _softmax_pv(q4, k_all, v_all, masks, _sink_column(sink_ref, g, n_meta))
    for h in range(Q_PER_KV):
        o_ref[0, :, h * HEAD_DIM:(h + 1) * HEAD_DIM] = out[h * n_meta:(h + 1) * n_meta].astype(_BF16)


def _attention(qkv, sink, *, n_real, n_meta, qb, side_cast=()):
    B, T, _ = qkv.shape
    nblk = n_real // BLOCK
    ntile = nblk // qb
    kcol = Q_DIM // HEAD_DIM
    vcol = (Q_DIM + KV_DIM) // HEAD_DIM
    mrow = n_real // n_meta

    def edge_spec(rowf, col0):
        return pl.BlockSpec((1, BLOCK, HEAD_DIM), lambda b, g, j, s: (b, rowf(j), col0 + g))

    def main_spec(col0):
        return pl.BlockSpec((1, qb * BLOCK, HEAD_DIM), lambda b, g, j, s: (b, j, col0 + g))

    def meta_spec(col0):
        return pl.BlockSpec((1, n_meta, HEAD_DIM), lambda b, g, j, s: (b, mrow, col0 + g))

    def prev_i(j):
        return jnp.maximum(qb * j - 1, 0)

    def next_i(j):
        return jnp.minimum(qb * j + qb, nblk - 1)

    side_specs, side_ops, side_shapes, side_vmem = _side_cast(
        side_cast, B * N_KV_HEADS * ntile, lambda b, g, j, s: (b * N_KV_HEADS + g) * ntile + j)
    attn, *cast = pl.pallas_call(
        functools.partial(_attn_body, qb=qb, ntile=ntile, n_meta=n_meta, n_side=len(side_cast)),
        grid_spec=pltpu.PrefetchScalarGridSpec(
            num_scalar_prefetch=1,
            grid=(B, N_KV_HEADS, ntile),
            in_specs=[
                pl.BlockSpec((1, qb * BLOCK, PROJ_CHUNK), lambda b, g, j, s: (b, j, g)),
                edge_spec(prev_i, kcol), main_spec(kcol), edge_spec(next_i, kcol), meta_spec(kcol),
                edge_spec(prev_i, vcol), main_spec(vcol), edge_spec(next_i, vcol), meta_spec(vcol),
            ] + side_specs,
            out_specs=[pl.BlockSpec((1, qb * BLOCK, PROJ_CHUNK), lambda b, g, j, s: (b, j, g))] + side_specs,
            scratch_shapes=[
                pltpu.VMEM(((qb + 3) * BLOCK, HEAD_DIM), _BF16),
                pltpu.VMEM(((qb + 3) * BLOCK, 2 * HEAD_DIM), _BF16),
                pltpu.VMEM((2, Q_PER_KV * BLOCK, 4 * BLOCK), _F32),
                pltpu.VMEM((2, Q_PER_KV * BLOCK, 4 * BLOCK), _BF16),
            ],
        ),
        out_shape=[jax.ShapeDtypeStruct((B, T, Q_DIM), _BF16)] + side_shapes,
        compiler_params=_cparams(("parallel", "parallel", "arbitrary"), SMALL_CALL_VMEM_BYTES + side_vmem),
        name="windowed_gqa",
    )(sink, qkv, qkv, qkv, qkv, qkv, qkv, qkv, qkv, qkv, *side_ops)
    cast = [c.reshape(w.shape) for c, w in zip(cast, side_cast)]

    def blk0_spec(col0):
        return pl.BlockSpec((1, BLOCK, HEAD_DIM), lambda b, g, s: (b, 0, col0 + g))

    def meta2_spec(col0):
        return pl.BlockSpec((1, n_meta, HEAD_DIM), lambda b, g, s: (b, mrow, col0 + g))

    return pl.pallas_call(
        functools.partial(_attn_meta_body, n_meta=n_meta),
        grid_spec=pltpu.PrefetchScalarGridSpec(
            num_scalar_prefetch=1,
            grid=(B, N_KV_HEADS),
            in_specs=[
                pl.BlockSpec((1, n_meta, PROJ_CHUNK), lambda b, g, s: (b, mrow, g)),
                blk0_spec(kcol), meta2_spec(kcol), blk0_spec(vcol), meta2_spec(vcol),
                pl.BlockSpec(memory_space=pl.ANY),
            ],
            out_specs=pl.BlockSpec((1, n_meta, PROJ_CHUNK), lambda b, g, s: (b, mrow, g)),
        ),
        out_shape=jax.ShapeDtypeStruct((B, T, Q_DIM), _BF16),
        input_output_aliases={6: 0},
        compiler_params=_cparams(("parallel", "arbitrary"), SMALL_CALL_VMEM_BYTES),
        name="windowed_gqa_meta",
    )(sink, qkv, qkv, qkv, qkv, qkv, attn), cast


def _fft_factors(T):
    best = None
    for n2 in range(BF16_SUBLANES, T + 1, BF16_SUBLANES):
        if T % n2:
            continue
        n1 = T // n2
        r1 = n1 * BF16_SUBLANES
        cost = (4 * _round_up(r1, MXU_DEPTH) * r1 * (n2 // BF16_SUBLANES)
                + 2 * _round_up(n2, MXU_DEPTH) * n2 * n1)
        if best is None or cost < best[0]:
            best = (cost, n1, n2)
    assert best is not None, "sequence length must be a multiple of 16"
    return best[1], best[2]


def _fft1_body(z_ref, mc_ref, ms_ref, twc_ref, tws_ref, a_ref, *, n1, R, C):
    rows = n1 * R
    ct = min(DFT_COL_TILE, C)
    twc = jnp.concatenate([twc_ref[0]] * (ct // LANES), axis=1)
    tws = jnp.concatenate([tws_ref[0]] * (ct // LANES), axis=1)
    for t in range(C // ct):
        cols = slice(t * ct, (t + 1) * ct)
        rhs = jnp.concatenate([z_ref[0, 0, :, :, cols].reshape(rows, ct), z_ref[0, 1, :, :, cols].reshape(rows, ct)],
                              axis=1)
        xc = jnp.dot(mc_ref[...], rhs, preferred_element_type=_F32)
        xs = jnp.dot(ms_ref[...], rhs, preferred_element_type=_F32)
        ar = xc[:, :ct] + xs[:, ct:]
        ai = xc[:, ct:] - xs[:, :ct]
        a_ref[0, 0, :, :, cols] = (ar * twc + ai * tws).reshape(n1, R, ct).astype(_BF16)
        a_ref[0, 1, :, :, cols] = (ai * twc - ar * tws).reshape(n1, R, ct).astype(_BF16)


def _fft2_body(ar_ref, ai_ref, mc_ref, ms_ref, f_ref):
    for b in range(ar_ref.shape[0]):
        res = (jnp.dot(mc_ref[0], ar_ref[b, 0, 0], preferred_element_type=_F32)
               + jnp.dot(ms_ref[0], ai_ref[b, 0, 0], preferred_element_type=_F32))
        f_ref[b] = res.astype(_BF16).reshape(f_ref.shape[1:])


def _unpermute_body(p_ref, x_ref, o_ref):
    for b in range(x_ref.shape[0]):
        o_ref[b] = jnp.dot(p_ref[...], x_ref[b], preferred_element_type=_F32).astype(_BF16)


def _fft_tables(T, n1, n2, n_meta, R):
    two_pi = 2.0 * np.pi
    k1 = jnp.arange(n1, dtype=jnp.int32)
    ang1 = ((k1[:, None] * k1[None, :]) % n1).astype(_F32) * (two_pi / n1)
    eye = jnp.eye(R, dtype=_F32)
    m1c = jnp.kron(jnp.cos(ang1) * (n1 ** -0.5), eye).astype(_BF16)
    m1s = jnp.kron(jnp.sin(ang1) * (n1 ** -0.5), eye).astype(_BF16)
    nn2 = jnp.arange(n2, dtype=jnp.int32).reshape(n2 // R, 1, R)
    angt = ((k1[None, :, None] * (nn2 + n_meta)) % T).astype(_F32) * (two_pi / T)
    angt = jnp.broadcast_to(angt.reshape(n2 // R, n1 * R, 1), (n2 // R, n1 * R, LANES))
    twc, tws = jnp.cos(angt), jnp.sin(angt)
    k2 = jnp.arange(n2, dtype=jnp.int32)[:, None]
    mm2 = jnp.arange(n2, dtype=jnp.int32)[None, :]
    outs_c, outs_s = [], []
    for rot in (n_meta // n1, n_meta // n1 + 1):
        ph2 = (((k2 + rot) % n2) * (mm2 + n_meta)) % n2
        ang2 = ph2.astype(_F32) * (two_pi / n2)
        outs_c.append(jnp.cos(ang2) * (n2 ** -0.5))
        outs_s.append(jnp.sin(ang2) * (n2 ** -0.5))
    return m1c, m1s, twc, tws, jnp.stack(outs_c).astype(_BF16), jnp.stack(outs_s).astype(_BF16)


def _position_dft(z, tables, *, n1, n2, R, n_meta):
    B, _, T, C = z.shape
    m1c, m1s, twc, tws, m2c, m2s = tables
    rows = n1 * R
    zv = z.reshape(B, 2, n1, n2, C)
    av = pl.pallas_call(
        functools.partial(_fft1_body, n1=n1, R=R, C=C),
        grid=(B, n2 // R),
        in_specs=[
            pl.BlockSpec((1, 2, n1, R, C), lambda b, t: (b, 0, 0, t, 0)),
            _const_spec((rows, rows)),
            _const_spec((rows, rows)),
            pl.BlockSpec((1, rows, LANES), lambda b, t: (t, 0, 0)),
            pl.BlockSpec((1, rows, LANES), lambda b, t: (t, 0, 0)),
        ],
        out_specs=pl.BlockSpec((1, 2, n1, R, C), lambda b, t: (b, 0, 0, t, 0)),
        out_shape=jax.ShapeDtypeStruct((B, 2, n1, n2, C), _BF16),
        compiler_params=_cparams(("parallel", "arbitrary"),
                                 4 * 2 * rows * C * 2 + 2 * 2 * rows * rows * 2 + 6 * rows * C * 4 + VMEM_RESERVE_BYTES),
        name="dft_stage1",
    )(zv, m1c, m1s, twc, tws)
    shift = n1 - (n_meta % n1)

    f = pl.pallas_call(
        _fft2_body,
        grid=(n1,),
        in_specs=[
            pl.BlockSpec((B, 1, 1, n2, C), lambda k: (0, 0, k, 0, 0)),
            pl.BlockSpec((B, 1, 1, n2, C), lambda k: (0, 1, k, 0, 0)),
            pl.BlockSpec((1, n2, n2), lambda k: (jnp.where(k < n_meta % n1, 1, 0), 0, 0)),
            pl.BlockSpec((1, n2, n2), lambda k: (jnp.where(k < n_meta % n1, 1, 0), 0, 0)),
        ],
        out_specs=pl.BlockSpec((B, n2 // R, None, R, C), lambda k: (0, 0, (k + shift) % n1, 0, 0)),
        out_shape=jax.ShapeDtypeStruct((B, n2 // R, n1, R, C), _BF16),
        compiler_params=_cparams(("arbitrary",), SMALL_CALL_VMEM_BYTES),
        name="dft_stage2",
    )(av, av, m2c, m2s)
    q = jnp.arange(rows, dtype=jnp.int32)
    src = R * (q % n1) + q // n1
    perm = (jnp.arange(rows, dtype=jnp.int32)[None, :] == src[:, None]).astype(_BF16)
    return pl.pallas_call(
        _unpermute_body,
        grid=(n2 // R,),
        in_specs=[_const_spec((rows, rows)), pl.BlockSpec((B, rows, C), lambda i: (0, i, 0))],
        out_specs=pl.BlockSpec((B, rows, C), lambda i: (0, i, 0)),
        out_shape=jax.ShapeDtypeStruct((B, T, C), _BF16),
        compiler_params=_cparams(("parallel",), SMALL_CALL_VMEM_BYTES),
        name="dft_unpermute",
    )(perm, f.reshape(B, T, C))


def _mix_body(*refs, tn, n_alias):
    h_ref, xn_ref, a_ref, f_ref, wpa_ref, wpf_ref, wg_ref, wo_ref = refs[:8]
    (o_ref,) = refs[8 + n_alias:]
    D = h_ref.shape[-1]
    o_ref[...] = h_ref[...]
    xn = xn_ref[...].reshape(-1, D)
    a = a_ref[...].reshape(-1, Q_DIM)
    f = f_ref[...].reshape(-1, FOURIER_DIM)
    for j in range(D // tn):
        cols = slice(j * tn, (j + 1) * tn)
        av = jnp.dot(a, wpa_ref[:, cols], preferred_element_type=_F32)
        fv = jnp.dot(f, wpf_ref[:, cols], preferred_element_type=_F32)
        ga = jax.nn.sigmoid(jnp.dot(xn, wg_ref[:, cols], preferred_element_type=_F32))
        gf = jax.nn.sigmoid(jnp.dot(xn, wg_ref[:, D + j * tn:D + (j + 1) * tn], preferred_element_type=_F32))
        mixed = (ga * av + gf * fv).astype(_BF16)
        o_ref[...] += jnp.dot(mixed, wo_ref[cols, :], preferred_element_type=_F32).reshape(o_ref.shape)


def _mix_out(rows, h, xn, attn, four, wpa, wpf, wg, wo, *, layer, tn, prev_out=None):
    B, T, D = h.shape

    def resident(k, n):
        return pl.BlockSpec((None, k, n), lambda *_: (layer, 0, 0), pipeline_mode=pl.Buffered(1))

    in_specs = [rows.spec(D), rows.spec(D), rows.spec(Q_DIM), rows.spec(FOURIER_DIM),
                resident(Q_DIM, D), resident(FOURIER_DIM, D), resident(D, 2 * D), resident(D, D)]
    args = [h, xn, attn, four, wpa, wpf, wg, wo]
    aliases = {}
    if prev_out is not None:
        extra, aliases = _alias_tail(len(args), [prev_out])
        in_specs += extra
        args.append(prev_out)
    m = rows.m
    vmem = ((Q_DIM + FOURIER_DIM + 3 * D) * D * 2 + 4 * m * D * 4 + 2 * m * D * 2
            + 2 * m * (Q_DIM + FOURIER_DIM) * 2 + 10 * m * tn * 4 + VMEM_RESERVE_BYTES)
    return pl.pallas_call(
        functools.partial(_mix_body, tn=tn, n_alias=len(aliases)),
        grid=rows.grid,
        in_specs=in_specs,
        out_specs=rows.spec(D),
        out_shape=jax.ShapeDtypeStruct((B, T, D), _F32),
        input_output_aliases=aliases,
        compiler_params=_cparams(("parallel", "arbitrary"), vmem),
        name="mixer_out" + ("_meta" if rows.meta else ""),
    )(*args)


def _rope_tables(n_real, n_meta):
    T = n_real + n_meta
    half = ROT_DIM // 2
    inv = 1.0 / (ROPE_THETA ** (jnp.arange(0, ROT_DIM, 2, dtype=_F32) / ROT_DIM))
    ang = jnp.arange(T, dtype=_F32)[:, None] * inv[None, :]
    cos, sin = jnp.cos(ang), jnp.sin(ang)
    cos = jnp.concatenate([cos[n_meta:], cos[:n_meta]], axis=0)
    sin = jnp.concatenate([sin[n_meta:], sin[:n_meta]], axis=0)
    ones = jnp.ones((T, HEAD_DIM - ROT_DIM), _F32)
    zeros_h = jnp.zeros((T, half), _F32)
    zeros_r = jnp.zeros((T, HEAD_DIM - ROT_DIM), _F32)
    c = jnp.concatenate([cos, cos, ones], axis=1)
    s1 = jnp.concatenate([-sin, zeros_h, zeros_r], axis=1)
    s2 = jnp.concatenate([zeros_h, sin, zeros_r], axis=1)
    return c, s1, s2


def _channel_dft_table():
    n = FOURIER_GROUP_DIM
    c = jnp.arange(n, dtype=jnp.int32)[:, None]
    k = jnp.arange(n, dtype=jnp.int32)[None, :]
    ang = ((c * k) % n).astype(_F32) * (2.0 * np.pi / n)
    return (jnp.concatenate([jnp.cos(ang), -jnp.sin(ang)], axis=1) * (n ** -0.5)).astype(_BF16)


def _pick_tile(n, pref):
    t = pref
    while n % t:
        t //= 2
    return t


def kernel(x, meta_tokens, ffn1_norm, ffn1_w_gate, ffn1_w_up, ffn1_w_down, mix_norm, w_in, w_gate, sink,
           w_attn_branch, w_fourier_branch, w_out, ffn2_norm, ffn2_w_gate, ffn2_w_up, ffn2_w_down, final_norm):
    B, S, D = x.shape
    n_meta = meta_tokens.shape[0]
    depth = ffn1_norm.shape[0]
    d_ff = ffn1_w_gate.shape[-1]
    T = S + n_meta
    assert S % (4 * BLOCK) == 0 and S % n_meta == 0 and BLOCK % n_meta == 0 and n_meta < BLOCK
    assert w_in.shape[-1] == IN_DIM and KV_DIM == PROJ_CHUNK

    tf = _pick_tile(d_ff, FF_CHUNK)
    tn = _pick_tile(D, MIX_CHUNK)
    qb = _pick_tile(S // BLOCK, ATTN_BLOCKS_PER_STEP)
    rows_ffn = _Rows(B, S, n_meta, _pick_tile(S, FFN_ROW_TILE), meta=False)
    rows_main = _Rows(B, S, n_meta, _pick_tile(S, PROJ_ROW_TILE), meta=False)
    rows_mix = _Rows(B, S, n_meta, _pick_tile(S, MIX_ROW_TILE), meta=False)
    rows_meta = _Rows(B, S, n_meta, 0, meta=True)

    n1, n2 = _fft_factors(T)
    fft_tables = _fft_tables(T, n1, n2, n_meta, BF16_SUBLANES)
    rope_c, rope_s1, rope_s2 = _rope_tables(S, n_meta)
    dft_c = _channel_dft_table()

    ffn1_w = tuple(_to_bf16(w) for w in (ffn1_w_gate, ffn1_w_up, ffn1_w_down))
    w_in_b = _to_bf16(w_in)

    def ffn(h, h_meta, gain, wts, layer, final_gain=None):
        out = _ffn(rows_ffn, h, gain, *wts, layer=layer, tf=tf, T=T, final_gain=final_gain)
        if final_gain is None:
            src, blk0 = h_meta
            out = _ffn(rows_meta, src, gain, *wts, layer=layer, tf=tf, T=T, h_row_block0=blk0, prev_out=out)
        return out

    meta = jnp.broadcast_to(meta_tokens[None].astype(x.dtype), (B, n_meta, D))
    h = None
    for l in range(depth):
        if l == 0:
            h = ffn(x, (meta, 0), ffn1_norm[l][None], ffn1_w, l)
        else:
            h = ffn(h, (h, None), ffn1_norm[l][None], ffn1_w, l)

        gain = mix_norm[l][None]
        proj_args = (gain, w_in_b, rope_c, rope_s1, rope_s2, dft_c)
        if l == 0:
            qkv, z, xn, *ffn2_w = _proj(rows_main, h, *proj_args, layer=l,
                                        side_cast=(ffn2_w_gate, ffn2_w_up, ffn2_w_down))
        else:
            qkv, z, xn = _proj(rows_main, h, *proj_args, layer=l)
        qkv, z, xn = _proj(rows_meta, h, *proj_args, layer=l, prev_out=(qkv, z, xn))
        if l == 0:
            attn, mix_w = _attention(qkv, sink[l], n_real=S, n_meta=n_meta, qb=qb,
                                     side_cast=(w_attn_branch, w_fourier_branch, w_gate, w_out))
        else:
            attn, _ = _attention(qkv, sink[l], n_real=S, n_meta=n_meta, qb=qb)
        four = _position_dft(z, fft_tables, n1=n1, n2=n2, R=BF16_SUBLANES, n_meta=n_meta)
        mix_args = (xn, attn, four) + tuple(mix_w)
        h2 = _mix_out(rows_mix, h, *mix_args, layer=l, tn=tn)
        h = _mix_out(rows_meta, h, *mix_args, layer=l, tn=tn, prev_out=h2)

        last = l == depth - 1
        h = ffn(h, (h, None), ffn2_norm[l][None], ffn2_w, l, final_gain=final_norm[None] if last else None)
    return h
```

```python
import functools

import numpy as np
import jax
import jax.numpy as jnp
from jax import lax
from jax.experimental import pallas as pl
from jax.experimental.pallas import tpu as pltpu

_F32 = jnp.float32
_BF16 = jnp.bfloat16

HEAD_DIM = 128
N_Q_HEADS = 16
N_KV_HEADS = 4
Q_PER_KV = N_Q_HEADS // N_KV_HEADS
WINDOW = 128
BLOCK = 128
ROPE_THETA = 500000.0
ROT_DIM = HEAD_DIM // 4
N_FOURIER_GROUPS = 8
FOURIER_GROUP_DIM = 128
FOURIER_DIM = N_FOURIER_GROUPS * FOURIER_GROUP_DIM
Q_DIM = N_Q_HEADS * HEAD_DIM
KV_DIM = N_KV_HEADS * HEAD_DIM
IN_DIM = Q_DIM + 2 * KV_DIM + FOURIER_DIM
RMS_EPS = 1e-6
NEG_INF = -1e30

LANES = 128
BF16_SUBLANES = 16
PROJ_CHUNK = Q_PER_KV * HEAD_DIM
V7X_VMEM_BYTES = 64 * 1024 * 1024
VMEM_RESERVE_BYTES = 4 << 20
SMALL_CALL_VMEM_BYTES = 32 << 20
CAST_BLOCK_BYTES = 6 << 20
LOG2_E = 1.4426950408889634
MXU_DEPTH = 256

FFN_ROW_TILE = 1024
FF_CHUNK = 512
FFN_COL_TILE = 256
DFT_COL_TILE = 256
PROJ_ROW_TILE = 512
MIX_ROW_TILE = 256
MIX_CHUNK = 1024
ATTN_BLOCKS_PER_STEP = 32


def _round_up(a, m):
    return (a + m - 1) // m * m


def _cparams(sem, vmem_bytes):
    return pltpu.CompilerParams(dimension_semantics=sem,
                                vmem_limit_bytes=int(min(vmem_bytes, V7X_VMEM_BYTES - VMEM_RESERVE_BYTES)))


def _rms(x, g):
    return x * lax.rsqrt(jnp.mean(x * x, axis=-1, keepdims=True) + RMS_EPS) * g


class _Rows:
    def __init__(self, B, n_real, n_meta, tm, meta):
        self.B, self.n_real, self.n_meta, self.meta = B, n_real, n_meta, meta
        self.nb = B if meta else 1
        self.rows = n_meta if meta else tm
        self.grid = (1, 1) if meta else (B, n_real // tm)
        self.m = self.nb * self.rows

    def spec(self, width, lead=(), row_block0=None):
        nl = len(lead)
        if self.meta:
            r0 = self.n_real // self.n_meta if row_block0 is None else row_block0
            return pl.BlockSpec((self.nb,) + tuple(lead) + (self.rows, width),
                                lambda b, i, *_: (0,) + (0,) * nl + (r0, 0))
        return pl.BlockSpec((1,) + tuple(lead) + (self.rows, width),
                            lambda b, i, *_: (b,) + (0,) * nl + (i, 0))

    def table_spec(self, width):
        if self.meta:
            r0 = self.n_real // self.n_meta
            return pl.BlockSpec((self.rows, width), lambda b, i, *_: (r0, 0))
        return pl.BlockSpec((self.rows, width), lambda b, i, *_: (i, 0))


def _const_spec(shape, single_buffer=False):
    nd = len(shape)
    if single_buffer:
        return pl.BlockSpec(shape, lambda *_: (0,) * nd, pipeline_mode=pl.Buffered(1))
    return pl.BlockSpec(shape, lambda *_: (0,) * nd)


def _alias_tail(n_in, outs):
    specs = [pl.BlockSpec(memory_space=pl.ANY) for _ in outs]
    return specs, {n_in + k: k for k in range(len(outs))}


def _side_cast(stacks, n_steps, step_index):
    specs, operands, shapes, vmem = [], [], [], 0
    for w in stacks:
        L, K, N = w.shape
        rps = L * K // n_steps
        assert L * K % n_steps == 0 and rps % BF16_SUBLANES == 0
        specs.append(pl.BlockSpec((rps, N), lambda *idx: (step_index(*idx), 0)))
        operands.append(w.reshape(L * K, N))
        shapes.append(jax.ShapeDtypeStruct((L * K, N), _BF16))
        vmem += 2 * rps * N * (4 + 2)
    return specs, operands, shapes, vmem


def _cast_blocks(src_refs, dst_refs):
    for src_ref, dst_ref in zip(src_refs, dst_refs):
        dst_ref[...] = src_ref[...].astype(_BF16)


def _cast_body(w_ref, o_ref):
    o_ref[...] = w_ref[...].astype(_BF16)


def _to_bf16(w):
    L, K, N = w.shape
    tk = _pick_tile(K, 512)
    while tk > 8 and tk * N * 4 > CAST_BLOCK_BYTES:
        tk //= 2
    return pl.pallas_call(
        _cast_body,
        grid=(L, K // tk),
        in_specs=[pl.BlockSpec((1, tk, N), lambda l, i: (l, i, 0))],
        out_specs=pl.BlockSpec((1, tk, N), lambda l, i: (l, i, 0)),
        out_shape=jax.ShapeDtypeStruct(w.shape, _BF16),
        compiler_params=_cparams(("parallel", "parallel"), 2 * tk * N * (4 + 2) + VMEM_RESERVE_BYTES),
        name="cast_bf16",
    )(w)


def _ffn_body(*refs, nj, final, n_alias):
    h_ref, gain_ref, wg_ref, wu_ref, wd_ref = refs[:5]
    rest = refs[5:]
    fgain_ref = None
    if final:
        fgain_ref, rest = rest[0], rest[1:]
    o_ref, xn_ref = rest[n_alias:]
    j = pl.program_id(2)
    D = h_ref.shape[-1]

    @pl.when(j == 0)
    def _():
        x = h_ref[...].reshape(-1, D)
        xn_ref[...] = _rms(x, gain_ref[...]).astype(_BF16)
        o_ref[...] = x.reshape(o_ref.shape)

    tf = wg_ref.shape[-1]
    ct = min(FFN_COL_TILE, tf)
    acts = []
    for t in range(tf // ct):
        cols = slice(t * ct, (t + 1) * ct)
        g = jnp.dot(xn_ref[...], wg_ref[:, cols], preferred_element_type=_F32)
        u = jnp.dot(xn_ref[...], wu_ref[:, cols], preferred_element_type=_F32)
        acts.append(((g * jax.nn.sigmoid(g)) * u * 0.5).astype(_BF16))
    act = jnp.concatenate(acts, axis=1)
    o_ref[...] += jnp.dot(act, wd_ref[...], preferred_element_type=_F32).reshape(o_ref.shape)

    if final:
        @pl.when(j == nj - 1)
        def _():
            o_ref[...] = _rms(o_ref[...].reshape(-1, D), fgain_ref[...]).reshape(o_ref.shape)


def _ffn(rows, h, gain, wg, wu, wd, *, layer, tf, T, final_gain=None, h_row_block0=None, prev_out=None):
    D = h.shape[-1]
    nj = wg.shape[2] // tf
    final = final_gain is not None
    in_specs = [rows.spec(D, row_block0=h_row_block0), _const_spec((1, D)),
                pl.BlockSpec((None, D, tf), lambda b, i, j: (layer, 0, j)),
                pl.BlockSpec((None, D, tf), lambda b, i, j: (layer, 0, j)),
                pl.BlockSpec((None, tf, D), lambda b, i, j: (layer, j, 0))]
    args = [h, gain, wg, wu, wd]
    if final:
        in_specs.append(_const_spec((1, D)))
        args.append(final_gain)
    aliases = {}
    if prev_out is not None:
        extra, aliases = _alias_tail(len(args), [prev_out])
        in_specs += extra
        args.append(prev_out)
    m = rows.m
    vmem = (4 * m * D * 4 + m * D * 2 + 2 * (D * 2 * tf + tf * D) * 2 + m * 2 * tf * 4 + m * tf * 2
            + VMEM_RESERVE_BYTES)
    out_rows = rows.n_real if final else T
    return pl.pallas_call(
        functools.partial(_ffn_body, nj=nj, final=final, n_alias=len(aliases)),
        grid=rows.grid + (nj,),
        in_specs=in_specs,
        out_specs=rows.spec(D),
        out_shape=jax.ShapeDtypeStruct((rows.B, out_rows, D), _F32),
        scratch_shapes=[pltpu.VMEM((m, D), _BF16)],
        input_output_aliases=aliases,
        compiler_params=_cparams(("parallel", "parallel", "arbitrary"), vmem),
        name=("ffn_final" if final else "ffn") + ("_meta" if rows.meta else ""),
    )(*args)


def _proj_body(*refs, n_rope, n_qkv, n_chunks, nb, n_alias, n_side):
    h_ref, gain_ref, w_ref, c_ref, s1_ref, s2_ref, dft_ref = refs[:7]
    side_in = refs[7:7 + n_side]
    qkv_ref, z_ref, xno_ref = refs[7 + n_side + n_alias:10 + n_side + n_alias]
    side_out = refs[10 + n_side + n_alias:]
    _cast_blocks(side_in, side_out)
    D = h_ref.shape[-1]
    rows = h_ref.shape[-2]
    xno_ref[...] = _rms(h_ref[...].reshape(-1, D), gain_ref[...]).astype(_BF16).reshape(xno_ref.shape)
    c = jnp.concatenate([c_ref[...]] * nb, axis=0)
    s1 = jnp.concatenate([s1_ref[...]] * nb, axis=0)
    s2 = jnp.concatenate([s2_ref[...]] * nb, axis=0)
    half = ROT_DIM // 2
    for ch in range(n_chunks):
        cols = slice(ch * PROJ_CHUNK, (ch + 1) * PROJ_CHUNK)
        y = jnp.dot(xno_ref[...].reshape(-1, D), w_ref[:, cols], preferred_element_type=_F32)
        if ch < n_rope:
            for hh in range(PROJ_CHUNK // HEAD_DIM):
                xh = y[:, hh * HEAD_DIM:(hh + 1) * HEAD_DIM]
                up = pltpu.roll(xh, HEAD_DIM - half, 1)
                dn = pltpu.roll(xh, half, 1)
                r = (xh * c + up * s1 + dn * s2).astype(_BF16)
                lo = ch * PROJ_CHUNK + hh * HEAD_DIM
                qkv_ref[:, :, lo:lo + HEAD_DIM] = r.reshape(nb, rows, HEAD_DIM)
        elif ch < n_qkv:
            qkv_ref[:, :, cols] = y.astype(_BF16).reshape(nb, rows, PROJ_CHUNK)
        else:
            ub = y.astype(_BF16)
            for gg in range(PROJ_CHUNK // FOURIER_GROUP_DIM):
                zz = jnp.dot(ub[:, gg * FOURIER_GROUP_DIM:(gg + 1) * FOURIER_GROUP_DIM], dft_ref[...],
                             preferred_element_type=_F32)
                lo = (ch - n_qkv) * PROJ_CHUNK + gg * FOURIER_GROUP_DIM
                z_ref[:, 0, :, lo:lo + FOURIER_GROUP_DIM] = (
                    zz[:, :FOURIER_GROUP_DIM].astype(_BF16).reshape(nb, rows, FOURIER_GROUP_DIM))
                z_ref[:, 1, :, lo:lo + FOURIER_GROUP_DIM] = (
                    zz[:, FOURIER_GROUP_DIM:].astype(_BF16).reshape(nb, rows, FOURIER_GROUP_DIM))


def _proj(rows, h, gain, w_in, rope_c, rope_s1, rope_s2, dft_c, *, layer, prev_out=None, side_cast=()):
    B, T, D = h.shape
    n_rope = (Q_DIM + KV_DIM) // PROJ_CHUNK
    n_qkv = (Q_DIM + 2 * KV_DIM) // PROJ_CHUNK
    n_chunks = IN_DIM // PROJ_CHUNK
    w_spec = pl.BlockSpec((None, D, IN_DIM), lambda *_: (layer, 0, 0), pipeline_mode=pl.Buffered(1))
    in_specs = [rows.spec(D), _const_spec((1, D)), w_spec,
                rows.table_spec(LANES), rows.table_spec(LANES), rows.table_spec(LANES),
                _const_spec((FOURIER_GROUP_DIM, 2 * FOURIER_GROUP_DIM))]
    args = [h, gain, w_in, rope_c, rope_s1, rope_s2, dft_c]
    out_specs = [rows.spec(Q_DIM + 2 * KV_DIM), rows.spec(FOURIER_DIM, lead=(2,)), rows.spec(D)]
    out_shape = [jax.ShapeDtypeStruct((B, T, Q_DIM + 2 * KV_DIM), _BF16),
                 jax.ShapeDtypeStruct((B, 2, T, FOURIER_DIM), _BF16),
                 jax.ShapeDtypeStruct((B, T, D), _BF16)]
    m = rows.m
    vmem = (2 * m * D * 4 + 2 * m * D * 2 + D * IN_DIM * 2 + 6 * m * LANES * 4
            + 2 * m * (Q_DIM + 2 * KV_DIM) * 2 + 4 * m * FOURIER_DIM * 2 + 8 * m * PROJ_CHUNK * 4
            + VMEM_RESERVE_BYTES)
    side_specs, side_ops, side_shapes, side_vmem = _side_cast(
        side_cast, rows.grid[0] * rows.grid[1], lambda b, i: b * rows.grid[1] + i)
    in_specs += side_specs
    args += side_ops
    out_specs += side_specs
    out_shape += side_shapes
    vmem += side_vmem
    aliases = {}
    if prev_out is not None:
        extra, aliases = _alias_tail(len(args), prev_out)
        in_specs += extra
        args += list(prev_out)
    outs = pl.pallas_call(
        functools.partial(_proj_body, n_rope=n_rope, n_qkv=n_qkv, n_chunks=n_chunks, nb=rows.nb,
                          n_alias=len(aliases), n_side=len(side_cast)),
        grid=rows.grid,
        in_specs=in_specs,
        out_specs=out_specs,
        out_shape=out_shape,
        input_output_aliases=aliases,
        compiler_params=_cparams(("parallel", "arbitrary"), vmem),
        name="mixer_proj" + ("_meta" if rows.meta else ""),
    )(*args)
    return tuple(outs[:3]) + tuple(o.reshape(w.shape) for o, w in zip(outs[3:], side_cast))


def _stack_heads(q):
    return jnp.concatenate([q[:, h * HEAD_DIM:(h + 1) * HEAD_DIM] for h in range(Q_PER_KV)], axis=0)


def _sink_column(sink_ref, g, rows_per_head):
    return jnp.concatenate(
        [jnp.full((rows_per_head, 1), sink_ref[g * Q_PER_KV + h], _F32) for h in range(Q_PER_KV)], axis=0)


def _pad_rows(a, rows):
    return jnp.concatenate([a, jnp.zeros((rows - a.shape[0], a.shape[1]), a.dtype)], axis=0)


def _softmax_pv(q4, k_all, v_all, masks, sink):
    s = lax.dot_general(q4, k_all, (((1,), (1,)), ((), ())), preferred_element_type=_F32) * (HEAD_DIM ** -0.5)
    blocks = []
    for i, mk in enumerate(masks):
        sb = s[:, i * BLOCK:(i + 1) * BLOCK]
        blocks.append(sb if mk is None else jnp.where(mk, sb, NEG_INF))
    mx = jnp.maximum(jnp.max(functools.reduce(jnp.maximum, blocks), axis=-1, keepdims=True), sink)
    p = jnp.concatenate([jnp.exp(sb - mx).astype(_BF16) for sb in blocks], axis=1)
    v_ext = jnp.concatenate([v_all, jnp.ones(v_all.shape, _BF16)], axis=1)
    o = jnp.dot(p, v_ext, preferred_element_type=_F32)
    den = o[:, HEAD_DIM:] + jnp.exp(sink - mx)
    return o[:, :HEAD_DIM] / den


def _attn_body(sink_ref, *refs, qb, ntile, n_meta, n_side):
    q_ref, kp_ref, kc_ref, kn_ref, km_ref, vp_ref, vc_ref, vn_ref, vm_ref = refs[:9]
    o_ref = refs[9 + n_side]
    kbuf, vbuf, s_buf, p_buf = refs[10 + 2 * n_side:]
    _cast_blocks(refs[9:9 + n_side], refs[10 + n_side:10 + 2 * n_side])
    g = pl.program_id(1)
    jt = pl.program_id(2)
    rows = Q_PER_KV * BLOCK
    band = 3 * BLOCK
    meta0 = (qb + 2) * BLOCK
    scale = HEAD_DIM ** -0.5
    c_exp2 = scale * LOG2_E

    kbuf[0:BLOCK] = kp_ref[0]
    kbuf[BLOCK:(qb + 1) * BLOCK] = kc_ref[0]
    kbuf[(qb + 1) * BLOCK:meta0] = kn_ref[0]
    kbuf[meta0:meta0 + BLOCK] = _pad_rows(km_ref[0], BLOCK)
    vbuf[0:BLOCK, 0:HEAD_DIM] = vp_ref[0]
    vbuf[BLOCK:(qb + 1) * BLOCK, 0:HEAD_DIM] = vc_ref[0]
    vbuf[(qb + 1) * BLOCK:meta0, 0:HEAD_DIM] = vn_ref[0]
    vbuf[meta0:meta0 + BLOCK, 0:HEAD_DIM] = _pad_rows(vm_ref[0], BLOCK)
    vbuf[:, HEAD_DIM:] = jnp.ones((meta0 + BLOCK, HEAD_DIM), _BF16)

    row = lax.broadcasted_iota(jnp.int32, (rows, BLOCK), 0) % BLOCK
    col = lax.broadcasted_iota(jnp.int32, (rows, BLOCK), 1)
    diff = col - row
    never = 2 * BLOCK
    sink_s = _sink_column(sink_ref, g, BLOCK) * (1.0 / scale)
    meta_fill = jnp.where(col == n_meta, sink_s, NEG_INF)
    meta_ok = col < n_meta

    def dot_t(a, b):
        return lax.dot_general(a, b, (((1,), (1,)), ((), ())), preferred_element_type=_F32)

    def stage_a(i, slot):
        r0 = pl.multiple_of(i * BLOCK, BLOCK)
        q4 = _stack_heads(q_ref[0, pl.ds(r0, BLOCK), :])
        s_buf[slot, :, 0:band] = dot_t(q4, kbuf[pl.ds(r0, band), :])
        s_buf[slot, :, band:] = dot_t(q4, kbuf[meta0:meta0 + BLOCK, :])

    def stage_b(i, slot):
        first = jnp.logical_and(jt == 0, i == 0)
        last = jnp.logical_and(jt == ntile - 1, i == qb - 1)
        off_p = jnp.where(first, never, 0)
        off_n = jnp.where(last, -never, 0)
        blocks = [jnp.where(diff >= off_p, s_buf[slot, :, 0:BLOCK], NEG_INF),
                  s_buf[slot, :, BLOCK:2 * BLOCK],
                  jnp.where(diff <= off_n, s_buf[slot, :, 2 * BLOCK:band], NEG_INF),
                  jnp.where(meta_ok, s_buf[slot, :, band:], meta_fill)]
        mx = jnp.max(functools.reduce(jnp.maximum, blocks), axis=-1, keepdims=True)
        for n, sb in enumerate(blocks):
            p_buf[slot, :, n * BLOCK:(n + 1) * BLOCK] = jnp.exp2((sb - mx) * c_exp2).astype(_BF16)

    def stage_c(i, slot):
        r0 = pl.multiple_of(i * BLOCK, BLOCK)
        o = (jnp.dot(p_buf[slot, :, 0:band], vbuf[pl.ds(r0, band), :], preferred_element_type=_F32)
             + jnp.dot(p_buf[slot, :, band:], vbuf[meta0:meta0 + BLOCK, :], preferred_element_type=_F32))
        out = o[:, :HEAD_DIM] / o[:, HEAD_DIM:]
        for h in range(Q_PER_KV):
            o_ref[0, pl.ds(r0, BLOCK), h * HEAD_DIM:(h + 1) * HEAD_DIM] = (
                out[h * BLOCK:(h + 1) * BLOCK].astype(_BF16))

    stage_a(0, 0)
    stage_a(1, 1)
    stage_b(0, 0)

    def pair(t, carry):
        i = 2 * t
        stage_c(i, 0)
        stage_b(i + 1, 1)
        stage_a(i + 2, 0)
        stage_c(i + 1, 1)
        stage_b(i + 2, 0)
        stage_a(i + 3, 1)
        return carry

    lax.fori_loop(0, (qb - 2) // 2, pair, 0)
    stage_c(qb - 2, 0)
    stage_b(qb - 1, 1)
    stage_c(qb - 1, 1)


def _attn_meta_body(sink_ref, q_ref, k0_ref, km_ref, v0_ref, vm_ref, prev_out_ref, o_ref, *, n_meta):
    del prev_out_ref
    g = pl.program_id(1)
    rows = Q_PER_KV * n_meta
    row = lax.broadcasted_iota(jnp.int32, (rows, BLOCK), 0) % n_meta
    col = lax.broadcasted_iota(jnp.int32, (rows, BLOCK), 1)
    q4 = _stack_heads(q_ref[0])
    masks = [col <= row + (WINDOW - n_meta), col < n_meta]
    k_all = jnp.concatenate([k0_ref[0], _pad_rows(km_ref[0], BLOCK)], axis=0)
    v_all = jnp.concatenate([v0_ref[0], _pad_rows(vm_ref[0], BLOCK)], axis=0)
    out = _softmax_pv(q4, k_all, v_all, masks, _sink_column(sink_ref, g, n_meta))
    for h in range(Q_PER_KV):
        o_ref[0, :, h * HEAD_DIM:(h + 1) * HEAD_DIM] = out[h * n_meta:(h + 1) * n_meta].astype(_BF16)


def _attention(qkv, sink, *, n_real, n_meta, qb, side_cast=(), meta_queries=True):
    B, T, _ = qkv.shape
    nblk = n_real // BLOCK
    ntile = nblk // qb
    kcol = Q_DIM // HEAD_DIM
    vcol = (Q_DIM + KV_DIM) // HEAD_DIM
    mrow = n_real // n_meta

    def edge_spec(rowf, col0):
        return pl.BlockSpec((1, BLOCK, HEAD_DIM), lambda b, g, j, s: (b, rowf(j), col0 + g))

    def main_spec(col0):
        return pl.BlockSpec((1, qb * BLOCK, HEAD_DIM), lambda b, g, j, s: (b, j, col0 + g))

    def meta_spec(col0):
        return pl.BlockSpec((1, n_meta, HEAD_DIM), lambda b, g, j, s: (b, mrow, col0 + g))

    def prev_i(j):
        return jnp.maximum(qb * j - 1, 0)

    def next_i(j):
        return jnp.minimum(qb * j + qb, nblk - 1)

    side_specs, side_ops, side_shapes, side_vmem = _side_cast(
        side_cast, B * N_KV_HEADS * ntile, lambda b, g, j, s: (b * N_KV_HEADS + g) * ntile + j)
    attn, *cast = pl.pallas_call(
        functools.partial(_attn_body, qb=qb, ntile=ntile, n_meta=n_meta, n_side=len(side_cast)),
        grid_spec=pltpu.PrefetchScalarGridSpec(
            num_scalar_prefetch=1,
            grid=(B, N_KV_HEADS, ntile),
            in_specs=[
                pl.BlockSpec((1, qb * BLOCK, PROJ_CHUNK), lambda b, g, j, s: (b, j, g)),
                edge_spec(prev_i, kcol), main_spec(kcol), edge_spec(next_i, kcol), meta_spec(kcol),
                edge_spec(prev_i, vcol), main_spec(vcol), edge_spec(next_i, vcol), meta_spec(vcol),
            ] + side_specs,
            out_specs=[pl.BlockSpec((1, qb * BLOCK, PROJ_CHUNK), lambda b, g, j, s: (b, j, g))] + side_specs,
            scratch_shapes=[
                pltpu.VMEM(((qb + 3) * BLOCK, HEAD_DIM), _BF16),
                pltpu.VMEM(((qb + 3) * BLOCK, 2 * HEAD_DIM), _BF16),
                pltpu.VMEM((2, Q_PER_KV * BLOCK, 4 * BLOCK), _F32),
                pltpu.VMEM((2, Q_PER_KV * BLOCK, 4 * BLOCK), _BF16),
            ],
        ),
        out_shape=[jax.ShapeDtypeStruct((B, T, Q_DIM), _BF16)] + side_shapes,
        compiler_params=_cparams(("parallel", "parallel", "arbitrary"), SMALL_CALL_VMEM_BYTES + side_vmem),
        name="windowed_gqa",
    )(sink, qkv, qkv, qkv, qkv, qkv, qkv, qkv, qkv, qkv, *side_ops)
    cast = [c.reshape(w.shape) for c, w in zip(cast, side_cast)]
    if not meta_queries:
        return attn, cast

    def blk0_spec(col0):
        return pl.BlockSpec((1, BLOCK, HEAD_DIM), lambda b, g, s: (b, 0, col0 + g))

    def meta2_spec(col0):
        return pl.BlockSpec((1, n_meta, HEAD_DIM), lambda b, g, s: (b, mrow, col0 + g))

    return pl.pallas_call(
        functools.partial(_attn_meta_body, n_meta=n_meta),
        grid_spec=pltpu.PrefetchScalarGridSpec(
            num_scalar_prefetch=1,
            grid=(B, N_KV_HEADS),
            in_specs=[
                pl.BlockSpec((1, n_meta, PROJ_CHUNK), lambda b, g, s: (b, mrow, g)),
                blk0_spec(kcol), meta2_spec(kcol), blk0_spec(vcol), meta2_spec(vcol),
                pl.BlockSpec(memory_space=pl.ANY),
            ],
            out_specs=pl.BlockSpec((1, n_meta, PROJ_CHUNK), lambda b, g, s: (b, mrow, g)),
        ),
        out_shape=jax.ShapeDtypeStruct((B, T, Q_DIM), _BF16),
        input_output_aliases={6: 0},
        compiler_params=_cparams(("parallel", "arbitrary"), SMALL_CALL_VMEM_BYTES),
        name="windowed_gqa_meta",
    )(sink, qkv, qkv, qkv, qkv, qkv, attn), cast


def _fft_factors(T):
    best = None
    for n2 in range(BF16_SUBLANES, T + 1, BF16_SUBLANES):
        if T % n2:
            continue
        n1 = T // n2
        r1 = n1 * BF16_SUBLANES
        cost = (4 * _round_up(r1, MXU_DEPTH) * r1 * (n2 // BF16_SUBLANES)
                + 2 * _round_up(n2, MXU_DEPTH) * n2 * n1)
        if best is None or cost < best[0]:
            best = (cost, n1, n2)
    assert best is not None, "sequence length must be a multiple of 16"
    return best[1], best[2]


def _fft1_body(z_ref, mc_ref, ms_ref, twc_ref, tws_ref, a_ref, *, n1, R, C):
    rows = n1 * R
    ct = min(DFT_COL_TILE, C)
    twc = jnp.concatenate([twc_ref[0]] * (ct // LANES), axis=1)
    tws = jnp.concatenate([tws_ref[0]] * (ct // LANES), axis=1)
    for t in range(C // ct):
        cols = slice(t * ct, (t + 1) * ct)
        rhs = jnp.concatenate([z_ref[0, 0, :, :, cols].reshape(rows, ct), z_ref[0, 1, :, :, cols].reshape(rows, ct)],
                              axis=1)
        xc = jnp.dot(mc_ref[...], rhs, preferred_element_type=_F32)
        xs = jnp.dot(ms_ref[...], rhs, preferred_element_type=_F32)
        ar = xc[:, :ct] + xs[:, ct:]
        ai = xc[:, ct:] - xs[:, :ct]
        a_ref[0, 0, :, :, cols] = (ar * twc + ai * tws).reshape(n1, R, ct).astype(_BF16)
        a_ref[0, 1, :, :, cols] = (ai * twc - ar * tws).reshape(n1, R, ct).astype(_BF16)


def _fft2_body(ar_ref, ai_ref, mc_ref, ms_ref, f_ref):
    for b in range(ar_ref.shape[0]):
        res = (jnp.dot(mc_ref[0], ar_ref[b, 0, 0], preferred_element_type=_F32)
               + jnp.dot(ms_ref[0], ai_ref[b, 0, 0], preferred_element_type=_F32))
        f_ref[b] = res.astype(_BF16).reshape(f_ref.shape[1:])


def _unpermute_body(p_ref, x_ref, o_ref):
    for b in range(x_ref.shape[0]):
        o_ref[b] = jnp.dot(p_ref[...], x_ref[b], preferred_element_type=_F32).astype(_BF16)


def _fft_tables(T, n1, n2, n_meta, R):
    two_pi = 2.0 * np.pi
    k1 = jnp.arange(n1, dtype=jnp.int32)
    ang1 = ((k1[:, None] * k1[None, :]) % n1).astype(_F32) * (two_pi / n1)
    eye = jnp.eye(R, dtype=_F32)
    m1c = jnp.kron(jnp.cos(ang1) * (n1 ** -0.5), eye).astype(_BF16)
    m1s = jnp.kron(jnp.sin(ang1) * (n1 ** -0.5), eye).astype(_BF16)
    nn2 = jnp.arange(n2, dtype=jnp.int32).reshape(n2 // R, 1, R)
    angt = ((k1[None, :, None] * (nn2 + n_meta)) % T).astype(_F32) * (two_pi / T)
    angt = jnp.broadcast_to(angt.reshape(n2 // R, n1 * R, 1), (n2 // R, n1 * R, LANES))
    twc, tws = jnp.cos(angt), jnp.sin(angt)
    k2 = jnp.arange(n2, dtype=jnp.int32)[:, None]
    mm2 = jnp.arange(n2, dtype=jnp.int32)[None, :]
    outs_c, outs_s = [], []
    for rot in (n_meta // n1, n_meta // n1 + 1):
        ph2 = (((k2 + rot) % n2) * (mm2 + n_meta)) % n2
        ang2 = ph2.astype(_F32) * (two_pi / n2)
        outs_c.append(jnp.cos(ang2) * (n2 ** -0.5))
        outs_s.append(jnp.sin(ang2) * (n2 ** -0.5))
    return m1c, m1s, twc, tws, jnp.stack(outs_c).astype(_BF16), jnp.stack(outs_s).astype(_BF16)


def _position_dft(z, tables, *, n1, n2, R, n_meta):
    B, _, T, C = z.shape
    m1c, m1s, twc, tws, m2c, m2s = tables
    rows = n1 * R
    zv = z.reshape(B, 2, n1, n2, C)
    av = pl.pallas_call(
        functools.partial(_fft1_body, n1=n1, R=R, C=C),
        grid=(B, n2 // R),
        in_specs=[
            pl.BlockSpec((1, 2, n1, R, C), lambda b, t: (b, 0, 0, t, 0)),
            _const_spec((rows, rows)),
            _const_spec((rows, rows)),
            pl.BlockSpec((1, rows, LANES), lambda b, t: (t, 0, 0)),
            pl.BlockSpec((1, rows, LANES), lambda b, t: (t, 0, 0)),
        ],
        out_specs=pl.BlockSpec((1, 2, n1, R, C), lambda b, t: (b, 0, 0, t, 0)),
        out_shape=jax.ShapeDtypeStruct((B, 2, n1, n2, C), _BF16),
        compiler_params=_cparams(("parallel", "arbitrary"),
                                 4 * 2 * rows * C * 2 + 2 * 2 * rows * rows * 2 + 6 * rows * C * 4 + VMEM_RESERVE_BYTES),
        name="dft_stage1",
    )(zv, m1c, m1s, twc, tws)
    shift = n1 - (n_meta % n1)

    f = pl.pallas_call(
        _fft2_body,
        grid=(n1,),
        in_specs=[
            pl.BlockSpec((B, 1, 1, n2, C), lambda k: (0, 0, k, 0, 0)),
            pl.BlockSpec((B, 1, 1, n2, C), lambda k: (0, 1, k, 0, 0)),
            pl.BlockSpec((1, n2, n2), lambda k: (jnp.where(k < n_meta % n1, 1, 0), 0, 0)),
            pl.BlockSpec((1, n2, n2), lambda k: (jnp.where(k < n_meta % n1, 1, 0), 0, 0)),
        ],
        out_specs=pl.BlockSpec((B, n2 // R, None, R, C), lambda k: (0, 0, (k + shift) % n1, 0, 0)),
        out_shape=jax.ShapeDtypeStruct((B, n2 // R, n1, R, C), _BF16),
        compiler_params=_cparams(("arbitrary",), SMALL_CALL_VMEM_BYTES),
        name="dft_stage2",
    )(av, av, m2c, m2s)
    q = jnp.arange(rows, dtype=jnp.int32)
    src = R * (q % n1) + q // n1
    perm = (jnp.arange(rows, dtype=jnp.int32)[None, :] == src[:, None]).astype(_BF16)
    return pl.pallas_call(
        _unpermute_body,
        grid=(n2 // R,),
        in_specs=[_const_spec((rows, rows)), pl.BlockSpec((B, rows, C), lambda i: (0, i, 0))],
        out_specs=pl.BlockSpec((B, rows, C), lambda i: (0, i, 0)),
        out_shape=jax.ShapeDtypeStruct((B, T, C), _BF16),
        compiler_params=_cparams(("parallel",), SMALL_CALL_VMEM_BYTES),
        name="dft_unpermute",
    )(perm, f.reshape(B, T, C))


def _mix_body(*refs, tn, n_alias):
    h_ref, xn_ref, a_ref, f_ref, wpa_ref, wpf_ref, wg_ref, wo_ref = refs[:8]
    (o_ref,) = refs[8 + n_alias:]
    D = h_ref.shape[-1]
    o_ref[...] = h_ref[...]
    xn = xn_ref[...].reshape(-1, D)
    a = a_ref[...].reshape(-1, Q_DIM)
    f = f_ref[...].reshape(-1, FOURIER_DIM)
    for j in range(D // tn):
        cols = slice(j * tn, (j + 1) * tn)
        av = jnp.dot(a, wpa_ref[:, cols], preferred_element_type=_F32)
        fv = jnp.dot(f, wpf_ref[:, cols], preferred_element_type=_F32)
        ga = jax.nn.sigmoid(jnp.dot(xn, wg_ref[:, cols], preferred_element_type=_F32))
        gf = jax.nn.sigmoid(jnp.dot(xn, wg_ref[:, D + j * tn:D + (j + 1) * tn], preferred_element_type=_F32))
        mixed = (ga * av + gf * fv).astype(_BF16)
        o_ref[...] += jnp.dot(mixed, wo_ref[cols, :], preferred_element_type=_F32).reshape(o_ref.shape)


def _mix_out(rows, h, xn, attn, four, wpa, wpf, wg, wo, *, layer, tn, prev_out=None):
    B, T, D = h.shape

    def resident(k, n):
        return pl.BlockSpec((None, k, n), lambda *_: (layer, 0, 0), pipeline_mode=pl.Buffered(1))

    in_specs = [rows.spec(D), rows.spec(D), rows.spec(Q_DIM), rows.spec(FOURIER_DIM),
                resident(Q_DIM, D), resident(FOURIER_DIM, D), resident(D, 2 * D), resident(D, D)]
    args = [h, xn, attn, four, wpa, wpf, wg, wo]
    aliases = {}
    if prev_out is not None:
        extra, aliases = _alias_tail(len(args), [prev_out])
        in_specs += extra
        args.append(prev_out)
    m = rows.m
    vmem = ((Q_DIM + FOURIER_DIM + 3 * D) * D * 2 + 4 * m * D * 4 + 2 * m * D * 2
            + 2 * m * (Q_DIM + FOURIER_DIM) * 2 + 10 * m * tn * 4 + VMEM_RESERVE_BYTES)
    return pl.pallas_call(
        functools.partial(_mix_body, tn=tn, n_alias=len(aliases)),
        grid=rows.grid,
        in_specs=in_specs,
        out_specs=rows.spec(D),
        out_shape=jax.ShapeDtypeStruct((B, T, D), _F32),
        input_output_aliases=aliases,
        compiler_params=_cparams(("parallel", "arbitrary"), vmem),
        name="mixer_out" + ("_meta" if rows.meta else ""),
    )(*args)


def _rope_tables(n_real, n_meta):
    T = n_real + n_meta
    half = ROT_DIM // 2
    inv = 1.0 / (ROPE_THETA ** (jnp.arange(0, ROT_DIM, 2, dtype=_F32) / ROT_DIM))
    ang = jnp.arange(T, dtype=_F32)[:, None] * inv[None, :]
    cos, sin = jnp.cos(ang), jnp.sin(ang)
    cos = jnp.concatenate([cos[n_meta:], cos[:n_meta]], axis=0)
    sin = jnp.concatenate([sin[n_meta:], sin[:n_meta]], axis=0)
    ones = jnp.ones((T, HEAD_DIM - ROT_DIM), _F32)
    zeros_h = jnp.zeros((T, half), _F32)
    zeros_r = jnp.zeros((T, HEAD_DIM - ROT_DIM), _F32)
    c = jnp.concatenate([cos, cos, ones], axis=1)
    s1 = jnp.concatenate([-sin, zeros_h, zeros_r], axis=1)
    s2 = jnp.concatenate([zeros_h, sin, zeros_r], axis=1)
    return c, s1, s2


def _channel_dft_table():
    n = FOURIER_GROUP_DIM
    c = jnp.arange(n, dtype=jnp.int32)[:, None]
    k = jnp.arange(n, dtype=jnp.int32)[None, :]
    ang = ((c * k) % n).astype(_F32) * (2.0 * np.pi / n)
    return (jnp.concatenate([jnp.cos(ang), -jnp.sin(ang)], axis=1) * (n ** -0.5)).astype(_BF16)


def _pick_tile(n, pref):
    t = pref
    while n % t:
        t //= 2
    return t


def kernel(x, meta_tokens, ffn1_norm, ffn1_w_gate, ffn1_w_up, ffn1_w_down, mix_norm, w_in, w_gate, sink,
           w_attn_branch, w_fourier_branch, w_out, ffn2_norm, ffn2_w_gate, ffn2_w_up, ffn2_w_down, final_norm):
    B, S, D = x.shape
    n_meta = meta_tokens.shape[0]
    depth = ffn1_norm.shape[0]
    d_ff = ffn1_w_gate.shape[-1]
    T = S + n_meta
    assert S % (4 * BLOCK) == 0 and S % n_meta == 0 and BLOCK % n_meta == 0 and n_meta < BLOCK
    assert w_in.shape[-1] == IN_DIM and KV_DIM == PROJ_CHUNK

    tf = _pick_tile(d_ff, FF_CHUNK)
    tn = _pick_tile(D, MIX_CHUNK)
    qb = _pick_tile(S // BLOCK, ATTN_BLOCKS_PER_STEP)
    rows_ffn = _Rows(B, S, n_meta, _pick_tile(S, FFN_ROW_TILE), meta=False)
    rows_main = _Rows(B, S, n_meta, _pick_tile(S, PROJ_ROW_TILE), meta=False)
    rows_mix = _Rows(B, S, n_meta, _pick_tile(S, MIX_ROW_TILE), meta=False)
    rows_meta = _Rows(B, S, n_meta, 0, meta=True)

    n1, n2 = _fft_factors(T)
    fft_tables = _fft_tables(T, n1, n2, n_meta, BF16_SUBLANES)
    rope_c, rope_s1, rope_s2 = _rope_tables(S, n_meta)
    dft_c = _channel_dft_table()

    ffn1_w = tuple(_to_bf16(w) for w in (ffn1_w_gate, ffn1_w_up, ffn1_w_down))
    w_in_b = _to_bf16(w_in)

    def ffn(h, h_meta, gain, wts, layer, final_gain=None):
        out = _ffn(rows_ffn, h, gain, *wts, layer=layer, tf=tf, T=T, final_gain=final_gain)
        if final_gain is None:
            src, blk0 = h_meta
            out = _ffn(rows_meta, src, gain, *wts, layer=layer, tf=tf, T=T, h_row_block0=blk0, prev_out=out)
        return out

    meta = jnp.broadcast_to(meta_tokens[None].astype(x.dtype), (B, n_meta, D))
    h = None
    for l in range(depth):
        if l == 0:
            h = ffn(x, (meta, 0), ffn1_norm[l][None], ffn1_w, l)
        else:
            h = ffn(h, (h, None), ffn1_norm[l][None], ffn1_w, l)

        last = l == depth - 1
        gain = mix_norm[l][None]
        proj_args = (gain, w_in_b, rope_c, rope_s1, rope_s2, dft_c)
        if l == 0:
            qkv, z, xn, *ffn2_w = _proj(rows_main, h, *proj_args, layer=l,
                                        side_cast=(ffn2_w_gate, ffn2_w_up, ffn2_w_down))
        else:
            qkv, z, xn = _proj(rows_main, h, *proj_args, layer=l)
        qkv, z, xn = _proj(rows_meta, h, *proj_args, layer=l, prev_out=(qkv, z, xn))
        if l == 0:
            attn, mix_w = _attention(qkv, sink[l], n_real=S, n_meta=n_meta, qb=qb, meta_queries=not last,
                                     side_cast=(w_attn_branch, w_fourier_branch, w_gate, w_out))
        else:
            attn, _ = _attention(qkv, sink[l], n_real=S, n_meta=n_meta, qb=qb, meta_queries=not last)
        four = _position_dft(z, fft_tables, n1=n1, n2=n2, R=BF16_SUBLANES, n_meta=n_meta)
        mix_args = (xn, attn, four) + tuple(mix_w)
        h2 = _mix_out(rows_mix, h, *mix_args, layer=l, tn=tn)
        h = h2 if last else _mix_out(rows_meta, h, *mix_args, layer=l, tn=tn, prev_out=h2)

        h = ffn(h, (h, None), ffn2_norm[l][None], ffn2_w, l, final_gain=final_norm[None] if last else None)
    return h
```
